```python
import jax, jax.numpy as jnp
from jax import lax
import numpy as np

D_MODEL = 1024
BATCH = 8
SEQ = 8192
DEPTH = 2

PLE_DIM = 256
D_FF = 2816
D_POOL = D_MODEL
N_POOL_GROUPS = 4
POOL_GROUP = D_POOL // N_POOL_GROUPS
POOL_WINDOWS = (2, 4, 8, 16)
D_CONV = D_MODEL
CONV_K = 31
N_IN = D_POOL + 2 * D_CONV + 2 * D_MODEL
RMS_EPS = 1e-6
LN_EPS = 1e-5

kernel_name = "hybrid_pool_conformer_macaron_ple"


def rmsnorm(x, g):
    x32 = x.astype(jnp.float32)
    y = x32 * lax.rsqrt(jnp.mean(x32 * x32, axis=-1, keepdims=True) + RMS_EPS)
    return (y * g.astype(jnp.float32)).astype(x.dtype)


def layernorm(x, g, b):
    x32 = x.astype(jnp.float32)
    mu = jnp.mean(x32, axis=-1, keepdims=True)
    var = jnp.mean(jnp.square(x32 - mu), axis=-1, keepdims=True)
    y = (x32 - mu) * lax.rsqrt(var + LN_EPS)
    return (y * g.astype(jnp.float32) + b.astype(jnp.float32)).astype(x.dtype)


def swiglu(x, w_gate, w_up, w_down):
    return (jax.nn.silu(x @ w_gate) * (x @ w_up)) @ w_down


def causal_multiscale_pool(z):
    S = z.shape[1]
    cs = jnp.cumsum(z.astype(jnp.float32), axis=1)
    pos = jnp.arange(S, dtype=jnp.int32)
    outs = []
    for g, w in enumerate(POOL_WINDOWS):
        sl = slice(g * POOL_GROUP, (g + 1) * POOL_GROUP)
        cs_g = cs[..., sl]
        lower = jnp.pad(cs_g, ((0, 0), (w, 0), (0, 0)))[:, :S]
        count = jnp.minimum(pos + 1, w).astype(jnp.float32)[None, :, None]
        mean = (cs_g - lower) / count
        outs.append(mean - z[..., sl].astype(jnp.float32))
    return jnp.concatenate(outs, axis=-1).astype(z.dtype)


def causal_depthwise_conv(x, w, b):
    K, C = w.shape
    y = lax.conv_general_dilated(
        x, w[:, None, :].astype(x.dtype), window_strides=(1,), padding=[(K - 1, 0)],
        dimension_numbers=("NWC", "WIO", "NWC"), feature_group_count=C)
    return y + b


def _fwd_setup_inputs(seed: int = 0) -> dict:
    key = jax.random.key(seed)
    ks = iter(jax.random.split(key, 32))

    def nrm(shape, fan_in):
        return jax.random.normal(next(ks), shape, jnp.float32) * (fan_in ** -0.5)

    def gain(shape):
        return 1.0 + 0.02 * jax.random.normal(next(ks), shape, jnp.float32)

    def bias(shape):
        return 0.02 * jax.random.normal(next(ks), shape, jnp.float32)

    L = DEPTH
    return {
        "x": jax.random.normal(next(ks), (BATCH, SEQ, D_MODEL), jnp.float32),
        "p": jax.random.normal(next(ks), (DEPTH, BATCH, SEQ, PLE_DIM), jnp.float32),
        "ffn1_norm": gain((L, D_MODEL)),
        "ffn1_w_gate": nrm((L, D_MODEL, D_FF), D_MODEL),
        "ffn1_w_up": nrm((L, D_MODEL, D_FF), D_MODEL),
        "ffn1_w_down": nrm((L, D_FF, D_MODEL), D_FF),
        "mix_norm": gain((L, D_MODEL)),
        "w_in": nrm((L, D_MODEL, N_IN), D_MODEL),
        "pool_w": nrm((L, N_POOL_GROUPS, POOL_GROUP, POOL_GROUP), POOL_GROUP),
        "pool_scale": gain((L, D_POOL)),
        "conv_dw_w": nrm((L, CONV_K, D_CONV), CONV_K),
        "conv_dw_b": bias((L, D_CONV)),
        "conv_ln_g": gain((L, D_CONV)),
        "conv_ln_b": bias((L, D_CONV)),
        "conv_w_out": nrm((L, D_CONV, D_MODEL), D_CONV),
        "w_out": nrm((L, D_MODEL, D_MODEL), D_MODEL),
        "ffn2_norm": gain((L, D_MODEL)),
        "ffn2_w_gate": nrm((L, D_MODEL, D_FF), D_MODEL),
        "ffn2_w_up": nrm((L, D_MODEL, D_FF), D_MODEL),
        "ffn2_w_down": nrm((L, D_FF, D_MODEL), D_FF),
        "ple_norm": gain((L, D_MODEL)),
        "ple_w_gate": nrm((L, D_MODEL, D_MODEL), D_MODEL),
        "ple_w_proj": nrm((L, PLE_DIM, D_MODEL), PLE_DIM),
        "final_norm": gain((D_MODEL,)),
    }


def _fwd_reference(x, p, ffn1_norm, ffn1_w_gate, ffn1_w_up, ffn1_w_down, mix_norm, w_in,
              pool_w, pool_scale, conv_dw_w, conv_dw_b, conv_ln_g, conv_ln_b, conv_w_out,
              w_out, ffn2_norm, ffn2_w_gate, ffn2_w_up, ffn2_w_down, ple_norm, ple_w_gate,
              ple_w_proj, final_norm):
    B, S, _ = x.shape
    h = x
    split_pts = [D_POOL, D_POOL + D_CONV, D_POOL + 2 * D_CONV, D_POOL + 2 * D_CONV + D_MODEL]
    for i in range(DEPTH):
        h = h + 0.5 * swiglu(rmsnorm(h, ffn1_norm[i]), ffn1_w_gate[i], ffn1_w_up[i], ffn1_w_down[i])

        u = rmsnorm(h, mix_norm[i])
        z = u @ w_in[i]
        z_pool, z_glu_a, z_glu_g, g_pool, g_conv = jnp.split(z, split_pts, axis=-1)

        pooled = causal_multiscale_pool(z_pool).reshape(B, S, N_POOL_GROUPS, POOL_GROUP)
        a = jnp.einsum("bsgc,gcd->bsgd", pooled, pool_w[i]).reshape(B, S, D_POOL)
        a = a * pool_scale[i]

        c = z_glu_a * jax.nn.sigmoid(z_glu_g)
        c = causal_depthwise_conv(c, conv_dw_w[i], conv_dw_b[i])
        c = jax.nn.silu(layernorm(c, conv_ln_g[i], conv_ln_b[i]))
        c = c @ conv_w_out[i]

        m = jax.nn.sigmoid(g_pool) * a + jax.nn.sigmoid(g_conv) * c
        h = h + m @ w_out[i]

        h = h + 0.5 * swiglu(rmsnorm(h, ffn2_norm[i]), ffn2_w_gate[i], ffn2_w_up[i], ffn2_w_down[i])

        gate = jax.nn.sigmoid(rmsnorm(h, ple_norm[i]) @ ple_w_gate[i])
        h = h + gate * (p[i] @ ple_w_proj[i])
    return rmsnorm(h, final_norm)


import jax as _jax
import jax.numpy as _jnp

TWIN_FORMAT = 'train_step'
FWD_PARAMS = ['x', 'p', 'ffn1_norm', 'ffn1_w_gate', 'ffn1_w_up', 'ffn1_w_down', 'mix_norm', 'w_in', 'pool_w', 'pool_scale', 'conv_dw_w', 'conv_dw_b', 'conv_ln_g', 'conv_ln_b', 'conv_w_out', 'w_out', 'ffn2_norm', 'ffn2_w_gate', 'ffn2_w_up', 'ffn2_w_down', 'ple_norm', 'ple_w_gate', 'ple_w_proj', 'final_norm']
TWIN_WEIGHTS = ['ffn1_norm', 'ffn1_w_gate', 'ffn1_w_up', 'ffn1_w_down', 'mix_norm', 'w_in', 'pool_w', 'pool_scale', 'conv_dw_w', 'conv_dw_b', 'conv_ln_g', 'conv_ln_b', 'conv_w_out', 'w_out', 'ffn2_norm', 'ffn2_w_gate', 'ffn2_w_up', 'ffn2_w_down', 'ple_norm', 'ple_w_gate', 'ple_w_proj', 'final_norm']
TWIN_DIFF_INPUT = 'x'
TWIN_INPUTS = ['x', 'p', 'ffn1_norm', 'ffn1_w_gate', 'ffn1_w_up', 'ffn1_w_down', 'mix_norm', 'w_in', 'pool_w', 'pool_scale', 'conv_dw_w', 'conv_dw_b', 'conv_ln_g', 'conv_ln_b', 'conv_w_out', 'w_out', 'ffn2_norm', 'ffn2_w_gate', 'ffn2_w_up', 'ffn2_w_down', 'ple_norm', 'ple_w_gate', 'ple_w_proj', 'final_norm', 'loss_target', 'm_ffn1_norm', 'm_ffn1_w_gate', 'm_ffn1_w_up', 'm_ffn1_w_down', 'm_mix_norm', 'm_w_in', 'm_pool_w', 'm_pool_scale', 'm_conv_dw_w', 'm_conv_dw_b', 'm_conv_ln_g', 'm_conv_ln_b', 'm_conv_w_out', 'm_w_out', 'm_ffn2_norm', 'm_ffn2_w_gate', 'm_ffn2_w_up', 'm_ffn2_w_down', 'm_ple_norm', 'm_ple_w_gate', 'm_ple_w_proj', 'm_final_norm', 'v_ffn1_norm', 'v_ffn1_w_gate', 'v_ffn1_w_up', 'v_ffn1_w_down', 'v_mix_norm', 'v_w_in', 'v_pool_w', 'v_pool_scale', 'v_conv_dw_w', 'v_conv_dw_b', 'v_conv_ln_g', 'v_conv_ln_b', 'v_conv_w_out', 'v_w_out', 'v_ffn2_norm', 'v_ffn2_w_gate', 'v_ffn2_w_up', 'v_ffn2_w_down', 'v_ple_norm', 'v_ple_w_gate', 'v_ple_w_proj', 'v_final_norm']
TWIN_OUTPUTS = ['loss', 'grad_x', 'grad_ffn1_norm', 'grad_ffn1_w_gate', 'grad_ffn1_w_up', 'grad_ffn1_w_down', 'grad_mix_norm', 'grad_w_in', 'grad_pool_w', 'grad_pool_scale', 'grad_conv_dw_w', 'grad_conv_dw_b', 'grad_conv_ln_g', 'grad_conv_ln_b', 'grad_conv_w_out', 'grad_w_out', 'grad_ffn2_norm', 'grad_ffn2_w_gate', 'grad_ffn2_w_up', 'grad_ffn2_w_down', 'grad_ple_norm', 'grad_ple_w_gate', 'grad_ple_w_proj', 'grad_final_norm', 'delta_ffn1_norm', 'delta_ffn1_w_gate', 'delta_ffn1_w_up', 'delta_ffn1_w_down', 'delta_mix_norm', 'delta_w_in', 'delta_pool_w', 'delta_pool_scale', 'delta_conv_dw_w', 'delta_conv_dw_b', 'delta_conv_ln_g', 'delta_conv_ln_b', 'delta_conv_w_out', 'delta_w_out', 'delta_ffn2_norm', 'delta_ffn2_w_gate', 'delta_ffn2_w_up', 'delta_ffn2_w_down', 'delta_ple_norm', 'delta_ple_w_gate', 'delta_ple_w_proj', 'delta_final_norm', 'new_m_ffn1_norm', 'new_m_ffn1_w_gate', 'new_m_ffn1_w_up', 'new_m_ffn1_w_down', 'new_m_mix_norm', 'new_m_w_in', 'new_m_pool_w', 'new_m_pool_scale', 'new_m_conv_dw_w', 'new_m_conv_dw_b', 'new_m_conv_ln_g', 'new_m_conv_ln_b', 'new_m_conv_w_out', 'new_m_w_out', 'new_m_ffn2_norm', 'new_m_ffn2_w_gate', 'new_m_ffn2_w_up', 'new_m_ffn2_w_down', 'new_m_ple_norm', 'new_m_ple_w_gate', 'new_m_ple_w_proj', 'new_m_final_norm', 'new_v_ffn1_norm', 'new_v_ffn1_w_gate', 'new_v_ffn1_w_up', 'new_v_ffn1_w_down', 'new_v_mix_norm', 'new_v_w_in', 'new_v_pool_w', 'new_v_pool_scale', 'new_v_conv_dw_w', 'new_v_conv_dw_b', 'new_v_conv_ln_g', 'new_v_conv_ln_b', 'new_v_conv_w_out', 'new_v_w_out', 'new_v_ffn2_norm', 'new_v_ffn2_w_gate', 'new_v_ffn2_w_up', 'new_v_ffn2_w_down', 'new_v_ple_norm', 'new_v_ple_w_gate', 'new_v_ple_w_proj', 'new_v_final_norm']
TWIN_LEAF_KINDS = {'loss': 'loss', 'grad_x': 'grad_x', 'grad_ffn1_norm': 'grad_w', 'grad_ffn1_w_gate': 'grad_w', 'grad_ffn1_w_up': 'grad_w', 'grad_ffn1_w_down': 'grad_w', 'grad_mix_norm': 'grad_w', 'grad_w_in': 'grad_w', 'grad_pool_w': 'grad_w', 'grad_pool_scale': 'grad_w', 'grad_conv_dw_w': 'grad_w', 'grad_conv_dw_b': 'grad_w', 'grad_conv_ln_g': 'grad_w', 'grad_conv_ln_b': 'grad_w', 'grad_conv_w_out': 'grad_w', 'grad_w_out': 'grad_w', 'grad_ffn2_norm': 'grad_w', 'grad_ffn2_w_gate': 'grad_w', 'grad_ffn2_w_up': 'grad_w', 'grad_ffn2_w_down': 'grad_w', 'grad_ple_norm': 'grad_w', 'grad_ple_w_gate': 'grad_w', 'grad_ple_w_proj': 'grad_w', 'grad_final_norm': 'grad_w', 'delta_ffn1_norm': 'delta_w', 'delta_ffn1_w_gate': 'delta_w', 'delta_ffn1_w_up': 'delta_w', 'delta_ffn1_w_down': 'delta_w', 'delta_mix_norm': 'delta_w', 'delta_w_in': 'delta_w', 'delta_pool_w': 'delta_w', 'delta_pool_scale': 'delta_w', 'delta_conv_dw_w': 'delta_w', 'delta_conv_dw_b': 'delta_w', 'delta_conv_ln_g': 'delta_w', 'delta_conv_ln_b': 'delta_w', 'delta_conv_w_out': 'delta_w', 'delta_w_out': 'delta_w', 'delta_ffn2_norm': 'delta_w', 'delta_ffn2_w_gate': 'delta_w', 'delta_ffn2_w_up': 'delta_w', 'delta_ffn2_w_down': 'delta_w', 'delta_ple_norm': 'delta_w', 'delta_ple_w_gate': 'delta_w', 'delta_ple_w_proj': 'delta_w', 'delta_final_norm': 'delta_w', 'new_m_ffn1_norm': 'new_m', 'new_m_ffn1_w_gate': 'new_m', 'new_m_ffn1_w_up': 'new_m', 'new_m_ffn1_w_down': 'new_m', 'new_m_mix_norm': 'new_m', 'new_m_w_in': 'new_m', 'new_m_pool_w': 'new_m', 'new_m_pool_scale': 'new_m', 'new_m_conv_dw_w': 'new_m', 'new_m_conv_dw_b': 'new_m', 'new_m_conv_ln_g': 'new_m', 'new_m_conv_ln_b': 'new_m', 'new_m_conv_w_out': 'new_m', 'new_m_w_out': 'new_m', 'new_m_ffn2_norm': 'new_m', 'new_m_ffn2_w_gate': 'new_m', 'new_m_ffn2_w_up': 'new_m', 'new_m_ffn2_w_down': 'new_m', 'new_m_ple_norm': 'new_m', 'new_m_ple_w_gate': 'new_m', 'new_m_ple_w_proj': 'new_m', 'new_m_final_norm': 'new_m', 'new_v_ffn1_norm': 'new_v', 'new_v_ffn1_w_gate': 'new_v', 'new_v_ffn1_w_up': 'new_v', 'new_v_ffn1_w_down': 'new_v', 'new_v_mix_norm': 'new_v', 'new_v_w_in': 'new_v', 'new_v_pool_w': 'new_v', 'new_v_pool_scale': 'new_v', 'new_v_conv_dw_w': 'new_v', 'new_v_conv_dw_b': 'new_v', 'new_v_conv_ln_g': 'new_v', 'new_v_conv_ln_b': 'new_v', 'new_v_conv_w_out': 'new_v', 'new_v_w_out': 'new_v', 'new_v_ffn2_norm': 'new_v', 'new_v_ffn2_w_gate': 'new_v', 'new_v_ffn2_w_up': 'new_v', 'new_v_ffn2_w_down': 'new_v', 'new_v_ple_norm': 'new_v', 'new_v_ple_w_gate': 'new_v', 'new_v_ple_w_proj': 'new_v', 'new_v_final_norm': 'new_v'}


def _forward(args):
    return _fwd_reference(*[args[k] for k in FWD_PARAMS])


def _output_shape():
    out = _jax.eval_shape(lambda: _forward(_fwd_setup_inputs(0)))
    return out.shape, out.dtype

N_MICROBATCH = 1
ADAM_LR = 0.001
ADAM_B1 = 0.9
ADAM_B2 = 0.999
ADAM_EPS = 1e-08
ADAM_WD = 0.01
ADAM_STEP = 10
PER_EXAMPLE_BATCH_AXIS = {'x': 0, 'p': 1, 'loss_target': 0}
SHARED_INPUTS = []
_WEIGHT_DTYPES = {'ffn1_norm': _jnp.float32, 'ffn1_w_gate': _jnp.float32, 'ffn1_w_up': _jnp.float32, 'ffn1_w_down': _jnp.float32, 'mix_norm': _jnp.float32, 'w_in': _jnp.float32, 'pool_w': _jnp.float32, 'pool_scale': _jnp.float32, 'conv_dw_w': _jnp.float32, 'conv_dw_b': _jnp.float32, 'conv_ln_g': _jnp.float32, 'conv_ln_b': _jnp.float32, 'conv_w_out': _jnp.float32, 'w_out': _jnp.float32, 'ffn2_norm': _jnp.float32, 'ffn2_w_gate': _jnp.float32, 'ffn2_w_up': _jnp.float32, 'ffn2_w_down': _jnp.float32, 'ple_norm': _jnp.float32, 'ple_w_gate': _jnp.float32, 'ple_w_proj': _jnp.float32, 'final_norm': _jnp.float32}
MOMENT_SCALE = {'ffn1_norm': 9.376261e-02, 'ffn1_w_gate': 3.931854e-02, 'ffn1_w_up': 3.811361e-02, 'ffn1_w_down': 6.299401e-02, 'mix_norm': 1.204944e-01, 'w_in': 5.268713e-02, 'pool_w': 8.836557e-02, 'pool_scale': 8.711621e-02, 'conv_dw_w': 6.215057e-02, 'conv_dw_b': 1.338184e-01, 'conv_ln_g': 7.721788e-02, 'conv_ln_b': 7.141174e-02, 'conv_w_out': 6.075029e-02, 'w_out': 1.069699e-01, 'ffn2_norm': 7.586465e-02, 'ffn2_w_gate': 3.259182e-02, 'ffn2_w_up': 3.153845e-02, 'ffn2_w_down': 5.232333e-02, 'ple_norm': 3.698060e-02, 'ple_w_gate': 3.665746e-02, 'ple_w_proj': 9.364406e-02, 'final_norm': 6.379361e+01}


def _to_microbatches(a, axis):
    t = _jnp.moveaxis(a, axis, 0)
    t = t.reshape((N_MICROBATCH, t.shape[0] // N_MICROBATCH) + t.shape[1:])
    return _jnp.moveaxis(t, 1, axis + 1)


def setup_inputs(seed: int = 0) -> dict:
    inp = _fwd_setup_inputs(seed)
    key = _jax.random.fold_in(_jax.random.key(seed), 7919)
    shape, _ = _output_shape()
    out = dict(inp)
    out["loss_target"] = _jax.random.normal(_jax.random.fold_in(key, 0), shape, _jnp.float32)
    for i, name in enumerate(TWIN_WEIGHTS):
        w = inp[name].astype(_jnp.float32)
        if MOMENT_SCALE is None:
            s = _jnp.sqrt(_jnp.mean(_jnp.square(w)) + 1e-30)
        else:
            s = MOMENT_SCALE[name]
        km, kv = _jax.random.split(_jax.random.fold_in(key, i + 1))
        out[name] = w
        out["m_" + name] = s * _jax.random.normal(km, w.shape, _jnp.float32)
        out["v_" + name] = (s * s) * _jax.random.uniform(kv, w.shape, _jnp.float32, 0.5, 1.5)
    if N_MICROBATCH > 1:
        for name, axis in PER_EXAMPLE_BATCH_AXIS.items():
            out[name] = _to_microbatches(out[name], axis)
    return {'x': out['x'], 'p': out['p'], 'ffn1_norm': out['ffn1_norm'], 'ffn1_w_gate': out['ffn1_w_gate'], 'ffn1_w_up': out['ffn1_w_up'], 'ffn1_w_down': out['ffn1_w_down'], 'mix_norm': out['mix_norm'], 'w_in': out['w_in'], 'pool_w': out['pool_w'], 'pool_scale': out['pool_scale'], 'conv_dw_w': out['conv_dw_w'], 'conv_dw_b': out['conv_dw_b'], 'conv_ln_g': out['conv_ln_g'], 'conv_ln_b': out['conv_ln_b'], 'conv_w_out': out['conv_w_out'], 'w_out': out['w_out'], 'ffn2_norm': out['ffn2_norm'], 'ffn2_w_gate': out['ffn2_w_gate'], 'ffn2_w_up': out['ffn2_w_up'], 'ffn2_w_down': out['ffn2_w_down'], 'ple_norm': out['ple_norm'], 'ple_w_gate': out['ple_w_gate'], 'ple_w_proj': out['ple_w_proj'], 'final_norm': out['final_norm'], 'loss_target': out['loss_target'], 'm_ffn1_norm': out['m_ffn1_norm'], 'm_ffn1_w_gate': out['m_ffn1_w_gate'], 'm_ffn1_w_up': out['m_ffn1_w_up'], 'm_ffn1_w_down': out['m_ffn1_w_down'], 'm_mix_norm': out['m_mix_norm'], 'm_w_in': out['m_w_in'], 'm_pool_w': out['m_pool_w'], 'm_pool_scale': out['m_pool_scale'], 'm_conv_dw_w': out['m_conv_dw_w'], 'm_conv_dw_b': out['m_conv_dw_b'], 'm_conv_ln_g': out['m_conv_ln_g'], 'm_conv_ln_b': out['m_conv_ln_b'], 'm_conv_w_out': out['m_conv_w_out'], 'm_w_out': out['m_w_out'], 'm_ffn2_norm': out['m_ffn2_norm'], 'm_ffn2_w_gate': out['m_ffn2_w_gate'], 'm_ffn2_w_up': out['m_ffn2_w_up'], 'm_ffn2_w_down': out['m_ffn2_w_down'], 'm_ple_norm': out['m_ple_norm'], 'm_ple_w_gate': out['m_ple_w_gate'], 'm_ple_w_proj': out['m_ple_w_proj'], 'm_final_norm': out['m_final_norm'], 'v_ffn1_norm': out['v_ffn1_norm'], 'v_ffn1_w_gate': out['v_ffn1_w_gate'], 'v_ffn1_w_up': out['v_ffn1_w_up'], 'v_ffn1_w_down': out['v_ffn1_w_down'], 'v_mix_norm': out['v_mix_norm'], 'v_w_in': out['v_w_in'], 'v_pool_w': out['v_pool_w'], 'v_pool_scale': out['v_pool_scale'], 'v_conv_dw_w': out['v_conv_dw_w'], 'v_conv_dw_b': out['v_conv_dw_b'], 'v_conv_ln_g': out['v_conv_ln_g'], 'v_conv_ln_b': out['v_conv_ln_b'], 'v_conv_w_out': out['v_conv_w_out'], 'v_w_out': out['v_w_out'], 'v_ffn2_norm': out['v_ffn2_norm'], 'v_ffn2_w_gate': out['v_ffn2_w_gate'], 'v_ffn2_w_up': out['v_ffn2_w_up'], 'v_ffn2_w_down': out['v_ffn2_w_down'], 'v_ple_norm': out['v_ple_norm'], 'v_ple_w_gate': out['v_ple_w_gate'], 'v_ple_w_proj': out['v_ple_w_proj'], 'v_final_norm': out['v_final_norm']}


def _loss(weights, diff, rest, loss_target):
    with _jax.named_scope("forward"):
        args = {**rest, TWIN_DIFF_INPUT: diff, **{k: w.astype(_WEIGHT_DTYPES[k]) for k, w in weights.items()}}
        y = _forward(args)
    with _jax.named_scope("loss_head"):
        err = _jnp.square(y.astype(_jnp.float32) - loss_target)
        return 0.5 * _jnp.sum(_jnp.mean(err, axis=-1)) if err.ndim else 0.5 * err


def _adamw(w, g, m, v):
    m = ADAM_B1 * m + (1.0 - ADAM_B1) * g
    v = ADAM_B2 * v + (1.0 - ADAM_B2) * _jnp.square(g)
    m_hat = m / (1.0 - ADAM_B1 ** ADAM_STEP)
    v_hat = v / (1.0 - ADAM_B2 ** ADAM_STEP)
    delta = -ADAM_LR * (m_hat / (_jnp.sqrt(v_hat) + ADAM_EPS) + ADAM_WD * w)
    return delta, m, v


def reference(x, p, ffn1_norm, ffn1_w_gate, ffn1_w_up, ffn1_w_down, mix_norm, w_in, pool_w, pool_scale, conv_dw_w, conv_dw_b, conv_ln_g, conv_ln_b, conv_w_out, w_out, ffn2_norm, ffn2_w_gate, ffn2_w_up, ffn2_w_down, ple_norm, ple_w_gate, ple_w_proj, final_norm, loss_target, m_ffn1_norm, m_ffn1_w_gate, m_ffn1_w_up, m_ffn1_w_down, m_mix_norm, m_w_in, m_pool_w, m_pool_scale, m_conv_dw_w, m_conv_dw_b, m_conv_ln_g, m_conv_ln_b, m_conv_w_out, m_w_out, m_ffn2_norm, m_ffn2_w_gate, m_ffn2_w_up, m_ffn2_w_down, m_ple_norm, m_ple_w_gate, m_ple_w_proj, m_final_norm, v_ffn1_norm, v_ffn1_w_gate, v_ffn1_w_up, v_ffn1_w_down, v_mix_norm, v_w_in, v_pool_w, v_pool_scale, v_conv_dw_w, v_conv_dw_b, v_conv_ln_g, v_conv_ln_b, v_conv_w_out, v_w_out, v_ffn2_norm, v_ffn2_w_gate, v_ffn2_w_up, v_ffn2_w_down, v_ple_norm, v_ple_w_gate, v_ple_w_proj, v_final_norm):
    given = dict(x=x, p=p, ffn1_norm=ffn1_norm, ffn1_w_gate=ffn1_w_gate, ffn1_w_up=ffn1_w_up, ffn1_w_down=ffn1_w_down, mix_norm=mix_norm, w_in=w_in, pool_w=pool_w, pool_scale=pool_scale, conv_dw_w=conv_dw_w, conv_dw_b=conv_dw_b, conv_ln_g=conv_ln_g, conv_ln_b=conv_ln_b, conv_w_out=conv_w_out, w_out=w_out, ffn2_norm=ffn2_norm, ffn2_w_gate=ffn2_w_gate, ffn2_w_up=ffn2_w_up, ffn2_w_down=ffn2_w_down, ple_norm=ple_norm, ple_w_gate=ple_w_gate, ple_w_proj=ple_w_proj, final_norm=final_norm, loss_target=loss_target, m_ffn1_norm=m_ffn1_norm, m_ffn1_w_gate=m_ffn1_w_gate, m_ffn1_w_up=m_ffn1_w_up, m_ffn1_w_down=m_ffn1_w_down, m_mix_norm=m_mix_norm, m_w_in=m_w_in, m_pool_w=m_pool_w, m_pool_scale=m_pool_scale, m_conv_dw_w=m_conv_dw_w, m_conv_dw_b=m_conv_dw_b, m_conv_ln_g=m_conv_ln_g, m_conv_ln_b=m_conv_ln_b, m_conv_w_out=m_conv_w_out, m_w_out=m_w_out, m_ffn2_norm=m_ffn2_norm, m_ffn2_w_gate=m_ffn2_w_gate, m_ffn2_w_up=m_ffn2_w_up, m_ffn2_w_down=m_ffn2_w_down, m_ple_norm=m_ple_norm, m_ple_w_gate=m_ple_w_gate, m_ple_w_proj=m_ple_w_proj, m_final_norm=m_final_norm, v_ffn1_norm=v_ffn1_norm, v_ffn1_w_gate=v_ffn1_w_gate, v_ffn1_w_up=v_ffn1_w_up, v_ffn1_w_down=v_ffn1_w_down, v_mix_norm=v_mix_norm, v_w_in=v_w_in, v_pool_w=v_pool_w, v_pool_scale=v_pool_scale, v_conv_dw_w=v_conv_dw_w, v_conv_dw_b=v_conv_dw_b, v_conv_ln_g=v_conv_ln_g, v_conv_ln_b=v_conv_ln_b, v_conv_w_out=v_conv_w_out, v_w_out=v_w_out, v_ffn2_norm=v_ffn2_norm, v_ffn2_w_gate=v_ffn2_w_gate, v_ffn2_w_up=v_ffn2_w_up, v_ffn2_w_down=v_ffn2_w_down, v_ple_norm=v_ple_norm, v_ple_w_gate=v_ple_w_gate, v_ple_w_proj=v_ple_w_proj, v_final_norm=v_final_norm)
    weights = {n: given[n] for n in TWIN_WEIGHTS}
    shared = {n: given[n] for n in SHARED_INPUTS}
    per_example = {n: given[n] for n in ['x', 'p']}
    grad_fn = _jax.value_and_grad(_loss, argnums=(0, 1))

    def one_microbatch(ex, loss_target):
        ex = dict(ex)
        diff = ex.pop(TWIN_DIFF_INPUT)
        return grad_fn(weights, diff, {**shared, **ex}, loss_target)

    if N_MICROBATCH == 1:
        loss, (grad_w, grad_x) = one_microbatch(per_example, given["loss_target"])
    else:
        def body(carry, xs):
            loss_sum, grad_sum = carry
            l_k, (gw_k, gx_k) = one_microbatch(xs[0], xs[1])
            with _jax.named_scope("update"):
                return (loss_sum + l_k, _jax.tree.map(_jnp.add, grad_sum, gw_k)), gx_k

        init = (_jnp.zeros((), _jnp.float32), _jax.tree.map(_jnp.zeros_like, weights))
        (loss, grad_w), grad_x = _jax.lax.scan(body, init, (per_example, given["loss_target"]))
    with _jax.named_scope("update"):
        delta_w, new_m, new_v = {}, {}, {}
        for n in TWIN_WEIGHTS:
            delta_w[n], new_m[n], new_v[n] = _adamw(weights[n], grad_w[n], given["m_" + n], given["v_" + n])
    return (loss, grad_x, *[grad_w[n] for n in TWIN_WEIGHTS], *[delta_w[n] for n in TWIN_WEIGHTS],
            *[new_m[n] for n in TWIN_WEIGHTS], *[new_v[n] for n in TWIN_WEIGHTS])
```

```python
import functools

import jax
import jax.numpy as jnp
from jax import lax
from jax.experimental import pallas as pl
from jax.experimental.pallas import tpu as pltpu

T = 8192
D = 1024
F = 2816
PD = 256
NG = 4
WINDOWS = (2, 4, 8, 16)
KC = 31
HALO = 32
DEPTH = 2
NCHIP = 4
RMS_EPS = 1e-6
LN_EPS = 1e-5

ADAM_LR = 0.001
ADAM_B1 = 0.9
ADAM_B2 = 0.999
ADAM_EPS = 1e-08
ADAM_WD = 0.01
ADAM_STEP = 10

TM = 512
TMB = 256
TMW = 512
VMEM_LIMIT = 56 * 1024 * 1024

BF = jnp.bfloat16
F32 = jnp.float32
MESH = pl.DeviceIdType.MESH
ANY = pl.BlockSpec(memory_space=pl.ANY)


def _layout():
    fs, ins, ds = F // NCHIP, 5 * D // NCHIP, D // NCHIP
    pps = ds * PD // D
    pws = NG * (D // NG // NCHIP) * (D // NG) // D
    names = [("wg1", fs), ("wu1", fs), ("wd1", fs), ("wg2", fs), ("wu2", fs), ("wd2", fs),
             ("win", ins), ("wco", ds), ("wo", ds), ("wpg", ds), ("wpp", pps), ("pw", pws)]
    off, r = {}, 0
    for n, s in names:
        off[n] = (r, s)
        r += s
    return off, r


def _sig(v):
    return 1.0 / (1.0 + jnp.exp(-v))


def _dot_nn(a, b):
    return jnp.dot(a, b, preferred_element_type=F32)


def _dot_nt(a, b):
    return lax.dot_general(a, b, (((1,), (1,)), ((), ())), preferred_element_type=F32)


def _dot_tn(a, b):
    return lax.dot_general(a, b, (((0,), (0,)), ((), ())), preferred_element_type=F32)


def _params(sem=("arbitrary",)):
    return pltpu.CompilerParams(dimension_semantics=sem, vmem_limit_bytes=VMEM_LIMIT)


def _weight_copies(w_hbm, li, specs, sem):
    lay, _ = _layout()
    cps = []
    for i, (name, dst) in enumerate(specs):
        off, rs = lay[name]
        for k in range(NCHIP):
            cps.append(pltpu.make_async_copy(w_hbm.at[li, k, pl.ds(off, rs), :],
                                             dst.at[pl.ds(k * rs, rs), :], sem.at[i * NCHIP + k]))
    return cps


def _load_weights_once(w_hbm, li, specs, sem):
    @pl.when(pl.program_id(0) == 0)
    def _():
        cps = _weight_copies(w_hbm, li, specs, sem)
        for cp in cps:
            cp.start()
        for cp in cps:
            cp.wait()


def _grad_copies(stage, g_hbm, li, name, row0, rows, piece, sem, sem0):
    lay, _ = _layout()
    off, rs = lay[name]
    cps = []
    for i in range(rows // piece):
        rglob = row0 + i * piece
        k = rglob // rs
        loc = rglob - k * rs
        start = off + loc
        if not isinstance(start, int):
            start = pl.multiple_of(start, 16)
        dst = g_hbm.at[li, k, pl.ds(start, piece), :]
        cps.append(pltpu.make_async_copy(stage.at[pl.ds(i * piece, piece), :], dst, sem.at[sem0 + i]))
    return cps


def _row(v):
    return v.reshape(1, -1)


def _ffn_fwd(h, g, wfull, li, which):
    nt = T // TM
    fc = F // 2
    names = ("wg%d" % which, "wu%d" % which, "wd%d" % which)

    def body(h_ref, g_ref, w_hbm, ho_ref, a_ref, b_ref, wg, wu, wd, sem):
        _load_weights_once(w_hbm, li, ((names[0], wg), (names[1], wu), (names[2], wd)), sem)
        x = h_ref[...]
        r = lax.rsqrt(jnp.mean(x * x, axis=-1, keepdims=True) + RMS_EPS)
        n = (x * r * g_ref[...]).astype(BF)
        acc = jnp.zeros((TM, D), F32)
        for c in range(F // fc):
            sl = pl.ds(c * fc, fc)
            a = _dot_nt(n, wg[sl, :])
            b = _dot_nt(n, wu[sl, :])
            a_ref[:, sl] = a.astype(BF)
            b_ref[:, sl] = b.astype(BF)
            s = (a * _sig(a) * b).astype(BF)
            acc = acc + _dot_nn(s, wd[sl, :])
        ho_ref[...] = x + 0.5 * acc

    tile = lambda w: pl.BlockSpec((TM, w), lambda i: (i, 0))
    return pl.pallas_call(
        body, name="ffn_fwd", grid=(nt,),
        in_specs=[tile(D), pl.BlockSpec((1, D), lambda i: (0, 0)), ANY],
        out_specs=[tile(D), tile(F), tile(F)],
        out_shape=[jax.ShapeDtypeStruct((T, D), F32), jax.ShapeDtypeStruct((T, F), BF),
                   jax.ShapeDtypeStruct((T, F), BF)],
        scratch_shapes=[pltpu.VMEM((F, D), BF), pltpu.VMEM((F, D), BF), pltpu.VMEM((F, D), BF),
                        pltpu.SemaphoreType.DMA((3 * NCHIP,))],
        compiler_params=_params(),
    )(h, g, wfull)


def _mix_in_fwd(h, g, wfull, li):
    nt = T // TM
    nin = 5 * D

    def body(h_ref, g_ref, w_hbm, z_ref, win, sem):
        _load_weights_once(w_hbm, li, (("win", win),), sem)
        x = h_ref[...]
        r = lax.rsqrt(jnp.mean(x * x, axis=-1, keepdims=True) + RMS_EPS)
        u = (x * r * g_ref[...]).astype(BF)
        for c in range(5):
            sl = pl.ds(c * D, D)
            z_ref[:, sl] = _dot_nt(u, win[sl, :]).astype(BF)

    return pl.pallas_call(
        body, name="mix_in_fwd", grid=(nt,),
        in_specs=[pl.BlockSpec((TM, D), lambda i: (i, 0)), pl.BlockSpec((1, D), lambda i: (0, 0)), ANY],
        out_specs=pl.BlockSpec((TM, nin), lambda i: (i, 0)),
        out_shape=jax.ShapeDtypeStruct((T, nin), BF),
        scratch_shapes=[pltpu.VMEM((nin, D), BF), pltpu.SemaphoreType.DMA((NCHIP,))],
        compiler_params=_params(),
    )(h, g, wfull)


def _pool_counts(i, rows, w):
    t = i * TM + lax.broadcasted_iota(jnp.int32, (rows, 1), 0)
    return jnp.minimum(t + 1, w).astype(F32)


def _mix_mid_fwd(h, z, pwcat, pscale, wdw, bdw, lg, lb, wfull, li):
    nt = T // TM
    pg = D // NG
    hb = TM // HALO

    def body(h_ref, z_ref, zh_ref, pw_ref, ps_ref, wdw_ref, bdw_ref, lg_ref, lb_ref, w_hbm,
             h2_ref, p_ref, c1_ref, cc_ref, wco, wo, pbuf, cbuf, ambuf, sem):
        i = pl.program_id(0)
        _load_weights_once(w_hbm, li, (("wco", wco), ("wo", wo)), sem)
        keep = (i > 0).astype(F32)
        zh = zh_ref[...].astype(F32) * keep
        zp = z_ref[:, 0:D].astype(F32)
        za = z_ref[:, D:2 * D].astype(F32)
        zg = z_ref[:, 2 * D:3 * D].astype(F32)
        pbuf[0:HALO, :] = zh[:, 0:D]
        pbuf[HALO:HALO + TM, :] = zp
        cbuf[0:HALO, :] = zh[:, D:2 * D] * _sig(zh[:, 2 * D:3 * D])
        cbuf[HALO:HALO + TM, :] = za * _sig(zg)
        for g, w in enumerate(WINDOWS):
            ls = pl.ds(g * pg, pg)
            cur = pbuf[pl.ds(HALO, TM), ls]
            acc = cur
            for j in range(1, w):
                acc = acc + pbuf[pl.ds(HALO - j, TM), ls]
            pooled = (acc / _pool_counts(i, TM, w) - cur).astype(BF)
            p_ref[:, ls] = pooled
            ambuf[:, ls] = _dot_nn(pooled, pw_ref[:, ls])
        am = ambuf[...] * ps_ref[...]
        acc = jnp.broadcast_to(bdw_ref[...], (TM, D))
        for k in range(KC):
            acc = acc + wdw_ref[k:k + 1, :] * cbuf[pl.ds(HALO - (KC - 1) + k, TM), :]
        c1b = acc.astype(BF)
        c1_ref[...] = c1b
        c1 = c1b.astype(F32)
        mu = jnp.mean(c1, axis=-1, keepdims=True)
        xc = c1 - mu
        var = jnp.mean(xc * xc, axis=-1, keepdims=True)
        c2 = xc * lax.rsqrt(var + LN_EPS) * lg_ref[...] + lb_ref[...]
        c3 = (c2 * _sig(c2)).astype(BF)
        ccb = _dot_nn(c3, wco[...]).astype(BF)
        cc_ref[...] = ccb
        gp = z_ref[:, 3 * D:4 * D].astype(F32)
        gc = z_ref[:, 4 * D:5 * D].astype(F32)
        m = (_sig(gp) * am + _sig(gc) * ccb.astype(F32)).astype(BF)
        h2_ref[...] = h_ref[...] + _dot_nn(m, wo[...])

    tile = pl.BlockSpec((TM, D), lambda i: (i, 0))
    vec = pl.BlockSpec((1, D), lambda i: (0, 0))
    return pl.pallas_call(
        body, name="mix_mid_fwd", grid=(nt,),
        in_specs=[tile, pl.BlockSpec((TM, 5 * D), lambda i: (i, 0)),
                  pl.BlockSpec((HALO, 3 * D), lambda i: (jnp.maximum(i * hb - 1, 0), 0)),
                  pl.BlockSpec((pg, D), lambda i: (0, 0)), vec,
                  pl.BlockSpec((HALO, D), lambda i: (0, 0)), vec, vec, vec, ANY],
        out_specs=[tile, tile, tile, tile],
        out_shape=[jax.ShapeDtypeStruct((T, D), F32), jax.ShapeDtypeStruct((T, D), BF),
                   jax.ShapeDtypeStruct((T, D), BF), jax.ShapeDtypeStruct((T, D), BF)],
        scratch_shapes=[pltpu.VMEM((D, D), BF), pltpu.VMEM((D, D), BF),
                        pltpu.VMEM((HALO + TM, D), F32), pltpu.VMEM((HALO + TM, D), F32),
                        pltpu.VMEM((TM, D), F32), pltpu.SemaphoreType.DMA((2 * NCHIP,))],
        compiler_params=_params(),
    )(h, z, z, pwcat, pscale, wdw, bdw, lg, lb, wfull)


def _ple_fwd(h, p, g, wppt, wfull, li):
    nt = T // TM

    def body(h_ref, p_ref, g_ref, wpp_ref, w_hbm, ho_ref, wpg, sem):
        _load_weights_once(w_hbm, li, (("wpg", wpg),), sem)
        x = h_ref[...]
        r = lax.rsqrt(jnp.mean(x * x, axis=-1, keepdims=True) + RMS_EPS)
        n = (x * r * g_ref[...]).astype(BF)
        gate = _sig(_dot_nn(n, wpg[...]))
        pe = _dot_nt(p_ref[...].astype(BF), wpp_ref[...])
        ho_ref[...] = x + gate * pe

    tile = pl.BlockSpec((TM, D), lambda i: (i, 0))
    return pl.pallas_call(
        body, name="ple_fwd", grid=(nt,),
        in_specs=[tile, pl.BlockSpec((TM, PD), lambda i: (i, 0)), pl.BlockSpec((1, D), lambda i: (0, 0)),
                  pl.BlockSpec((D, PD), lambda i: (0, 0)), ANY],
        out_specs=tile, out_shape=jax.ShapeDtypeStruct((T, D), F32),
        scratch_shapes=[pltpu.VMEM((D, D), BF), pltpu.SemaphoreType.DMA((NCHIP,))],
        compiler_params=_params(),
    )(h, p, g, wppt, wfull)


def _loss_bwd(h, tgt, g):
    nt = T // TM

    def body(h_ref, t_ref, g_ref, dh_ref, loss_ref, dg_ref):
        @pl.when(pl.program_id(0) == 0)
        def _():
            loss_ref[...] = jnp.zeros_like(loss_ref)
            dg_ref[...] = jnp.zeros_like(dg_ref)
        x = h_ref[...]
        r = lax.rsqrt(jnp.mean(x * x, axis=-1, keepdims=True) + RMS_EPS)
        xh = x * r
        gv = g_ref[...]
        e = xh * gv - t_ref[...]
        loss_ref[...] += jnp.sum(e * e, axis=0, keepdims=True) * (0.5 / D)
        dy = e * (1.0 / D)
        dg_ref[...] += jnp.sum(dy * xh, axis=0, keepdims=True)
        dxh = dy * gv
        dh_ref[...] = r * (dxh - xh * jnp.mean(dxh * xh, axis=-1, keepdims=True))

    tile = pl.BlockSpec((TM, D), lambda i: (i, 0))
    vec = pl.BlockSpec((1, D), lambda i: (0, 0))
    return pl.pallas_call(
        body, name="loss_bwd", grid=(nt,), in_specs=[tile, tile, vec], out_specs=[tile, vec, vec],
        out_shape=[jax.ShapeDtypeStruct((T, D), F32), jax.ShapeDtypeStruct((1, D), F32),
                   jax.ShapeDtypeStruct((1, D), F32)],
        compiler_params=_params(),
    )(h, tgt, g)


def _ple_bwd(h, dh, p, g, wppt, wfull, gbuf, li):
    nt = T // TM
    _, rtot = _layout()
    first = gbuf is None

    def body(*refs):
        if first:
            (h_ref, dh_ref, p_ref, g_ref, wpp_ref, w_hbm,
             dho_ref, dg_ref, dwpp_ref, g_out, wpg, acc, stage, sem, osem) = refs
        else:
            (h_ref, dh_ref, p_ref, g_ref, wpp_ref, w_hbm, _g_in,
             dho_ref, dg_ref, dwpp_ref, g_out, wpg, acc, stage, sem, osem) = refs
        i = pl.program_id(0)
        _load_weights_once(w_hbm, li, (("wpg", wpg),), sem)

        @pl.when(i == 0)
        def _():
            dg_ref[...] = jnp.zeros_like(dg_ref)
            dwpp_ref[...] = jnp.zeros_like(dwpp_ref)
            acc[...] = jnp.zeros_like(acc)

        x = h_ref[...]
        r = lax.rsqrt(jnp.mean(x * x, axis=-1, keepdims=True) + RMS_EPS)
        xh = x * r
        gv = g_ref[...]
        n = (xh * gv).astype(BF)
        gate = _sig(_dot_nn(n, wpg[...]))
        pb = p_ref[...].astype(BF)
        pe = _dot_nt(pb, wpp_ref[...])
        d = dh_ref[...]
        dpe = (d * gate).astype(BF)
        dq = (d * pe * gate * (1.0 - gate)).astype(BF)
        dwpp_ref[...] += _dot_tn(dpe, pb)
        acc[...] += _dot_tn(n, dq)
        dn = _dot_nt(dq, wpg[...])
        dg_ref[...] += jnp.sum(dn * xh, axis=0, keepdims=True)
        dxh = dn * gv
        dho_ref[...] = d + r * (dxh - xh * jnp.mean(dxh * xh, axis=-1, keepdims=True))

        @pl.when(i == nt - 1)
        def _():
            stage[...] = acc[...].astype(BF)
            cps = _grad_copies(stage, g_out, li, "wpg", 0, D, D // NCHIP, osem, 0)
            for cp in cps:
                cp.start()
            for cp in cps:
                cp.wait()

    tile = pl.BlockSpec((TM, D), lambda i: (i, 0))
    vec = pl.BlockSpec((1, D), lambda i: (0, 0))
    in_specs = [tile, tile, pl.BlockSpec((TM, PD), lambda i: (i, 0)), vec,
                pl.BlockSpec((D, PD), lambda i: (0, 0)), ANY]
    args = [h, dh, p, g, wppt, wfull]
    aliases = {}
    if not first:
        in_specs.append(ANY)
        args.append(gbuf)
        aliases = {6: 3}
    return pl.pallas_call(
        body, name="ple_bwd", grid=(nt,), in_specs=in_specs,
        out_specs=[tile, vec, pl.BlockSpec((D, PD), lambda i: (0, 0)), ANY],
        out_shape=[jax.ShapeDtypeStruct((T, D), F32), jax.ShapeDtypeStruct((1, D), F32),
                   jax.ShapeDtypeStruct((D, PD), F32),
                   jax.ShapeDtypeStruct((DEPTH, NCHIP, rtot, D), BF)],
        scratch_shapes=[pltpu.VMEM((D, D), BF), pltpu.VMEM((D, D), F32), pltpu.VMEM((D, D), BF),
                        pltpu.SemaphoreType.DMA((NCHIP,)), pltpu.SemaphoreType.DMA((NCHIP,))],
        input_output_aliases=aliases,
        compiler_params=_params(),
    )(*args)


def _ffn_bwd(h, dh, a, b, g, wfull, li, which):
    TM = TMB
    nt = T // TM
    fc = F // 2
    names = ("wg%d" % which, "wu%d" % which, "wd%d" % which)

    def body(h_ref, dh_ref, a_ref, b_ref, g_ref, w_hbm,
             dho_ref, da_ref, db_ref, s_ref, n_ref, dg_ref, wg, wu, wd, sem):
        _load_weights_once(w_hbm, li, ((names[0], wg), (names[1], wu), (names[2], wd)), sem)

        @pl.when(pl.program_id(0) == 0)
        def _():
            dg_ref[...] = jnp.zeros_like(dg_ref)

        x = h_ref[...]
        r = lax.rsqrt(jnp.mean(x * x, axis=-1, keepdims=True) + RMS_EPS)
        xh = x * r
        gv = g_ref[...]
        n_ref[...] = (xh * gv).astype(BF)
        d = dh_ref[...]
        df = (0.5 * d).astype(BF)
        dn = jnp.zeros((TM, D), F32)
        for c in range(F // fc):
            sl = pl.ds(c * fc, fc)
            av = a_ref[:, sl].astype(F32)
            bv = b_ref[:, sl].astype(F32)
            ds = _dot_nt(df, wd[sl, :])
            sg = _sig(av)
            sil = av * sg
            s_ref[:, sl] = (sil * bv).astype(BF)
            da = (ds * bv * (sg * (1.0 + av * (1.0 - sg)))).astype(BF)
            db = (ds * sil).astype(BF)
            da_ref[:, sl] = da
            db_ref[:, sl] = db
            dn = dn + _dot_nn(da, wg[sl, :]) + _dot_nn(db, wu[sl, :])
        dg_ref[...] += jnp.sum(dn * xh, axis=0, keepdims=True)
        dxh = dn * gv
        dho_ref[...] = d + r * (dxh - xh * jnp.mean(dxh * xh, axis=-1, keepdims=True))

    tile = lambda w: pl.BlockSpec((TM, w), lambda i: (i, 0))
    vec = pl.BlockSpec((1, D), lambda i: (0, 0))
    return pl.pallas_call(
        body, name="ffn_bwd", grid=(nt,),
        in_specs=[tile(D), tile(D), tile(F), tile(F), vec, ANY],
        out_specs=[tile(D), tile(F), tile(F), tile(F), tile(D), vec],
        out_shape=[jax.ShapeDtypeStruct((T, D), F32), jax.ShapeDtypeStruct((T, F), BF),
                   jax.ShapeDtypeStruct((T, F), BF), jax.ShapeDtypeStruct((T, F), BF),
                   jax.ShapeDtypeStruct((T, D), BF), jax.ShapeDtypeStruct((1, D), F32)],
        scratch_shapes=[pltpu.VMEM((F, D), BF), pltpu.VMEM((F, D), BF), pltpu.VMEM((F, D), BF),
                        pltpu.SemaphoreType.DMA((3 * NCHIP,))],
        compiler_params=_params(),
    )(h, dh, a, b, g, wfull)


def _wgrad(xs, y, gbuf, li, names, row0, rb, piece, yscale=None):
    nx = len(xs)
    rx = xs[0].shape[1]
    nj = rx // rb
    nt = T // TMW
    npiece = rb // piece

    def body(*refs):
        x_refs = refs[:nx]
        y_ref = refs[nx]
        g_out = refs[nx + 2]
        accs = refs[nx + 3:2 * nx + 3]
        stages = refs[2 * nx + 3:3 * nx + 3]
        osem = refs[3 * nx + 3]
        j = pl.program_id(0)
        t = pl.program_id(1)

        @pl.when(t == 0)
        def _():
            for acc in accs:
                acc[...] = jnp.zeros_like(acc)

        yv = y_ref[...]
        if yscale is not None:
            yv = (yscale * yv).astype(BF)
        for x_ref, acc in zip(x_refs, accs):
            acc[...] += _dot_tn(x_ref[...], yv)

        @pl.when(t == nt - 1)
        def _():
            cps = []
            for xi in range(nx):
                stages[xi][...] = accs[xi][...].astype(BF)
                cps += _grad_copies(stages[xi], g_out, li, names[xi], row0 + j * rb, rb, piece,
                                    osem, xi * npiece)
            for cp in cps:
                cp.start()
            for cp in cps:
                cp.wait()

    in_specs = [pl.BlockSpec((TMW, rb), lambda j, t: (t, j)) for _ in xs]
    in_specs += [pl.BlockSpec((TMW, D), lambda j, t: (t, 0)), ANY]
    return pl.pallas_call(
        body, name="wgrad", grid=(nj, nt), in_specs=in_specs, out_specs=ANY,
        out_shape=jax.ShapeDtypeStruct(gbuf.shape, gbuf.dtype),
        scratch_shapes=([pltpu.VMEM((rb, D), F32) for _ in xs] + [pltpu.VMEM((rb, D), BF) for _ in xs]
                        + [pltpu.SemaphoreType.DMA((nx * npiece,))]),
        input_output_aliases={nx + 1: 0},
        compiler_params=_params(("arbitrary", "arbitrary")),
    )(*xs, y, gbuf)


def _mix_b1(dh, z, pooled, c1, cc, pwcat, pscale, lg, lb, wfull, gbuf, li):
    nt = T // TM
    pg = D // NG

    def body(dh_ref, gp_ref, gc_ref, p_ref, c1_ref, cc_ref, pw_ref, ps_ref, lg_ref, lb_ref, w_hbm, _g_in,
             dp_ref, dc1_ref, dzb_ref, small_ref, dpw_ref, g_out,
             wco, wo, acc_o, acc_co, stage_o, stage_co, qbuf, dpbuf, sem, osem):
        i = pl.program_id(0)
        _load_weights_once(w_hbm, li, (("wco", wco), ("wo", wo)), sem)

        @pl.when(i == 0)
        def _():
            small_ref[...] = jnp.zeros_like(small_ref)
            dpw_ref[...] = jnp.zeros_like(dpw_ref)
            acc_o[...] = jnp.zeros_like(acc_o)
            acc_co[...] = jnp.zeros_like(acc_co)

        dhb = dh_ref[...].astype(BF)
        dm = _dot_nt(dhb, wo[...])
        sp = _sig(gp_ref[...].astype(F32))
        sc = _sig(gc_ref[...].astype(F32))
        for g in range(NG):
            ls = pl.ds(g * pg, pg)
            qbuf[:, ls] = _dot_nn(p_ref[:, ls], pw_ref[:, ls])
        q = qbuf[...]
        psv = ps_ref[...]
        am = q * psv
        ccv = cc_ref[...].astype(F32)
        m = (sp * am + sc * ccv).astype(BF)
        acc_o[...] += _dot_tn(m, dhb)
        dam = dm * sp
        dzb_ref[:, 0:D] = (dm * am * sp * (1.0 - sp)).astype(BF)
        dccb = (dm * sc).astype(BF)
        dzb_ref[:, D:2 * D] = (dm * ccv * sc * (1.0 - sc)).astype(BF)
        small_ref[0:1, :] += jnp.sum(dam * q, axis=0, keepdims=True)
        dq = (dam * psv).astype(BF)
        for g in range(NG):
            ls = pl.ds(g * pg, pg)
            dqg = dq[:, g * pg:(g + 1) * pg]
            dp_ref[:, ls] = _dot_nt(dqg, pw_ref[:, ls]).astype(BF)
            dpw_ref[:, ls] += _dot_tn(p_ref[:, ls], dqg)
        c1v = c1_ref[...].astype(F32)
        mu = jnp.mean(c1v, axis=-1, keepdims=True)
        xc = c1v - mu
        var = jnp.mean(xc * xc, axis=-1, keepdims=True)
        rs = lax.rsqrt(var + LN_EPS)
        c2n = xc * rs
        lgv = lg_ref[...]
        c2 = c2n * lgv + lb_ref[...]
        sg2 = _sig(c2)
        c3 = (c2 * sg2).astype(BF)
        acc_co[...] += _dot_tn(c3, dccb)
        dc3 = _dot_nt(dccb, wco[...])
        dc2 = dc3 * (sg2 * (1.0 + c2 * (1.0 - sg2)))
        small_ref[2:3, :] += jnp.sum(dc2 * c2n, axis=0, keepdims=True)
        small_ref[3:4, :] += jnp.sum(dc2, axis=0, keepdims=True)
        dc2n = dc2 * lgv
        dc1 = rs * (dc2n - jnp.mean(dc2n, axis=-1, keepdims=True)
                    - c2n * jnp.mean(dc2n * c2n, axis=-1, keepdims=True))
        small_ref[1:2, :] += jnp.sum(dc1, axis=0, keepdims=True)
        dc1_ref[...] = dc1.astype(BF)

        @pl.when(i == nt - 1)
        def _():
            stage_o[...] = acc_o[...].astype(BF)
            stage_co[...] = acc_co[...].astype(BF)
            cps = _grad_copies(stage_o, g_out, li, "wo", 0, D, D // NCHIP, osem, 0)
            cps += _grad_copies(stage_co, g_out, li, "wco", 0, D, D // NCHIP, osem, NCHIP)
            for cp in cps:
                cp.start()
            for cp in cps:
                cp.wait()

    tile = pl.BlockSpec((TM, D), lambda i: (i, 0))
    vec = pl.BlockSpec((1, D), lambda i: (0, 0))
    full = lambda r: pl.BlockSpec((r, D), lambda i: (0, 0))
    return pl.pallas_call(
        body, name="mix_b1", grid=(nt,),
        in_specs=[tile, pl.BlockSpec((TM, D), lambda i: (i, 3)), pl.BlockSpec((TM, D), lambda i: (i, 4)),
                  tile, tile, tile, full(pg), vec, vec, vec, ANY, ANY],
        out_specs=[tile, tile, pl.BlockSpec((TM, 2 * D), lambda i: (i, 0)), full(8), full(pg), ANY],
        out_shape=[jax.ShapeDtypeStruct((T, D), BF), jax.ShapeDtypeStruct((T, D), BF),
                   jax.ShapeDtypeStruct((T, 2 * D), BF), jax.ShapeDtypeStruct((8, D), F32),
                   jax.ShapeDtypeStruct((pg, D), F32), jax.ShapeDtypeStruct(gbuf.shape, gbuf.dtype)],
        scratch_shapes=[pltpu.VMEM((D, D), BF), pltpu.VMEM((D, D), BF),
                        pltpu.VMEM((D, D), F32), pltpu.VMEM((D, D), F32),
                        pltpu.VMEM((D, D), BF), pltpu.VMEM((D, D), BF),
                        pltpu.VMEM((TM, D), F32), pltpu.VMEM((TM, D), F32),
                        pltpu.SemaphoreType.DMA((2 * NCHIP,)), pltpu.SemaphoreType.DMA((2 * NCHIP,))],
        input_output_aliases={11: 5},
        compiler_params=_params(),
    )(dh, z, z, pooled, c1, cc, pwcat, pscale, lg, lb, wfull, gbuf)


def _mix_b2(dp, dc1, z, wdw):
    nt = T // TM
    pg = D // NG
    hb = TM // HALO
    nhb = T // HALO

    def body(dp_ref, dpn_ref, dc_ref, dcn_ref, za_ref, zg_ref, zah_ref, zgh_ref, wdw_ref,
             dza_ref, dw_ref, pbuf, cbuf, c0buf):
        i = pl.program_id(0)

        @pl.when(i == 0)
        def _():
            dw_ref[...] = jnp.zeros_like(dw_ref)

        more = (i < nt - 1).astype(F32)
        prev = (i > 0).astype(F32)
        for g, w in enumerate(WINDOWS):
            ls = pl.ds(g * pg, pg)
            cur = dp_ref[:, ls].astype(F32)
            pbuf[0:TM, ls] = cur / _pool_counts(i, TM, w)
            pbuf[TM:TM + HALO, ls] = dpn_ref[:, ls].astype(F32) * (more / w)
            acc = pbuf[pl.ds(0, TM), ls]
            for j in range(1, w):
                acc = acc + pbuf[pl.ds(j, TM), ls]
            dza_ref[:, ls] = (acc - cur).astype(BF)
        dc1 = dc_ref[...].astype(F32)
        cbuf[0:TM, :] = dc1
        cbuf[TM:TM + HALO, :] = dcn_ref[...].astype(F32) * more
        acc = jnp.zeros((TM, D), F32)
        for k in range(KC):
            acc = acc + wdw_ref[k:k + 1, :] * cbuf[pl.ds(KC - 1 - k, TM), :]
        za = za_ref[...].astype(F32)
        sg = _sig(zg_ref[...].astype(F32))
        dza_ref[:, D:2 * D] = (acc * sg).astype(BF)
        dza_ref[:, 2 * D:3 * D] = (acc * za * sg * (1.0 - sg)).astype(BF)
        c0buf[0:HALO, :] = zah_ref[...].astype(F32) * _sig(zgh_ref[...].astype(F32)) * prev
        c0buf[HALO:HALO + TM, :] = za * sg
        for k in range(KC):
            dw_ref[k:k + 1, :] += jnp.sum(dc1 * c0buf[pl.ds(HALO - (KC - 1) + k, TM), :],
                                           axis=0, keepdims=True)

    tile = pl.BlockSpec((TM, D), lambda i: (i, 0))
    nxt = pl.BlockSpec((HALO, D), lambda i: (jnp.minimum((i + 1) * hb, nhb - 1), 0))
    prv = lambda c: pl.BlockSpec((HALO, D), lambda i: (jnp.maximum(i * hb - 1, 0), c))
    return pl.pallas_call(
        body, name="mix_b2", grid=(nt,),
        in_specs=[tile, nxt, tile, nxt, pl.BlockSpec((TM, D), lambda i: (i, 1)),
                  pl.BlockSpec((TM, D), lambda i: (i, 2)), prv(1), prv(2),
                  pl.BlockSpec((HALO, D), lambda i: (0, 0))],
        out_specs=[pl.BlockSpec((TM, 3 * D), lambda i: (i, 0)), pl.BlockSpec((HALO, D), lambda i: (0, 0))],
        out_shape=[jax.ShapeDtypeStruct((T, 3 * D), BF), jax.ShapeDtypeStruct((HALO, D), F32)],
        scratch_shapes=[pltpu.VMEM((TM + HALO, D), F32), pltpu.VMEM((TM + HALO, D), F32),
                        pltpu.VMEM((HALO + TM, D), F32)],
        compiler_params=_params(),
    )(dp, dp, dc1, dc1, z, z, z, z, wdw)


def _mix_b3(h, dh, dza, dzb, g, wfull, li):
    nt = T // TM

    def body(h_ref, dh_ref, dza_ref, dzb_ref, g_ref, w_hbm, dho_ref, u_ref, dg_ref, win, sem):
        _load_weights_once(w_hbm, li, (("win", win),), sem)

        @pl.when(pl.program_id(0) == 0)
        def _():
            dg_ref[...] = jnp.zeros_like(dg_ref)

        x = h_ref[...]
        r = lax.rsqrt(jnp.mean(x * x, axis=-1, keepdims=True) + RMS_EPS)
        xh = x * r
        gv = g_ref[...]
        u_ref[...] = (xh * gv).astype(BF)
        du = _dot_nn(dza_ref[...], win[0:3 * D, :]) + _dot_nn(dzb_ref[...], win[3 * D:5 * D, :])
        dg_ref[...] += jnp.sum(du * xh, axis=0, keepdims=True)
        dxh = du * gv
        dho_ref[...] = dh_ref[...] + r * (dxh - xh * jnp.mean(dxh * xh, axis=-1, keepdims=True))

    tile = lambda w: pl.BlockSpec((TM, w), lambda i: (i, 0))
    vec = pl.BlockSpec((1, D), lambda i: (0, 0))
    return pl.pallas_call(
        body, name="mix_b3", grid=(nt,),
        in_specs=[tile(D), tile(D), tile(3 * D), tile(2 * D), vec, ANY],
        out_specs=[tile(D), tile(D), vec],
        out_shape=[jax.ShapeDtypeStruct((T, D), F32), jax.ShapeDtypeStruct((T, D), BF),
                   jax.ShapeDtypeStruct((1, D), F32)],
        scratch_shapes=[pltpu.VMEM((5 * D, D), BF), pltpu.SemaphoreType.DMA((NCHIP,))],
        compiler_params=_params(),
    )(h, dh, dza, dzb, g, wfull)


def _mesh_pos():
    x, y, c = lax.axis_index("x"), lax.axis_index("y"), lax.axis_index("c")
    chips = [(1 - x, y), (x, 1 - y), (1 - x, 1 - y)]
    return x, y, c, 2 * x + y, chips


def _gather_weights(packed):
    _, rtot = _layout()

    def body(p_ref, w_ref, send_sems, recv_sems, lsem):
        x, y, c, kme, chips = _mesh_pos()
        sib = (x, y, 1 - c)
        ks = [2 * cx + cy for cx, cy in chips]
        local = [pltpu.make_async_copy(p_ref.at[l], w_ref.at[l, kme], lsem.at[l]) for l in range(DEPTH)]
        for cp in local:
            cp.start()

        def copy(sem, src, dst, to):
            return pltpu.make_async_remote_copy(src_ref=src, dst_ref=dst, send_sem=send_sems.at[sem],
                                                recv_sem=recv_sems.at[sem], device_id=to, device_id_type=MESH)

        first = [copy(j, p_ref.at[c], w_ref.at[c, kme], (*chips[j], c)) for j in range(3)]
        for cp in first:
            cp.start()
        passed = [copy(3 + j, w_ref.at[c, ks[j]], w_ref.at[c, ks[j]], sib) for j in range(3)]
        for j in range(3):
            copy(j, p_ref.at[c], w_ref.at[c, ks[j]], sib).wait_recv()
            passed[j].start()
        for j in range(3):
            copy(3 + j, p_ref.at[c], w_ref.at[1 - c, ks[j]], sib).wait_recv()
        for cp in first + passed:
            cp.wait_send()
        for cp in local:
            cp.wait()

    return pl.pallas_call(
        body, name="gather_weights", in_specs=[ANY], out_specs=ANY,
        out_shape=jax.ShapeDtypeStruct((DEPTH, NCHIP, rtot, D), BF),
        scratch_shapes=[pltpu.SemaphoreType.DMA((6,)), pltpu.SemaphoreType.DMA((6,)),
                        pltpu.SemaphoreType.DMA((DEPTH,))],
    )(packed)


def _sibling_swap_layer(gbuf):
    _, rtot = _layout()

    def body(g_ref, r_ref, send_sem, recv_sem):
        x, y, c, _, _ = _mesh_pos()
        cp = pltpu.make_async_remote_copy(src_ref=g_ref.at[1 - c], dst_ref=r_ref, send_sem=send_sem,
                                          recv_sem=recv_sem, device_id=(x, y, 1 - c), device_id_type=MESH)
        cp.start()
        cp.wait()

    return pl.pallas_call(
        body, name="sibling_swap_layer", in_specs=[ANY], out_specs=ANY,
        out_shape=jax.ShapeDtypeStruct((NCHIP, rtot, D), BF),
        scratch_shapes=[pltpu.SemaphoreType.DMA, pltpu.SemaphoreType.DMA],
    )(gbuf)


def _chip_sum(gbuf, rbuf, cidx):
    _, rtot = _layout()
    rt = rtot // 10 if rtot % 80 == 0 else rtot

    def body(c_ref, g_ref, r_ref, o_ref):
        o_ref[...] = (g_ref[...].astype(F32) + r_ref[...].astype(F32)).astype(BF)

    return pl.pallas_call(
        body, name="chip_sum",
        grid_spec=pltpu.PrefetchScalarGridSpec(
            num_scalar_prefetch=1, grid=(NCHIP, rtot // rt),
            in_specs=[pl.BlockSpec((None, None, rt, D), lambda k, r, c: (c[0], k, r, 0)),
                      pl.BlockSpec((None, rt, D), lambda k, r, c: (k, r, 0))],
            out_specs=pl.BlockSpec((None, rt, D), lambda k, r, c: (k, r, 0))),
        out_shape=jax.ShapeDtypeStruct((NCHIP, rtot, D), BF),
        compiler_params=_params(("arbitrary", "arbitrary")),
    )(cidx, gbuf, rbuf)


def _chip_exchange(sbuf):
    _, rtot = _layout()

    def body(s_ref, x_ref, send_sems, recv_sems):
        x, y, c, _, chips = _mesh_pos()
        cps = []
        for j, (cx, cy) in enumerate(chips):
            cps.append(pltpu.make_async_remote_copy(
                src_ref=s_ref.at[2 * cx + cy], dst_ref=x_ref.at[j], send_sem=send_sems.at[j],
                recv_sem=recv_sems.at[j], device_id=(cx, cy, c), device_id_type=MESH))
        for cp in cps:
            cp.start()
        for cp in cps:
            cp.wait()

    return pl.pallas_call(
        body, name="chip_exchange", in_specs=[ANY], out_specs=ANY,
        out_shape=jax.ShapeDtypeStruct((3, rtot, D), BF),
        scratch_shapes=[pltpu.SemaphoreType.DMA((3,)), pltpu.SemaphoreType.DMA((3,))],
    )(sbuf)


def _shard_sum(gbuf, rbuf, xbuf, ck):
    _, rtot = _layout()
    rt = rtot // 10 if rtot % 80 == 0 else rtot

    def body(ck_ref, g_ref, r_ref, x_ref, o_ref):
        acc = g_ref[...].astype(F32) + r_ref[...].astype(F32)
        for j in range(3):
            acc = acc + x_ref[j].astype(F32)
        o_ref[...] = acc

    return pl.pallas_call(
        body, name="shard_sum",
        grid_spec=pltpu.PrefetchScalarGridSpec(
            num_scalar_prefetch=1, grid=(rtot // rt,),
            in_specs=[pl.BlockSpec((None, None, rt, D), lambda r, ck: (ck[0], ck[1], r, 0)),
                      pl.BlockSpec((None, rt, D), lambda r, ck: (ck[1], r, 0)),
                      pl.BlockSpec((3, rt, D), lambda r, ck: (0, r, 0))],
            out_specs=pl.BlockSpec((rt, D), lambda r, ck: (r, 0))),
        out_shape=jax.ShapeDtypeStruct((rtot, D), F32),
        compiler_params=_params(),
    )(ck, gbuf, rbuf, xbuf)


def _sibling_share(red):
    rtot = red.shape[0]

    def body(r_ref, o_ref, send_sem, recv_sem, lsem):
        x, y, c, _, _ = _mesh_pos()
        local = pltpu.make_async_copy(r_ref, o_ref.at[c], lsem)
        local.start()
        cp = pltpu.make_async_remote_copy(src_ref=r_ref, dst_ref=o_ref.at[c], send_sem=send_sem,
                                          recv_sem=recv_sem, device_id=(x, y, 1 - c), device_id_type=MESH)
        cp.start()
        cp.wait_send()
        pltpu.make_async_remote_copy(src_ref=r_ref, dst_ref=o_ref.at[1 - c], send_sem=send_sem,
                                     recv_sem=recv_sem, device_id=(x, y, 1 - c), device_id_type=MESH).wait_recv()
        local.wait()

    return pl.pallas_call(
        body, name="sibling_share", in_specs=[ANY], out_specs=ANY,
        out_shape=jax.ShapeDtypeStruct((DEPTH, rtot, D), F32),
        scratch_shapes=[pltpu.SemaphoreType.DMA, pltpu.SemaphoreType.DMA, pltpu.SemaphoreType.DMA],
    )(red)


def _allreduce_small(v):
    rows = v.shape[0]
    ndev = 2 * NCHIP

    def body(v_ref, o_ref, gat, send_sems, recv_sems, lsem):
        x, y, c, _, chips = _mesh_pos()
        me, sib = (x, y, c), (x, y, 1 - c)

        def blk(px, py, pc):
            return gat.at[pl.ds((4 * px + 2 * py + pc) * rows, rows), :]

        def copy(k, block, to, src=None):
            return pltpu.make_async_remote_copy(
                src_ref=blk(*block) if src is None else src, dst_ref=blk(*block),
                send_sem=send_sems.at[k], recv_sem=recv_sems.at[k], device_id=to, device_id_type=MESH)

        mine = pltpu.make_async_copy(v_ref, blk(*me), lsem)
        mine.start()
        first = [copy(0, me, sib, src=v_ref)]
        first += [copy(1 + j, me, (*chip, c), src=v_ref) for j, chip in enumerate(chips)]
        for cp in first:
            cp.start()
        passed = [copy(4 + j, (*chip, c), sib) for j, chip in enumerate(chips)]
        for j, chip in enumerate(chips):
            copy(1 + j, (*chip, c), me).wait_recv()
            passed[j].start()
        copy(0, sib, me).wait_recv()
        for j, chip in enumerate(chips):
            copy(4 + j, (*chip, 1 - c), me).wait_recv()
        for cp in first + passed:
            cp.wait_send()
        mine.wait()
        acc = gat[0:rows, :]
        for d in range(1, ndev):
            acc = acc + gat[d * rows:(d + 1) * rows, :]
        o_ref[...] = acc

    vm = pl.BlockSpec(memory_space=pltpu.VMEM)
    return pl.pallas_call(
        body, name="allreduce_small", in_specs=[vm], out_specs=vm,
        out_shape=jax.ShapeDtypeStruct((rows, D), F32),
        scratch_shapes=[pltpu.VMEM((ndev * rows, D), F32), pltpu.SemaphoreType.DMA((7,)),
                        pltpu.SemaphoreType.DMA((7,)), pltpu.SemaphoreType.DMA],
    )(v)


def _adamw(w, g, m, v):
    shape = w.shape
    cols = shape[-1]
    rows = w.size // cols
    bm = rows
    for cand in (512, 256, 128, 64, 32, 16, 8):
        if rows % cand == 0:
            bm = cand
            break
    bc1 = 1.0 - ADAM_B1 ** ADAM_STEP
    bc2 = 1.0 - ADAM_B2 ** ADAM_STEP

    def body(w_ref, g_ref, m_ref, v_ref, d_ref, mo_ref, vo_ref):
        gv = g_ref[...]
        mn = ADAM_B1 * m_ref[...] + (1.0 - ADAM_B1) * gv
        vn = ADAM_B2 * v_ref[...] + (1.0 - ADAM_B2) * (gv * gv)
        mo_ref[...] = mn
        vo_ref[...] = vn
        d_ref[...] = -ADAM_LR * ((mn / bc1) / (jnp.sqrt(vn / bc2) + ADAM_EPS) + ADAM_WD * w_ref[...])

    spec = pl.BlockSpec((bm, cols), lambda i: (i, 0))
    out = jax.ShapeDtypeStruct((rows, cols), F32)
    d, mo, vo = pl.pallas_call(
        body, name="adamw", grid=(rows // bm,), in_specs=[spec] * 4, out_specs=[spec] * 3,
        out_shape=[out, out, out], compiler_params=_params(),
    )(*[t.reshape(rows, cols) for t in (w, g, m, v)])
    return d.reshape(shape), mo.reshape(shape), vo.reshape(shape)


def _pack_shards(ws, li):
    pg = D // NG
    t = lambda a: jnp.swapaxes(a[li], 0, 1)
    parts = [t(ws["ffn1_w_gate"]), t(ws["ffn1_w_up"]), ws["ffn1_w_down"][li],
             t(ws["ffn2_w_gate"]), t(ws["ffn2_w_up"]), ws["ffn2_w_down"][li],
             t(ws["w_in"]), ws["conv_w_out"][li], ws["w_out"][li], ws["ple_w_gate"][li],
             t(ws["ple_w_proj"]).reshape(-1, D),
             jnp.swapaxes(ws["pool_w"][li], 0, 1).reshape(pg // NCHIP, D)]
    return jnp.concatenate([p.astype(BF) for p in parts], axis=0)


def _unpack_shards(red, li):
    lay, _ = _layout()
    pg = D // NG
    rows = lambda n: red[li, lay[n][0]:lay[n][0] + lay[n][1], :]
    t = lambda a: jnp.swapaxes(a, 0, 1)
    return {
        "ffn1_w_gate": t(rows("wg1")), "ffn1_w_up": t(rows("wu1")), "ffn1_w_down": rows("wd1"),
        "ffn2_w_gate": t(rows("wg2")), "ffn2_w_up": t(rows("wu2")), "ffn2_w_down": rows("wd2"),
        "w_in": t(rows("win")), "conv_w_out": rows("wco"), "w_out": rows("wo"), "ple_w_gate": rows("wpg"),
        "ple_w_proj": t(rows("wpp").reshape(D // NCHIP, PD)),
        "pool_w": jnp.swapaxes(rows("pw").reshape(pg // NCHIP, NG, pg), 0, 1),
    }


_BIG = ("ffn1_w_gate", "ffn1_w_up", "ffn1_w_down", "w_in", "pool_w", "conv_w_out", "w_out",
        "ffn2_w_gate", "ffn2_w_up", "ffn2_w_down", "ple_w_gate", "ple_w_proj")
_VECS = ("ffn1_norm", "mix_norm", "pool_scale", "conv_dw_b", "conv_ln_g", "conv_ln_b", "ffn2_norm", "ple_norm")
_WEIGHTS = ("ffn1_norm", "ffn1_w_gate", "ffn1_w_up", "ffn1_w_down", "mix_norm", "w_in", "pool_w", "pool_scale",
            "conv_dw_w", "conv_dw_b", "conv_ln_g", "conv_ln_b", "conv_w_out", "w_out", "ffn2_norm",
            "ffn2_w_gate", "ffn2_w_up", "ffn2_w_down", "ple_norm", "ple_w_gate", "ple_w_proj", "final_norm")


def _step(x, p, tgt, ws, ms, vs):
    lay, rtot = _layout()
    pg = D // NG
    cpos = lax.axis_index("c")
    kme = 2 * lax.axis_index("x") + lax.axis_index("y")
    h = x.reshape(T, D)
    tgt = tgt.reshape(T, D)

    packed = jnp.stack([_pack_shards(ws, li) for li in range(DEPTH)])
    wfull = _gather_weights(packed)
    o, s = lay["wpp"]
    wppt = [wfull[li, :, o:o + s, :].reshape(D, PD) for li in range(DEPTH)]
    o, s = lay["pw"]
    pwcat = [wfull[li, :, o:o + s, :].reshape(pg, D) for li in range(DEPTH)]
    kk = ws["conv_dw_w"].shape[1]
    wdw_mine = jnp.zeros((DEPTH * HALO, D), F32)
    for li in range(DEPTH):
        blockw = jnp.zeros((kk, D), F32)
        mine = jnp.where(cpos == 0, ws["conv_dw_w"][li], 0.0)
        blockw = lax.dynamic_update_slice(blockw, mine, (0, kme * (D // NCHIP)))
        wdw_mine = wdw_mine.at[li * HALO:li * HALO + kk, :].set(blockw)
    wdw_all = _allreduce_small(wdw_mine)
    wdw = [wdw_all[li * HALO:(li + 1) * HALO, :] for li in range(DEPTH)]
    vec = lambda name, li: _row(ws[name][li])

    saved = []
    for li in range(DEPTH):
        h0 = h
        h1, a1, b1 = _ffn_fwd(h0, vec("ffn1_norm", li), wfull, li, 1)
        z = _mix_in_fwd(h1, vec("mix_norm", li), wfull, li)
        h2, pooled, c1, cc = _mix_mid_fwd(h1, z, pwcat[li], vec("pool_scale", li), wdw[li],
                                          vec("conv_dw_b", li), vec("conv_ln_g", li), vec("conv_ln_b", li),
                                          wfull, li)
        h3, a2, b2 = _ffn_fwd(h2, vec("ffn2_norm", li), wfull, li, 2)
        h = _ple_fwd(h3, p[li, 0], vec("ple_norm", li), wppt[li], wfull, li)
        saved.append((h0, a1, b1, h1, z, pooled, c1, cc, h2, a2, b2, h3))

    dh, losscols, dgf = _loss_bwd(h, tgt, _row(ws["final_norm"]))
    loss = lax.psum(jnp.sum(losscols), ("x", "y", "c"))
    gbuf = None
    vecg = [dict() for _ in range(DEPTH)]
    dwdw = [None] * DEPTH
    for li in reversed(range(DEPTH)):
        h0, a1, b1, h1, z, pooled, c1, cc, h2, a2, b2, h3 = saved[li]
        dh, dgp, dwpp, gbuf = _ple_bwd(h3, dh, p[li, 0], vec("ple_norm", li), wppt[li], wfull, gbuf, li)
        vecg[li]["ple_norm"] = dgp
        dh_in, da, db, sact, n, dg = _ffn_bwd(h2, dh, a2, b2, vec("ffn2_norm", li), wfull, li, 2)
        vecg[li]["ffn2_norm"] = dg
        gbuf = _wgrad([da, db], n, gbuf, li, ("wg2", "wu2"), 0, F // 2, F // NCHIP)
        gbuf = _wgrad([sact], dh, gbuf, li, ("wd2",), 0, F // 2, F // NCHIP, yscale=0.5)
        dh = dh_in
        dp, dc1, dzb, small, dpw, gbuf = _mix_b1(dh, z, pooled, c1, cc, pwcat[li], vec("pool_scale", li),
                                                 vec("conv_ln_g", li), vec("conv_ln_b", li), wfull, gbuf, li)
        vecg[li]["pool_scale"] = small[0:1]
        vecg[li]["conv_dw_b"] = small[1:2]
        vecg[li]["conv_ln_g"] = small[2:3]
        vecg[li]["conv_ln_b"] = small[3:4]
        dza, dwdw[li] = _mix_b2(dp, dc1, z, wdw[li])
        dh_in, u, dg = _mix_b3(h1, dh, dza, dzb, vec("mix_norm", li), wfull, li)
        vecg[li]["mix_norm"] = dg
        gbuf = _wgrad([dza], u, gbuf, li, ("win",), 0, D, D // NCHIP)
        gbuf = _wgrad([dzb], u, gbuf, li, ("win",), 3 * D, D, D // NCHIP)
        dh = dh_in
        dh_in, da, db, sact, n, dg = _ffn_bwd(h0, dh, a1, b1, vec("ffn1_norm", li), wfull, li, 1)
        vecg[li]["ffn1_norm"] = dg
        gbuf = _wgrad([da, db], n, gbuf, li, ("wg1", "wu1"), 0, F // 2, F // NCHIP)
        gbuf = _wgrad([sact], dh, gbuf, li, ("wd1",), 0, F // 2, F // NCHIP, yscale=0.5)
        dh = dh_in
        o, s = lay["wpp"]
        small_rows = jnp.concatenate([dwpp.reshape(NCHIP, s, D), dpw.reshape(NCHIP, lay["pw"][1], D)], axis=1)
        gbuf = lax.dynamic_update_slice(gbuf, small_rows.astype(BF)[None], (li, 0, o, 0))
    grad_x = dh.reshape(x.shape)

    rbuf = _sibling_swap_layer(gbuf)
    sbuf = _chip_sum(gbuf, rbuf, jnp.stack([cpos]).astype(jnp.int32))
    xbuf = _chip_exchange(sbuf)
    red_c = _shard_sum(gbuf, rbuf, xbuf, jnp.stack([cpos, kme]).astype(jnp.int32))
    red = _sibling_share(red_c)
    rows = [vecg[li][n] for li in range(DEPTH) for n in _VECS] + [dgf]
    rows.append(jnp.zeros((8 - (len(rows) % 8), D), F32))
    vsum = _allreduce_small(jnp.concatenate(rows + dwdw, axis=0))
    nvec = len(_VECS)
    grads = {}
    for n in _BIG:
        grads[n] = []
    for li in range(DEPTH):
        un = _unpack_shards(red, li)
        for n in _BIG:
            grads[n].append(un[n])
    for n in _BIG:
        grads[n] = jnp.stack(grads[n])
    for i, n in enumerate(_VECS):
        grads[n] = jnp.stack([vsum[li * nvec + i] for li in range(DEPTH)])
    grads["final_norm"] = vsum[DEPTH * nvec]
    base = DEPTH * nvec + 8 - ((DEPTH * nvec + 1) % 8) + 1
    dcols = D // NCHIP
    grads["conv_dw_w"] = jnp.stack([
        lax.dynamic_slice(vsum[base + li * HALO: base + li * HALO + kk, :], (0, kme * dcols), (kk, dcols))
        for li in range(DEPTH)])

    outs_g, outs_d, outs_m, outs_v = [], [], [], []
    for n in _WEIGHTS:
        d, mo, vo = _adamw(ws[n], grads[n], ms[n], vs[n])
        outs_g.append(grads[n])
        outs_d.append(d)
        outs_m.append(mo)
        outs_v.append(vo)
    return (loss, grad_x, *outs_g, *outs_d, *outs_m, *outs_v)


def kernel(x, p, ffn1_norm, ffn1_w_gate, ffn1_w_up, ffn1_w_down, mix_norm, w_in, pool_w, pool_scale, conv_dw_w, conv_dw_b, conv_ln_g, conv_ln_b, conv_w_out, w_out, ffn2_norm, ffn2_w_gate, ffn2_w_up, ffn2_w_down, ple_norm, ple_w_gate, ple_w_proj, final_norm, loss_target, m_ffn1_norm, m_ffn1_w_gate, m_ffn1_w_up, m_ffn1_w_down, m_mix_norm, m_w_in, m_pool_w, m_pool_scale, m_conv_dw_w, m_conv_dw_b, m_conv_ln_g, m_conv_ln_b, m_conv_w_out, m_w_out, m_ffn2_norm, m_ffn2_w_gate, m_ffn2_w_up, m_ffn2_w_down, m_ple_norm, m_ple_w_gate, m_ple_w_proj, m_final_norm, v_ffn1_norm, v_ffn1_w_gate, v_ffn1_w_up, v_ffn1_w_down, v_mix_norm, v_w_in, v_pool_w, v_pool_scale, v_conv_dw_w, v_conv_dw_b, v_conv_ln_g, v_conv_ln_b, v_conv_w_out, v_w_out, v_ffn2_norm, v_ffn2_w_gate, v_ffn2_w_up, v_ffn2_w_down, v_ple_norm, v_ple_w_gate, v_ple_w_proj, v_final_norm):
    ws = dict(zip(_WEIGHTS, (ffn1_norm, ffn1_w_gate, ffn1_w_up, ffn1_w_down, mix_norm, w_in, pool_w, pool_scale, conv_dw_w, conv_dw_b, conv_ln_g, conv_ln_b, conv_w_out, w_out, ffn2_norm, ffn2_w_gate, ffn2_w_up, ffn2_w_down, ple_norm, ple_w_gate, ple_w_proj, final_norm)))
    ms = dict(zip(_WEIGHTS, (m_ffn1_norm, m_ffn1_w_gate, m_ffn1_w_up, m_ffn1_w_down, m_mix_norm, m_w_in, m_pool_w, m_pool_scale, m_conv_dw_w, m_conv_dw_b, m_conv_ln_g, m_conv_ln_b, m_conv_w_out, m_w_out, m_ffn2_norm, m_ffn2_w_gate, m_ffn2_w_up, m_ffn2_w_down, m_ple_norm, m_ple_w_gate, m_ple_w_proj, m_final_norm)))
    vs = dict(zip(_WEIGHTS, (v_ffn1_norm, v_ffn1_w_gate, v_ffn1_w_up, v_ffn1_w_down, v_mix_norm, v_w_in, v_pool_w, v_pool_scale, v_conv_dw_w, v_conv_dw_b, v_conv_ln_g, v_conv_ln_b, v_conv_w_out, v_w_out, v_ffn2_norm, v_ffn2_w_gate, v_ffn2_w_up, v_ffn2_w_down, v_ple_norm, v_ple_w_gate, v_ple_w_proj, v_final_norm)))
    return _step(x, p, loss_target, ws, ms, vs)
```

```python
import jax
import jax.numpy as jnp
from jax import lax
from jax.experimental import pallas as pl
from jax.experimental.pallas import tpu as pltpu
from jax.experimental.pallas import tpu_sc as plsc

T = 8192
D = 1024
F = 2816
PD = 256
NG = 4
WINDOWS = (2, 4, 8, 16)
KC = 31
HALO = 32
DEPTH = 2
NCHIP = 4
RMS_EPS = 1e-6
LN_EPS = 1e-5

ADAM_LR = 0.001
ADAM_B1 = 0.9
ADAM_B2 = 0.999
ADAM_EPS = 1e-08
ADAM_WD = 0.01
ADAM_STEP = 10

TM = 512
TMB = 256
TMW = 512
LANES = 128
SUBLANES = 8
VMEM_LIMIT = 56 * 1024 * 1024

BF = jnp.bfloat16
F32 = jnp.float32
MESH = pl.DeviceIdType.MESH
ANY = pl.BlockSpec(memory_space=pl.ANY)
HBM = pltpu.MemorySpace.HBM


def _layout():
    fs, ins, ds = F // NCHIP, 5 * D // NCHIP, D // NCHIP
    pps = ds * PD // D
    pws = NG * (D // NG // NCHIP) * (D // NG) // D
    names = [("wg1", fs), ("wu1", fs), ("wd1", fs), ("win", ins), ("wco", ds), ("wo", ds),
             ("wg2", fs), ("wu2", fs), ("wd2", fs), ("wpg", ds), ("wpp", pps), ("pw", pws)]
    off, r = {}, 0
    for n, s in names:
        off[n] = (r, s)
        r += s
    return off, r


def _sig(v):
    return 1.0 / (1.0 + jnp.exp(-v))


def _dot_nn(a, b):
    return jnp.dot(a, b, preferred_element_type=F32)


def _dot_nt(a, b):
    return lax.dot_general(a, b, (((1,), (1,)), ((), ())), preferred_element_type=F32)


def _dot_tn(a, b):
    return lax.dot_general(a, b, (((0,), (0,)), ((), ())), preferred_element_type=F32)


def _params(sem=("arbitrary",)):
    return pltpu.CompilerParams(dimension_semantics=sem, vmem_limit_bytes=VMEM_LIMIT)


def _weight_copies(w_hbm, specs, sem):
    lay, _ = _layout()
    cps = []
    for i, (name, dst) in enumerate(specs):
        off, rs = lay[name]
        for k in range(NCHIP):
            cps.append(pltpu.make_async_copy(w_hbm.at[k, pl.ds(off, rs), :],
                                             dst.at[pl.ds(k * rs, rs), :], sem.at[i * NCHIP + k]))
    return cps


def _load_weights_once(w_hbm, specs, sem):
    @pl.when(pl.program_id(0) == 0)
    def _():
        cps = _weight_copies(w_hbm, specs, sem)
        for cp in cps:
            cp.start()
        for cp in cps:
            cp.wait()


def _grad_copies(stage, g_hbm, name, row0, rows, piece, sem, sem0):
    lay, _ = _layout()
    off, rs = lay[name]
    cps = []
    for i in range(rows // piece):
        rglob = row0 + i * piece
        k = rglob // rs
        loc = rglob - k * rs
        start = off + loc
        if not isinstance(start, int):
            start = pl.multiple_of(start, 16)
        dst = g_hbm.at[k, pl.ds(start, piece), :]
        cps.append(pltpu.make_async_copy(stage.at[pl.ds(i * piece, piece), :], dst, sem.at[sem0 + i]))
    return cps


def _row(v):
    return v.reshape(1, -1)


def _shifted_source(buf, sh, base, s, lanes):
    a, b = divmod(s, SUBLANES)
    rows = pl.ds(pl.multiple_of(base + SUBLANES * a, SUBLANES), SUBLANES)
    if b == 0:
        return buf[rows, lanes]
    return sh[b - 1, rows, :]


def _fill_shifted(buf, sh, lanes):
    rows = sh.shape[1]
    for b in range(1, SUBLANES):
        sh[b - 1, :, :] = buf[pl.ds(b, rows), lanes]


def _ffn_fwd(h, g, wfull, which):
    nt = T // TM
    fc = F // 2
    names = ("wg%d" % which, "wu%d" % which, "wd%d" % which)

    def body(h_ref, g_ref, w_hbm, ho_ref, a_ref, b_ref, wg, wu, wd, sem):
        _load_weights_once(w_hbm, ((names[0], wg), (names[1], wu), (names[2], wd)), sem)
        x = h_ref[...]
        r = lax.rsqrt(jnp.mean(x * x, axis=-1, keepdims=True) + RMS_EPS)
        n = (x * r * g_ref[...]).astype(BF)
        acc = jnp.zeros((TM, D), F32)
        for c in range(F // fc):
            sl = pl.ds(c * fc, fc)
            a = _dot_nt(n, wg[sl, :])
            b = _dot_nt(n, wu[sl, :])
            a_ref[:, sl] = a.astype(BF)
            b_ref[:, sl] = b.astype(BF)
            s = (a * _sig(a) * b).astype(BF)
            acc = acc + _dot_nn(s, wd[sl, :])
        ho_ref[...] = x + 0.5 * acc

    tile = lambda w: pl.BlockSpec((TM, w), lambda i: (i, 0))
    return pl.pallas_call(
        body, name="ffn_fwd", grid=(nt,),
        in_specs=[tile(D), pl.BlockSpec((1, D), lambda i: (0, 0)), ANY],
        out_specs=[tile(D), tile(F), tile(F)],
        out_shape=[jax.ShapeDtypeStruct((T, D), F32), jax.ShapeDtypeStruct((T, F), BF),
                   jax.ShapeDtypeStruct((T, F), BF)],
        scratch_shapes=[pltpu.VMEM((F, D), BF), pltpu.VMEM((F, D), BF), pltpu.VMEM((F, D), BF),
                        pltpu.SemaphoreType.DMA((3 * NCHIP,))],
        compiler_params=_params(),
    )(h, g, wfull)


def _mix_in_fwd(h, g, wfull):
    nt = T // TM
    nin = 5 * D

    def body(h_ref, g_ref, w_hbm, z_ref, win, sem):
        _load_weights_once(w_hbm, (("win", win),), sem)
        x = h_ref[...]
        r = lax.rsqrt(jnp.mean(x * x, axis=-1, keepdims=True) + RMS_EPS)
        u = (x * r * g_ref[...]).astype(BF)
        for c in range(5):
            sl = pl.ds(c * D, D)
            z_ref[:, sl] = _dot_nt(u, win[sl, :]).astype(BF)

    return pl.pallas_call(
        body, name="mix_in_fwd", grid=(nt,),
        in_specs=[pl.BlockSpec((TM, D), lambda i: (i, 0)), pl.BlockSpec((1, D), lambda i: (0, 0)), ANY],
        out_specs=pl.BlockSpec((TM, nin), lambda i: (i, 0)),
        out_shape=jax.ShapeDtypeStruct((T, nin), BF),
        scratch_shapes=[pltpu.VMEM((nin, D), BF), pltpu.SemaphoreType.DMA((NCHIP,))],
        compiler_params=_params(),
    )(h, g, wfull)


def _pool_counts(i, rows, w):
    t = i * TM + lax.broadcasted_iota(jnp.int32, (rows, 1), 0)
    return jnp.minimum(t + 1, w).astype(F32)


def _mix_mid_fwd(h, z, pwcat, pscale, wdw, bdw, lg, lb, wfull):
    nt = T // TM
    pg = D // NG
    hb = TM // HALO
    n_ext = HALO + TM
    pad = SUBLANES

    def body(h_ref, z_ref, zh_ref, pw_ref, ps_ref, wdw_ref, bdw_ref, lg_ref, lb_ref, w_hbm,
             h2_ref, p_ref, c1_ref, cc_ref, wco, wo, pa, pb, cbuf, sh, c1buf, ambuf, sem):
        i = pl.program_id(0)
        _load_weights_once(w_hbm, (("wco", wco), ("wo", wo)), sem)

        @pl.when(i == 0)
        def _():
            pa[0:pad, :] = jnp.zeros((pad, D), F32)
            pb[0:pad, :] = jnp.zeros((pad, D), F32)

        keep = (i > 0).astype(F32)
        zh = zh_ref[...].astype(F32) * keep
        za = z_ref[:, D:2 * D].astype(F32)
        zg = z_ref[:, 2 * D:3 * D].astype(F32)
        pa[pad:pad + HALO, :] = zh[:, 0:D]
        pa[pad + HALO:pad + n_ext, :] = z_ref[:, 0:D].astype(F32)
        cbuf[0:HALO, :] = zh[:, D:2 * D] * _sig(zh[:, 2 * D:3 * D])
        cbuf[HALO:n_ext, :] = za * _sig(zg)
        for g, w in enumerate(WINDOWS):
            ls = pl.ds(g * pg, pg)
            cur, nxt = pa, pb
            d = 1
            while d < w:
                nxt[pl.ds(pad, n_ext), ls] = cur[pl.ds(pad, n_ext), ls] + cur[pl.ds(pad - d, n_ext), ls]
                cur, nxt = nxt, cur
                d *= 2
            tok = z_ref[:, ls].astype(F32)
            pooled = (cur[pl.ds(pad + HALO, TM), ls] / _pool_counts(i, TM, w) - tok).astype(BF)
            p_ref[:, ls] = pooled
            ambuf[:, ls] = _dot_nn(pooled, pw_ref[:, ls])
        am = ambuf[...] * ps_ref[...]
        for l in range(D // LANES):
            lanes = pl.ds(l * LANES, LANES)
            _fill_shifted(cbuf, sh, lanes)
            bias = jnp.broadcast_to(bdw_ref[:, lanes], (SUBLANES, LANES))

            def conv_rows(r, carry):
                base = r * SUBLANES
                acc = bias
                for k in range(KC):
                    src = _shifted_source(cbuf, sh, base, HALO - (KC - 1) + k, lanes)
                    acc = acc + wdw_ref[k:k + 1, lanes] * src
                c1buf[pl.ds(pl.multiple_of(base, SUBLANES), SUBLANES), lanes] = acc
                return carry

            lax.fori_loop(0, TM // SUBLANES, conv_rows, 0)
        c1b = c1buf[...].astype(BF)
        c1_ref[...] = c1b
        c1 = c1b.astype(F32)
        mu = jnp.mean(c1, axis=-1, keepdims=True)
        xc = c1 - mu
        var = jnp.mean(xc * xc, axis=-1, keepdims=True)
        c2 = xc * lax.rsqrt(var + LN_EPS) * lg_ref[...] + lb_ref[...]
        c3 = (c2 * _sig(c2)).astype(BF)
        ccb = _dot_nn(c3, wco[...]).astype(BF)
        cc_ref[...] = ccb
        gp = z_ref[:, 3 * D:4 * D].astype(F32)
        gc = z_ref[:, 4 * D:5 * D].astype(F32)
        m = (_sig(gp) * am + _sig(gc) * ccb.astype(F32)).astype(BF)
        h2_ref[...] = h_ref[...] + _dot_nn(m, wo[...])

    tile = pl.BlockSpec((TM, D), lambda i: (i, 0))
    vec = pl.BlockSpec((1, D), lambda i: (0, 0))
    return pl.pallas_call(
        body, name="mix_mid_fwd", grid=(nt,),
        in_specs=[tile, pl.BlockSpec((TM, 5 * D), lambda i: (i, 0)),
                  pl.BlockSpec((HALO, 3 * D), lambda i: (jnp.maximum(i * hb - 1, 0), 0)),
                  pl.BlockSpec((pg, D), lambda i: (0, 0)), vec,
                  pl.BlockSpec((HALO, D), lambda i: (0, 0)), vec, vec, vec, ANY],
        out_specs=[tile, tile, tile, tile],
        out_shape=[jax.ShapeDtypeStruct((T, D), F32), jax.ShapeDtypeStruct((T, D), BF),
                   jax.ShapeDtypeStruct((T, D), BF), jax.ShapeDtypeStruct((T, D), BF)],
        scratch_shapes=[pltpu.VMEM((D, D), BF), pltpu.VMEM((D, D), BF),
                        pltpu.VMEM((pad + n_ext, D), F32), pltpu.VMEM((pad + n_ext, D), F32),
                        pltpu.VMEM((n_ext, D), F32), pltpu.VMEM((SUBLANES - 1, n_ext - SUBLANES, LANES), F32),
                        pltpu.VMEM((TM, D), F32), pltpu.VMEM((TM, D), F32),
                        pltpu.SemaphoreType.DMA((2 * NCHIP,))],
        compiler_params=_params(),
    )(h, z, z, pwcat, pscale, wdw, bdw, lg, lb, wfull)


def _ple_fwd(h, p, g, wppt, wfull):
    nt = T // TM

    def body(h_ref, p_ref, g_ref, wpp_ref, w_hbm, ho_ref, wpg, sem):
        _load_weights_once(w_hbm, (("wpg", wpg),), sem)
        x = h_ref[...]
        r = lax.rsqrt(jnp.mean(x * x, axis=-1, keepdims=True) + RMS_EPS)
        n = (x * r * g_ref[...]).astype(BF)
        gate = _sig(_dot_nn(n, wpg[...]))
        pe = _dot_nt(p_ref[...].astype(BF), wpp_ref[...])
        ho_ref[...] = x + gate * pe

    tile = pl.BlockSpec((TM, D), lambda i: (i, 0))
    return pl.pallas_call(
        body, name="ple_fwd", grid=(nt,),
        in_specs=[tile, pl.BlockSpec((TM, PD), lambda i: (i, 0)), pl.BlockSpec((1, D), lambda i: (0, 0)),
                  pl.BlockSpec((D, PD), lambda i: (0, 0)), ANY],
        out_specs=tile, out_shape=jax.ShapeDtypeStruct((T, D), F32),
        scratch_shapes=[pltpu.VMEM((D, D), BF), pltpu.SemaphoreType.DMA((NCHIP,))],
        compiler_params=_params(),
    )(h, p, g, wppt, wfull)


def _loss_bwd(h, tgt, g):
    nt = T // TM

    def body(h_ref, t_ref, g_ref, dh_ref, loss_ref, dg_ref):
        @pl.when(pl.program_id(0) == 0)
        def _():
            loss_ref[...] = jnp.zeros_like(loss_ref)
            dg_ref[...] = jnp.zeros_like(dg_ref)
        x = h_ref[...]
        r = lax.rsqrt(jnp.mean(x * x, axis=-1, keepdims=True) + RMS_EPS)
        xh = x * r
        gv = g_ref[...]
        e = xh * gv - t_ref[...]
        loss_ref[...] += jnp.sum(e * e, axis=0, keepdims=True) * (0.5 / D)
        dy = e * (1.0 / D)
        dg_ref[...] += jnp.sum(dy * xh, axis=0, keepdims=True)
        dxh = dy * gv
        dh_ref[...] = r * (dxh - xh * jnp.mean(dxh * xh, axis=-1, keepdims=True))

    tile = pl.BlockSpec((TM, D), lambda i: (i, 0))
    vec = pl.BlockSpec((1, D), lambda i: (0, 0))
    return pl.pallas_call(
        body, name="loss_bwd", grid=(nt,), in_specs=[tile, tile, vec], out_specs=[tile, vec, vec],
        out_shape=[jax.ShapeDtypeStruct((T, D), F32), jax.ShapeDtypeStruct((1, D), F32),
                   jax.ShapeDtypeStruct((1, D), F32)],
        compiler_params=_params(),
    )(h, tgt, g)


def _ple_bwd(h, dh, p, g, wppt, wfull):
    nt = T // TM
    _, rtot = _layout()

    def body(h_ref, dh_ref, p_ref, g_ref, wpp_ref, w_hbm,
             dho_ref, dg_ref, dwpp_ref, g_out, wpg, acc, stage, sem, osem):
        i = pl.program_id(0)
        _load_weights_once(w_hbm, (("wpg", wpg),), sem)

        @pl.when(i == 0)
        def _():
            dg_ref[...] = jnp.zeros_like(dg_ref)
            dwpp_ref[...] = jnp.zeros_like(dwpp_ref)
            acc[...] = jnp.zeros_like(acc)

        x = h_ref[...]
        r = lax.rsqrt(jnp.mean(x * x, axis=-1, keepdims=True) + RMS_EPS)
        xh = x * r
        gv = g_ref[...]
        n = (xh * gv).astype(BF)
        gate = _sig(_dot_nn(n, wpg[...]))
        pb = p_ref[...].astype(BF)
        pe = _dot_nt(pb, wpp_ref[...])
        d = dh_ref[...]
        dpe = (d * gate).astype(BF)
        dq = (d * pe * gate * (1.0 - gate)).astype(BF)
        dwpp_ref[...] += _dot_tn(dpe, pb)
        acc[...] += _dot_tn(n, dq)
        dn = _dot_nt(dq, wpg[...])
        dg_ref[...] += jnp.sum(dn * xh, axis=0, keepdims=True)
        dxh = dn * gv
        dho_ref[...] = d + r * (dxh - xh * jnp.mean(dxh * xh, axis=-1, keepdims=True))

        @pl.when(i == nt - 1)
        def _():
            stage[...] = acc[...].astype(BF)
            cps = _grad_copies(stage, g_out, "wpg", 0, D, D // NCHIP, osem, 0)
            for cp in cps:
                cp.start()
            for cp in cps:
                cp.wait()

    tile = pl.BlockSpec((TM, D), lambda i: (i, 0))
    vec = pl.BlockSpec((1, D), lambda i: (0, 0))
    return pl.pallas_call(
        body, name="ple_bwd", grid=(nt,),
        in_specs=[tile, tile, pl.BlockSpec((TM, PD), lambda i: (i, 0)), vec,
                  pl.BlockSpec((D, PD), lambda i: (0, 0)), ANY],
        out_specs=[tile, vec, pl.BlockSpec((D, PD), lambda i: (0, 0)), ANY],
        out_shape=[jax.ShapeDtypeStruct((T, D), F32), jax.ShapeDtypeStruct((1, D), F32),
                   jax.ShapeDtypeStruct((D, PD), F32),
                   jax.ShapeDtypeStruct((NCHIP, rtot, D), BF)],
        scratch_shapes=[pltpu.VMEM((D, D), BF), pltpu.VMEM((D, D), F32), pltpu.VMEM((D, D), BF),
                        pltpu.SemaphoreType.DMA((NCHIP,)), pltpu.SemaphoreType.DMA((NCHIP,))],
        compiler_params=_params(),
    )(h, dh, p, g, wppt, wfull)


def _ffn_bwd(h, dh, a, b, g, wfull, which):
    tm = TMB
    nt = T // tm
    fc = F // 2
    names = ("wg%d" % which, "wu%d" % which, "wd%d" % which)

    def body(h_ref, dh_ref, a_ref, b_ref, g_ref, w_hbm,
             dho_ref, da_ref, db_ref, s_ref, n_ref, dg_ref, wg, wu, wd, sem):
        _load_weights_once(w_hbm, ((names[0], wg), (names[1], wu), (names[2], wd)), sem)

        @pl.when(pl.program_id(0) == 0)
        def _():
            dg_ref[...] = jnp.zeros_like(dg_ref)

        x = h_ref[...]
        r = lax.rsqrt(jnp.mean(x * x, axis=-1, keepdims=True) + RMS_EPS)
        xh = x * r
        gv = g_ref[...]
        n_ref[...] = (xh * gv).astype(BF)
        d = dh_ref[...]
        df = (0.5 * d).astype(BF)
        dn = jnp.zeros((tm, D), F32)
        for c in range(F // fc):
            sl = pl.ds(c * fc, fc)
            av = a_ref[:, sl].astype(F32)
            bv = b_ref[:, sl].astype(F32)
            ds = _dot_nt(df, wd[sl, :])
            sg = _sig(av)
            sil = av * sg
            s_ref[:, sl] = (sil * bv).astype(BF)
            da = (ds * bv * (sg * (1.0 + av * (1.0 - sg)))).astype(BF)
            db = (ds * sil).astype(BF)
            da_ref[:, sl] = da
            db_ref[:, sl] = db
            dn = dn + _dot_nn(da, wg[sl, :]) + _dot_nn(db, wu[sl, :])
        dg_ref[...] += jnp.sum(dn * xh, axis=0, keepdims=True)
        dxh = dn * gv
        dho_ref[...] = d + r * (dxh - xh * jnp.mean(dxh * xh, axis=-1, keepdims=True))

    tile = lambda w: pl.BlockSpec((tm, w), lambda i: (i, 0))
    vec = pl.BlockSpec((1, D), lambda i: (0, 0))
    return pl.pallas_call(
        body, name="ffn_bwd", grid=(nt,),
        in_specs=[tile(D), tile(D), tile(F), tile(F), vec, ANY],
        out_specs=[tile(D), tile(F), tile(F), tile(F), tile(D), vec],
        out_shape=[jax.ShapeDtypeStruct((T, D), F32), jax.ShapeDtypeStruct((T, F), BF),
                   jax.ShapeDtypeStruct((T, F), BF), jax.ShapeDtypeStruct((T, F), BF),
                   jax.ShapeDtypeStruct((T, D), BF), jax.ShapeDtypeStruct((1, D), F32)],
        scratch_shapes=[pltpu.VMEM((F, D), BF), pltpu.VMEM((F, D), BF), pltpu.VMEM((F, D), BF),
                        pltpu.SemaphoreType.DMA((3 * NCHIP,))],
        compiler_params=_params(),
    )(h, dh, a, b, g, wfull)


def _wgrad(xs, y, gbuf, names, row0, rb, piece, yscale=None):
    nx = len(xs)
    rx = xs[0].shape[1]
    nj = rx // rb
    nt = T // TMW
    npiece = rb // piece

    def body(*refs):
        x_refs = refs[:nx]
        y_ref = refs[nx]
        g_out = refs[nx + 2]
        accs = refs[nx + 3:2 * nx + 3]
        stages = refs[2 * nx + 3:3 * nx + 3]
        osem = refs[3 * nx + 3]
        j = pl.program_id(0)
        t = pl.program_id(1)

        @pl.when(t == 0)
        def _():
            for acc in accs:
                acc[...] = jnp.zeros_like(acc)

        yv = y_ref[...]
        if yscale is not None:
            yv = (yscale * yv).astype(BF)
        for x_ref, acc in zip(x_refs, accs):
            acc[...] += _dot_tn(x_ref[...], yv)

        @pl.when(t == nt - 1)
        def _():
            cps = []
            for xi in range(nx):
                stages[xi][...] = accs[xi][...].astype(BF)
                cps += _grad_copies(stages[xi], g_out, names[xi], row0 + j * rb, rb, piece,
                                    osem, xi * npiece)
            for cp in cps:
                cp.start()
            for cp in cps:
                cp.wait()

    in_specs = [pl.BlockSpec((TMW, rb), lambda j, t: (t, j)) for _ in xs]
    in_specs += [pl.BlockSpec((TMW, D), lambda j, t: (t, 0)), ANY]
    return pl.pallas_call(
        body, name="wgrad", grid=(nj, nt), in_specs=in_specs, out_specs=ANY,
        out_shape=jax.ShapeDtypeStruct(gbuf.shape, gbuf.dtype),
        scratch_shapes=([pltpu.VMEM((rb, D), F32) for _ in xs] + [pltpu.VMEM((rb, D), BF) for _ in xs]
                        + [pltpu.SemaphoreType.DMA((nx * npiece,))]),
        input_output_aliases={nx + 1: 0},
        compiler_params=_params(("arbitrary", "arbitrary")),
    )(*xs, y, gbuf)


def _mix_b1(dh, z, pooled, c1, cc, pwcat, pscale, lg, lb, wfull, gbuf):
    nt = T // TM
    pg = D // NG

    def body(dh_ref, gp_ref, gc_ref, p_ref, c1_ref, cc_ref, pw_ref, ps_ref, lg_ref, lb_ref, w_hbm, _g_in,
             dp_ref, dc1_ref, dzb_ref, small_ref, dpw_ref, g_out,
             wco, wo, acc_o, acc_co, stage_o, stage_co, qbuf, sem, osem):
        i = pl.program_id(0)
        _load_weights_once(w_hbm, (("wco", wco), ("wo", wo)), sem)

        @pl.when(i == 0)
        def _():
            small_ref[...] = jnp.zeros_like(small_ref)
            dpw_ref[...] = jnp.zeros_like(dpw_ref)
            acc_o[...] = jnp.zeros_like(acc_o)
            acc_co[...] = jnp.zeros_like(acc_co)

        dhb = dh_ref[...].astype(BF)
        dm = _dot_nt(dhb, wo[...])
        sp = _sig(gp_ref[...].astype(F32))
        sc = _sig(gc_ref[...].astype(F32))
        for g in range(NG):
            ls = pl.ds(g * pg, pg)
            qbuf[:, ls] = _dot_nn(p_ref[:, ls], pw_ref[:, ls])
        q = qbuf[...]
        psv = ps_ref[...]
        am = q * psv
        ccv = cc_ref[...].astype(F32)
        m = (sp * am + sc * ccv).astype(BF)
        acc_o[...] += _dot_tn(m, dhb)
        dam = dm * sp
        dzb_ref[:, 0:D] = (dm * am * sp * (1.0 - sp)).astype(BF)
        dccb = (dm * sc).astype(BF)
        dzb_ref[:, D:2 * D] = (dm * ccv * sc * (1.0 - sc)).astype(BF)
        small_ref[0:1, :] += jnp.sum(dam * q, axis=0, keepdims=True)
        dq = (dam * psv).astype(BF)
        for g in range(NG):
            ls = pl.ds(g * pg, pg)
            dqg = dq[:, g * pg:(g + 1) * pg]
            dp_ref[:, ls] = _dot_nt(dqg, pw_ref[:, ls]).astype(BF)
            dpw_ref[:, ls] += _dot_tn(p_ref[:, ls], dqg)
        c1v = c1_ref[...].astype(F32)
        mu = jnp.mean(c1v, axis=-1, keepdims=True)
        xc = c1v - mu
        var = jnp.mean(xc * xc, axis=-1, keepdims=True)
        rs = lax.rsqrt(var + LN_EPS)
        c2n = xc * rs
        lgv = lg_ref[...]
        c2 = c2n * lgv + lb_ref[...]
        sg2 = _sig(c2)
        c3 = (c2 * sg2).astype(BF)
        acc_co[...] += _dot_tn(c3, dccb)
        dc3 = _dot_nt(dccb, wco[...])
        dc2 = dc3 * (sg2 * (1.0 + c2 * (1.0 - sg2)))
        small_ref[2:3, :] += jnp.sum(dc2 * c2n, axis=0, keepdims=True)
        small_ref[3:4, :] += jnp.sum(dc2, axis=0, keepdims=True)
        dc2n = dc2 * lgv
        dc1 = rs * (dc2n - jnp.mean(dc2n, axis=-1, keepdims=True)
                    - c2n * jnp.mean(dc2n * c2n, axis=-1, keepdims=True))
        small_ref[1:2, :] += jnp.sum(dc1, axis=0, keepdims=True)
        dc1_ref[...] = dc1.astype(BF)

        @pl.when(i == nt - 1)
        def _():
            stage_o[...] = acc_o[...].astype(BF)
            stage_co[...] = acc_co[...].astype(BF)
            cps = _grad_copies(stage_o, g_out, "wo", 0, D, D // NCHIP, osem, 0)
            cps += _grad_copies(stage_co, g_out, "wco", 0, D, D // NCHIP, osem, NCHIP)
            for cp in cps:
                cp.start()
            for cp in cps:
                cp.wait()

    tile = pl.BlockSpec((TM, D), lambda i: (i, 0))
    vec = pl.BlockSpec((1, D), lambda i: (0, 0))
    full = lambda r: pl.BlockSpec((r, D), lambda i: (0, 0))
    return pl.pallas_call(
        body, name="mix_b1", grid=(nt,),
        in_specs=[tile, pl.BlockSpec((TM, D), lambda i: (i, 3)), pl.BlockSpec((TM, D), lambda i: (i, 4)),
                  tile, tile, tile, full(pg), vec, vec, vec, ANY, ANY],
        out_specs=[tile, tile, pl.BlockSpec((TM, 2 * D), lambda i: (i, 0)), full(8), full(pg), ANY],
        out_shape=[jax.ShapeDtypeStruct((T, D), BF), jax.ShapeDtypeStruct((T, D), BF),
                   jax.ShapeDtypeStruct((T, 2 * D), BF), jax.ShapeDtypeStruct((8, D), F32),
                   jax.ShapeDtypeStruct((pg, D), F32), jax.ShapeDtypeStruct(gbuf.shape, gbuf.dtype)],
        scratch_shapes=[pltpu.VMEM((D, D), BF), pltpu.VMEM((D, D), BF),
                        pltpu.VMEM((D, D), F32), pltpu.VMEM((D, D), F32),
                        pltpu.VMEM((D, D), BF), pltpu.VMEM((D, D), BF),
                        pltpu.VMEM((TM, D), F32),
                        pltpu.SemaphoreType.DMA((2 * NCHIP,)), pltpu.SemaphoreType.DMA((2 * NCHIP,))],
        input_output_aliases={11: 5},
        compiler_params=_params(),
    )(dh, z, z, pooled, c1, cc, pwcat, pscale, lg, lb, wfull, gbuf)


def _mix_b2(dp, dc1, z, wdw):
    nt = T // TM
    pg = D // NG
    hb = TM // HALO
    nhb = T // HALO
    n_ext = TM + HALO
    pad = SUBLANES

    def body(dp_ref, dpn_ref, dc_ref, dcn_ref, za_ref, zg_ref, wdw_ref,
             dza_ref, dw_ref, pa, pb, cbuf, sh, c0buf, dc0buf):
        i = pl.program_id(0)

        @pl.when(i == 0)
        def _():
            dw_ref[...] = jnp.zeros_like(dw_ref)
            pa[n_ext:n_ext + pad, :] = jnp.zeros((pad, D), F32)
            pb[n_ext:n_ext + pad, :] = jnp.zeros((pad, D), F32)

        more = (i < nt - 1).astype(F32)
        for g, w in enumerate(WINDOWS):
            ls = pl.ds(g * pg, pg)
            cur_dp = dp_ref[:, ls].astype(F32)
            pa[0:TM, ls] = cur_dp / _pool_counts(i, TM, w)
            pa[TM:n_ext, ls] = dpn_ref[:, ls].astype(F32) * (more / w)
            cur, nxt = pa, pb
            d = 1
            while d < w:
                nxt[pl.ds(0, n_ext), ls] = cur[pl.ds(0, n_ext), ls] + cur[pl.ds(d, n_ext), ls]
                cur, nxt = nxt, cur
                d *= 2
            dza_ref[:, ls] = (cur[pl.ds(0, TM), ls] - cur_dp).astype(BF)
        za = za_ref[...].astype(F32)
        sg = _sig(zg_ref[...].astype(F32))
        cbuf[0:TM, :] = dc_ref[...].astype(F32)
        cbuf[TM:n_ext, :] = dcn_ref[...].astype(F32) * more
        c0buf[...] = za * sg
        for l in range(D // LANES):
            lanes = pl.ds(l * LANES, LANES)
            _fill_shifted(cbuf, sh, lanes)

            def conv_rows(r, accs):
                base = r * SUBLANES
                rows = pl.ds(pl.multiple_of(base, SUBLANES), SUBLANES)
                c0v = c0buf[rows, lanes]
                acc = jnp.zeros((SUBLANES, LANES), F32)
                new = []
                for k in range(KC):
                    src = _shifted_source(cbuf, sh, base, KC - 1 - k, lanes)
                    acc = acc + wdw_ref[k:k + 1, lanes] * src
                    new.append(accs[k] + c0v * src)
                dc0buf[rows, lanes] = acc
                return tuple(new)

            init = tuple(jnp.zeros((SUBLANES, LANES), F32) for _ in range(KC))
            accs = lax.fori_loop(0, TM // SUBLANES, conv_rows, init)
            for k in range(KC):
                dw_ref[k:k + 1, lanes] += jnp.sum(accs[k], axis=0, keepdims=True)
        dc0 = dc0buf[...]
        dza_ref[:, D:2 * D] = (dc0 * sg).astype(BF)
        dza_ref[:, 2 * D:3 * D] = (dc0 * za * sg * (1.0 - sg)).astype(BF)

    tile = pl.BlockSpec((TM, D), lambda i: (i, 0))
    nxt_spec = pl.BlockSpec((HALO, D), lambda i: (jnp.minimum((i + 1) * hb, nhb - 1), 0))
    return pl.pallas_call(
        body, name="mix_b2", grid=(nt,),
        in_specs=[tile, nxt_spec, tile, nxt_spec, pl.BlockSpec((TM, D), lambda i: (i, 1)),
                  pl.BlockSpec((TM, D), lambda i: (i, 2)), pl.BlockSpec((HALO, D), lambda i: (0, 0))],
        out_specs=[pl.BlockSpec((TM, 3 * D), lambda i: (i, 0)), pl.BlockSpec((HALO, D), lambda i: (0, 0))],
        out_shape=[jax.ShapeDtypeStruct((T, 3 * D), BF), jax.ShapeDtypeStruct((HALO, D), F32)],
        scratch_shapes=[pltpu.VMEM((n_ext + pad, D), F32), pltpu.VMEM((n_ext + pad, D), F32),
                        pltpu.VMEM((n_ext, D), F32), pltpu.VMEM((SUBLANES - 1, n_ext - SUBLANES, LANES), F32),
                        pltpu.VMEM((TM, D), F32), pltpu.VMEM((TM, D), F32)],
        compiler_params=_params(),
    )(dp, dp, dc1, dc1, z, z, wdw)


def _mix_b3(h, dh, dza, dzb, g, wfull):
    nt = T // TM

    def body(h_ref, dh_ref, dza_ref, dzb_ref, g_ref, w_hbm, dho_ref, u_ref, dg_ref, win, sem):
        _load_weights_once(w_hbm, (("win", win),), sem)

        @pl.when(pl.program_id(0) == 0)
        def _():
            dg_ref[...] = jnp.zeros_like(dg_ref)

        x = h_ref[...]
        r = lax.rsqrt(jnp.mean(x * x, axis=-1, keepdims=True) + RMS_EPS)
        xh = x * r
        gv = g_ref[...]
        u_ref[...] = (xh * gv).astype(BF)
        du = _dot_nn(dza_ref[...], win[0:3 * D, :]) + _dot_nn(dzb_ref[...], win[3 * D:5 * D, :])
        dg_ref[...] += jnp.sum(du * xh, axis=0, keepdims=True)
        dxh = du * gv
        dho_ref[...] = dh_ref[...] + r * (dxh - xh * jnp.mean(dxh * xh, axis=-1, keepdims=True))

    tile = lambda w: pl.BlockSpec((TM, w), lambda i: (i, 0))
    vec = pl.BlockSpec((1, D), lambda i: (0, 0))
    return pl.pallas_call(
        body, name="mix_b3", grid=(nt,),
        in_specs=[tile(D), tile(D), tile(3 * D), tile(2 * D), vec, ANY],
        out_specs=[tile(D), tile(D), vec],
        out_shape=[jax.ShapeDtypeStruct((T, D), F32), jax.ShapeDtypeStruct((T, D), BF),
                   jax.ShapeDtypeStruct((1, D), F32)],
        scratch_shapes=[pltpu.VMEM((5 * D, D), BF), pltpu.SemaphoreType.DMA((NCHIP,))],
        compiler_params=_params(),
    )(h, dh, dza, dzb, g, wfull)


def _mesh_pos():
    x, y, c = lax.axis_index("x"), lax.axis_index("y"), lax.axis_index("c")
    chips = [(1 - x, y), (x, 1 - y), (1 - x, 1 - y)]
    return x, y, c, 2 * x + y, chips


def _handshake(peers):
    barrier = pltpu.get_barrier_semaphore()
    for peer in peers:
        pl.semaphore_signal(barrier, inc=1, device_id=peer, device_id_type=MESH)
    pl.semaphore_wait(barrier, len(peers))


def _run_comm(body, name, cid, ins, out_types, scratch):
    in_refs = [jax.new_ref(a, memory_space=HBM) for a in ins]
    out_refs = [jax.empty_ref(t, memory_space=HBM) for t in out_types]

    @pl.kernel(mesh=plsc.ScalarSubcoreMesh(axis_name="seq", num_cores=1), name=name,
               scratch_types=scratch, compiler_params=pltpu.CompilerParams(collective_id=cid))
    def launch(*scr):
        body(*in_refs, *out_refs, *scr)

    launch()
    return [r[...] for r in out_refs]


def _remote(src, dst, send_sem, recv_sem, to):
    return pltpu.make_async_remote_copy(src_ref=src, dst_ref=dst, send_sem=send_sem, recv_sem=recv_sem,
                                        device_id=to, device_id_type=MESH)


def _gather_layer(packed, cid):
    rtot = packed.shape[0]
    half = rtot // 2

    def body(p_ref, w_ref, send_sems, recv_sems, lsem):
        x, y, c, kme, chips = _mesh_pos()
        sib = (x, y, 1 - c)
        _handshake([(*ch, c) for ch in chips] + [sib])
        ks = [2 * cx + cy for cx, cy in chips]
        mine = pl.ds(pl.multiple_of(c * half, 16), half)
        other = pl.ds(pl.multiple_of((1 - c) * half, 16), half)
        local = pltpu.make_async_copy(p_ref, w_ref.at[kme], lsem)
        local.start()
        first = [_remote(p_ref.at[mine], w_ref.at[kme, mine], send_sems.at[j], recv_sems.at[j], (*chips[j], c))
                 for j in range(3)]
        for cp in first:
            cp.start()
        passed = [_remote(w_ref.at[ks[j], mine], w_ref.at[ks[j], mine], send_sems.at[3 + j], recv_sems.at[3 + j], sib)
                  for j in range(3)]
        for j in range(3):
            _remote(p_ref.at[mine], w_ref.at[ks[j], mine], send_sems.at[j], recv_sems.at[j], sib).wait_recv()
            passed[j].start()
        for j in range(3):
            _remote(p_ref.at[mine], w_ref.at[ks[j], other], send_sems.at[3 + j], recv_sems.at[3 + j], sib).wait_recv()
        for cp in first + passed:
            cp.wait_send()
        local.wait()

    return _run_comm(body, "gather_layer_%d" % cid, cid, [packed],
                     [jax.ShapeDtypeStruct((NCHIP, rtot, D), BF)],
                     (pltpu.SemaphoreType.DMA((6,)), pltpu.SemaphoreType.DMA((6,)), pltpu.SemaphoreType.DMA))[0]


def _sibling_swap(gbuf, cid):
    rtot = gbuf.shape[1]
    half = rtot // 2

    def body(g_ref, r_ref, send_sem, recv_sem):
        x, y, c, _, _ = _mesh_pos()
        sib = (x, y, 1 - c)
        _handshake([sib])
        other = pl.ds(pl.multiple_of((1 - c) * half, 16), half)
        cp = _remote(g_ref.at[:, other, :], r_ref, send_sem, recv_sem, sib)
        cp.start()
        cp.wait()

    return _run_comm(body, "sibling_swap_%d" % cid, cid, [gbuf],
                     [jax.ShapeDtypeStruct((NCHIP, half, D), BF)],
                     (pltpu.SemaphoreType.DMA, pltpu.SemaphoreType.DMA))[0]


def _row_tile(rows):
    for cand in (640, 512, 400, 256, 128):
        if rows % cand == 0:
            return cand
    return rows


def _chip_sum(gbuf, rbuf, cidx):
    half = rbuf.shape[1]
    rt = _row_tile(half)
    nb = half // rt

    def body(c_ref, g_ref, r_ref, o_ref):
        o_ref[...] = (g_ref[...].astype(F32) + r_ref[...].astype(F32)).astype(BF)

    return pl.pallas_call(
        body, name="chip_sum",
        grid_spec=pltpu.PrefetchScalarGridSpec(
            num_scalar_prefetch=1, grid=(NCHIP, nb),
            in_specs=[pl.BlockSpec((None, rt, D), lambda k, r, c: (k, c[0] * nb + r, 0)),
                      pl.BlockSpec((None, rt, D), lambda k, r, c: (k, r, 0))],
            out_specs=pl.BlockSpec((None, rt, D), lambda k, r, c: (k, r, 0))),
        out_shape=jax.ShapeDtypeStruct((NCHIP, half, D), BF),
        compiler_params=_params(("arbitrary", "arbitrary")),
    )(cidx, gbuf, rbuf)


def _chip_exchange(sbuf, cid):
    half = sbuf.shape[1]

    def body(s_ref, x_ref, send_sems, recv_sems):
        x, y, c, _, chips = _mesh_pos()
        _handshake([(*ch, c) for ch in chips])
        cps = [_remote(s_ref.at[2 * cx + cy], x_ref.at[j], send_sems.at[j], recv_sems.at[j], (cx, cy, c))
               for j, (cx, cy) in enumerate(chips)]
        for cp in cps:
            cp.start()
        for cp in cps:
            cp.wait()

    return _run_comm(body, "chip_exchange_%d" % cid, cid, [sbuf],
                     [jax.ShapeDtypeStruct((3, half, D), BF)],
                     (pltpu.SemaphoreType.DMA((3,)), pltpu.SemaphoreType.DMA((3,))))[0]


def _shard_sum(gbuf, rbuf, xbuf, ck):
    half = rbuf.shape[1]
    rt = _row_tile(half)
    nb = half // rt

    def body(ck_ref, g_ref, r_ref, x_ref, o_ref):
        acc = g_ref[...].astype(F32) + r_ref[...].astype(F32)
        for j in range(3):
            acc = acc + x_ref[j].astype(F32)
        o_ref[...] = acc

    return pl.pallas_call(
        body, name="shard_sum",
        grid_spec=pltpu.PrefetchScalarGridSpec(
            num_scalar_prefetch=1, grid=(nb,),
            in_specs=[pl.BlockSpec((None, rt, D), lambda r, ck: (ck[1], ck[0] * nb + r, 0)),
                      pl.BlockSpec((None, rt, D), lambda r, ck: (ck[1], r, 0)),
                      pl.BlockSpec((3, rt, D), lambda r, ck: (0, r, 0))],
            out_specs=pl.BlockSpec((rt, D), lambda r, ck: (r, 0))),
        out_shape=jax.ShapeDtypeStruct((half, D), F32),
        compiler_params=_params(),
    )(ck, gbuf, rbuf, xbuf)


def _sibling_share(red, cid):
    half = red.shape[0]

    def body(r_ref, o_ref, send_sem, recv_sem, lsem):
        x, y, c, _, _ = _mesh_pos()
        sib = (x, y, 1 - c)
        _handshake([sib])
        mine = pl.ds(pl.multiple_of(c * half, 8), half)
        other = pl.ds(pl.multiple_of((1 - c) * half, 8), half)
        local = pltpu.make_async_copy(r_ref, o_ref.at[mine], lsem)
        local.start()
        cp = _remote(r_ref, o_ref.at[mine], send_sem, recv_sem, sib)
        cp.start()
        cp.wait_send()
        _remote(r_ref, o_ref.at[other], send_sem, recv_sem, sib).wait_recv()
        local.wait()

    return _run_comm(body, "sibling_share_%d" % cid, cid, [red],
                     [jax.ShapeDtypeStruct((2 * half, D), F32)],
                     (pltpu.SemaphoreType.DMA, pltpu.SemaphoreType.DMA, pltpu.SemaphoreType.DMA))[0]


def _allreduce_small(v):
    rows = v.shape[0]
    ndev = 2 * NCHIP

    def body(v_ref, o_ref, gat, send_sems, recv_sems, lsem):
        x, y, c, _, chips = _mesh_pos()
        me, sib = (x, y, c), (x, y, 1 - c)

        def blk(px, py, pc):
            return gat.at[pl.ds((4 * px + 2 * py + pc) * rows, rows), :]

        def copy(k, block, to, src=None):
            return pltpu.make_async_remote_copy(
                src_ref=blk(*block) if src is None else src, dst_ref=blk(*block),
                send_sem=send_sems.at[k], recv_sem=recv_sems.at[k], device_id=to, device_id_type=MESH)

        mine = pltpu.make_async_copy(v_ref, blk(*me), lsem)
        mine.start()
        first = [copy(0, me, sib, src=v_ref)]
        first += [copy(1 + j, me, (*chip, c), src=v_ref) for j, chip in enumerate(chips)]
        for cp in first:
            cp.start()
        passed = [copy(4 + j, (*chip, c), sib) for j, chip in enumerate(chips)]
        for j, chip in enumerate(chips):
            copy(1 + j, (*chip, c), me).wait_recv()
            passed[j].start()
        copy(0, sib, me).wait_recv()
        for j, chip in enumerate(chips):
            copy(4 + j, (*chip, 1 - c), me).wait_recv()
        for cp in first + passed:
            cp.wait_send()
        mine.wait()
        acc = gat[0:rows, :]
        for d in range(1, ndev):
            acc = acc + gat[d * rows:(d + 1) * rows, :]
        o_ref[...] = acc

    vm = pl.BlockSpec(memory_space=pltpu.VMEM)
    return pl.pallas_call(
        body, name="allreduce_small", in_specs=[vm], out_specs=vm,
        out_shape=jax.ShapeDtypeStruct((rows, D), F32),
        scratch_shapes=[pltpu.VMEM((ndev * rows, D), F32), pltpu.SemaphoreType.DMA((7,)),
                        pltpu.SemaphoreType.DMA((7,)), pltpu.SemaphoreType.DMA],
    )(v)


def _adamw(w, g, m, v):
    shape = w.shape
    cols = shape[-1]
    rows = w.size // cols
    bm = rows
    for cand in (512, 256, 128, 64, 32, 16, 8):
        if rows % cand == 0:
            bm = cand
            break
    bc1 = 1.0 - ADAM_B1 ** ADAM_STEP
    bc2 = 1.0 - ADAM_B2 ** ADAM_STEP

    def body(w_ref, g_ref, m_ref, v_ref, d_ref, mo_ref, vo_ref):
        gv = g_ref[...]
        mn = ADAM_B1 * m_ref[...] + (1.0 - ADAM_B1) * gv
        vn = ADAM_B2 * v_ref[...] + (1.0 - ADAM_B2) * (gv * gv)
        mo_ref[...] = mn
        vo_ref[...] = vn
        d_ref[...] = -ADAM_LR * ((mn / bc1) / (jnp.sqrt(vn / bc2) + ADAM_EPS) + ADAM_WD * w_ref[...])

    spec = pl.BlockSpec((bm, cols), lambda i: (i, 0))
    out = jax.ShapeDtypeStruct((rows, cols), F32)
    d, mo, vo = pl.pallas_call(
        body, name="adamw", grid=(rows // bm,), in_specs=[spec] * 4, out_specs=[spec] * 3,
        out_shape=[out, out, out], compiler_params=_params(),
    )(*[t.reshape(rows, cols) for t in (w, g, m, v)])
    return d.reshape(shape), mo.reshape(shape), vo.reshape(shape)


def _pack_shards(ws, li):
    pg = D // NG
    t = lambda a: jnp.swapaxes(a[li], 0, 1)
    parts = [t(ws["ffn1_w_gate"]), t(ws["ffn1_w_up"]), ws["ffn1_w_down"][li],
             t(ws["w_in"]), ws["conv_w_out"][li], ws["w_out"][li],
             t(ws["ffn2_w_gate"]), t(ws["ffn2_w_up"]), ws["ffn2_w_down"][li], ws["ple_w_gate"][li],
             t(ws["ple_w_proj"]).reshape(-1, D),
             jnp.swapaxes(ws["pool_w"][li], 0, 1).reshape(pg // NCHIP, D)]
    return jnp.concatenate([p.astype(BF) for p in parts], axis=0)


def _unpack_shards(red):
    lay, _ = _layout()
    pg = D // NG
    rows = lambda n: red[lay[n][0]:lay[n][0] + lay[n][1], :]
    t = lambda a: jnp.swapaxes(a, 0, 1)
    return {
        "ffn1_w_gate": t(rows("wg1")), "ffn1_w_up": t(rows("wu1")), "ffn1_w_down": rows("wd1"),
        "ffn2_w_gate": t(rows("wg2")), "ffn2_w_up": t(rows("wu2")), "ffn2_w_down": rows("wd2"),
        "w_in": t(rows("win")), "conv_w_out": rows("wco"), "w_out": rows("wo"), "ple_w_gate": rows("wpg"),
        "ple_w_proj": t(rows("wpp").reshape(D // NCHIP, PD)),
        "pool_w": jnp.swapaxes(rows("pw").reshape(pg // NCHIP, NG, pg), 0, 1),
    }


_BIG = ("ffn1_w_gate", "ffn1_w_up", "ffn1_w_down", "w_in", "pool_w", "conv_w_out", "w_out",
        "ffn2_w_gate", "ffn2_w_up", "ffn2_w_down", "ple_w_gate", "ple_w_proj")
_VECS = ("ffn1_norm", "mix_norm", "pool_scale", "conv_dw_b", "conv_ln_g", "conv_ln_b", "ffn2_norm", "ple_norm")
_WEIGHTS = ("ffn1_norm", "ffn1_w_gate", "ffn1_w_up", "ffn1_w_down", "mix_norm", "w_in", "pool_w", "pool_scale",
            "conv_dw_w", "conv_dw_b", "conv_ln_g", "conv_ln_b", "conv_w_out", "w_out", "ffn2_norm",
            "ffn2_w_gate", "ffn2_w_up", "ffn2_w_down", "ple_norm", "ple_w_gate", "ple_w_proj", "final_norm")


def _step(x, p, tgt, ws, ms, vs):
    lay, rtot = _layout()
    pg = D // NG
    cpos = lax.axis_index("c")
    kme = 2 * lax.axis_index("x") + lax.axis_index("y")
    cidx = jnp.stack([cpos]).astype(jnp.int32)
    ck = jnp.stack([cpos, kme]).astype(jnp.int32)
    h = x.reshape(T, D)
    tgt = tgt.reshape(T, D)

    wfull = []
    for li in range(DEPTH):
        packed = _pack_shards(ws, li)
        if li > 0:
            packed, wfull[li - 1] = lax.optimization_barrier((packed, wfull[li - 1]))
        wfull.append(_gather_layer(packed, li))
    wppt, pwcat = [None] * DEPTH, [None] * DEPTH
    kk = ws["conv_dw_w"].shape[1]
    wdw_mine = jnp.zeros((DEPTH * HALO, D), F32)
    for li in range(DEPTH):
        blockw = jnp.zeros((kk, D), F32)
        mine = jnp.where(cpos == 0, ws["conv_dw_w"][li], 0.0)
        blockw = lax.dynamic_update_slice(blockw, mine, (0, kme * (D // NCHIP)))
        wdw_mine = wdw_mine.at[li * HALO:li * HALO + kk, :].set(blockw)
    wdw_all = _allreduce_small(wdw_mine)
    wdw = [wdw_all[li * HALO:(li + 1) * HALO, :] for li in range(DEPTH)]
    vec = lambda name, li: _row(ws[name][li])

    saved = []
    for li in range(DEPTH):
        if li > 0:
            wfull[li], h = lax.optimization_barrier((wfull[li], h))
        o, s = lay["wpp"]
        wppt[li] = wfull[li][:, o:o + s, :].reshape(D, PD)
        o, s = lay["pw"]
        pwcat[li] = wfull[li][:, o:o + s, :].reshape(pg, D)
        h0 = h
        h1, a1, b1 = _ffn_fwd(h0, vec("ffn1_norm", li), wfull[li], 1)
        z = _mix_in_fwd(h1, vec("mix_norm", li), wfull[li])
        h2, pooled, c1, cc = _mix_mid_fwd(h1, z, pwcat[li], vec("pool_scale", li), wdw[li],
                                          vec("conv_dw_b", li), vec("conv_ln_g", li), vec("conv_ln_b", li),
                                          wfull[li])
        h3, a2, b2 = _ffn_fwd(h2, vec("ffn2_norm", li), wfull[li], 2)
        h = _ple_fwd(h3, p[li, 0], vec("ple_norm", li), wppt[li], wfull[li])
        saved.append((h0, a1, b1, h1, z, pooled, c1, cc, h2, a2, b2, h3))

    dh, losscols, dgf = _loss_bwd(h, tgt, _row(ws["final_norm"]))
    loss = lax.psum(jnp.sum(losscols), ("x", "y", "c"))
    vecg = [dict() for _ in range(DEPTH)]
    dwdw = [None] * DEPTH
    red = [None] * DEPTH
    above = None
    for li in reversed(range(DEPTH)):
        h0, a1, b1, h1, z, pooled, c1, cc, h2, a2, b2, h3 = saved[li]
        w = wfull[li]
        dh, dgp, dwpp, gbuf = _ple_bwd(h3, dh, p[li, 0], vec("ple_norm", li), wppt[li], w)
        vecg[li]["ple_norm"] = dgp
        if above is not None:
            la, ga, ra = above
            dh, ra = lax.optimization_barrier((dh, ra))
            sbuf = _chip_sum(ga, ra, cidx)
            dh, sbuf = lax.optimization_barrier((dh, sbuf))
            xa = _chip_exchange(sbuf, 2 * DEPTH + la)
        dh_in, da, db, sact, n, dg = _ffn_bwd(h2, dh, a2, b2, vec("ffn2_norm", li), w, 2)
        vecg[li]["ffn2_norm"] = dg
        gbuf = _wgrad([da, db], n, gbuf, ("wg2", "wu2"), 0, F // 2, F // NCHIP)
        gbuf = _wgrad([sact], dh, gbuf, ("wd2",), 0, F // 2, F // NCHIP, yscale=0.5)
        dh = dh_in
        dp, dc1, dzb, small, dpw, gbuf = _mix_b1(dh, z, pooled, c1, cc, pwcat[li], vec("pool_scale", li),
                                                 vec("conv_ln_g", li), vec("conv_ln_b", li), w, gbuf)
        vecg[li]["pool_scale"] = small[0:1]
        vecg[li]["conv_dw_b"] = small[1:2]
        vecg[li]["conv_ln_g"] = small[2:3]
        vecg[li]["conv_ln_b"] = small[3:4]
        dza, dwdw[li] = _mix_b2(dp, dc1, z, wdw[li])
        if above is not None:
            dza, xa = lax.optimization_barrier((dza, xa))
            red_a = _shard_sum(ga, ra, xa, ck)
            dza, red_a = lax.optimization_barrier((dza, red_a))
            red[la] = _sibling_share(red_a, 3 * DEPTH + la)
        dh_in, u, dg = _mix_b3(h1, dh, dza, dzb, vec("mix_norm", li), w)
        vecg[li]["mix_norm"] = dg
        gbuf = _wgrad([dza], u, gbuf, ("win",), 0, D, D // NCHIP)
        gbuf = _wgrad([dzb], u, gbuf, ("win",), 3 * D, D, D // NCHIP)
        dh = dh_in
        dh_in, da, db, sact, n, dg = _ffn_bwd(h0, dh, a1, b1, vec("ffn1_norm", li), w, 1)
        vecg[li]["ffn1_norm"] = dg
        gbuf = _wgrad([da, db], n, gbuf, ("wg1", "wu1"), 0, F // 2, F // NCHIP)
        gbuf = _wgrad([sact], dh, gbuf, ("wd1",), 0, F // 2, F // NCHIP, yscale=0.5)
        dh = dh_in
        o, s = lay["wpp"]
        small_rows = jnp.concatenate([dwpp.reshape(NCHIP, s, D), dpw.reshape(NCHIP, lay["pw"][1], D)], axis=1)
        gbuf = lax.dynamic_update_slice(gbuf, small_rows.astype(BF), (0, o, 0))
        above = (li, gbuf, _sibling_swap(gbuf, DEPTH + li))
    la, ga, ra = above
    xa = _chip_exchange(_chip_sum(ga, ra, cidx), 2 * DEPTH + la)
    red[la] = _sibling_share(_shard_sum(ga, ra, xa, ck), 3 * DEPTH + la)
    grad_x = dh.reshape(x.shape)

    rows = [vecg[li][n] for li in range(DEPTH) for n in _VECS] + [dgf]
    rows.append(jnp.zeros((8 - (len(rows) % 8), D), F32))
    vsum = _allreduce_small(jnp.concatenate(rows + dwdw, axis=0))
    nvec = len(_VECS)
    grads = {}
    unpacked = [_unpack_shards(red[li]) for li in range(DEPTH)]
    for n in _BIG:
        grads[n] = jnp.stack([unpacked[li][n] for li in range(DEPTH)])
    for i, n in enumerate(_VECS):
        grads[n] = jnp.stack([vsum[li * nvec + i] for li in range(DEPTH)])
    grads["final_norm"] = vsum[DEPTH * nvec]
    base = DEPTH * nvec + 8 - ((DEPTH * nvec + 1) % 8) + 1
    dcols = D // NCHIP
    grads["conv_dw_w"] = jnp.stack([
        lax.dynamic_slice(vsum[base + li * HALO: base + li * HALO + kk, :], (0, kme * dcols), (kk, dcols))
        for li in range(DEPTH)])

    outs_g, outs_d, outs_m, outs_v = [], [], [], []
    for n in _WEIGHTS:
        d, mo, vo = _adamw(ws[n], grads[n], ms[n], vs[n])
        outs_g.append(grads[n])
        outs_d.append(d)
        outs_m.append(mo)
        outs_v.append(vo)
    return (loss, grad_x, *outs_g, *outs_d, *outs_m, *outs_v)


def kernel(x, p, ffn1_norm, ffn1_w_gate, ffn1_w_up, ffn1_w_down, mix_norm, w_in, pool_w, pool_scale, conv_dw_w, conv_dw_b, conv_ln_g, conv_ln_b, conv_w_out, w_out, ffn2_norm, ffn2_w_gate, ffn2_w_up, ffn2_w_down, ple_norm, ple_w_gate, ple_w_proj, final_norm, loss_target, m_ffn1_norm, m_ffn1_w_gate, m_ffn1_w_up, m_ffn1_w_down, m_mix_norm, m_w_in, m_pool_w, m_pool_scale, m_conv_dw_w, m_conv_dw_b, m_conv_ln_g, m_conv_ln_b, m_conv_w_out, m_w_out, m_ffn2_norm, m_ffn2_w_gate, m_ffn2_w_up, m_ffn2_w_down, m_ple_norm, m_ple_w_gate, m_ple_w_proj, m_final_norm, v_ffn1_norm, v_ffn1_w_gate, v_ffn1_w_up, v_ffn1_w_down, v_mix_norm, v_w_in, v_pool_w, v_pool_scale, v_conv_dw_w, v_conv_dw_b, v_conv_ln_g, v_conv_ln_b, v_conv_w_out, v_w_out, v_ffn2_norm, v_ffn2_w_gate, v_ffn2_w_up, v_ffn2_w_down, v_ple_norm, v_ple_w_gate, v_ple_w_proj, v_final_norm):
    ws = dict(zip(_WEIGHTS, (ffn1_norm, ffn1_w_gate, ffn1_w_up, ffn1_w_down, mix_norm, w_in, pool_w, pool_scale, conv_dw_w, conv_dw_b, conv_ln_g, conv_ln_b, conv_w_out, w_out, ffn2_norm, ffn2_w_gate, ffn2_w_up, ffn2_w_down, ple_norm, ple_w_gate, ple_w_proj, final_norm)))
    ms = dict(zip(_WEIGHTS, (m_ffn1_norm, m_ffn1_w_gate, m_ffn1_w_up, m_ffn1_w_down, m_mix_norm, m_w_in, m_pool_w, m_pool_scale, m_conv_dw_w, m_conv_dw_b, m_conv_ln_g, m_conv_ln_b, m_conv_w_out, m_w_out, m_ffn2_norm, m_ffn2_w_gate, m_ffn2_w_up, m_ffn2_w_down, m_ple_norm, m_ple_w_gate, m_ple_w_proj, m_final_norm)))
    vs = dict(zip(_WEIGHTS, (v_ffn1_norm, v_ffn1_w_gate, v_ffn1_w_up, v_ffn1_w_down, v_mix_norm, v_w_in, v_pool_w, v_pool_scale, v_conv_dw_w, v_conv_dw_b, v_conv_ln_g, v_conv_ln_b, v_conv_w_out, v_w_out, v_ffn2_norm, v_ffn2_w_gate, v_ffn2_w_up, v_ffn2_w_down, v_ple_norm, v_ple_w_gate, v_ple_w_proj, v_final_norm)))
    return _step(x, p, loss_target, ws, ms, vs)
```

```python
import jax
import jax.numpy as jnp
from jax import lax
from jax.experimental import pallas as pl
from jax.experimental.pallas import tpu as pltpu
from jax.experimental.pallas import tpu_sc as plsc

T = 8192
D = 1024
F = 2816
PD = 256
NG = 4
WINDOWS = (2, 4, 8, 16)
KC = 31
HALO = 32
DEPTH = 2
NCHIP = 4
RMS_EPS = 1e-6
LN_EPS = 1e-5

ADAM_LR = 0.001
ADAM_B1 = 0.9
ADAM_B2 = 0.999
ADAM_EPS = 1e-08
ADAM_WD = 0.01
ADAM_STEP = 10

TM = 512
TMB = 256
TMW = 512
LANES = 128
SUBLANES = 8
CHUNKS = 4
VMEM_LIMIT = 56 * 1024 * 1024

BF = jnp.bfloat16
F32 = jnp.float32
MESH = pl.DeviceIdType.MESH
ANY = pl.BlockSpec(memory_space=pl.ANY)
HBM = pltpu.MemorySpace.HBM


def _layout():
    fs, ins, ds = F // NCHIP, 5 * D // NCHIP, D // NCHIP
    pps = ds * PD // D
    pws = NG * (D // NG // NCHIP) * (D // NG) // D
    names = [("wg1", fs), ("wu1", fs), ("wd1", fs), ("win", ins), ("wco", ds), ("wo", ds),
             ("wg2", fs), ("wu2", fs), ("wd2", fs), ("wpg", ds), ("wpp", pps), ("pw", pws)]
    off, r = {}, 0
    for n, s in names:
        off[n] = (r, s)
        r += s
    return off, r


def _sig(v):
    return 1.0 / (1.0 + jnp.exp(-v))


def _dot_nn(a, b):
    return jnp.dot(a, b, preferred_element_type=F32)


def _dot_nt(a, b):
    return lax.dot_general(a, b, (((1,), (1,)), ((), ())), preferred_element_type=F32)


def _dot_tn(a, b):
    return lax.dot_general(a, b, (((0,), (0,)), ((), ())), preferred_element_type=F32)


def _params(sem=("arbitrary",)):
    return pltpu.CompilerParams(dimension_semantics=sem, vmem_limit_bytes=VMEM_LIMIT)


def _first_rows():
    return _layout()[0]["win"][0]


def _weight_copies(w_hbm, specs, sem):
    lay, _ = _layout()
    ra = _first_rows()
    cps = []
    for i, (name, dst) in enumerate(specs):
        off, rs = lay[name]
        off = off if off < ra else off - ra
        for k in range(NCHIP):
            cps.append(pltpu.make_async_copy(w_hbm.at[k, pl.ds(off, rs), :],
                                             dst.at[pl.ds(k * rs, rs), :], sem.at[i * NCHIP + k]))
    return cps


def _load_weights_once(w_hbm, specs, sem):
    @pl.when(pl.program_id(0) == 0)
    def _():
        cps = _weight_copies(w_hbm, specs, sem)
        for cp in cps:
            cp.start()
        for cp in cps:
            cp.wait()


def _grad_copies(stage, g_hbm, name, row0, rows, piece, sem, sem0):
    lay, _ = _layout()
    off, rs = lay[name]
    cps = []
    for i in range(rows // piece):
        rglob = row0 + i * piece
        k = rglob // rs
        loc = rglob - k * rs
        start = off + loc
        if not isinstance(start, int):
            start = pl.multiple_of(start, 16)
        dst = g_hbm.at[k, pl.ds(start, piece), :]
        cps.append(pltpu.make_async_copy(stage.at[pl.ds(i * piece, piece), :], dst, sem.at[sem0 + i]))
    return cps


def _row(v):
    return v.reshape(1, -1)


def _shifted_source(buf, sh, base, s, lanes):
    a, b = divmod(s, SUBLANES)
    rows = pl.ds(pl.multiple_of(base + SUBLANES * a, SUBLANES), SUBLANES)
    if b == 0:
        return buf[rows, lanes]
    return sh[b - 1, rows, :]


def _fill_shifted(buf, sh, lanes):
    rows = sh.shape[1]
    for b in range(1, SUBLANES):
        sh[b - 1, :, :] = buf[pl.ds(b, rows), lanes]


def _ffn_fwd(h, g, wfull, which):
    nt = T // TM
    fc = F // 2
    names = ("wg%d" % which, "wu%d" % which, "wd%d" % which)

    def body(h_ref, g_ref, w_hbm, ho_ref, a_ref, b_ref, wg, wu, wd, sem):
        _load_weights_once(w_hbm, ((names[0], wg), (names[1], wu), (names[2], wd)), sem)
        x = h_ref[...]
        r = lax.rsqrt(jnp.mean(x * x, axis=-1, keepdims=True) + RMS_EPS)
        n = (x * r * g_ref[...]).astype(BF)
        acc = jnp.zeros((TM, D), F32)
        for c in range(F // fc):
            sl = pl.ds(c * fc, fc)
            a = _dot_nt(n, wg[sl, :])
            b = _dot_nt(n, wu[sl, :])
            a_ref[:, sl] = a.astype(BF)
            b_ref[:, sl] = b.astype(BF)
            s = (a * _sig(a) * b).astype(BF)
            acc = acc + _dot_nn(s, wd[sl, :])
        ho_ref[...] = x + 0.5 * acc

    tile = lambda w: pl.BlockSpec((TM, w), lambda i: (i, 0))
    return pl.pallas_call(
        body, name="ffn_fwd", grid=(nt,),
        in_specs=[tile(D), pl.BlockSpec((1, D), lambda i: (0, 0)), ANY],
        out_specs=[tile(D), tile(F), tile(F)],
        out_shape=[jax.ShapeDtypeStruct((T, D), F32), jax.ShapeDtypeStruct((T, F), BF),
                   jax.ShapeDtypeStruct((T, F), BF)],
        scratch_shapes=[pltpu.VMEM((F, D), BF), pltpu.VMEM((F, D), BF), pltpu.VMEM((F, D), BF),
                        pltpu.SemaphoreType.DMA((3 * NCHIP,))],
        compiler_params=_params(),
    )(h, g, wfull)


def _mix_in_fwd(h, g, wfull):
    nt = T // TM
    nin = 5 * D

    def body(h_ref, g_ref, w_hbm, z_ref, win, sem):
        _load_weights_once(w_hbm, (("win", win),), sem)
        x = h_ref[...]
        r = lax.rsqrt(jnp.mean(x * x, axis=-1, keepdims=True) + RMS_EPS)
        u = (x * r * g_ref[...]).astype(BF)
        for c in range(5):
            sl = pl.ds(c * D, D)
            z_ref[:, sl] = _dot_nt(u, win[sl, :]).astype(BF)

    return pl.pallas_call(
        body, name="mix_in_fwd", grid=(nt,),
        in_specs=[pl.BlockSpec((TM, D), lambda i: (i, 0)), pl.BlockSpec((1, D), lambda i: (0, 0)), ANY],
        out_specs=pl.BlockSpec((TM, nin), lambda i: (i, 0)),
        out_shape=jax.ShapeDtypeStruct((T, nin), BF),
        scratch_shapes=[pltpu.VMEM((nin, D), BF), pltpu.SemaphoreType.DMA((NCHIP,))],
        compiler_params=_params(),
    )(h, g, wfull)


def _pool_counts(i, rows, w):
    t = i * TM + lax.broadcasted_iota(jnp.int32, (rows, 1), 0)
    return jnp.minimum(t + 1, w).astype(F32)


def _mix_mid_fwd(h, z, pwcat, pscale, wdw, bdw, lg, lb, wfull):
    nt = T // TM
    pg = D // NG
    hb = TM // HALO
    n_ext = HALO + TM
    pad = SUBLANES

    def body(h_ref, z_ref, zh_ref, pw_ref, ps_ref, wdw_ref, bdw_ref, lg_ref, lb_ref, w_hbm,
             h2_ref, p_ref, c1_ref, cc_ref, wco, wo, pa, pb, cbuf, sh, c1buf, ambuf, sem):
        i = pl.program_id(0)
        _load_weights_once(w_hbm, (("wco", wco), ("wo", wo)), sem)

        @pl.when(i == 0)
        def _():
            pa[0:pad, :] = jnp.zeros((pad, D), F32)
            pb[0:pad, :] = jnp.zeros((pad, D), F32)

        keep = (i > 0).astype(F32)
        zh = zh_ref[...].astype(F32) * keep
        za = z_ref[:, D:2 * D].astype(F32)
        zg = z_ref[:, 2 * D:3 * D].astype(F32)
        pa[pad:pad + HALO, :] = zh[:, 0:D]
        pa[pad + HALO:pad + n_ext, :] = z_ref[:, 0:D].astype(F32)
        cbuf[0:HALO, :] = zh[:, D:2 * D] * _sig(zh[:, 2 * D:3 * D])
        cbuf[HALO:n_ext, :] = za * _sig(zg)
        for g, w in enumerate(WINDOWS):
            ls = pl.ds(g * pg, pg)
            cur, nxt = pa, pb
            d = 1
            while d < w:
                nxt[pl.ds(pad, n_ext), ls] = cur[pl.ds(pad, n_ext), ls] + cur[pl.ds(pad - d, n_ext), ls]
                cur, nxt = nxt, cur
                d *= 2
            tok = z_ref[:, ls].astype(F32)
            pooled = (cur[pl.ds(pad + HALO, TM), ls] / _pool_counts(i, TM, w) - tok).astype(BF)
            p_ref[:, ls] = pooled
            ambuf[:, ls] = _dot_nn(pooled, pw_ref[:, ls])
        am = ambuf[...] * ps_ref[...]
        for l in range(D // LANES):
            lanes = pl.ds(l * LANES, LANES)
            _fill_shifted(cbuf, sh, lanes)
            bias = jnp.broadcast_to(bdw_ref[:, lanes], (SUBLANES, LANES))

            def conv_rows(r, carry):
                base = r * (CHUNKS * SUBLANES)
                accs = [bias] * CHUNKS
                for k in range(KC):
                    wk = jnp.broadcast_to(wdw_ref[k:k + 1, lanes], (SUBLANES, LANES))
                    for j in range(CHUNKS):
                        src = _shifted_source(cbuf, sh, base + j * SUBLANES, HALO - (KC - 1) + k, lanes)
                        accs[j] = accs[j] + wk * src
                for j in range(CHUNKS):
                    c1buf[pl.ds(pl.multiple_of(base + j * SUBLANES, SUBLANES), SUBLANES), lanes] = accs[j]
                return carry

            lax.fori_loop(0, TM // (CHUNKS * SUBLANES), conv_rows, 0)
        c1b = c1buf[...].astype(BF)
        c1_ref[...] = c1b
        c1 = c1b.astype(F32)
        mu = jnp.mean(c1, axis=-1, keepdims=True)
        xc = c1 - mu
        var = jnp.mean(xc * xc, axis=-1, keepdims=True)
        c2 = xc * lax.rsqrt(var + LN_EPS) * lg_ref[...] + lb_ref[...]
        c3 = (c2 * _sig(c2)).astype(BF)
        ccb = _dot_nn(c3, wco[...]).astype(BF)
        cc_ref[...] = ccb
        gp = z_ref[:, 3 * D:4 * D].astype(F32)
        gc = z_ref[:, 4 * D:5 * D].astype(F32)
        m = (_sig(gp) * am + _sig(gc) * ccb.astype(F32)).astype(BF)
        h2_ref[...] = h_ref[...] + _dot_nn(m, wo[...])

    tile = pl.BlockSpec((TM, D), lambda i: (i, 0))
    vec = pl.BlockSpec((1, D), lambda i: (0, 0))
    return pl.pallas_call(
        body, name="mix_mid_fwd", grid=(nt,),
        in_specs=[tile, pl.BlockSpec((TM, 5 * D), lambda i: (i, 0)),
                  pl.BlockSpec((HALO, 3 * D), lambda i: (jnp.maximum(i * hb - 1, 0), 0)),
                  pl.BlockSpec((pg, D), lambda i: (0, 0)), vec,
                  pl.BlockSpec((HALO, D), lambda i: (0, 0)), vec, vec, vec, ANY],
        out_specs=[tile, tile, tile, tile],
        out_shape=[jax.ShapeDtypeStruct((T, D), F32), jax.ShapeDtypeStruct((T, D), BF),
                   jax.ShapeDtypeStruct((T, D), BF), jax.ShapeDtypeStruct((T, D), BF)],
        scratch_shapes=[pltpu.VMEM((D, D), BF), pltpu.VMEM((D, D), BF),
                        pltpu.VMEM((pad + n_ext, D), F32), pltpu.VMEM((pad + n_ext, D), F32),
                        pltpu.VMEM((n_ext, D), F32), pltpu.VMEM((SUBLANES - 1, n_ext - SUBLANES, LANES), F32),
                        pltpu.VMEM((TM, D), F32), pltpu.VMEM((TM, D), F32),
                        pltpu.SemaphoreType.DMA((2 * NCHIP,))],
        compiler_params=_params(),
    )(h, z, z, pwcat, pscale, wdw, bdw, lg, lb, wfull)


def _ple_fwd(h, p, g, wppt, wfull):
    nt = T // TM

    def body(h_ref, p_ref, g_ref, wpp_ref, w_hbm, ho_ref, wpg, sem):
        _load_weights_once(w_hbm, (("wpg", wpg),), sem)
        x = h_ref[...]
        r = lax.rsqrt(jnp.mean(x * x, axis=-1, keepdims=True) + RMS_EPS)
        n = (x * r * g_ref[...]).astype(BF)
        gate = _sig(_dot_nn(n, wpg[...]))
        pe = _dot_nt(p_ref[...].astype(BF), wpp_ref[...])
        ho_ref[...] = x + gate * pe

    tile = pl.BlockSpec((TM, D), lambda i: (i, 0))
    return pl.pallas_call(
        body, name="ple_fwd", grid=(nt,),
        in_specs=[tile, pl.BlockSpec((TM, PD), lambda i: (i, 0)), pl.BlockSpec((1, D), lambda i: (0, 0)),
                  pl.BlockSpec((D, PD), lambda i: (0, 0)), ANY],
        out_specs=tile, out_shape=jax.ShapeDtypeStruct((T, D), F32),
        scratch_shapes=[pltpu.VMEM((D, D), BF), pltpu.SemaphoreType.DMA((NCHIP,))],
        compiler_params=_params(),
    )(h, p, g, wppt, wfull)


def _loss_bwd(h, tgt, g):
    nt = T // TM

    def body(h_ref, t_ref, g_ref, dh_ref, loss_ref, dg_ref):
        @pl.when(pl.program_id(0) == 0)
        def _():
            loss_ref[...] = jnp.zeros_like(loss_ref)
            dg_ref[...] = jnp.zeros_like(dg_ref)
        x = h_ref[...]
        r = lax.rsqrt(jnp.mean(x * x, axis=-1, keepdims=True) + RMS_EPS)
        xh = x * r
        gv = g_ref[...]
        e = xh * gv - t_ref[...]
        loss_ref[...] += jnp.sum(e * e, axis=0, keepdims=True) * (0.5 / D)
        dy = e * (1.0 / D)
        dg_ref[...] += jnp.sum(dy * xh, axis=0, keepdims=True)
        dxh = dy * gv
        dh_ref[...] = r * (dxh - xh * jnp.mean(dxh * xh, axis=-1, keepdims=True))

    tile = pl.BlockSpec((TM, D), lambda i: (i, 0))
    vec = pl.BlockSpec((1, D), lambda i: (0, 0))
    return pl.pallas_call(
        body, name="loss_bwd", grid=(nt,), in_specs=[tile, tile, vec], out_specs=[tile, vec, vec],
        out_shape=[jax.ShapeDtypeStruct((T, D), F32), jax.ShapeDtypeStruct((1, D), F32),
                   jax.ShapeDtypeStruct((1, D), F32)],
        compiler_params=_params(),
    )(h, tgt, g)


def _ple_bwd(h, dh, p, g, wppt, wfull):
    nt = T // TM
    _, rtot = _layout()

    def body(h_ref, dh_ref, p_ref, g_ref, wpp_ref, w_hbm,
             dho_ref, dg_ref, dwpp_ref, g_out, wpg, acc, stage, sem, osem):
        i = pl.program_id(0)
        _load_weights_once(w_hbm, (("wpg", wpg),), sem)

        @pl.when(i == 0)
        def _():
            dg_ref[...] = jnp.zeros_like(dg_ref)
            dwpp_ref[...] = jnp.zeros_like(dwpp_ref)
            acc[...] = jnp.zeros_like(acc)

        x = h_ref[...]
        r = lax.rsqrt(jnp.mean(x * x, axis=-1, keepdims=True) + RMS_EPS)
        xh = x * r
        gv = g_ref[...]
        n = (xh * gv).astype(BF)
        gate = _sig(_dot_nn(n, wpg[...]))
        pb = p_ref[...].astype(BF)
        pe = _dot_nt(pb, wpp_ref[...])
        d = dh_ref[...]
        dpe = (d * gate).astype(BF)
        dq = (d * pe * gate * (1.0 - gate)).astype(BF)
        dwpp_ref[...] += _dot_tn(dpe, pb)
        acc[...] += _dot_tn(n, dq)
        dn = _dot_nt(dq, wpg[...])
        dg_ref[...] += jnp.sum(dn * xh, axis=0, keepdims=True)
        dxh = dn * gv
        dho_ref[...] = d + r * (dxh - xh * jnp.mean(dxh * xh, axis=-1, keepdims=True))

        @pl.when(i == nt - 1)
        def _():
            stage[...] = acc[...].astype(BF)
            cps = _grad_copies(stage, g_out, "wpg", 0, D, D // NCHIP, osem, 0)
            for cp in cps:
                cp.start()
            for cp in cps:
                cp.wait()

    tile = pl.BlockSpec((TM, D), lambda i: (i, 0))
    vec = pl.BlockSpec((1, D), lambda i: (0, 0))
    return pl.pallas_call(
        body, name="ple_bwd", grid=(nt,),
        in_specs=[tile, tile, pl.BlockSpec((TM, PD), lambda i: (i, 0)), vec,
                  pl.BlockSpec((D, PD), lambda i: (0, 0)), ANY],
        out_specs=[tile, vec, pl.BlockSpec((D, PD), lambda i: (0, 0)), ANY],
        out_shape=[jax.ShapeDtypeStruct((T, D), F32), jax.ShapeDtypeStruct((1, D), F32),
                   jax.ShapeDtypeStruct((D, PD), F32),
                   jax.ShapeDtypeStruct((NCHIP, rtot, D), BF)],
        scratch_shapes=[pltpu.VMEM((D, D), BF), pltpu.VMEM((D, D), F32), pltpu.VMEM((D, D), BF),
                        pltpu.SemaphoreType.DMA((NCHIP,)), pltpu.SemaphoreType.DMA((NCHIP,))],
        compiler_params=_params(),
    )(h, dh, p, g, wppt, wfull)


def _ffn_bwd(h, dh, a, b, g, wfull, which):
    tm = TMB
    nt = T // tm
    fc = F // 2
    names = ("wg%d" % which, "wu%d" % which, "wd%d" % which)

    def body(h_ref, dh_ref, a_ref, b_ref, g_ref, w_hbm,
             dho_ref, da_ref, db_ref, s_ref, n_ref, dg_ref, wg, wu, wd, sem):
        _load_weights_once(w_hbm, ((names[0], wg), (names[1], wu), (names[2], wd)), sem)

        @pl.when(pl.program_id(0) == 0)
        def _():
            dg_ref[...] = jnp.zeros_like(dg_ref)

        x = h_ref[...]
        r = lax.rsqrt(jnp.mean(x * x, axis=-1, keepdims=True) + RMS_EPS)
        xh = x * r
        gv = g_ref[...]
        n_ref[...] = (xh * gv).astype(BF)
        d = dh_ref[...]
        df = (0.5 * d).astype(BF)
        dn = jnp.zeros((tm, D), F32)
        for c in range(F // fc):
            sl = pl.ds(c * fc, fc)
            av = a_ref[:, sl].astype(F32)
            bv = b_ref[:, sl].astype(F32)
            ds = _dot_nt(df, wd[sl, :])
            sg = _sig(av)
            sil = av * sg
            s_ref[:, sl] = (sil * bv).astype(BF)
            da = (ds * bv * (sg * (1.0 + av * (1.0 - sg)))).astype(BF)
            db = (ds * sil).astype(BF)
            da_ref[:, sl] = da
            db_ref[:, sl] = db
            dn = dn + _dot_nn(da, wg[sl, :]) + _dot_nn(db, wu[sl, :])
        dg_ref[...] += jnp.sum(dn * xh, axis=0, keepdims=True)
        dxh = dn * gv
        dho_ref[...] = d + r * (dxh - xh * jnp.mean(dxh * xh, axis=-1, keepdims=True))

    tile = lambda w: pl.BlockSpec((tm, w), lambda i: (i, 0))
    vec = pl.BlockSpec((1, D), lambda i: (0, 0))
    return pl.pallas_call(
        body, name="ffn_bwd", grid=(nt,),
        in_specs=[tile(D), tile(D), tile(F), tile(F), vec, ANY],
        out_specs=[tile(D), tile(F), tile(F), tile(F), tile(D), vec],
        out_shape=[jax.ShapeDtypeStruct((T, D), F32), jax.ShapeDtypeStruct((T, F), BF),
                   jax.ShapeDtypeStruct((T, F), BF), jax.ShapeDtypeStruct((T, F), BF),
                   jax.ShapeDtypeStruct((T, D), BF), jax.ShapeDtypeStruct((1, D), F32)],
        scratch_shapes=[pltpu.VMEM((F, D), BF), pltpu.VMEM((F, D), BF), pltpu.VMEM((F, D), BF),
                        pltpu.SemaphoreType.DMA((3 * NCHIP,))],
        compiler_params=_params(),
    )(h, dh, a, b, g, wfull)


def _wgrad(xs, y, gbuf, names, row0, rb, piece, yscale=None):
    nx = len(xs)
    rx = xs[0].shape[1]
    nj = rx // rb
    nt = T // TMW
    npiece = rb // piece

    def body(*refs):
        x_refs = refs[:nx]
        y_ref = refs[nx]
        g_out = refs[nx + 2]
        accs = refs[nx + 3:2 * nx + 3]
        stages = refs[2 * nx + 3:3 * nx + 3]
        osem = refs[3 * nx + 3]
        j = pl.program_id(0)
        t = pl.program_id(1)

        @pl.when(t == 0)
        def _():
            for acc in accs:
                acc[...] = jnp.zeros_like(acc)

        yv = y_ref[...]
        if yscale is not None:
            yv = (yscale * yv).astype(BF)
        for x_ref, acc in zip(x_refs, accs):
            acc[...] += _dot_tn(x_ref[...], yv)

        @pl.when(t == nt - 1)
        def _():
            cps = []
            for xi in range(nx):
                stages[xi][...] = accs[xi][...].astype(BF)
                cps += _grad_copies(stages[xi], g_out, names[xi], row0 + j * rb, rb, piece,
                                    osem, xi * npiece)
            for cp in cps:
                cp.start()
            for cp in cps:
                cp.wait()

    in_specs = [pl.BlockSpec((TMW, rb), lambda j, t: (t, j)) for _ in xs]
    in_specs += [pl.BlockSpec((TMW, D), lambda j, t: (t, 0)), ANY]
    return pl.pallas_call(
        body, name="wgrad", grid=(nj, nt), in_specs=in_specs, out_specs=ANY,
        out_shape=jax.ShapeDtypeStruct(gbuf.shape, gbuf.dtype),
        scratch_shapes=([pltpu.VMEM((rb, D), F32) for _ in xs] + [pltpu.VMEM((rb, D), BF) for _ in xs]
                        + [pltpu.SemaphoreType.DMA((nx * npiece,))]),
        input_output_aliases={nx + 1: 0},
        compiler_params=_params(("arbitrary", "arbitrary")),
    )(*xs, y, gbuf)


def _mix_b1(dh, z, pooled, c1, cc, pwcat, pscale, lg, lb, wfull, gbuf):
    nt = T // TM
    pg = D // NG

    def body(dh_ref, gp_ref, gc_ref, p_ref, c1_ref, cc_ref, pw_ref, ps_ref, lg_ref, lb_ref, w_hbm, _g_in,
             dp_ref, dc1_ref, dzb_ref, small_ref, dpw_ref, g_out,
             wco, wo, acc_o, acc_co, stage_o, stage_co, qbuf, sem, osem):
        i = pl.program_id(0)
        _load_weights_once(w_hbm, (("wco", wco), ("wo", wo)), sem)

        @pl.when(i == 0)
        def _():
            small_ref[...] = jnp.zeros_like(small_ref)
            dpw_ref[...] = jnp.zeros_like(dpw_ref)
            acc_o[...] = jnp.zeros_like(acc_o)
            acc_co[...] = jnp.zeros_like(acc_co)

        dhb = dh_ref[...].astype(BF)
        dm = _dot_nt(dhb, wo[...])
        sp = _sig(gp_ref[...].astype(F32))
        sc = _sig(gc_ref[...].astype(F32))
        for g in range(NG):
            ls = pl.ds(g * pg, pg)
            qbuf[:, ls] = _dot_nn(p_ref[:, ls], pw_ref[:, ls])
        q = qbuf[...]
        psv = ps_ref[...]
        am = q * psv
        ccv = cc_ref[...].astype(F32)
        m = (sp * am + sc * ccv).astype(BF)
        acc_o[...] += _dot_tn(m, dhb)
        dam = dm * sp
        dzb_ref[:, 0:D] = (dm * am * sp * (1.0 - sp)).astype(BF)
        dccb = (dm * sc).astype(BF)
        dzb_ref[:, D:2 * D] = (dm * ccv * sc * (1.0 - sc)).astype(BF)
        small_ref[0:1, :] += jnp.sum(dam * q, axis=0, keepdims=True)
        dq = (dam * psv).astype(BF)
        for g in range(NG):
            ls = pl.ds(g * pg, pg)
            dqg = dq[:, g * pg:(g + 1) * pg]
            dp_ref[:, ls] = _dot_nt(dqg, pw_ref[:, ls]).astype(BF)
            dpw_ref[:, ls] += _dot_tn(p_ref[:, ls], dqg)
        c1v = c1_ref[...].astype(F32)
        mu = jnp.mean(c1v, axis=-1, keepdims=True)
        xc = c1v - mu
        var = jnp.mean(xc * xc, axis=-1, keepdims=True)
        rs = lax.rsqrt(var + LN_EPS)
        c2n = xc * rs
        lgv = lg_ref[...]
        c2 = c2n * lgv + lb_ref[...]
        sg2 = _sig(c2)
        c3 = (c2 * sg2).astype(BF)
        acc_co[...] += _dot_tn(c3, dccb)
        dc3 = _dot_nt(dccb, wco[...])
        dc2 = dc3 * (sg2 * (1.0 + c2 * (1.0 - sg2)))
        small_ref[2:3, :] += jnp.sum(dc2 * c2n, axis=0, keepdims=True)
        small_ref[3:4, :] += jnp.sum(dc2, axis=0, keepdims=True)
        dc2n = dc2 * lgv
        dc1 = rs * (dc2n - jnp.mean(dc2n, axis=-1, keepdims=True)
                    - c2n * jnp.mean(dc2n * c2n, axis=-1, keepdims=True))
        small_ref[1:2, :] += jnp.sum(dc1, axis=0, keepdims=True)
        dc1_ref[...] = dc1.astype(BF)

        @pl.when(i == nt - 1)
        def _():
            stage_o[...] = acc_o[...].astype(BF)
            stage_co[...] = acc_co[...].astype(BF)
            cps = _grad_copies(stage_o, g_out, "wo", 0, D, D // NCHIP, osem, 0)
            cps += _grad_copies(stage_co, g_out, "wco", 0, D, D // NCHIP, osem, NCHIP)
            for cp in cps:
                cp.start()
            for cp in cps:
                cp.wait()

    tile = pl.BlockSpec((TM, D), lambda i: (i, 0))
    vec = pl.BlockSpec((1, D), lambda i: (0, 0))
    full = lambda r: pl.BlockSpec((r, D), lambda i: (0, 0))
    return pl.pallas_call(
        body, name="mix_b1", grid=(nt,),
        in_specs=[tile, pl.BlockSpec((TM, D), lambda i: (i, 3)), pl.BlockSpec((TM, D), lambda i: (i, 4)),
                  tile, tile, tile, full(pg), vec, vec, vec, ANY, ANY],
        out_specs=[tile, tile, pl.BlockSpec((TM, 2 * D), lambda i: (i, 0)), full(8), full(pg), ANY],
        out_shape=[jax.ShapeDtypeStruct((T, D), BF), jax.ShapeDtypeStruct((T, D), BF),
                   jax.ShapeDtypeStruct((T, 2 * D), BF), jax.ShapeDtypeStruct((8, D), F32),
                   jax.ShapeDtypeStruct((pg, D), F32), jax.ShapeDtypeStruct(gbuf.shape, gbuf.dtype)],
        scratch_shapes=[pltpu.VMEM((D, D), BF), pltpu.VMEM((D, D), BF),
                        pltpu.VMEM((D, D), F32), pltpu.VMEM((D, D), F32),
                        pltpu.VMEM((D, D), BF), pltpu.VMEM((D, D), BF),
                        pltpu.VMEM((TM, D), F32),
                        pltpu.SemaphoreType.DMA((2 * NCHIP,)), pltpu.SemaphoreType.DMA((2 * NCHIP,))],
        input_output_aliases={11: 5},
        compiler_params=_params(),
    )(dh, z, z, pooled, c1, cc, pwcat, pscale, lg, lb, wfull, gbuf)


def _mix_b2(dp, dc1, z, wdw):
    nt = T // TM
    pg = D // NG
    hb = TM // HALO
    nhb = T // HALO
    n_ext = TM + HALO
    pad = SUBLANES

    def body(dp_ref, dpn_ref, dc_ref, dcn_ref, za_ref, zg_ref, wdw_ref,
             dza_ref, dw_ref, pa, pb, cbuf, sh, c0buf, dc0buf):
        i = pl.program_id(0)

        @pl.when(i == 0)
        def _():
            dw_ref[...] = jnp.zeros_like(dw_ref)
            pa[n_ext:n_ext + pad, :] = jnp.zeros((pad, D), F32)
            pb[n_ext:n_ext + pad, :] = jnp.zeros((pad, D), F32)

        more = (i < nt - 1).astype(F32)
        for g, w in enumerate(WINDOWS):
            ls = pl.ds(g * pg, pg)
            cur_dp = dp_ref[:, ls].astype(F32)
            pa[0:TM, ls] = cur_dp / _pool_counts(i, TM, w)
            pa[TM:n_ext, ls] = dpn_ref[:, ls].astype(F32) * (more / w)
            cur, nxt = pa, pb
            d = 1
            while d < w:
                nxt[pl.ds(0, n_ext), ls] = cur[pl.ds(0, n_ext), ls] + cur[pl.ds(d, n_ext), ls]
                cur, nxt = nxt, cur
                d *= 2
            dza_ref[:, ls] = (cur[pl.ds(0, TM), ls] - cur_dp).astype(BF)
        za = za_ref[...].astype(F32)
        sg = _sig(zg_ref[...].astype(F32))
        cbuf[0:TM, :] = dc_ref[...].astype(F32)
        cbuf[TM:n_ext, :] = dcn_ref[...].astype(F32) * more
        c0buf[...] = za * sg
        for l in range(D // LANES):
            lanes = pl.ds(l * LANES, LANES)
            _fill_shifted(cbuf, sh, lanes)

            def conv_rows(r, accs):
                base = r * (CHUNKS * SUBLANES)
                rows = [pl.ds(pl.multiple_of(base + j * SUBLANES, SUBLANES), SUBLANES) for j in range(CHUNKS)]
                c0v = [c0buf[rows[j], lanes] for j in range(CHUNKS)]
                acc = [jnp.zeros((SUBLANES, LANES), F32)] * CHUNKS
                new = list(accs)
                for k in range(KC):
                    wk = jnp.broadcast_to(wdw_ref[k:k + 1, lanes], (SUBLANES, LANES))
                    for j in range(CHUNKS):
                        src = _shifted_source(cbuf, sh, base + j * SUBLANES, KC - 1 - k, lanes)
                        acc[j] = acc[j] + wk * src
                        new[k] = new[k] + c0v[j] * src
                for j in range(CHUNKS):
                    dc0buf[rows[j], lanes] = acc[j]
                return tuple(new)

            init = tuple(jnp.zeros((SUBLANES, LANES), F32) for _ in range(KC))
            accs = lax.fori_loop(0, TM // (CHUNKS * SUBLANES), conv_rows, init)
            for k in range(KC):
                dw_ref[k:k + 1, lanes] += jnp.sum(accs[k], axis=0, keepdims=True)
        dc0 = dc0buf[...]
        dza_ref[:, D:2 * D] = (dc0 * sg).astype(BF)
        dza_ref[:, 2 * D:3 * D] = (dc0 * za * sg * (1.0 - sg)).astype(BF)

    tile = pl.BlockSpec((TM, D), lambda i: (i, 0))
    nxt_spec = pl.BlockSpec((HALO, D), lambda i: (jnp.minimum((i + 1) * hb, nhb - 1), 0))
    return pl.pallas_call(
        body, name="mix_b2", grid=(nt,),
        in_specs=[tile, nxt_spec, tile, nxt_spec, pl.BlockSpec((TM, D), lambda i: (i, 1)),
                  pl.BlockSpec((TM, D), lambda i: (i, 2)), pl.BlockSpec((HALO, D), lambda i: (0, 0))],
        out_specs=[pl.BlockSpec((TM, 3 * D), lambda i: (i, 0)), pl.BlockSpec((HALO, D), lambda i: (0, 0))],
        out_shape=[jax.ShapeDtypeStruct((T, 3 * D), BF), jax.ShapeDtypeStruct((HALO, D), F32)],
        scratch_shapes=[pltpu.VMEM((n_ext + pad, D), F32), pltpu.VMEM((n_ext + pad, D), F32),
                        pltpu.VMEM((n_ext, D), F32), pltpu.VMEM((SUBLANES - 1, n_ext - SUBLANES, LANES), F32),
                        pltpu.VMEM((TM, D), F32), pltpu.VMEM((TM, D), F32)],
        compiler_params=_params(),
    )(dp, dp, dc1, dc1, z, z, wdw)


def _mix_b3(h, dh, dza, dzb, g, wfull):
    nt = T // TM

    def body(h_ref, dh_ref, dza_ref, dzb_ref, g_ref, w_hbm, dho_ref, u_ref, dg_ref, win, sem):
        _load_weights_once(w_hbm, (("win", win),), sem)

        @pl.when(pl.program_id(0) == 0)
        def _():
            dg_ref[...] = jnp.zeros_like(dg_ref)

        x = h_ref[...]
        r = lax.rsqrt(jnp.mean(x * x, axis=-1, keepdims=True) + RMS_EPS)
        xh = x * r
        gv = g_ref[...]
        u_ref[...] = (xh * gv).astype(BF)
        du = _dot_nn(dza_ref[...], win[0:3 * D, :]) + _dot_nn(dzb_ref[...], win[3 * D:5 * D, :])
        dg_ref[...] += jnp.sum(du * xh, axis=0, keepdims=True)
        dxh = du * gv
        dho_ref[...] = dh_ref[...] + r * (dxh - xh * jnp.mean(dxh * xh, axis=-1, keepdims=True))

    tile = lambda w: pl.BlockSpec((TM, w), lambda i: (i, 0))
    vec = pl.BlockSpec((1, D), lambda i: (0, 0))
    return pl.pallas_call(
        body, name="mix_b3", grid=(nt,),
        in_specs=[tile(D), tile(D), tile(3 * D), tile(2 * D), vec, ANY],
        out_specs=[tile(D), tile(D), vec],
        out_shape=[jax.ShapeDtypeStruct((T, D), F32), jax.ShapeDtypeStruct((T, D), BF),
                   jax.ShapeDtypeStruct((1, D), F32)],
        scratch_shapes=[pltpu.VMEM((5 * D, D), BF), pltpu.SemaphoreType.DMA((NCHIP,))],
        compiler_params=_params(),
    )(h, dh, dza, dzb, g, wfull)


def _mesh_pos():
    x, y, c = lax.axis_index("x"), lax.axis_index("y"), lax.axis_index("c")
    chips = [(1 - x, y), (x, 1 - y), (1 - x, 1 - y)]
    return x, y, c, 2 * x + y, chips


def _handshake(peers):
    barrier = pltpu.get_barrier_semaphore()
    for peer in peers:
        pl.semaphore_signal(barrier, inc=1, device_id=peer, device_id_type=MESH)
    pl.semaphore_wait(barrier, len(peers))


def _run_comm(body, name, cid, ins, inouts, out_types, scratch):
    in_refs = [jax.new_ref(a, memory_space=HBM) for a in ins]
    inout_refs = [jax.new_ref(a, memory_space=HBM) for a in inouts]
    out_refs = [jax.empty_ref(t, memory_space=HBM) for t in out_types]

    @pl.kernel(mesh=plsc.ScalarSubcoreMesh(axis_name="seq", num_cores=1), name=name,
               scratch_types=scratch, compiler_params=pltpu.CompilerParams(collective_id=cid))
    def launch(*scr):
        body(*in_refs, *inout_refs, *out_refs, *scr)

    launch()
    return [r[...] for r in inout_refs + out_refs]


def _remote(src, dst, send_sem, recv_sem, to):
    return pltpu.make_async_remote_copy(src_ref=src, dst_ref=dst, send_sem=send_sem, recv_sem=recv_sem,
                                        device_id=to, device_id_type=MESH)


def _gather_layer(packed, cid):
    rtot = packed.shape[0]
    half = rtot // 2
    kme = 2 * lax.axis_index("x") + lax.axis_index("y")
    landing = lax.dynamic_update_slice(lax.empty((NCHIP, rtot, D), BF), packed[None], (kme, 0, 0))

    def body(p_ref, w_ref, send_sems, recv_sems):
        x, y, c, kme, chips = _mesh_pos()
        sib = (x, y, 1 - c)
        _handshake([(*ch, c) for ch in chips] + [sib])
        ks = [2 * cx + cy for cx, cy in chips]
        mine = pl.ds(pl.multiple_of(c * half, 16), half)
        other = pl.ds(pl.multiple_of((1 - c) * half, 16), half)
        first = [_remote(p_ref.at[mine], w_ref.at[kme, mine], send_sems.at[j], recv_sems.at[j], (*chips[j], c))
                 for j in range(3)]
        for cp in first:
            cp.start()
        passed = [_remote(w_ref.at[ks[j], mine], w_ref.at[ks[j], mine], send_sems.at[3 + j], recv_sems.at[3 + j], sib)
                  for j in range(3)]
        for j in range(3):
            _remote(p_ref.at[mine], w_ref.at[ks[j], mine], send_sems.at[j], recv_sems.at[j], sib).wait_recv()
            passed[j].start()
        for j in range(3):
            _remote(p_ref.at[mine], w_ref.at[ks[j], other], send_sems.at[3 + j], recv_sems.at[3 + j], sib).wait_recv()
        for cp in first + passed:
            cp.wait_send()

    return _run_comm(body, "gather_layer_%d" % cid, cid, [packed], [landing], [],
                     (pltpu.SemaphoreType.DMA((6,)), pltpu.SemaphoreType.DMA((6,))))[0]


def _sibling_swap(gbuf, cid):
    rtot = gbuf.shape[1]
    half = rtot // 2

    def body(g_ref, r_ref, send_sem, recv_sem):
        x, y, c, _, _ = _mesh_pos()
        sib = (x, y, 1 - c)
        _handshake([sib])
        other = pl.ds(pl.multiple_of((1 - c) * half, 16), half)
        cp = _remote(g_ref.at[:, other, :], r_ref, send_sem, recv_sem, sib)
        cp.start()
        cp.wait()

    return _run_comm(body, "sibling_swap_%d" % cid, cid, [gbuf], [],
                     [jax.ShapeDtypeStruct((NCHIP, half, D), BF)],
                     (pltpu.SemaphoreType.DMA, pltpu.SemaphoreType.DMA))[0]


def _row_tile(rows):
    for cand in range(min(rows, 1280) // 16 * 16, 0, -16):
        if rows % cand == 0:
            return cand
    return rows


def _chip_sum(gbuf, rbuf, cidx):
    half = rbuf.shape[1]
    rt = _row_tile(half)
    nb = half // rt

    def body(c_ref, g_ref, r_ref, o_ref):
        o_ref[...] = (g_ref[...].astype(F32) + r_ref[...].astype(F32)).astype(BF)

    return pl.pallas_call(
        body, name="chip_sum",
        grid_spec=pltpu.PrefetchScalarGridSpec(
            num_scalar_prefetch=1, grid=(NCHIP, nb),
            in_specs=[pl.BlockSpec((None, rt, D), lambda k, r, c: (k, c[0] * nb + r, 0)),
                      pl.BlockSpec((None, rt, D), lambda k, r, c: (k, r, 0))],
            out_specs=pl.BlockSpec((None, rt, D), lambda k, r, c: (k, r, 0))),
        out_shape=jax.ShapeDtypeStruct((NCHIP, half, D), BF),
        compiler_params=_params(("arbitrary", "arbitrary")),
    )(cidx, gbuf, rbuf)


def _chip_exchange(sbuf, cid):
    half = sbuf.shape[1]

    def body(s_ref, x_ref, send_sems, recv_sems):
        x, y, c, _, chips = _mesh_pos()
        _handshake([(*ch, c) for ch in chips])
        cps = [_remote(s_ref.at[2 * cx + cy], x_ref.at[j], send_sems.at[j], recv_sems.at[j], (cx, cy, c))
               for j, (cx, cy) in enumerate(chips)]
        for cp in cps:
            cp.start()
        for cp in cps:
            cp.wait()

    return _run_comm(body, "chip_exchange_%d" % cid, cid, [sbuf], [],
                     [jax.ShapeDtypeStruct((3, half, D), BF)],
                     (pltpu.SemaphoreType.DMA((3,)), pltpu.SemaphoreType.DMA((3,))))[0]


def _shard_sum(gbuf, rbuf, xbuf, ck):
    half = rbuf.shape[1]
    rt = _row_tile(half)
    nb = half // rt

    def body(ck_ref, g_ref, r_ref, x_ref, o_ref):
        acc = g_ref[...].astype(F32) + r_ref[...].astype(F32)
        for j in range(3):
            acc = acc + x_ref[j].astype(F32)
        o_ref[...] = acc

    return pl.pallas_call(
        body, name="shard_sum",
        grid_spec=pltpu.PrefetchScalarGridSpec(
            num_scalar_prefetch=1, grid=(nb,),
            in_specs=[pl.BlockSpec((None, rt, D), lambda r, ck: (ck[1], ck[0] * nb + r, 0)),
                      pl.BlockSpec((None, rt, D), lambda r, ck: (ck[1], r, 0)),
                      pl.BlockSpec((3, rt, D), lambda r, ck: (0, r, 0))],
            out_specs=pl.BlockSpec((rt, D), lambda r, ck: (ck[0] * nb + r, 0))),
        out_shape=jax.ShapeDtypeStruct((2 * half, D), F32),
        compiler_params=_params(),
    )(ck, gbuf, rbuf, xbuf)


def _sibling_share(red, cid):
    half = red.shape[0] // 2

    def body(o_ref, send_sem, recv_sem):
        x, y, c, _, _ = _mesh_pos()
        sib = (x, y, 1 - c)
        _handshake([sib])
        mine = pl.ds(pl.multiple_of(c * half, 8), half)
        other = pl.ds(pl.multiple_of((1 - c) * half, 8), half)
        cp = _remote(o_ref.at[mine], o_ref.at[mine], send_sem, recv_sem, sib)
        cp.start()
        cp.wait_send()
        _remote(o_ref.at[mine], o_ref.at[other], send_sem, recv_sem, sib).wait_recv()

    return _run_comm(body, "sibling_share_%d" % cid, cid, [], [red], [],
                     (pltpu.SemaphoreType.DMA, pltpu.SemaphoreType.DMA))[0]


def _allreduce_small(v):
    rows = v.shape[0]
    ndev = 2 * NCHIP

    def body(v_ref, o_ref, gat, send_sems, recv_sems, lsem):
        x, y, c, _, chips = _mesh_pos()
        me, sib = (x, y, c), (x, y, 1 - c)

        def blk(px, py, pc):
            return gat.at[pl.ds((4 * px + 2 * py + pc) * rows, rows), :]

        def copy(k, block, to, src=None):
            return pltpu.make_async_remote_copy(
                src_ref=blk(*block) if src is None else src, dst_ref=blk(*block),
                send_sem=send_sems.at[k], recv_sem=recv_sems.at[k], device_id=to, device_id_type=MESH)

        mine = pltpu.make_async_copy(v_ref, blk(*me), lsem)
        mine.start()
        first = [copy(0, me, sib, src=v_ref)]
        first += [copy(1 + j, me, (*chip, c), src=v_ref) for j, chip in enumerate(chips)]
        for cp in first:
            cp.start()
        passed = [copy(4 + j, (*chip, c), sib) for j, chip in enumerate(chips)]
        for j, chip in enumerate(chips):
            copy(1 + j, (*chip, c), me).wait_recv()
            passed[j].start()
        copy(0, sib, me).wait_recv()
        for j, chip in enumerate(chips):
            copy(4 + j, (*chip, 1 - c), me).wait_recv()
        for cp in first + passed:
            cp.wait_send()
        mine.wait()
        acc = gat[0:rows, :]
        for d in range(1, ndev):
            acc = acc + gat[d * rows:(d + 1) * rows, :]
        o_ref[...] = acc

    vm = pl.BlockSpec(memory_space=pltpu.VMEM)
    return pl.pallas_call(
        body, name="allreduce_small", in_specs=[vm], out_specs=vm,
        out_shape=jax.ShapeDtypeStruct((rows, D), F32),
        scratch_shapes=[pltpu.VMEM((ndev * rows, D), F32), pltpu.SemaphoreType.DMA((7,)),
                        pltpu.SemaphoreType.DMA((7,)), pltpu.SemaphoreType.DMA],
    )(v)


def _adamw(w, g, m, v):
    shape = w.shape
    cols = shape[-1]
    rows = w.size // cols
    bm = rows
    for cand in (512, 256, 128, 64, 32, 16, 8):
        if rows % cand == 0:
            bm = cand
            break
    bc1 = 1.0 - ADAM_B1 ** ADAM_STEP
    bc2 = 1.0 - ADAM_B2 ** ADAM_STEP

    def body(w_ref, g_ref, m_ref, v_ref, d_ref, mo_ref, vo_ref):
        gv = g_ref[...]
        mn = ADAM_B1 * m_ref[...] + (1.0 - ADAM_B1) * gv
        vn = ADAM_B2 * v_ref[...] + (1.0 - ADAM_B2) * (gv * gv)
        mo_ref[...] = mn
        vo_ref[...] = vn
        d_ref[...] = -ADAM_LR * ((mn / bc1) / (jnp.sqrt(vn / bc2) + ADAM_EPS) + ADAM_WD * w_ref[...])

    spec = pl.BlockSpec((bm, cols), lambda i: (i, 0))
    out = jax.ShapeDtypeStruct((rows, cols), F32)
    d, mo, vo = pl.pallas_call(
        body, name="adamw", grid=(rows // bm,), in_specs=[spec] * 4, out_specs=[spec] * 3,
        out_shape=[out, out, out], compiler_params=_params(),
    )(*[t.reshape(rows, cols) for t in (w, g, m, v)])
    return d.reshape(shape), mo.reshape(shape), vo.reshape(shape)


def _pack_shards(ws, li):
    pg = D // NG
    t = lambda a: jnp.swapaxes(a[li], 0, 1)
    parts = [t(ws["ffn1_w_gate"]), t(ws["ffn1_w_up"]), ws["ffn1_w_down"][li],
             t(ws["w_in"]), ws["conv_w_out"][li], ws["w_out"][li],
             t(ws["ffn2_w_gate"]), t(ws["ffn2_w_up"]), ws["ffn2_w_down"][li], ws["ple_w_gate"][li],
             t(ws["ple_w_proj"]).reshape(-1, D),
             jnp.swapaxes(ws["pool_w"][li], 0, 1).reshape(pg // NCHIP, D)]
    return jnp.concatenate([p.astype(BF) for p in parts], axis=0)


def _unpack_shards(parts):
    lay, _ = _layout()
    pg = D // NG

    def rows(n):
        off, rs = lay[n]
        for r0, arr in parts:
            if r0 <= off and off + rs <= r0 + arr.shape[0]:
                return arr[off - r0:off - r0 + rs, :]
        raise ValueError(n)

    t = lambda a: jnp.swapaxes(a, 0, 1)
    return {
        "ffn1_w_gate": t(rows("wg1")), "ffn1_w_up": t(rows("wu1")), "ffn1_w_down": rows("wd1"),
        "ffn2_w_gate": t(rows("wg2")), "ffn2_w_up": t(rows("wu2")), "ffn2_w_down": rows("wd2"),
        "w_in": t(rows("win")), "conv_w_out": rows("wco"), "w_out": rows("wo"), "ple_w_gate": rows("wpg"),
        "ple_w_proj": t(rows("wpp").reshape(D // NCHIP, PD)),
        "pool_w": jnp.swapaxes(rows("pw").reshape(pg // NCHIP, NG, pg), 0, 1),
    }


_BIG = ("ffn1_w_gate", "ffn1_w_up", "ffn1_w_down", "w_in", "pool_w", "conv_w_out", "w_out",
        "ffn2_w_gate", "ffn2_w_up", "ffn2_w_down", "ple_w_gate", "ple_w_proj")
_VECS = ("ffn1_norm", "mix_norm", "pool_scale", "conv_dw_b", "conv_ln_g", "conv_ln_b", "ffn2_norm", "ple_norm")
_WEIGHTS = ("ffn1_norm", "ffn1_w_gate", "ffn1_w_up", "ffn1_w_down", "mix_norm", "w_in", "pool_w", "pool_scale",
            "conv_dw_w", "conv_dw_b", "conv_ln_g", "conv_ln_b", "conv_w_out", "w_out", "ffn2_norm",
            "ffn2_w_gate", "ffn2_w_up", "ffn2_w_down", "ple_norm", "ple_w_gate", "ple_w_proj", "final_norm")


def _step(x, p, tgt, ws, ms, vs):
    lay, rtot = _layout()
    pg = D // NG
    cpos = lax.axis_index("c")
    kme = 2 * lax.axis_index("x") + lax.axis_index("y")
    cidx = jnp.stack([cpos]).astype(jnp.int32)
    ck = jnp.stack([cpos, kme]).astype(jnp.int32)
    h = x.reshape(T, D)
    tgt = tgt.reshape(T, D)

    nfirst = _first_rows()
    packed = [_pack_shards(ws, li) for li in range(DEPTH)]
    wfirst, wrest = [None] * DEPTH, [None] * DEPTH
    wfirst[0] = _gather_layer(packed[0][:nfirst], 0)
    piece, wfirst[0] = lax.optimization_barrier((packed[0][nfirst:], wfirst[0]))
    wrest[0] = _gather_layer(piece, 1)
    wppt, pwcat = [None] * DEPTH, [None] * DEPTH
    kk = ws["conv_dw_w"].shape[1]
    wdw_mine = jnp.zeros((DEPTH * HALO, D), F32)
    for li in range(DEPTH):
        blockw = jnp.zeros((kk, D), F32)
        mine = jnp.where(cpos == 0, ws["conv_dw_w"][li], 0.0)
        blockw = lax.dynamic_update_slice(blockw, mine, (0, kme * (D // NCHIP)))
        wdw_mine = wdw_mine.at[li * HALO:li * HALO + kk, :].set(blockw)
    wdw_all = _allreduce_small(wdw_mine)
    wdw = [wdw_all[li * HALO:(li + 1) * HALO, :] for li in range(DEPTH)]
    vec = lambda name, li: _row(ws[name][li])

    saved = []
    for li in range(DEPTH):
        if li > 0:
            wfirst[li], wrest[li], h = lax.optimization_barrier((wfirst[li], wrest[li], h))
        h0 = h
        h1, a1, b1 = _ffn_fwd(h0, vec("ffn1_norm", li), wfirst[li], 1)
        if li + 1 < DEPTH:
            piece, wrest[li], h1 = lax.optimization_barrier((packed[li + 1][:nfirst], wrest[li], h1))
            wfirst[li + 1] = _gather_layer(piece, 2 * li + 2)
        else:
            wrest[li], h1 = lax.optimization_barrier((wrest[li], h1))
        o, s = lay["wpp"]
        wppt[li] = wrest[li][:, o - nfirst:o - nfirst + s, :].reshape(D, PD)
        o, s = lay["pw"]
        pwcat[li] = wrest[li][:, o - nfirst:o - nfirst + s, :].reshape(pg, D)
        z = _mix_in_fwd(h1, vec("mix_norm", li), wrest[li])
        h2, pooled, c1, cc = _mix_mid_fwd(h1, z, pwcat[li], vec("pool_scale", li), wdw[li],
                                          vec("conv_dw_b", li), vec("conv_ln_g", li), vec("conv_ln_b", li),
                                          wrest[li])
        if li + 1 < DEPTH:
            piece, wfirst[li + 1], h2 = lax.optimization_barrier((packed[li + 1][nfirst:], wfirst[li + 1], h2))
            wrest[li + 1] = _gather_layer(piece, 2 * li + 3)
        h3, a2, b2 = _ffn_fwd(h2, vec("ffn2_norm", li), wrest[li], 2)
        h = _ple_fwd(h3, p[li, 0], vec("ple_norm", li), wppt[li], wrest[li])
        saved.append((h0, a1, b1, h1, z, pooled, c1, cc, h2, a2, b2, h3))

    dh, losscols, dgf = _loss_bwd(h, tgt, _row(ws["final_norm"]))
    loss = lax.psum(jnp.sum(losscols), ("x", "y", "c"))
    vecg = [dict() for _ in range(DEPTH)]
    dwdw = [None] * DEPTH
    def reduce_start(g, chain):
        return {"g": g, "r": _sibling_swap(g, 2 * DEPTH + 3 * chain), "cid": 2 * DEPTH + 3 * chain}

    def reduce_mid(st, anchor):
        if anchor is not None:
            anchor, st["r"] = lax.optimization_barrier((anchor, st["r"]))
        sbuf = _chip_sum(st["g"], st["r"], cidx)
        if anchor is not None:
            anchor, sbuf = lax.optimization_barrier((anchor, sbuf))
        st["x"] = _chip_exchange(sbuf, st["cid"] + 1)
        return anchor

    def reduce_end(st, anchor):
        xb = st["x"]
        if anchor is not None:
            anchor, xb = lax.optimization_barrier((anchor, xb))
        rsum = _shard_sum(st["g"], st["r"], xb, ck)
        if anchor is not None:
            anchor, rsum = lax.optimization_barrier((anchor, rsum))
        return anchor, _sibling_share(rsum, st["cid"] + 2)

    parts = [[] for _ in range(DEPTH)]
    above = None
    nchain = 0
    for li in reversed(range(DEPTH)):
        h0, a1, b1, h1, z, pooled, c1, cc, h2, a2, b2, h3 = saved[li]
        w = wrest[li]
        dh, dgp, dwpp, gbuf = _ple_bwd(h3, dh, p[li, 0], vec("ple_norm", li), wppt[li], w)
        vecg[li]["ple_norm"] = dgp
        if above is not None:
            dh = reduce_mid(above[1], dh)
        dh_in, da, db, sact, n, dg = _ffn_bwd(h2, dh, a2, b2, vec("ffn2_norm", li), w, 2)
        vecg[li]["ffn2_norm"] = dg
        gbuf = _wgrad([da, db], n, gbuf, ("wg2", "wu2"), 0, F // 2, F // NCHIP)
        gbuf = _wgrad([sact], dh, gbuf, ("wd2",), 0, F // 2, F // NCHIP, yscale=0.5)
        dh = dh_in
        dp, dc1, dzb, small, dpw, gbuf = _mix_b1(dh, z, pooled, c1, cc, pwcat[li], vec("pool_scale", li),
                                                 vec("conv_ln_g", li), vec("conv_ln_b", li), w, gbuf)
        vecg[li]["pool_scale"] = small[0:1]
        vecg[li]["conv_dw_b"] = small[1:2]
        vecg[li]["conv_ln_g"] = small[2:3]
        vecg[li]["conv_ln_b"] = small[3:4]
        dza, dwdw[li] = _mix_b2(dp, dc1, z, wdw[li])
        if above is not None:
            dza, rsum = reduce_end(above[1], dza)
            parts[above[0]].append((0, rsum))
            above = None
        dh_in, u, dg = _mix_b3(h1, dh, dza, dzb, vec("mix_norm", li), w)
        vecg[li]["mix_norm"] = dg
        gbuf = _wgrad([dza], u, gbuf, ("win",), 0, D, D // NCHIP)
        gbuf = _wgrad([dzb], u, gbuf, ("win",), 3 * D, D, D // NCHIP)
        dh = dh_in
        o, s = lay["wpp"]
        small_rows = jnp.concatenate([dwpp.reshape(NCHIP, s, D), dpw.reshape(NCHIP, lay["pw"][1], D)], axis=1)
        gbuf = lax.dynamic_update_slice(gbuf, small_rows.astype(BF), (0, o, 0))
        if li == 0:
            rest = reduce_start(gbuf[:, nfirst:, :], nchain)
            nchain += 1
        dh_in, da, db, sact, n, dg = _ffn_bwd(h0, dh, a1, b1, vec("ffn1_norm", li), wfirst[li], 1)
        vecg[li]["ffn1_norm"] = dg
        if li == 0:
            da = reduce_mid(rest, da)
        gbuf = _wgrad([da, db], n, gbuf, ("wg1", "wu1"), 0, F // 2, F // NCHIP)
        gbuf = _wgrad([sact], dh, gbuf, ("wd1",), 0, F // 2, F // NCHIP, yscale=0.5)
        dh = dh_in
        if li == 0:
            first = reduce_start(gbuf[:, :nfirst, :], nchain)
            nchain += 1
            parts[li].append((nfirst, reduce_end(rest, None)[1]))
            reduce_mid(first, None)
            parts[li].append((0, reduce_end(first, None)[1]))
        else:
            above = (li, reduce_start(gbuf, nchain))
            nchain += 1
    grad_x = dh.reshape(x.shape)

    rows = [vecg[li][n] for li in range(DEPTH) for n in _VECS] + [dgf]
    rows.append(jnp.zeros((8 - (len(rows) % 8), D), F32))
    vsum = _allreduce_small(jnp.concatenate(rows + dwdw, axis=0))
    nvec = len(_VECS)
    grads = {}
    unpacked = [_unpack_shards(parts[li]) for li in range(DEPTH)]
    for n in _BIG:
        grads[n] = jnp.stack([unpacked[li][n] for li in range(DEPTH)])
    for i, n in enumerate(_VECS):
        grads[n] = jnp.stack([vsum[li * nvec + i] for li in range(DEPTH)])
    grads["final_norm"] = vsum[DEPTH * nvec]
    base = DEPTH * nvec + 8 - ((DEPTH * nvec + 1) % 8) + 1
    dcols = D // NCHIP
    grads["conv_dw_w"] = jnp.stack([
        lax.dynamic_slice(vsum[base + li * HALO: base + li * HALO + kk, :], (0, kme * dcols), (kk, dcols))
        for li in range(DEPTH)])

    outs_g, outs_d, outs_m, outs_v = [], [], [], []
    for n in _WEIGHTS:
        d, mo, vo = _adamw(ws[n], grads[n], ms[n], vs[n])
        outs_g.append(grads[n])
        outs_d.append(d)
        outs_m.append(mo)
        outs_v.append(vo)
    return (loss, grad_x, *outs_g, *outs_d, *outs_m, *outs_v)


def kernel(x, p, ffn1_norm, ffn1_w_gate, ffn1_w_up, ffn1_w_down, mix_norm, w_in, pool_w, pool_scale, conv_dw_w, conv_dw_b, conv_ln_g, conv_ln_b, conv_w_out, w_out, ffn2_norm, ffn2_w_gate, ffn2_w_up, ffn2_w_down, ple_norm, ple_w_gate, ple_w_proj, final_norm, loss_target, m_ffn1_norm, m_ffn1_w_gate, m_ffn1_w_up, m_ffn1_w_down, m_mix_norm, m_w_in, m_pool_w, m_pool_scale, m_conv_dw_w, m_conv_dw_b, m_conv_ln_g, m_conv_ln_b, m_conv_w_out, m_w_out, m_ffn2_norm, m_ffn2_w_gate, m_ffn2_w_up, m_ffn2_w_down, m_ple_norm, m_ple_w_gate, m_ple_w_proj, m_final_norm, v_ffn1_norm, v_ffn1_w_gate, v_ffn1_w_up, v_ffn1_w_down, v_mix_norm, v_w_in, v_pool_w, v_pool_scale, v_conv_dw_w, v_conv_dw_b, v_conv_ln_g, v_conv_ln_b, v_conv_w_out, v_w_out, v_ffn2_norm, v_ffn2_w_gate, v_ffn2_w_up, v_ffn2_w_down, v_ple_norm, v_ple_w_gate, v_ple_w_proj, v_final_norm):
    ws = dict(zip(_WEIGHTS, (ffn1_norm, ffn1_w_gate, ffn1_w_up, ffn1_w_down, mix_norm, w_in, pool_w, pool_scale, conv_dw_w, conv_dw_b, conv_ln_g, conv_ln_b, conv_w_out, w_out, ffn2_norm, ffn2_w_gate, ffn2_w_up, ffn2_w_down, ple_norm, ple_w_gate, ple_w_proj, final_norm)))
    ms = dict(zip(_WEIGHTS, (m_ffn1_norm, m_ffn1_w_gate, m_ffn1_w_up, m_ffn1_w_down, m_mix_norm, m_w_in, m_pool_w, m_pool_scale, m_conv_dw_w, m_conv_dw_b, m_conv_ln_g, m_conv_ln_b, m_conv_w_out, m_w_out, m_ffn2_norm, m_ffn2_w_gate, m_ffn2_w_up, m_ffn2_w_down, m_ple_norm, m_ple_w_gate, m_ple_w_proj, m_final_norm)))
    vs = dict(zip(_WEIGHTS, (v_ffn1_norm, v_ffn1_w_gate, v_ffn1_w_up, v_ffn1_w_down, v_mix_norm, v_w_in, v_pool_w, v_pool_scale, v_conv_dw_w, v_conv_dw_b, v_conv_ln_g, v_conv_ln_b, v_conv_w_out, v_w_out, v_ffn2_norm, v_ffn2_w_gate, v_ffn2_w_up, v_ffn2_w_down, v_ple_norm, v_ple_w_gate, v_ple_w_proj, v_final_norm)))
    return _step(x, p, loss_target, ws, ms, vs)
```

```python
import jax
import jax.numpy as jnp
from jax import lax
from jax.experimental import pallas as pl
from jax.experimental.pallas import tpu as pltpu
from jax.experimental.pallas import tpu_sc as plsc

T = 8192
D = 1024
F = 2816
PD = 256
NG = 4
WINDOWS = (2, 4, 8, 16)
KC = 31
HALO = 32
DEPTH = 2
NCHIP = 4
RMS_EPS = 1e-6
LN_EPS = 1e-5

ADAM_LR = 0.001
ADAM_B1 = 0.9
ADAM_B2 = 0.999
ADAM_EPS = 1e-08
ADAM_WD = 0.01
ADAM_STEP = 10

TM = 512
TMB = 256
TMW = 512
LANES = 128
SUBLANES = 8
CHUNKS = 4
VMEM_LIMIT = 56 * 1024 * 1024

BF = jnp.bfloat16
F32 = jnp.float32
MESH = pl.DeviceIdType.MESH
ANY = pl.BlockSpec(memory_space=pl.ANY)
HBM = pltpu.MemorySpace.HBM


def _layout():
    fs, ins, ds = F // NCHIP, 5 * D // NCHIP, D // NCHIP
    pps = ds * PD // D
    pws = NG * (D // NG // NCHIP) * (D // NG) // D
    names = [("wg1", fs), ("wu1", fs), ("wd1", fs), ("win", ins), ("wco", ds), ("wo", ds),
             ("wg2", fs), ("wu2", fs), ("wd2", fs), ("wpg", ds), ("wpp", pps), ("pw", pws)]
    off, r = {}, 0
    for n, s in names:
        off[n] = (r, s)
        r += s
    return off, r


def _sig(v):
    return 1.0 / (1.0 + jnp.exp(-v))


def _dot_nn(a, b):
    return jnp.dot(a, b, preferred_element_type=F32)


def _dot_nt(a, b):
    return lax.dot_general(a, b, (((1,), (1,)), ((), ())), preferred_element_type=F32)


def _dot_tn(a, b):
    return lax.dot_general(a, b, (((0,), (0,)), ((), ())), preferred_element_type=F32)


def _params(sem=("arbitrary",)):
    return pltpu.CompilerParams(dimension_semantics=sem, vmem_limit_bytes=VMEM_LIMIT)


def _first_rows():
    return _layout()[0]["win"][0]


def _weight_copies(w_hbm, specs, sem):
    lay, _ = _layout()
    ra = _first_rows()
    cps = []
    for i, (name, dst) in enumerate(specs):
        off, rs = lay[name]
        off = off if off < ra else off - ra
        for k in range(NCHIP):
            cps.append(pltpu.make_async_copy(w_hbm.at[k, pl.ds(off, rs), :],
                                             dst.at[pl.ds(k * rs, rs), :], sem.at[i * NCHIP + k]))
    return cps


def _load_weights_once(w_hbm, specs, sem):
    @pl.when(pl.program_id(0) == 0)
    def _():
        cps = _weight_copies(w_hbm, specs, sem)
        for cp in cps:
            cp.start()
        for cp in cps:
            cp.wait()


def _grad_copies(stage, g_hbm, name, row0, rows, piece, sem, sem0):
    lay, _ = _layout()
    off, rs = lay[name]
    cps = []
    for i in range(rows // piece):
        rglob = row0 + i * piece
        k = rglob // rs
        loc = rglob - k * rs
        start = off + loc
        if not isinstance(start, int):
            start = pl.multiple_of(start, 16)
        dst = g_hbm.at[k, pl.ds(start, piece), :]
        cps.append(pltpu.make_async_copy(stage.at[pl.ds(i * piece, piece), :], dst, sem.at[sem0 + i]))
    return cps


def _row(v):
    return v.reshape(1, -1)


def _shifted_source(buf, sh, base, s, lanes):
    a, b = divmod(s, SUBLANES)
    rows = pl.ds(pl.multiple_of(base + SUBLANES * a, SUBLANES), SUBLANES)
    if b == 0:
        return buf[rows, lanes]
    return sh[b - 1, rows, :]


def _fill_shifted(buf, sh, lanes):
    rows = sh.shape[1]
    for b in range(1, SUBLANES):
        sh[b - 1, :, :] = buf[pl.ds(b, rows), lanes]


def _ffn_fwd(h, g, wfull, which):
    nt = T // TM
    fc = F // 2
    names = ("wg%d" % which, "wu%d" % which, "wd%d" % which)

    def body(h_ref, g_ref, w_hbm, ho_ref, a_ref, b_ref, wg, wu, wd, sem):
        _load_weights_once(w_hbm, ((names[0], wg), (names[1], wu), (names[2], wd)), sem)
        x = h_ref[...]
        r = lax.rsqrt(jnp.mean(x * x, axis=-1, keepdims=True) + RMS_EPS)
        n = (x * r * g_ref[...]).astype(BF)
        acc = jnp.zeros((TM, D), F32)
        for c in range(F // fc):
            sl = pl.ds(c * fc, fc)
            a = _dot_nt(n, wg[sl, :])
            b = _dot_nt(n, wu[sl, :])
            a_ref[:, sl] = a.astype(BF)
            b_ref[:, sl] = b.astype(BF)
            s = (a * _sig(a) * b).astype(BF)
            acc = acc + _dot_nn(s, wd[sl, :])
        ho_ref[...] = x + 0.5 * acc

    tile = lambda w: pl.BlockSpec((TM, w), lambda i: (i, 0))
    return pl.pallas_call(
        body, name="ffn_fwd", grid=(nt,),
        in_specs=[tile(D), pl.BlockSpec((1, D), lambda i: (0, 0)), ANY],
        out_specs=[tile(D), tile(F), tile(F)],
        out_shape=[jax.ShapeDtypeStruct((T, D), F32), jax.ShapeDtypeStruct((T, F), BF),
                   jax.ShapeDtypeStruct((T, F), BF)],
        scratch_shapes=[pltpu.VMEM((F, D), BF), pltpu.VMEM((F, D), BF), pltpu.VMEM((F, D), BF),
                        pltpu.SemaphoreType.DMA((3 * NCHIP,))],
        compiler_params=_params(),
    )(h, g, wfull)


def _mix_in_fwd(h, g, wfull):
    nt = T // TM
    nin = 5 * D

    def body(h_ref, g_ref, w_hbm, z_ref, win, sem):
        _load_weights_once(w_hbm, (("win", win),), sem)
        x = h_ref[...]
        r = lax.rsqrt(jnp.mean(x * x, axis=-1, keepdims=True) + RMS_EPS)
        u = (x * r * g_ref[...]).astype(BF)
        for c in range(5):
            sl = pl.ds(c * D, D)
            z_ref[:, sl] = _dot_nt(u, win[sl, :]).astype(BF)

    return pl.pallas_call(
        body, name="mix_in_fwd", grid=(nt,),
        in_specs=[pl.BlockSpec((TM, D), lambda i: (i, 0)), pl.BlockSpec((1, D), lambda i: (0, 0)), ANY],
        out_specs=pl.BlockSpec((TM, nin), lambda i: (i, 0)),
        out_shape=jax.ShapeDtypeStruct((T, nin), BF),
        scratch_shapes=[pltpu.VMEM((nin, D), BF), pltpu.SemaphoreType.DMA((NCHIP,))],
        compiler_params=_params(),
    )(h, g, wfull)


def _pool_counts(i, rows, w):
    t = i * TM + lax.broadcasted_iota(jnp.int32, (rows, 1), 0)
    return jnp.minimum(t + 1, w).astype(F32)


def _mix_mid_fwd(h, z, pwcat, pscale, wdw, bdw, lg, lb, wfull):
    nt = T // TM
    pg = D // NG
    hb = TM // HALO
    n_ext = HALO + TM
    pad = SUBLANES

    def body(h_ref, z_ref, zh_ref, pw_ref, ps_ref, wdw_ref, bdw_ref, lg_ref, lb_ref, w_hbm,
             h2_ref, p_ref, c1_ref, cc_ref, wco, wo, pa, pb, cbuf, sh, c1buf, ambuf, sem):
        i = pl.program_id(0)
        _load_weights_once(w_hbm, (("wco", wco), ("wo", wo)), sem)

        @pl.when(i == 0)
        def _():
            pa[0:pad, :] = jnp.zeros((pad, D), F32)
            pb[0:pad, :] = jnp.zeros((pad, D), F32)

        keep = (i > 0).astype(F32)
        zh = zh_ref[...].astype(F32) * keep
        za = z_ref[:, D:2 * D].astype(F32)
        zg = z_ref[:, 2 * D:3 * D].astype(F32)
        pa[pad:pad + HALO, :] = zh[:, 0:D]
        pa[pad + HALO:pad + n_ext, :] = z_ref[:, 0:D].astype(F32)
        cbuf[0:HALO, :] = zh[:, D:2 * D] * _sig(zh[:, 2 * D:3 * D])
        cbuf[HALO:n_ext, :] = za * _sig(zg)
        for g, w in enumerate(WINDOWS):
            ls = pl.ds(g * pg, pg)
            cur, nxt = pa, pb
            d = 1
            while d < w:
                nxt[pl.ds(pad, n_ext), ls] = cur[pl.ds(pad, n_ext), ls] + cur[pl.ds(pad - d, n_ext), ls]
                cur, nxt = nxt, cur
                d *= 2
            tok = z_ref[:, ls].astype(F32)
            pooled = (cur[pl.ds(pad + HALO, TM), ls] / _pool_counts(i, TM, w) - tok).astype(BF)
            p_ref[:, ls] = pooled
            ambuf[:, ls] = _dot_nn(pooled, pw_ref[:, ls])
        am = ambuf[...] * ps_ref[...]
        for l in range(D // LANES):
            lanes = pl.ds(l * LANES, LANES)
            _fill_shifted(cbuf, sh, lanes)
            bias = jnp.broadcast_to(bdw_ref[:, lanes], (SUBLANES, LANES))

            def conv_rows(r, carry):
                base = r * (CHUNKS * SUBLANES)
                accs = [bias] * CHUNKS
                for k in range(KC):
                    wk = jnp.broadcast_to(wdw_ref[k:k + 1, lanes], (SUBLANES, LANES))
                    for j in range(CHUNKS):
                        src = _shifted_source(cbuf, sh, base + j * SUBLANES, HALO - (KC - 1) + k, lanes)
                        accs[j] = accs[j] + wk * src
                for j in range(CHUNKS):
                    c1buf[pl.ds(pl.multiple_of(base + j * SUBLANES, SUBLANES), SUBLANES), lanes] = accs[j]
                return carry

            lax.fori_loop(0, TM // (CHUNKS * SUBLANES), conv_rows, 0)
        c1b = c1buf[...].astype(BF)
        c1_ref[...] = c1b
        c1 = c1b.astype(F32)
        mu = jnp.mean(c1, axis=-1, keepdims=True)
        xc = c1 - mu
        var = jnp.mean(xc * xc, axis=-1, keepdims=True)
        c2 = xc * lax.rsqrt(var + LN_EPS) * lg_ref[...] + lb_ref[...]
        c3 = (c2 * _sig(c2)).astype(BF)
        ccb = _dot_nn(c3, wco[...]).astype(BF)
        cc_ref[...] = ccb
        gp = z_ref[:, 3 * D:4 * D].astype(F32)
        gc = z_ref[:, 4 * D:5 * D].astype(F32)
        m = (_sig(gp) * am + _sig(gc) * ccb.astype(F32)).astype(BF)
        h2_ref[...] = h_ref[...] + _dot_nn(m, wo[...])

    tile = pl.BlockSpec((TM, D), lambda i: (i, 0))
    vec = pl.BlockSpec((1, D), lambda i: (0, 0))
    return pl.pallas_call(
        body, name="mix_mid_fwd", grid=(nt,),
        in_specs=[tile, pl.BlockSpec((TM, 5 * D), lambda i: (i, 0)),
                  pl.BlockSpec((HALO, 3 * D), lambda i: (jnp.maximum(i * hb - 1, 0), 0)),
                  pl.BlockSpec((pg, D), lambda i: (0, 0)), vec,
                  pl.BlockSpec((HALO, D), lambda i: (0, 0)), vec, vec, vec, ANY],
        out_specs=[tile, tile, tile, tile],
        out_shape=[jax.ShapeDtypeStruct((T, D), F32), jax.ShapeDtypeStruct((T, D), BF),
                   jax.ShapeDtypeStruct((T, D), BF), jax.ShapeDtypeStruct((T, D), BF)],
        scratch_shapes=[pltpu.VMEM((D, D), BF), pltpu.VMEM((D, D), BF),
                        pltpu.VMEM((pad + n_ext, D), F32), pltpu.VMEM((pad + n_ext, D), F32),
                        pltpu.VMEM((n_ext, D), F32), pltpu.VMEM((SUBLANES - 1, n_ext - SUBLANES, LANES), F32),
                        pltpu.VMEM((TM, D), F32), pltpu.VMEM((TM, D), F32),
                        pltpu.SemaphoreType.DMA((2 * NCHIP,))],
        compiler_params=_params(),
    )(h, z, z, pwcat, pscale, wdw, bdw, lg, lb, wfull)


def _ple_fwd(h, p, g, wppt, wfull):
    nt = T // TM

    def body(h_ref, p_ref, g_ref, wpp_ref, w_hbm, ho_ref, wpg, sem):
        _load_weights_once(w_hbm, (("wpg", wpg),), sem)
        x = h_ref[...]
        r = lax.rsqrt(jnp.mean(x * x, axis=-1, keepdims=True) + RMS_EPS)
        n = (x * r * g_ref[...]).astype(BF)
        gate = _sig(_dot_nn(n, wpg[...]))
        pe = _dot_nt(p_ref[...].astype(BF), wpp_ref[...])
        ho_ref[...] = x + gate * pe

    tile = pl.BlockSpec((TM, D), lambda i: (i, 0))
    return pl.pallas_call(
        body, name="ple_fwd", grid=(nt,),
        in_specs=[tile, pl.BlockSpec((TM, PD), lambda i: (i, 0)), pl.BlockSpec((1, D), lambda i: (0, 0)),
                  pl.BlockSpec((D, PD), lambda i: (0, 0)), ANY],
        out_specs=tile, out_shape=jax.ShapeDtypeStruct((T, D), F32),
        scratch_shapes=[pltpu.VMEM((D, D), BF), pltpu.SemaphoreType.DMA((NCHIP,))],
        compiler_params=_params(),
    )(h, p, g, wppt, wfull)


def _loss_bwd(h, tgt, g):
    nt = T // TM

    def body(h_ref, t_ref, g_ref, dh_ref, loss_ref, dg_ref):
        @pl.when(pl.program_id(0) == 0)
        def _():
            loss_ref[...] = jnp.zeros_like(loss_ref)
            dg_ref[...] = jnp.zeros_like(dg_ref)
        x = h_ref[...]
        r = lax.rsqrt(jnp.mean(x * x, axis=-1, keepdims=True) + RMS_EPS)
        xh = x * r
        gv = g_ref[...]
        e = xh * gv - t_ref[...]
        loss_ref[...] += jnp.sum(e * e, axis=0, keepdims=True) * (0.5 / D)
        dy = e * (1.0 / D)
        dg_ref[...] += jnp.sum(dy * xh, axis=0, keepdims=True)
        dxh = dy * gv
        dh_ref[...] = r * (dxh - xh * jnp.mean(dxh * xh, axis=-1, keepdims=True))

    tile = pl.BlockSpec((TM, D), lambda i: (i, 0))
    vec = pl.BlockSpec((1, D), lambda i: (0, 0))
    return pl.pallas_call(
        body, name="loss_bwd", grid=(nt,), in_specs=[tile, tile, vec], out_specs=[tile, vec, vec],
        out_shape=[jax.ShapeDtypeStruct((T, D), F32), jax.ShapeDtypeStruct((1, D), F32),
                   jax.ShapeDtypeStruct((1, D), F32)],
        compiler_params=_params(),
    )(h, tgt, g)


def _ple_bwd(h, dh, p, g, wppt, wfull):
    nt = T // TM
    _, rtot = _layout()

    def body(h_ref, dh_ref, p_ref, g_ref, wpp_ref, w_hbm,
             dho_ref, dg_ref, dwpp_ref, g_out, wpg, acc, stage, sem, osem):
        i = pl.program_id(0)
        _load_weights_once(w_hbm, (("wpg", wpg),), sem)

        @pl.when(i == 0)
        def _():
            dg_ref[...] = jnp.zeros_like(dg_ref)
            dwpp_ref[...] = jnp.zeros_like(dwpp_ref)
            acc[...] = jnp.zeros_like(acc)

        x = h_ref[...]
        r = lax.rsqrt(jnp.mean(x * x, axis=-1, keepdims=True) + RMS_EPS)
        xh = x * r
        gv = g_ref[...]
        n = (xh * gv).astype(BF)
        gate = _sig(_dot_nn(n, wpg[...]))
        pb = p_ref[...].astype(BF)
        pe = _dot_nt(pb, wpp_ref[...])
        d = dh_ref[...]
        dpe = (d * gate).astype(BF)
        dq = (d * pe * gate * (1.0 - gate)).astype(BF)
        dwpp_ref[...] += _dot_tn(dpe, pb)
        acc[...] += _dot_tn(n, dq)
        dn = _dot_nt(dq, wpg[...])
        dg_ref[...] += jnp.sum(dn * xh, axis=0, keepdims=True)
        dxh = dn * gv
        dho_ref[...] = d + r * (dxh - xh * jnp.mean(dxh * xh, axis=-1, keepdims=True))

        @pl.when(i == nt - 1)
        def _():
            stage[...] = acc[...].astype(BF)
            cps = _grad_copies(stage, g_out, "wpg", 0, D, D // NCHIP, osem, 0)
            for cp in cps:
                cp.start()
            for cp in cps:
                cp.wait()

    tile = pl.BlockSpec((TM, D), lambda i: (i, 0))
    vec = pl.BlockSpec((1, D), lambda i: (0, 0))
    return pl.pallas_call(
        body, name="ple_bwd", grid=(nt,),
        in_specs=[tile, tile, pl.BlockSpec((TM, PD), lambda i: (i, 0)), vec,
                  pl.BlockSpec((D, PD), lambda i: (0, 0)), ANY],
        out_specs=[tile, vec, pl.BlockSpec((D, PD), lambda i: (0, 0)), ANY],
        out_shape=[jax.ShapeDtypeStruct((T, D), F32), jax.ShapeDtypeStruct((1, D), F32),
                   jax.ShapeDtypeStruct((D, PD), F32),
                   jax.ShapeDtypeStruct((NCHIP, rtot, D), BF)],
        scratch_shapes=[pltpu.VMEM((D, D), BF), pltpu.VMEM((D, D), F32), pltpu.VMEM((D, D), BF),
                        pltpu.SemaphoreType.DMA((NCHIP,)), pltpu.SemaphoreType.DMA((NCHIP,))],
        compiler_params=_params(),
    )(h, dh, p, g, wppt, wfull)


def _ffn_bwd(h, dh, a, b, g, wfull, which):
    tm = TM
    nt = T // tm
    nh = 2
    fc = F // nh
    sub = [min(512, fc - c0) for c0 in range(0, fc, 512)]
    names = ("wg%d" % which, "wu%d" % which, "wd%d" % which)

    def body(h_ref, dh_ref, a_ref, b_ref, g_ref, w_hbm,
             dho_ref, da_ref, db_ref, s_ref, n_ref, dg_ref, wg, wu, wd, dn_acc, sem):
        i = pl.program_id(0)
        j = pl.program_id(1)

        @pl.when((i == 0) & (j == 0))
        def _():
            cps = _weight_copies(w_hbm, ((names[0], wg), (names[1], wu), (names[2], wd)), sem)
            for cp in cps:
                cp.start()
            for cp in cps:
                cp.wait()
            dg_ref[...] = jnp.zeros_like(dg_ref)

        def normed():
            x = h_ref[...]
            r = lax.rsqrt(jnp.mean(x * x, axis=-1, keepdims=True) + RMS_EPS)
            return r, x * r

        @pl.when(j == 0)
        def _():
            n_ref[...] = (normed()[1] * g_ref[...]).astype(BF)
            dn_acc[...] = jnp.zeros_like(dn_acc)

        df = (0.5 * dh_ref[...]).astype(BF)
        half = pl.multiple_of(j * fc, 16)
        c0 = 0
        for width in sub:
            sl = pl.ds(c0, width)
            av = a_ref[:, sl].astype(F32)
            bv = b_ref[:, sl].astype(F32)
            ds = _dot_nt(df, wd[pl.ds(half + c0, width), :])
            sg = _sig(av)
            sil = av * sg
            s_ref[:, sl] = (sil * bv).astype(BF)
            da_ref[:, sl] = (ds * bv * (sg * (1.0 + av * (1.0 - sg)))).astype(BF)
            db_ref[:, sl] = (ds * sil).astype(BF)
            c0 += width
        rows = pl.ds(half, fc)
        dn_acc[...] += _dot_nn(da_ref[...], wg[rows, :])
        dn_acc[...] += _dot_nn(db_ref[...], wu[rows, :])

        @pl.when(j == nh - 1)
        def _():
            r, xh = normed()
            dn = dn_acc[...]
            dg_ref[...] += jnp.sum(dn * xh, axis=0, keepdims=True)
            dxh = dn * g_ref[...]
            dho_ref[...] = dh_ref[...] + r * (dxh - xh * jnp.mean(dxh * xh, axis=-1, keepdims=True))

    tok = lambda w: pl.BlockSpec((tm, w), lambda i, j: (i, 0))
    hid = pl.BlockSpec((tm, fc), lambda i, j: (i, j))
    vec = pl.BlockSpec((1, D), lambda i, j: (0, 0))
    return pl.pallas_call(
        body, name="ffn_bwd", grid=(nt, nh),
        in_specs=[tok(D), tok(D), hid, hid, vec, ANY],
        out_specs=[tok(D), hid, hid, hid, tok(D), vec],
        out_shape=[jax.ShapeDtypeStruct((T, D), F32), jax.ShapeDtypeStruct((T, F), BF),
                   jax.ShapeDtypeStruct((T, F), BF), jax.ShapeDtypeStruct((T, F), BF),
                   jax.ShapeDtypeStruct((T, D), BF), jax.ShapeDtypeStruct((1, D), F32)],
        scratch_shapes=[pltpu.VMEM((F, D), BF), pltpu.VMEM((F, D), BF), pltpu.VMEM((F, D), BF),
                        pltpu.VMEM((tm, D), F32), pltpu.SemaphoreType.DMA((3 * NCHIP,))],
        compiler_params=_params(("arbitrary", "arbitrary")),
    )(h, dh, a, b, g, wfull)


def _wgrad(xs, y, gbuf, names, row0, rb, piece, yscale=None):
    nx = len(xs)
    rx = xs[0].shape[1]
    nj = rx // rb
    nt = T // TMW
    npiece = rb // piece

    def body(*refs):
        x_refs = refs[:nx]
        y_ref = refs[nx]
        g_out = refs[nx + 2]
        accs = refs[nx + 3:2 * nx + 3]
        stages = refs[2 * nx + 3:3 * nx + 3]
        osem = refs[3 * nx + 3]
        j = pl.program_id(0)
        t = pl.program_id(1)

        @pl.when(t == 0)
        def _():
            for acc in accs:
                acc[...] = jnp.zeros_like(acc)

        yv = y_ref[...]
        if yscale is not None:
            yv = (yscale * yv).astype(BF)
        for x_ref, acc in zip(x_refs, accs):
            acc[...] += _dot_tn(x_ref[...], yv)

        @pl.when(t == nt - 1)
        def _():
            cps = []
            for xi in range(nx):
                stages[xi][...] = accs[xi][...].astype(BF)
                cps += _grad_copies(stages[xi], g_out, names[xi], row0 + j * rb, rb, piece,
                                    osem, xi * npiece)
            for cp in cps:
                cp.start()
            for cp in cps:
                cp.wait()

    in_specs = [pl.BlockSpec((TMW, rb), lambda j, t: (t, j)) for _ in xs]
    in_specs += [pl.BlockSpec((TMW, D), lambda j, t: (t, 0)), ANY]
    return pl.pallas_call(
        body, name="wgrad", grid=(nj, nt), in_specs=in_specs, out_specs=ANY,
        out_shape=jax.ShapeDtypeStruct(gbuf.shape, gbuf.dtype),
        scratch_shapes=([pltpu.VMEM((rb, D), F32) for _ in xs] + [pltpu.VMEM((rb, D), BF) for _ in xs]
                        + [pltpu.SemaphoreType.DMA((nx * npiece,))]),
        input_output_aliases={nx + 1: 0},
        compiler_params=_params(("arbitrary", "arbitrary")),
    )(*xs, y, gbuf)


def _mix_b1(dh, z, pooled, c1, cc, pwcat, pscale, lg, lb, wfull, gbuf):
    nt = T // TM
    pg = D // NG

    def body(dh_ref, gp_ref, gc_ref, p_ref, c1_ref, cc_ref, pw_ref, ps_ref, lg_ref, lb_ref, w_hbm, _g_in,
             dp_ref, dc1_ref, dzb_ref, small_ref, dpw_ref, g_out,
             wco, wo, acc_o, acc_co, stage_o, stage_co, qbuf, sem, osem):
        i = pl.program_id(0)
        _load_weights_once(w_hbm, (("wco", wco), ("wo", wo)), sem)

        @pl.when(i == 0)
        def _():
            small_ref[...] = jnp.zeros_like(small_ref)
            dpw_ref[...] = jnp.zeros_like(dpw_ref)
            acc_o[...] = jnp.zeros_like(acc_o)
            acc_co[...] = jnp.zeros_like(acc_co)

        dhb = dh_ref[...].astype(BF)
        dm = _dot_nt(dhb, wo[...])
        sp = _sig(gp_ref[...].astype(F32))
        sc = _sig(gc_ref[...].astype(F32))
        for g in range(NG):
            ls = pl.ds(g * pg, pg)
            qbuf[:, ls] = _dot_nn(p_ref[:, ls], pw_ref[:, ls])
        q = qbuf[...]
        psv = ps_ref[...]
        am = q * psv
        ccv = cc_ref[...].astype(F32)
        m = (sp * am + sc * ccv).astype(BF)
        acc_o[...] += _dot_tn(m, dhb)
        dam = dm * sp
        dzb_ref[:, 0:D] = (dm * am * sp * (1.0 - sp)).astype(BF)
        dccb = (dm * sc).astype(BF)
        dzb_ref[:, D:2 * D] = (dm * ccv * sc * (1.0 - sc)).astype(BF)
        small_ref[0:1, :] += jnp.sum(dam * q, axis=0, keepdims=True)
        dq = (dam * psv).astype(BF)
        for g in range(NG):
            ls = pl.ds(g * pg, pg)
            dqg = dq[:, g * pg:(g + 1) * pg]
            dp_ref[:, ls] = _dot_nt(dqg, pw_ref[:, ls]).astype(BF)
            dpw_ref[:, ls] += _dot_tn(p_ref[:, ls], dqg)
        c1v = c1_ref[...].astype(F32)
        mu = jnp.mean(c1v, axis=-1, keepdims=True)
        xc = c1v - mu
        var = jnp.mean(xc * xc, axis=-1, keepdims=True)
        rs = lax.rsqrt(var + LN_EPS)
        c2n = xc * rs
        lgv = lg_ref[...]
        c2 = c2n * lgv + lb_ref[...]
        sg2 = _sig(c2)
        c3 = (c2 * sg2).astype(BF)
        acc_co[...] += _dot_tn(c3, dccb)
        dc3 = _dot_nt(dccb, wco[...])
        dc2 = dc3 * (sg2 * (1.0 + c2 * (1.0 - sg2)))
        small_ref[2:3, :] += jnp.sum(dc2 * c2n, axis=0, keepdims=True)
        small_ref[3:4, :] += jnp.sum(dc2, axis=0, keepdims=True)
        dc2n = dc2 * lgv
        dc1 = rs * (dc2n - jnp.mean(dc2n, axis=-1, keepdims=True)
                    - c2n * jnp.mean(dc2n * c2n, axis=-1, keepdims=True))
        small_ref[1:2, :] += jnp.sum(dc1, axis=0, keepdims=True)
        dc1_ref[...] = dc1.astype(BF)

        @pl.when(i == nt - 1)
        def _():
            stage_o[...] = acc_o[...].astype(BF)
            stage_co[...] = acc_co[...].astype(BF)
            cps = _grad_copies(stage_o, g_out, "wo", 0, D, D // NCHIP, osem, 0)
            cps += _grad_copies(stage_co, g_out, "wco", 0, D, D // NCHIP, osem, NCHIP)
            for cp in cps:
                cp.start()
            for cp in cps:
                cp.wait()

    tile = pl.BlockSpec((TM, D), lambda i: (i, 0))
    vec = pl.BlockSpec((1, D), lambda i: (0, 0))
    full = lambda r: pl.BlockSpec((r, D), lambda i: (0, 0))
    return pl.pallas_call(
        body, name="mix_b1", grid=(nt,),
        in_specs=[tile, pl.BlockSpec((TM, D), lambda i: (i, 3)), pl.BlockSpec((TM, D), lambda i: (i, 4)),
                  tile, tile, tile, full(pg), vec, vec, vec, ANY, ANY],
        out_specs=[tile, tile, pl.BlockSpec((TM, 2 * D), lambda i: (i, 0)), full(8), full(pg), ANY],
        out_shape=[jax.ShapeDtypeStruct((T, D), BF), jax.ShapeDtypeStruct((T, D), BF),
                   jax.ShapeDtypeStruct((T, 2 * D), BF), jax.ShapeDtypeStruct((8, D), F32),
                   jax.ShapeDtypeStruct((pg, D), F32), jax.ShapeDtypeStruct(gbuf.shape, gbuf.dtype)],
        scratch_shapes=[pltpu.VMEM((D, D), BF), pltpu.VMEM((D, D), BF),
                        pltpu.VMEM((D, D), F32), pltpu.VMEM((D, D), F32),
                        pltpu.VMEM((D, D), BF), pltpu.VMEM((D, D), BF),
                        pltpu.VMEM((TM, D), F32),
                        pltpu.SemaphoreType.DMA((2 * NCHIP,)), pltpu.SemaphoreType.DMA((2 * NCHIP,))],
        input_output_aliases={11: 5},
        compiler_params=_params(),
    )(dh, z, z, pooled, c1, cc, pwcat, pscale, lg, lb, wfull, gbuf)


def _mix_b2(dp, dc1, z, wdw):
    nt = T // TM
    pg = D // NG
    hb = TM // HALO
    nhb = T // HALO
    n_ext = TM + HALO
    pad = SUBLANES

    def body(dp_ref, dpn_ref, dc_ref, dcn_ref, za_ref, zg_ref, wdw_ref,
             dza_ref, dw_ref, pa, pb, cbuf, sh, c0buf, dc0buf):
        i = pl.program_id(0)

        @pl.when(i == 0)
        def _():
            dw_ref[...] = jnp.zeros_like(dw_ref)
            pa[n_ext:n_ext + pad, :] = jnp.zeros((pad, D), F32)
            pb[n_ext:n_ext + pad, :] = jnp.zeros((pad, D), F32)

        more = (i < nt - 1).astype(F32)
        for g, w in enumerate(WINDOWS):
            ls = pl.ds(g * pg, pg)
            cur_dp = dp_ref[:, ls].astype(F32)
            pa[0:TM, ls] = cur_dp / _pool_counts(i, TM, w)
            pa[TM:n_ext, ls] = dpn_ref[:, ls].astype(F32) * (more / w)
            cur, nxt = pa, pb
            d = 1
            while d < w:
                nxt[pl.ds(0, n_ext), ls] = cur[pl.ds(0, n_ext), ls] + cur[pl.ds(d, n_ext), ls]
                cur, nxt = nxt, cur
                d *= 2
            dza_ref[:, ls] = (cur[pl.ds(0, TM), ls] - cur_dp).astype(BF)
        za = za_ref[...].astype(F32)
        sg = _sig(zg_ref[...].astype(F32))
        cbuf[0:TM, :] = dc_ref[...].astype(F32)
        cbuf[TM:n_ext, :] = dcn_ref[...].astype(F32) * more
        c0buf[...] = za * sg
        for l in range(D // LANES):
            lanes = pl.ds(l * LANES, LANES)
            _fill_shifted(cbuf, sh, lanes)

            def conv_rows(r, accs):
                base = r * (CHUNKS * SUBLANES)
                rows = [pl.ds(pl.multiple_of(base + j * SUBLANES, SUBLANES), SUBLANES) for j in range(CHUNKS)]
                c0v = [c0buf[rows[j], lanes] for j in range(CHUNKS)]
                acc = [jnp.zeros((SUBLANES, LANES), F32)] * CHUNKS
                new = list(accs)
                for k in range(KC):
                    wk = jnp.broadcast_to(wdw_ref[k:k + 1, lanes], (SUBLANES, LANES))
                    for j in range(CHUNKS):
                        src = _shifted_source(cbuf, sh, base + j * SUBLANES, KC - 1 - k, lanes)
                        acc[j] = acc[j] + wk * src
                        new[k] = new[k] + c0v[j] * src
                for j in range(CHUNKS):
                    dc0buf[rows[j], lanes] = acc[j]
                return tuple(new)

            init = tuple(jnp.zeros((SUBLANES, LANES), F32) for _ in range(KC))
            accs = lax.fori_loop(0, TM // (CHUNKS * SUBLANES), conv_rows, init)
            for k in range(KC):
                dw_ref[k:k + 1, lanes] += jnp.sum(accs[k], axis=0, keepdims=True)
        dc0 = dc0buf[...]
        dza_ref[:, D:2 * D] = (dc0 * sg).astype(BF)
        dza_ref[:, 2 * D:3 * D] = (dc0 * za * sg * (1.0 - sg)).astype(BF)

    tile = pl.BlockSpec((TM, D), lambda i: (i, 0))
    nxt_spec = pl.BlockSpec((HALO, D), lambda i: (jnp.minimum((i + 1) * hb, nhb - 1), 0))
    return pl.pallas_call(
        body, name="mix_b2", grid=(nt,),
        in_specs=[tile, nxt_spec, tile, nxt_spec, pl.BlockSpec((TM, D), lambda i: (i, 1)),
                  pl.BlockSpec((TM, D), lambda i: (i, 2)), pl.BlockSpec((HALO, D), lambda i: (0, 0))],
        out_specs=[pl.BlockSpec((TM, 3 * D), lambda i: (i, 0)), pl.BlockSpec((HALO, D), lambda i: (0, 0))],
        out_shape=[jax.ShapeDtypeStruct((T, 3 * D), BF), jax.ShapeDtypeStruct((HALO, D), F32)],
        scratch_shapes=[pltpu.VMEM((n_ext + pad, D), F32), pltpu.VMEM((n_ext + pad, D), F32),
                        pltpu.VMEM((n_ext, D), F32), pltpu.VMEM((SUBLANES - 1, n_ext - SUBLANES, LANES), F32),
                        pltpu.VMEM((TM, D), F32), pltpu.VMEM((TM, D), F32)],
        compiler_params=_params(),
    )(dp, dp, dc1, dc1, z, z, wdw)


def _mix_b3(h, dh, dza, dzb, g, wfull):
    nt = T // TM

    def body(h_ref, dh_ref, dza_ref, dzb_ref, g_ref, w_hbm, dho_ref, u_ref, dg_ref, win, sem):
        _load_weights_once(w_hbm, (("win", win),), sem)

        @pl.when(pl.program_id(0) == 0)
        def _():
            dg_ref[...] = jnp.zeros_like(dg_ref)

        x = h_ref[...]
        r = lax.rsqrt(jnp.mean(x * x, axis=-1, keepdims=True) + RMS_EPS)
        xh = x * r
        gv = g_ref[...]
        u_ref[...] = (xh * gv).astype(BF)
        du = _dot_nn(dza_ref[...], win[0:3 * D, :]) + _dot_nn(dzb_ref[...], win[3 * D:5 * D, :])
        dg_ref[...] += jnp.sum(du * xh, axis=0, keepdims=True)
        dxh = du * gv
        dho_ref[...] = dh_ref[...] + r * (dxh - xh * jnp.mean(dxh * xh, axis=-1, keepdims=True))

    tile = lambda w: pl.BlockSpec((TM, w), lambda i: (i, 0))
    vec = pl.BlockSpec((1, D), lambda i: (0, 0))
    return pl.pallas_call(
        body, name="mix_b3", grid=(nt,),
        in_specs=[tile(D), tile(D), tile(3 * D), tile(2 * D), vec, ANY],
        out_specs=[tile(D), tile(D), vec],
        out_shape=[jax.ShapeDtypeStruct((T, D), F32), jax.ShapeDtypeStruct((T, D), BF),
                   jax.ShapeDtypeStruct((1, D), F32)],
        scratch_shapes=[pltpu.VMEM((5 * D, D), BF), pltpu.SemaphoreType.DMA((NCHIP,))],
        compiler_params=_params(),
    )(h, dh, dza, dzb, g, wfull)


def _mesh_pos():
    x, y, c = lax.axis_index("x"), lax.axis_index("y"), lax.axis_index("c")
    chips = [(1 - x, y), (x, 1 - y), (1 - x, 1 - y)]
    return x, y, c, 2 * x + y, chips


def _handshake(peers):
    barrier = pltpu.get_barrier_semaphore()
    for peer in peers:
        pl.semaphore_signal(barrier, inc=1, device_id=peer, device_id_type=MESH)
    pl.semaphore_wait(barrier, len(peers))


def _run_comm(body, name, cid, ins, inouts, out_types, scratch):
    in_refs = [jax.new_ref(a, memory_space=HBM) for a in ins]
    inout_refs = [jax.new_ref(a, memory_space=HBM) for a in inouts]
    out_refs = [jax.empty_ref(t, memory_space=HBM) for t in out_types]

    @pl.kernel(mesh=plsc.ScalarSubcoreMesh(axis_name="seq", num_cores=1), name=name,
               scratch_types=scratch, compiler_params=pltpu.CompilerParams(collective_id=cid))
    def launch(*scr):
        body(*in_refs, *inout_refs, *out_refs, *scr)

    launch()
    return [r[...] for r in inout_refs + out_refs]


def _remote(src, dst, send_sem, recv_sem, to):
    return pltpu.make_async_remote_copy(src_ref=src, dst_ref=dst, send_sem=send_sem, recv_sem=recv_sem,
                                        device_id=to, device_id_type=MESH)


def _gather_layer(packed, cid):
    rtot = packed.shape[0]
    half = rtot // 2
    kme = 2 * lax.axis_index("x") + lax.axis_index("y")
    landing = lax.dynamic_update_slice(lax.empty((NCHIP, rtot, D), BF), packed[None], (kme, 0, 0))

    def body(p_ref, w_ref, send_sems, recv_sems):
        x, y, c, kme, chips = _mesh_pos()
        sib = (x, y, 1 - c)
        _handshake([(*ch, c) for ch in chips] + [sib])
        ks = [2 * cx + cy for cx, cy in chips]
        mine = pl.ds(pl.multiple_of(c * half, 16), half)
        other = pl.ds(pl.multiple_of((1 - c) * half, 16), half)
        first = [_remote(p_ref.at[mine], w_ref.at[kme, mine], send_sems.at[j], recv_sems.at[j], (*chips[j], c))
                 for j in range(3)]
        for cp in first:
            cp.start()
        passed = [_remote(w_ref.at[ks[j], mine], w_ref.at[ks[j], mine], send_sems.at[3 + j], recv_sems.at[3 + j], sib)
                  for j in range(3)]
        for j in range(3):
            _remote(p_ref.at[mine], w_ref.at[ks[j], mine], send_sems.at[j], recv_sems.at[j], sib).wait_recv()
            passed[j].start()
        for j in range(3):
            _remote(p_ref.at[mine], w_ref.at[ks[j], other], send_sems.at[3 + j], recv_sems.at[3 + j], sib).wait_recv()
        for cp in first + passed:
            cp.wait_send()

    return _run_comm(body, "gather_layer_%d" % cid, cid, [packed], [landing], [],
                     (pltpu.SemaphoreType.DMA((6,)), pltpu.SemaphoreType.DMA((6,))))[0]


def _sibling_swap(gbuf, cid):
    rtot = gbuf.shape[1]
    half = rtot // 2

    def body(g_ref, r_ref, send_sem, recv_sem):
        x, y, c, _, _ = _mesh_pos()
        sib = (x, y, 1 - c)
        _handshake([sib])
        other = pl.ds(pl.multiple_of((1 - c) * half, 16), half)
        cp = _remote(g_ref.at[:, other, :], r_ref, send_sem, recv_sem, sib)
        cp.start()
        cp.wait()

    return _run_comm(body, "sibling_swap_%d" % cid, cid, [gbuf], [],
                     [jax.ShapeDtypeStruct((NCHIP, half, D), BF)],
                     (pltpu.SemaphoreType.DMA, pltpu.SemaphoreType.DMA))[0]


def _sibling_swap_inline(gbuf):
    rtot = gbuf.shape[1]
    half = rtot // 2

    def body(g_ref, r_ref, send_sem, recv_sem):
        x, y, c, _, _ = _mesh_pos()
        other = pl.ds(pl.multiple_of((1 - c) * half, 16), half)
        cp = _remote(g_ref.at[:, other, :], r_ref, send_sem, recv_sem, (x, y, 1 - c))
        cp.start()
        cp.wait()

    return pl.pallas_call(
        body, name="sibling_swap_inline", in_specs=[ANY], out_specs=ANY,
        out_shape=jax.ShapeDtypeStruct((NCHIP, half, D), BF),
        scratch_shapes=[pltpu.SemaphoreType.DMA, pltpu.SemaphoreType.DMA],
    )(gbuf)


def _row_tile(rows):
    for cand in range(min(rows, 1280) // 16 * 16, 0, -16):
        if rows % cand == 0:
            return cand
    return rows


def _chip_sum(gbuf, rbuf, cidx):
    half = rbuf.shape[1]
    rt = _row_tile(half)
    nb = half // rt

    def body(c_ref, g_ref, r_ref, o_ref):
        o_ref[...] = (g_ref[...].astype(F32) + r_ref[...].astype(F32)).astype(BF)

    return pl.pallas_call(
        body, name="chip_sum",
        grid_spec=pltpu.PrefetchScalarGridSpec(
            num_scalar_prefetch=1, grid=(NCHIP, nb),
            in_specs=[pl.BlockSpec((None, rt, D), lambda k, r, c: (k, c[0] * nb + r, 0)),
                      pl.BlockSpec((None, rt, D), lambda k, r, c: (k, r, 0))],
            out_specs=pl.BlockSpec((None, rt, D), lambda k, r, c: (k, r, 0))),
        out_shape=jax.ShapeDtypeStruct((NCHIP, half, D), BF),
        compiler_params=_params(("arbitrary", "arbitrary")),
    )(cidx, gbuf, rbuf)


def _chip_exchange(sbuf, cid):
    half = sbuf.shape[1]

    def body(s_ref, x_ref, send_sems, recv_sems):
        x, y, c, _, chips = _mesh_pos()
        _handshake([(*ch, c) for ch in chips])
        cps = [_remote(s_ref.at[2 * cx + cy], x_ref.at[j], send_sems.at[j], recv_sems.at[j], (cx, cy, c))
               for j, (cx, cy) in enumerate(chips)]
        for cp in cps:
            cp.start()
        for cp in cps:
            cp.wait()

    return _run_comm(body, "chip_exchange_%d" % cid, cid, [sbuf], [],
                     [jax.ShapeDtypeStruct((3, half, D), BF)],
                     (pltpu.SemaphoreType.DMA((3,)), pltpu.SemaphoreType.DMA((3,))))[0]


def _shard_sum(gbuf, rbuf, xbuf, ck):
    half = rbuf.shape[1]
    rt = _row_tile(half)
    nb = half // rt

    def body(ck_ref, g_ref, r_ref, x_ref, o_ref):
        acc = g_ref[...].astype(F32) + r_ref[...].astype(F32)
        for j in range(3):
            acc = acc + x_ref[j].astype(F32)
        o_ref[...] = acc

    return pl.pallas_call(
        body, name="shard_sum",
        grid_spec=pltpu.PrefetchScalarGridSpec(
            num_scalar_prefetch=1, grid=(nb,),
            in_specs=[pl.BlockSpec((None, rt, D), lambda r, ck: (ck[1], ck[0] * nb + r, 0)),
                      pl.BlockSpec((None, rt, D), lambda r, ck: (ck[1], r, 0)),
                      pl.BlockSpec((3, rt, D), lambda r, ck: (0, r, 0))],
            out_specs=pl.BlockSpec((rt, D), lambda r, ck: (ck[0] * nb + r, 0))),
        out_shape=jax.ShapeDtypeStruct((2 * half, D), F32),
        compiler_params=_params(),
    )(ck, gbuf, rbuf, xbuf)


def _sibling_share(red, cid):
    half = red.shape[0] // 2

    def body(o_ref, send_sem, recv_sem):
        x, y, c, _, _ = _mesh_pos()
        sib = (x, y, 1 - c)
        _handshake([sib])
        mine = pl.ds(pl.multiple_of(c * half, 8), half)
        other = pl.ds(pl.multiple_of((1 - c) * half, 8), half)
        cp = _remote(o_ref.at[mine], o_ref.at[mine], send_sem, recv_sem, sib)
        cp.start()
        cp.wait_send()
        _remote(o_ref.at[mine], o_ref.at[other], send_sem, recv_sem, sib).wait_recv()

    return _run_comm(body, "sibling_share_%d" % cid, cid, [], [red], [],
                     (pltpu.SemaphoreType.DMA, pltpu.SemaphoreType.DMA))[0]


def _allreduce_small(v):
    rows = v.shape[0]
    ndev = 2 * NCHIP

    def body(v_ref, o_ref, gat, send_sems, recv_sems, lsem):
        x, y, c, _, chips = _mesh_pos()
        me, sib = (x, y, c), (x, y, 1 - c)

        def blk(px, py, pc):
            return gat.at[pl.ds((4 * px + 2 * py + pc) * rows, rows), :]

        def copy(k, block, to, src=None):
            return pltpu.make_async_remote_copy(
                src_ref=blk(*block) if src is None else src, dst_ref=blk(*block),
                send_sem=send_sems.at[k], recv_sem=recv_sems.at[k], device_id=to, device_id_type=MESH)

        mine = pltpu.make_async_copy(v_ref, blk(*me), lsem)
        mine.start()
        first = [copy(0, me, sib, src=v_ref)]
        first += [copy(1 + j, me, (*chip, c), src=v_ref) for j, chip in enumerate(chips)]
        for cp in first:
            cp.start()
        passed = [copy(4 + j, (*chip, c), sib) for j, chip in enumerate(chips)]
        for j, chip in enumerate(chips):
            copy(1 + j, (*chip, c), me).wait_recv()
            passed[j].start()
        copy(0, sib, me).wait_recv()
        for j, chip in enumerate(chips):
            copy(4 + j, (*chip, 1 - c), me).wait_recv()
        for cp in first + passed:
            cp.wait_send()
        mine.wait()
        acc = gat[0:rows, :]
        for d in range(1, ndev):
            acc = acc + gat[d * rows:(d + 1) * rows, :]
        o_ref[...] = acc

    vm = pl.BlockSpec(memory_space=pltpu.VMEM)
    return pl.pallas_call(
        body, name="allreduce_small", in_specs=[vm], out_specs=vm,
        out_shape=jax.ShapeDtypeStruct((rows, D), F32),
        scratch_shapes=[pltpu.VMEM((ndev * rows, D), F32), pltpu.SemaphoreType.DMA((7,)),
                        pltpu.SemaphoreType.DMA((7,)), pltpu.SemaphoreType.DMA],
    )(v)


def _adamw(w, g, m, v):
    shape = w.shape
    cols = shape[-1]
    rows = w.size // cols
    bm = rows
    for cand in (512, 256, 128, 64, 32, 16, 8):
        if rows % cand == 0:
            bm = cand
            break
    bc1 = 1.0 - ADAM_B1 ** ADAM_STEP
    bc2 = 1.0 - ADAM_B2 ** ADAM_STEP

    def body(w_ref, g_ref, m_ref, v_ref, d_ref, mo_ref, vo_ref):
        gv = g_ref[...]
        mn = ADAM_B1 * m_ref[...] + (1.0 - ADAM_B1) * gv
        vn = ADAM_B2 * v_ref[...] + (1.0 - ADAM_B2) * (gv * gv)
        mo_ref[...] = mn
        vo_ref[...] = vn
        d_ref[...] = -ADAM_LR * ((mn / bc1) / (jnp.sqrt(vn / bc2) + ADAM_EPS) + ADAM_WD * w_ref[...])

    spec = pl.BlockSpec((bm, cols), lambda i: (i, 0))
    out = jax.ShapeDtypeStruct((rows, cols), F32)
    d, mo, vo = pl.pallas_call(
        body, name="adamw", grid=(rows // bm,), in_specs=[spec] * 4, out_specs=[spec] * 3,
        out_shape=[out, out, out], compiler_params=_params(),
    )(*[t.reshape(rows, cols) for t in (w, g, m, v)])
    return d.reshape(shape), mo.reshape(shape), vo.reshape(shape)


def _adamw_layer(w, g, m, v, li, prev):
    cols = w.shape[-1]
    rows = w[0].size // cols
    bm = rows
    for cand in (512, 256, 128, 64, 32, 16, 8):
        if rows % cand == 0:
            bm = cand
            break
    bc1 = 1.0 - ADAM_B1 ** ADAM_STEP
    bc2 = 1.0 - ADAM_B2 ** ADAM_STEP

    def body(*refs):
        w_ref, g_ref, m_ref, v_ref = refs[:4]
        go_ref, d_ref, mo_ref, vo_ref = refs[-4:]
        gv = g_ref[...]
        mn = ADAM_B1 * m_ref[...] + (1.0 - ADAM_B1) * gv
        vn = ADAM_B2 * v_ref[...] + (1.0 - ADAM_B2) * (gv * gv)
        go_ref[...] = gv
        mo_ref[...] = mn
        vo_ref[...] = vn
        d_ref[...] = -ADAM_LR * ((mn / bc1) / (jnp.sqrt(vn / bc2) + ADAM_EPS) + ADAM_WD * w_ref[...])

    layer = pl.BlockSpec((None, bm, cols), lambda i: (li, i, 0))
    in_specs = [layer, pl.BlockSpec((bm, cols), lambda i: (i, 0)), layer, layer]
    args = [w.reshape(DEPTH, rows, cols), g.reshape(rows, cols), m.reshape(DEPTH, rows, cols),
            v.reshape(DEPTH, rows, cols)]
    aliases = {}
    if prev is not None:
        in_specs += [ANY] * 4
        args += list(prev)
        aliases = {4 + i: i for i in range(4)}
    out = jax.ShapeDtypeStruct((DEPTH, rows, cols), F32)
    return pl.pallas_call(
        body, name="adamw_layer", grid=(rows // bm,), in_specs=in_specs, out_specs=[layer] * 4,
        out_shape=[out] * 4, input_output_aliases=aliases, compiler_params=_params(),
    )(*args)


def _pack_shards(ws, li):
    pg = D // NG
    t = lambda a: jnp.swapaxes(a[li], 0, 1)
    parts = [t(ws["ffn1_w_gate"]), t(ws["ffn1_w_up"]), ws["ffn1_w_down"][li],
             t(ws["w_in"]), ws["conv_w_out"][li], ws["w_out"][li],
             t(ws["ffn2_w_gate"]), t(ws["ffn2_w_up"]), ws["ffn2_w_down"][li], ws["ple_w_gate"][li],
             t(ws["ple_w_proj"]).reshape(-1, D),
             jnp.swapaxes(ws["pool_w"][li], 0, 1).reshape(pg // NCHIP, D)]
    return jnp.concatenate([p.astype(BF) for p in parts], axis=0)


def _unpack_shards(parts):
    lay, _ = _layout()
    pg = D // NG

    def rows(n):
        off, rs = lay[n]
        for r0, arr in parts:
            if r0 <= off and off + rs <= r0 + arr.shape[0]:
                return arr[off - r0:off - r0 + rs, :]
        raise ValueError(n)

    t = lambda a: jnp.swapaxes(a, 0, 1)
    return {
        "ffn1_w_gate": t(rows("wg1")), "ffn1_w_up": t(rows("wu1")), "ffn1_w_down": rows("wd1"),
        "ffn2_w_gate": t(rows("wg2")), "ffn2_w_up": t(rows("wu2")), "ffn2_w_down": rows("wd2"),
        "w_in": t(rows("win")), "conv_w_out": rows("wco"), "w_out": rows("wo"), "ple_w_gate": rows("wpg"),
        "ple_w_proj": t(rows("wpp").reshape(D // NCHIP, PD)),
        "pool_w": jnp.swapaxes(rows("pw").reshape(pg // NCHIP, NG, pg), 0, 1),
    }


_BIG = ("ffn1_w_gate", "ffn1_w_up", "ffn1_w_down", "w_in", "pool_w", "conv_w_out", "w_out",
        "ffn2_w_gate", "ffn2_w_up", "ffn2_w_down", "ple_w_gate", "ple_w_proj")
_VECS = ("ffn1_norm", "mix_norm", "pool_scale", "conv_dw_b", "conv_ln_g", "conv_ln_b", "ffn2_norm", "ple_norm")
_WEIGHTS = ("ffn1_norm", "ffn1_w_gate", "ffn1_w_up", "ffn1_w_down", "mix_norm", "w_in", "pool_w", "pool_scale",
            "conv_dw_w", "conv_dw_b", "conv_ln_g", "conv_ln_b", "conv_w_out", "w_out", "ffn2_norm",
            "ffn2_w_gate", "ffn2_w_up", "ffn2_w_down", "ple_norm", "ple_w_gate", "ple_w_proj", "final_norm")


def _step(x, p, tgt, ws, ms, vs):
    lay, rtot = _layout()
    pg = D // NG
    cpos = lax.axis_index("c")
    kme = 2 * lax.axis_index("x") + lax.axis_index("y")
    cidx = jnp.stack([cpos]).astype(jnp.int32)
    ck = jnp.stack([cpos, kme]).astype(jnp.int32)
    h = x.reshape(T, D)
    tgt = tgt.reshape(T, D)

    nfirst = _first_rows()
    packed = [_pack_shards(ws, li) for li in range(DEPTH)]
    wfirst, wrest = [None] * DEPTH, [None] * DEPTH
    wfirst[0] = _gather_layer(packed[0][:nfirst], 0)
    piece, wfirst[0] = lax.optimization_barrier((packed[0][nfirst:], wfirst[0]))
    wrest[0] = _gather_layer(piece, 1)
    wppt, pwcat = [None] * DEPTH, [None] * DEPTH
    kk = ws["conv_dw_w"].shape[1]
    wdw_mine = jnp.zeros((DEPTH * HALO, D), F32)
    for li in range(DEPTH):
        blockw = jnp.zeros((kk, D), F32)
        mine = jnp.where(cpos == 0, ws["conv_dw_w"][li], 0.0)
        blockw = lax.dynamic_update_slice(blockw, mine, (0, kme * (D // NCHIP)))
        wdw_mine = wdw_mine.at[li * HALO:li * HALO + kk, :].set(blockw)
    wdw_all = _allreduce_small(wdw_mine)
    wdw = [wdw_all[li * HALO:(li + 1) * HALO, :] for li in range(DEPTH)]
    vec = lambda name, li: _row(ws[name][li])

    saved = []
    for li in range(DEPTH):
        if li > 0:
            wfirst[li], wrest[li], h = lax.optimization_barrier((wfirst[li], wrest[li], h))
        h0 = h
        h1, a1, b1 = _ffn_fwd(h0, vec("ffn1_norm", li), wfirst[li], 1)
        if li + 1 < DEPTH:
            piece, wrest[li], h1 = lax.optimization_barrier((packed[li + 1][:nfirst], wrest[li], h1))
            wfirst[li + 1] = _gather_layer(piece, 2 * li + 2)
        else:
            wrest[li], h1 = lax.optimization_barrier((wrest[li], h1))
        o, s = lay["wpp"]
        wppt[li] = wrest[li][:, o - nfirst:o - nfirst + s, :].reshape(D, PD)
        o, s = lay["pw"]
        pwcat[li] = wrest[li][:, o - nfirst:o - nfirst + s, :].reshape(pg, D)
        z = _mix_in_fwd(h1, vec("mix_norm", li), wrest[li])
        h2, pooled, c1, cc = _mix_mid_fwd(h1, z, pwcat[li], vec("pool_scale", li), wdw[li],
                                          vec("conv_dw_b", li), vec("conv_ln_g", li), vec("conv_ln_b", li),
                                          wrest[li])
        if li + 1 < DEPTH:
            piece, wfirst[li + 1], h2 = lax.optimization_barrier((packed[li + 1][nfirst:], wfirst[li + 1], h2))
            wrest[li + 1] = _gather_layer(piece, 2 * li + 3)
        h3, a2, b2 = _ffn_fwd(h2, vec("ffn2_norm", li), wrest[li], 2)
        h = _ple_fwd(h3, p[li, 0], vec("ple_norm", li), wppt[li], wrest[li])
        saved.append((h0, a1, b1, h1, z, pooled, c1, cc, h2, a2, b2, h3))

    dh, losscols, dgf = _loss_bwd(h, tgt, _row(ws["final_norm"]))
    loss = lax.psum(jnp.sum(losscols), ("x", "y", "c"))
    vecg = [dict() for _ in range(DEPTH)]
    dwdw = [None] * DEPTH
    def reduce_start(g, chain, inline=False):
        cid = 2 * DEPTH + 3 * chain
        return {"g": g, "r": _sibling_swap_inline(g) if inline else _sibling_swap(g, cid), "cid": cid}

    def reduce_mid(st, anchor):
        if anchor is not None:
            anchor, st["r"] = lax.optimization_barrier((anchor, st["r"]))
        sbuf = _chip_sum(st["g"], st["r"], cidx)
        if anchor is not None:
            anchor, sbuf = lax.optimization_barrier((anchor, sbuf))
        st["x"] = _chip_exchange(sbuf, st["cid"] + 1)
        return anchor

    def reduce_end(st, anchor):
        xb = st["x"]
        if anchor is not None:
            anchor, xb = lax.optimization_barrier((anchor, xb))
        rsum = _shard_sum(st["g"], st["r"], xb, ck)
        if anchor is not None:
            anchor, rsum = lax.optimization_barrier((anchor, rsum))
        return anchor, _sibling_share(rsum, st["cid"] + 2)

    parts = [[] for _ in range(DEPTH)]
    above = None
    nchain = 0
    for li in reversed(range(DEPTH)):
        h0, a1, b1, h1, z, pooled, c1, cc, h2, a2, b2, h3 = saved[li]
        w = wrest[li]
        dh, dgp, dwpp, gbuf = _ple_bwd(h3, dh, p[li, 0], vec("ple_norm", li), wppt[li], w)
        vecg[li]["ple_norm"] = dgp
        if above is not None:
            dh = reduce_mid(above[1], dh)
        dh_in, da, db, sact, n, dg = _ffn_bwd(h2, dh, a2, b2, vec("ffn2_norm", li), w, 2)
        vecg[li]["ffn2_norm"] = dg
        gbuf = _wgrad([da, db], n, gbuf, ("wg2", "wu2"), 0, F // 2, F // NCHIP)
        gbuf = _wgrad([sact], dh, gbuf, ("wd2",), 0, F // 2, F // NCHIP, yscale=0.5)
        dh = dh_in
        dp, dc1, dzb, small, dpw, gbuf = _mix_b1(dh, z, pooled, c1, cc, pwcat[li], vec("pool_scale", li),
                                                 vec("conv_ln_g", li), vec("conv_ln_b", li), w, gbuf)
        vecg[li]["pool_scale"] = small[0:1]
        vecg[li]["conv_dw_b"] = small[1:2]
        vecg[li]["conv_ln_g"] = small[2:3]
        vecg[li]["conv_ln_b"] = small[3:4]
        dza, dwdw[li] = _mix_b2(dp, dc1, z, wdw[li])
        if above is not None:
            dza, rsum = reduce_end(above[1], dza)
            parts[above[0]].append((0, rsum))
            above = None
        dh_in, u, dg = _mix_b3(h1, dh, dza, dzb, vec("mix_norm", li), w)
        vecg[li]["mix_norm"] = dg
        gbuf = _wgrad([dza], u, gbuf, ("win",), 0, D, D // NCHIP)
        gbuf = _wgrad([dzb], u, gbuf, ("win",), 3 * D, D, D // NCHIP)
        dh = dh_in
        o, s = lay["wpp"]
        small_rows = jnp.concatenate([dwpp.reshape(NCHIP, s, D), dpw.reshape(NCHIP, lay["pw"][1], D)], axis=1)
        gbuf = lax.dynamic_update_slice(gbuf, small_rows.astype(BF), (0, o, 0))
        if li == 0:
            rest = reduce_start(gbuf[:, nfirst:, :], nchain)
            nchain += 1
        dh_in, da, db, sact, n, dg = _ffn_bwd(h0, dh, a1, b1, vec("ffn1_norm", li), wfirst[li], 1)
        vecg[li]["ffn1_norm"] = dg
        if li == 0:
            da = reduce_mid(rest, da)
        gbuf = _wgrad([da, db], n, gbuf, ("wg1", "wu1"), 0, F // 2, F // NCHIP)
        gbuf = _wgrad([sact], dh, gbuf, ("wd1",), 0, F // 2, F // NCHIP, yscale=0.5)
        dh = dh_in
        if li == 0:
            first = reduce_start(gbuf[:, :nfirst, :], nchain, inline=True)
            nchain += 1
            parts[li].append((nfirst, reduce_end(rest, None)[1]))
            reduce_mid(first, None)
            parts[li].append((0, reduce_end(first, None)[1]))
        else:
            above = (li, reduce_start(gbuf, nchain))
            nchain += 1
    grad_x = dh.reshape(x.shape)

    rows = [vecg[li][n] for li in range(DEPTH) for n in _VECS] + [dgf]
    rows.append(jnp.zeros((8 - (len(rows) % 8), D), F32))
    vsum = _allreduce_small(jnp.concatenate(rows + dwdw, axis=0))
    nvec = len(_VECS)
    grads = {}
    unpacked = [_unpack_shards(parts[li]) for li in range(DEPTH)]
    for i, n in enumerate(_VECS):
        grads[n] = jnp.stack([vsum[li * nvec + i] for li in range(DEPTH)])
    grads["final_norm"] = vsum[DEPTH * nvec]
    base = DEPTH * nvec + 8 - ((DEPTH * nvec + 1) % 8) + 1
    dcols = D // NCHIP
    grads["conv_dw_w"] = jnp.stack([
        lax.dynamic_slice(vsum[base + li * HALO: base + li * HALO + kk, :], (0, kme * dcols), (kk, dcols))
        for li in range(DEPTH)])

    outs_g, outs_d, outs_m, outs_v = [], [], [], []
    for n in _WEIGHTS:
        if n in _BIG:
            res = None
            for li in reversed(range(DEPTH)):
                res = _adamw_layer(ws[n], unpacked[li][n], ms[n], vs[n], li, res)
            gq, d, mo, vo = [t.reshape(ws[n].shape) for t in res]
        else:
            gq = grads[n]
            d, mo, vo = _adamw(ws[n], gq, ms[n], vs[n])
        outs_g.append(gq)
        outs_d.append(d)
        outs_m.append(mo)
        outs_v.append(vo)
    return (loss, grad_x, *outs_g, *outs_d, *outs_m, *outs_v)


def kernel(x, p, ffn1_norm, ffn1_w_gate, ffn1_w_up, ffn1_w_down, mix_norm, w_in, pool_w, pool_scale, conv_dw_w, conv_dw_b, conv_ln_g, conv_ln_b, conv_w_out, w_out, ffn2_norm, ffn2_w_gate, ffn2_w_up, ffn2_w_down, ple_norm, ple_w_gate, ple_w_proj, final_norm, loss_target, m_ffn1_norm, m_ffn1_w_gate, m_ffn1_w_up, m_ffn1_w_down, m_mix_norm, m_w_in, m_pool_w, m_pool_scale, m_conv_dw_w, m_conv_dw_b, m_conv_ln_g, m_conv_ln_b, m_conv_w_out, m_w_out, m_ffn2_norm, m_ffn2_w_gate, m_ffn2_w_up, m_ffn2_w_down, m_ple_norm, m_ple_w_gate, m_ple_w_proj, m_final_norm, v_ffn1_norm, v_ffn1_w_gate, v_ffn1_w_up, v_ffn1_w_down, v_mix_norm, v_w_in, v_pool_w, v_pool_scale, v_conv_dw_w, v_conv_dw_b, v_conv_ln_g, v_conv_ln_b, v_conv_w_out, v_w_out, v_ffn2_norm, v_ffn2_w_gate, v_ffn2_w_up, v_ffn2_w_down, v_ple_norm, v_ple_w_gate, v_ple_w_proj, v_final_norm):
    ws = dict(zip(_WEIGHTS, (ffn1_norm, ffn1_w_gate, ffn1_w_up, ffn1_w_down, mix_norm, w_in, pool_w, pool_scale, conv_dw_w, conv_dw_b, conv_ln_g, conv_ln_b, conv_w_out, w_out, ffn2_norm, ffn2_w_gate, ffn2_w_up, ffn2_w_down, ple_norm, ple_w_gate, ple_w_proj, final_norm)))
    ms = dict(zip(_WEIGHTS, (m_ffn1_norm, m_ffn1_w_gate, m_ffn1_w_up, m_ffn1_w_down, m_mix_norm, m_w_in, m_pool_w, m_pool_scale, m_conv_dw_w, m_conv_dw_b, m_conv_ln_g, m_conv_ln_b, m_conv_w_out, m_w_out, m_ffn2_norm, m_ffn2_w_gate, m_ffn2_w_up, m_ffn2_w_down, m_ple_norm, m_ple_w_gate, m_ple_w_proj, m_final_norm)))
    vs = dict(zip(_WEIGHTS, (v_ffn1_norm, v_ffn1_w_gate, v_ffn1_w_up, v_ffn1_w_down, v_mix_norm, v_w_in, v_pool_w, v_pool_scale, v_conv_dw_w, v_conv_dw_b, v_conv_ln_g, v_conv_ln_b, v_conv_w_out, v_w_out, v_ffn2_norm, v_ffn2_w_gate, v_ffn2_w_up, v_ffn2_w_down, v_ple_norm, v_ple_w_gate, v_ple_w_proj, v_final_norm)))
    return _step(x, p, loss_target, ws, ms, vs)
```

```python
import jax
import jax.numpy as jnp
from jax import lax
from jax.experimental import pallas as pl
from jax.experimental.pallas import tpu as pltpu
from jax.experimental.pallas import tpu_sc as plsc

T = 8192
D = 1024
F = 2816
PD = 256
NG = 4
WINDOWS = (2, 4, 8, 16)
KC = 31
HALO = 32
DEPTH = 2
NCHIP = 4
RMS_EPS = 1e-6
LN_EPS = 1e-5

ADAM_LR = 0.001
ADAM_B1 = 0.9
ADAM_B2 = 0.999
ADAM_EPS = 1e-08
ADAM_WD = 0.01
ADAM_STEP = 10

TM = 512
TMB = 256
TMW = 512
LANES = 128
SUBLANES = 8
CHUNKS = 4
VMEM_LIMIT = 56 * 1024 * 1024

BF = jnp.bfloat16
F32 = jnp.float32
MESH = pl.DeviceIdType.MESH
ANY = pl.BlockSpec(memory_space=pl.ANY)
HBM = pltpu.MemorySpace.HBM


def _layout():
    fs, ins, ds = F // NCHIP, 5 * D // NCHIP, D // NCHIP
    pps = ds * PD // D
    pws = NG * (D // NG // NCHIP) * (D // NG) // D
    names = [("wg1", fs), ("wu1", fs), ("wd1", fs), ("win", ins), ("wco", ds), ("wo", ds),
             ("wg2", fs), ("wu2", fs), ("wd2", fs), ("wpg", ds), ("wpp", pps), ("pw", pws)]
    off, r = {}, 0
    for n, s in names:
        off[n] = (r, s)
        r += s
    return off, r


def _sig(v):
    return 1.0 / (1.0 + jnp.exp(-v))


def _dot_nn(a, b):
    return jnp.dot(a, b, preferred_element_type=F32)


def _dot_nt(a, b):
    return lax.dot_general(a, b, (((1,), (1,)), ((), ())), preferred_element_type=F32)


def _dot_tn(a, b):
    return lax.dot_general(a, b, (((0,), (0,)), ((), ())), preferred_element_type=F32)


def _params(sem=("arbitrary",)):
    return pltpu.CompilerParams(dimension_semantics=sem, vmem_limit_bytes=VMEM_LIMIT)


def _first_rows():
    return _layout()[0]["win"][0]


def _weight_copies(w_hbm, specs, sem):
    lay, _ = _layout()
    ra = _first_rows()
    cps = []
    for i, (name, dst) in enumerate(specs):
        off, rs = lay[name]
        off = off if off < ra else off - ra
        for k in range(NCHIP):
            cps.append(pltpu.make_async_copy(w_hbm.at[k, pl.ds(off, rs), :],
                                             dst.at[pl.ds(k * rs, rs), :], sem.at[i * NCHIP + k]))
    return cps


def _load_weights_once(w_hbm, specs, sem):
    @pl.when(pl.program_id(0) == 0)
    def _():
        cps = _weight_copies(w_hbm, specs, sem)
        for cp in cps:
            cp.start()
        for cp in cps:
            cp.wait()


def _grad_copies(stage, g_hbm, name, row0, rows, piece, sem, sem0):
    lay, _ = _layout()
    off, rs = lay[name]
    cps = []
    for i in range(rows // piece):
        rglob = row0 + i * piece
        k = rglob // rs
        loc = rglob - k * rs
        start = off + loc
        if not isinstance(start, int):
            start = pl.multiple_of(start, 16)
        dst = g_hbm.at[k, pl.ds(start, piece), :]
        cps.append(pltpu.make_async_copy(stage.at[pl.ds(i * piece, piece), :], dst, sem.at[sem0 + i]))
    return cps


def _row(v):
    return v.reshape(1, -1)


def _shifted_source(buf, sh, base, s, lanes):
    a, b = divmod(s, SUBLANES)
    rows = pl.ds(pl.multiple_of(base + SUBLANES * a, SUBLANES), SUBLANES)
    if b == 0:
        return buf[rows, lanes]
    return sh[b - 1, rows, :]


def _fill_shifted(buf, sh, lanes):
    rows = sh.shape[1]
    for b in range(1, SUBLANES):
        sh[b - 1, :, :] = buf[pl.ds(b, rows), lanes]


def _ffn_fwd(h, g, wfull, which):
    nt = T // TM
    fc = F // 2
    names = ("wg%d" % which, "wu%d" % which, "wd%d" % which)

    def body(h_ref, g_ref, w_hbm, ho_ref, a_ref, b_ref, wg, wu, wd, sem):
        _load_weights_once(w_hbm, ((names[0], wg), (names[1], wu), (names[2], wd)), sem)
        x = h_ref[...]
        r = lax.rsqrt(jnp.mean(x * x, axis=-1, keepdims=True) + RMS_EPS)
        n = (x * r * g_ref[...]).astype(BF)
        acc = jnp.zeros((TM, D), F32)
        for c in range(F // fc):
            sl = pl.ds(c * fc, fc)
            a = _dot_nt(n, wg[sl, :])
            b = _dot_nt(n, wu[sl, :])
            a_ref[:, sl] = a.astype(BF)
            b_ref[:, sl] = b.astype(BF)
            s = (a * _sig(a) * b).astype(BF)
            acc = acc + _dot_nn(s, wd[sl, :])
        ho_ref[...] = x + 0.5 * acc

    tile = lambda w: pl.BlockSpec((TM, w), lambda i: (i, 0))
    return pl.pallas_call(
        body, name="ffn_fwd", grid=(nt,),
        in_specs=[tile(D), pl.BlockSpec((1, D), lambda i: (0, 0)), ANY],
        out_specs=[tile(D), tile(F), tile(F)],
        out_shape=[jax.ShapeDtypeStruct((T, D), F32), jax.ShapeDtypeStruct((T, F), BF),
                   jax.ShapeDtypeStruct((T, F), BF)],
        scratch_shapes=[pltpu.VMEM((F, D), BF), pltpu.VMEM((F, D), BF), pltpu.VMEM((F, D), BF),
                        pltpu.SemaphoreType.DMA((3 * NCHIP,))],
        compiler_params=_params(),
    )(h, g, wfull)


def _mix_in_fwd(h, g, wfull):
    nt = T // TM
    nin = 5 * D

    def body(h_ref, g_ref, w_hbm, z_ref, win, sem):
        _load_weights_once(w_hbm, (("win", win),), sem)
        x = h_ref[...]
        r = lax.rsqrt(jnp.mean(x * x, axis=-1, keepdims=True) + RMS_EPS)
        u = (x * r * g_ref[...]).astype(BF)
        for c in range(5):
            sl = pl.ds(c * D, D)
            z_ref[:, sl] = _dot_nt(u, win[sl, :]).astype(BF)

    return pl.pallas_call(
        body, name="mix_in_fwd", grid=(nt,),
        in_specs=[pl.BlockSpec((TM, D), lambda i: (i, 0)), pl.BlockSpec((1, D), lambda i: (0, 0)), ANY],
        out_specs=pl.BlockSpec((TM, nin), lambda i: (i, 0)),
        out_shape=jax.ShapeDtypeStruct((T, nin), BF),
        scratch_shapes=[pltpu.VMEM((nin, D), BF), pltpu.SemaphoreType.DMA((NCHIP,))],
        compiler_params=_params(),
    )(h, g, wfull)


def _pool_counts(i, rows, w):
    t = i * TM + lax.broadcasted_iota(jnp.int32, (rows, 1), 0)
    return jnp.minimum(t + 1, w).astype(F32)


def _mix_mid_fwd(h, z, pwcat, pscale, wdw, bdw, lg, lb, wfull):
    nt = T // TM
    pg = D // NG
    hb = TM // HALO
    n_ext = HALO + TM
    pad = SUBLANES

    def body(h_ref, z_ref, zh_ref, pw_ref, ps_ref, wdw_ref, bdw_ref, lg_ref, lb_ref, w_hbm,
             h2_ref, p_ref, c1_ref, cc_ref, wco, wo, pa, pb, cbuf, sh, c1buf, ambuf, sem):
        i = pl.program_id(0)
        _load_weights_once(w_hbm, (("wco", wco), ("wo", wo)), sem)

        @pl.when(i == 0)
        def _():
            pa[0:pad, :] = jnp.zeros((pad, D), F32)
            pb[0:pad, :] = jnp.zeros((pad, D), F32)

        keep = (i > 0).astype(F32)
        zh = zh_ref[...].astype(F32) * keep
        za = z_ref[:, D:2 * D].astype(F32)
        zg = z_ref[:, 2 * D:3 * D].astype(F32)
        pa[pad:pad + HALO, :] = zh[:, 0:D]
        pa[pad + HALO:pad + n_ext, :] = z_ref[:, 0:D].astype(F32)
        cbuf[0:HALO, :] = zh[:, D:2 * D] * _sig(zh[:, 2 * D:3 * D])
        cbuf[HALO:n_ext, :] = za * _sig(zg)
        for g, w in enumerate(WINDOWS):
            ls = pl.ds(g * pg, pg)
            cur, nxt = pa, pb
            d = 1
            while d < w:
                nxt[pl.ds(pad, n_ext), ls] = cur[pl.ds(pad, n_ext), ls] + cur[pl.ds(pad - d, n_ext), ls]
                cur, nxt = nxt, cur
                d *= 2
            tok = z_ref[:, ls].astype(F32)
            pooled = (cur[pl.ds(pad + HALO, TM), ls] / _pool_counts(i, TM, w) - tok).astype(BF)
            p_ref[:, ls] = pooled
            ambuf[:, ls] = _dot_nn(pooled, pw_ref[:, ls])
        am = ambuf[...] * ps_ref[...]
        for l in range(D // LANES):
            lanes = pl.ds(l * LANES, LANES)
            _fill_shifted(cbuf, sh, lanes)
            bias = jnp.broadcast_to(bdw_ref[:, lanes], (SUBLANES, LANES))

            def conv_rows(r, carry):
                base = r * (CHUNKS * SUBLANES)
                accs = [bias] * CHUNKS
                for k in range(KC):
                    wk = jnp.broadcast_to(wdw_ref[k:k + 1, lanes], (SUBLANES, LANES))
                    for j in range(CHUNKS):
                        src = _shifted_source(cbuf, sh, base + j * SUBLANES, HALO - (KC - 1) + k, lanes)
                        accs[j] = accs[j] + wk * src
                for j in range(CHUNKS):
                    c1buf[pl.ds(pl.multiple_of(base + j * SUBLANES, SUBLANES), SUBLANES), lanes] = accs[j]
                return carry

            lax.fori_loop(0, TM // (CHUNKS * SUBLANES), conv_rows, 0)
        c1b = c1buf[...].astype(BF)
        c1_ref[...] = c1b
        c1 = c1b.astype(F32)
        mu = jnp.mean(c1, axis=-1, keepdims=True)
        xc = c1 - mu
        var = jnp.mean(xc * xc, axis=-1, keepdims=True)
        c2 = xc * lax.rsqrt(var + LN_EPS) * lg_ref[...] + lb_ref[...]
        c3 = (c2 * _sig(c2)).astype(BF)
        ccb = _dot_nn(c3, wco[...]).astype(BF)
        cc_ref[...] = ccb
        gp = z_ref[:, 3 * D:4 * D].astype(F32)
        gc = z_ref[:, 4 * D:5 * D].astype(F32)
        m = (_sig(gp) * am + _sig(gc) * ccb.astype(F32)).astype(BF)
        h2_ref[...] = h_ref[...] + _dot_nn(m, wo[...])

    tile = pl.BlockSpec((TM, D), lambda i: (i, 0))
    vec = pl.BlockSpec((1, D), lambda i: (0, 0))
    return pl.pallas_call(
        body, name="mix_mid_fwd", grid=(nt,),
        in_specs=[tile, pl.BlockSpec((TM, 5 * D), lambda i: (i, 0)),
                  pl.BlockSpec((HALO, 3 * D), lambda i: (jnp.maximum(i * hb - 1, 0), 0)),
                  pl.BlockSpec((pg, D), lambda i: (0, 0)), vec,
                  pl.BlockSpec((HALO, D), lambda i: (0, 0)), vec, vec, vec, ANY],
        out_specs=[tile, tile, tile, tile],
        out_shape=[jax.ShapeDtypeStruct((T, D), F32), jax.ShapeDtypeStruct((T, D), BF),
                   jax.ShapeDtypeStruct((T, D), BF), jax.ShapeDtypeStruct((T, D), BF)],
        scratch_shapes=[pltpu.VMEM((D, D), BF), pltpu.VMEM((D, D), BF),
                        pltpu.VMEM((pad + n_ext, D), F32), pltpu.VMEM((pad + n_ext, D), F32),
                        pltpu.VMEM((n_ext, D), F32), pltpu.VMEM((SUBLANES - 1, n_ext - SUBLANES, LANES), F32),
                        pltpu.VMEM((TM, D), F32), pltpu.VMEM((TM, D), F32),
                        pltpu.SemaphoreType.DMA((2 * NCHIP,))],
        compiler_params=_params(),
    )(h, z, z, pwcat, pscale, wdw, bdw, lg, lb, wfull)


def _ple_fwd(h, p, g, wppt, wfull):
    nt = T // TM

    def body(h_ref, p_ref, g_ref, wpp_ref, w_hbm, ho_ref, wpg, sem):
        _load_weights_once(w_hbm, (("wpg", wpg),), sem)
        x = h_ref[...]
        r = lax.rsqrt(jnp.mean(x * x, axis=-1, keepdims=True) + RMS_EPS)
        n = (x * r * g_ref[...]).astype(BF)
        gate = _sig(_dot_nn(n, wpg[...]))
        pe = _dot_nt(p_ref[...].astype(BF), wpp_ref[...])
        ho_ref[...] = x + gate * pe

    tile = pl.BlockSpec((TM, D), lambda i: (i, 0))
    return pl.pallas_call(
        body, name="ple_fwd", grid=(nt,),
        in_specs=[tile, pl.BlockSpec((TM, PD), lambda i: (i, 0)), pl.BlockSpec((1, D), lambda i: (0, 0)),
                  pl.BlockSpec((D, PD), lambda i: (0, 0)), ANY],
        out_specs=tile, out_shape=jax.ShapeDtypeStruct((T, D), F32),
        scratch_shapes=[pltpu.VMEM((D, D), BF), pltpu.SemaphoreType.DMA((NCHIP,))],
        compiler_params=_params(),
    )(h, p, g, wppt, wfull)


def _loss_bwd(h, tgt, g):
    nt = T // TM

    def body(h_ref, t_ref, g_ref, dh_ref, loss_ref, dg_ref):
        @pl.when(pl.program_id(0) == 0)
        def _():
            loss_ref[...] = jnp.zeros_like(loss_ref)
            dg_ref[...] = jnp.zeros_like(dg_ref)
        x = h_ref[...]
        r = lax.rsqrt(jnp.mean(x * x, axis=-1, keepdims=True) + RMS_EPS)
        xh = x * r
        gv = g_ref[...]
        e = xh * gv - t_ref[...]
        loss_ref[...] += jnp.sum(e * e, axis=0, keepdims=True) * (0.5 / D)
        dy = e * (1.0 / D)
        dg_ref[...] += jnp.sum(dy * xh, axis=0, keepdims=True)
        dxh = dy * gv
        dh_ref[...] = r * (dxh - xh * jnp.mean(dxh * xh, axis=-1, keepdims=True))

    tile = pl.BlockSpec((TM, D), lambda i: (i, 0))
    vec = pl.BlockSpec((1, D), lambda i: (0, 0))
    return pl.pallas_call(
        body, name="loss_bwd", grid=(nt,), in_specs=[tile, tile, vec], out_specs=[tile, vec, vec],
        out_shape=[jax.ShapeDtypeStruct((T, D), F32), jax.ShapeDtypeStruct((1, D), F32),
                   jax.ShapeDtypeStruct((1, D), F32)],
        compiler_params=_params(),
    )(h, tgt, g)


def _ple_bwd(h, dh, p, g, wppt, wfull):
    nt = T // TM
    _, rtot = _layout()

    def body(h_ref, dh_ref, p_ref, g_ref, wpp_ref, w_hbm,
             dho_ref, dg_ref, dwpp_ref, g_out, wpg, acc, stage, sem, osem):
        i = pl.program_id(0)
        _load_weights_once(w_hbm, (("wpg", wpg),), sem)

        @pl.when(i == 0)
        def _():
            dg_ref[...] = jnp.zeros_like(dg_ref)
            dwpp_ref[...] = jnp.zeros_like(dwpp_ref)
            acc[...] = jnp.zeros_like(acc)

        x = h_ref[...]
        r = lax.rsqrt(jnp.mean(x * x, axis=-1, keepdims=True) + RMS_EPS)
        xh = x * r
        gv = g_ref[...]
        n = (xh * gv).astype(BF)
        gate = _sig(_dot_nn(n, wpg[...]))
        pb = p_ref[...].astype(BF)
        pe = _dot_nt(pb, wpp_ref[...])
        d = dh_ref[...]
        dpe = (d * gate).astype(BF)
        dq = (d * pe * gate * (1.0 - gate)).astype(BF)
        dwpp_ref[...] += _dot_tn(dpe, pb)
        acc[...] += _dot_tn(n, dq)
        dn = _dot_nt(dq, wpg[...])
        dg_ref[...] += jnp.sum(dn * xh, axis=0, keepdims=True)
        dxh = dn * gv
        dho_ref[...] = d + r * (dxh - xh * jnp.mean(dxh * xh, axis=-1, keepdims=True))

        @pl.when(i == nt - 1)
        def _():
            stage[...] = acc[...].astype(BF)
            cps = _grad_copies(stage, g_out, "wpg", 0, D, D // NCHIP, osem, 0)
            for cp in cps:
                cp.start()
            for cp in cps:
                cp.wait()

    tile = pl.BlockSpec((TM, D), lambda i: (i, 0))
    vec = pl.BlockSpec((1, D), lambda i: (0, 0))
    return pl.pallas_call(
        body, name="ple_bwd", grid=(nt,),
        in_specs=[tile, tile, pl.BlockSpec((TM, PD), lambda i: (i, 0)), vec,
                  pl.BlockSpec((D, PD), lambda i: (0, 0)), ANY],
        out_specs=[tile, vec, pl.BlockSpec((D, PD), lambda i: (0, 0)), ANY],
        out_shape=[jax.ShapeDtypeStruct((T, D), F32), jax.ShapeDtypeStruct((1, D), F32),
                   jax.ShapeDtypeStruct((D, PD), F32),
                   jax.ShapeDtypeStruct((NCHIP, rtot, D), BF)],
        scratch_shapes=[pltpu.VMEM((D, D), BF), pltpu.VMEM((D, D), F32), pltpu.VMEM((D, D), BF),
                        pltpu.SemaphoreType.DMA((NCHIP,)), pltpu.SemaphoreType.DMA((NCHIP,))],
        compiler_params=_params(),
    )(h, dh, p, g, wppt, wfull)


def _ffn_bwd(h, dh, a, b, g, wfull, which):
    tm = TMB
    nt = T // tm
    fc = F // 2
    names = ("wg%d" % which, "wu%d" % which, "wd%d" % which)

    def body(h_ref, dh_ref, a_ref, b_ref, g_ref, w_hbm,
             dho_ref, da_ref, db_ref, s_ref, n_ref, dg_ref, wg, wu, wd, sem):
        _load_weights_once(w_hbm, ((names[0], wg), (names[1], wu), (names[2], wd)), sem)

        @pl.when(pl.program_id(0) == 0)
        def _():
            dg_ref[...] = jnp.zeros_like(dg_ref)

        x = h_ref[...]
        r = lax.rsqrt(jnp.mean(x * x, axis=-1, keepdims=True) + RMS_EPS)
        xh = x * r
        gv = g_ref[...]
        n_ref[...] = (xh * gv).astype(BF)
        d = dh_ref[...]
        df = (0.5 * d).astype(BF)
        dn = jnp.zeros((tm, D), F32)
        for c in range(F // fc):
            sl = pl.ds(c * fc, fc)
            av = a_ref[:, sl].astype(F32)
            bv = b_ref[:, sl].astype(F32)
            ds = _dot_nt(df, wd[sl, :])
            sg = _sig(av)
            sil = av * sg
            s_ref[:, sl] = (sil * bv).astype(BF)
            da = (ds * bv * (sg * (1.0 + av * (1.0 - sg)))).astype(BF)
            db = (ds * sil).astype(BF)
            da_ref[:, sl] = da
            db_ref[:, sl] = db
            dn = dn + _dot_nn(da, wg[sl, :]) + _dot_nn(db, wu[sl, :])
        dg_ref[...] += jnp.sum(dn * xh, axis=0, keepdims=True)
        dxh = dn * gv
        dho_ref[...] = d + r * (dxh - xh * jnp.mean(dxh * xh, axis=-1, keepdims=True))

    tile = lambda w: pl.BlockSpec((tm, w), lambda i: (i, 0))
    vec = pl.BlockSpec((1, D), lambda i: (0, 0))
    return pl.pallas_call(
        body, name="ffn_bwd", grid=(nt,),
        in_specs=[tile(D), tile(D), tile(F), tile(F), vec, ANY],
        out_specs=[tile(D), tile(F), tile(F), tile(F), tile(D), vec],
        out_shape=[jax.ShapeDtypeStruct((T, D), F32), jax.ShapeDtypeStruct((T, F), BF),
                   jax.ShapeDtypeStruct((T, F), BF), jax.ShapeDtypeStruct((T, F), BF),
                   jax.ShapeDtypeStruct((T, D), BF), jax.ShapeDtypeStruct((1, D), F32)],
        scratch_shapes=[pltpu.VMEM((F, D), BF), pltpu.VMEM((F, D), BF), pltpu.VMEM((F, D), BF),
                        pltpu.SemaphoreType.DMA((3 * NCHIP,))],
        compiler_params=_params(),
    )(h, dh, a, b, g, wfull)


def _wgrad(xs, y, gbuf, names, row0, rb, piece, yscale=None):
    nx = len(xs)
    rx = xs[0].shape[1]
    nj = rx // rb
    nt = T // TMW
    npiece = rb // piece

    def body(*refs):
        x_refs = refs[:nx]
        y_ref = refs[nx]
        g_out = refs[nx + 2]
        accs = refs[nx + 3:2 * nx + 3]
        stages = refs[2 * nx + 3:3 * nx + 3]
        osem = refs[3 * nx + 3]
        j = pl.program_id(0)
        t = pl.program_id(1)

        @pl.when(t == 0)
        def _():
            for acc in accs:
                acc[...] = jnp.zeros_like(acc)

        yv = y_ref[...]
        if yscale is not None:
            yv = (yscale * yv).astype(BF)
        for x_ref, acc in zip(x_refs, accs):
            acc[...] += _dot_tn(x_ref[...], yv)

        @pl.when(t == nt - 1)
        def _():
            cps = []
            for xi in range(nx):
                stages[xi][...] = accs[xi][...].astype(BF)
                cps += _grad_copies(stages[xi], g_out, names[xi], row0 + j * rb, rb, piece,
                                    osem, xi * npiece)
            for cp in cps:
                cp.start()
            for cp in cps:
                cp.wait()

    in_specs = [pl.BlockSpec((TMW, rb), lambda j, t: (t, j)) for _ in xs]
    in_specs += [pl.BlockSpec((TMW, D), lambda j, t: (t, 0)), ANY]
    return pl.pallas_call(
        body, name="wgrad", grid=(nj, nt), in_specs=in_specs, out_specs=ANY,
        out_shape=jax.ShapeDtypeStruct(gbuf.shape, gbuf.dtype),
        scratch_shapes=([pltpu.VMEM((rb, D), F32) for _ in xs] + [pltpu.VMEM((rb, D), BF) for _ in xs]
                        + [pltpu.SemaphoreType.DMA((nx * npiece,))]),
        input_output_aliases={nx + 1: 0},
        compiler_params=_params(("arbitrary", "arbitrary")),
    )(*xs, y, gbuf)


def _mix_b1(dh, z, pooled, c1, cc, pwcat, pscale, lg, lb, wfull, gbuf):
    nt = T // TM
    pg = D // NG

    def body(dh_ref, gp_ref, gc_ref, p_ref, c1_ref, cc_ref, pw_ref, ps_ref, lg_ref, lb_ref, w_hbm, _g_in,
             dp_ref, dc1_ref, dzb_ref, small_ref, dpw_ref, g_out,
             wco, wo, acc_o, acc_co, stage_o, stage_co, qbuf, sem, osem):
        i = pl.program_id(0)
        _load_weights_once(w_hbm, (("wco", wco), ("wo", wo)), sem)

        @pl.when(i == 0)
        def _():
            small_ref[...] = jnp.zeros_like(small_ref)
            dpw_ref[...] = jnp.zeros_like(dpw_ref)
            acc_o[...] = jnp.zeros_like(acc_o)
            acc_co[...] = jnp.zeros_like(acc_co)

        dhb = dh_ref[...].astype(BF)
        dm = _dot_nt(dhb, wo[...])
        sp = _sig(gp_ref[...].astype(F32))
        sc = _sig(gc_ref[...].astype(F32))
        for g in range(NG):
            ls = pl.ds(g * pg, pg)
            qbuf[:, ls] = _dot_nn(p_ref[:, ls], pw_ref[:, ls])
        q = qbuf[...]
        psv = ps_ref[...]
        am = q * psv
        ccv = cc_ref[...].astype(F32)
        m = (sp * am + sc * ccv).astype(BF)
        acc_o[...] += _dot_tn(m, dhb)
        dam = dm * sp
        dzb_ref[:, 0:D] = (dm * am * sp * (1.0 - sp)).astype(BF)
        dccb = (dm * sc).astype(BF)
        dzb_ref[:, D:2 * D] = (dm * ccv * sc * (1.0 - sc)).astype(BF)
        small_ref[0:1, :] += jnp.sum(dam * q, axis=0, keepdims=True)
        dq = (dam * psv).astype(BF)
        for g in range(NG):
            ls = pl.ds(g * pg, pg)
            dqg = dq[:, g * pg:(g + 1) * pg]
            dp_ref[:, ls] = _dot_nt(dqg, pw_ref[:, ls]).astype(BF)
            dpw_ref[:, ls] += _dot_tn(p_ref[:, ls], dqg)
        c1v = c1_ref[...].astype(F32)
        mu = jnp.mean(c1v, axis=-1, keepdims=True)
        xc = c1v - mu
        var = jnp.mean(xc * xc, axis=-1, keepdims=True)
        rs = lax.rsqrt(var + LN_EPS)
        c2n = xc * rs
        lgv = lg_ref[...]
        c2 = c2n * lgv + lb_ref[...]
        sg2 = _sig(c2)
        c3 = (c2 * sg2).astype(BF)
        acc_co[...] += _dot_tn(c3, dccb)
        dc3 = _dot_nt(dccb, wco[...])
        dc2 = dc3 * (sg2 * (1.0 + c2 * (1.0 - sg2)))
        small_ref[2:3, :] += jnp.sum(dc2 * c2n, axis=0, keepdims=True)
        small_ref[3:4, :] += jnp.sum(dc2, axis=0, keepdims=True)
        dc2n = dc2 * lgv
        dc1 = rs * (dc2n - jnp.mean(dc2n, axis=-1, keepdims=True)
                    - c2n * jnp.mean(dc2n * c2n, axis=-1, keepdims=True))
        small_ref[1:2, :] += jnp.sum(dc1, axis=0, keepdims=True)
        dc1_ref[...] = dc1.astype(BF)

        @pl.when(i == nt - 1)
        def _():
            stage_o[...] = acc_o[...].astype(BF)
            stage_co[...] = acc_co[...].astype(BF)
            cps = _grad_copies(stage_o, g_out, "wo", 0, D, D // NCHIP, osem, 0)
            cps += _grad_copies(stage_co, g_out, "wco", 0, D, D // NCHIP, osem, NCHIP)
            for cp in cps:
                cp.start()
            for cp in cps:
                cp.wait()

    tile = pl.BlockSpec((TM, D), lambda i: (i, 0))
    vec = pl.BlockSpec((1, D), lambda i: (0, 0))
    full = lambda r: pl.BlockSpec((r, D), lambda i: (0, 0))
    return pl.pallas_call(
        body, name="mix_b1", grid=(nt,),
        in_specs=[tile, pl.BlockSpec((TM, D), lambda i: (i, 3)), pl.BlockSpec((TM, D), lambda i: (i, 4)),
                  tile, tile, tile, full(pg), vec, vec, vec, ANY, ANY],
        out_specs=[tile, tile, pl.BlockSpec((TM, 2 * D), lambda i: (i, 0)), full(8), full(pg), ANY],
        out_shape=[jax.ShapeDtypeStruct((T, D), BF), jax.ShapeDtypeStruct((T, D), BF),
                   jax.ShapeDtypeStruct((T, 2 * D), BF), jax.ShapeDtypeStruct((8, D), F32),
                   jax.ShapeDtypeStruct((pg, D), F32), jax.ShapeDtypeStruct(gbuf.shape, gbuf.dtype)],
        scratch_shapes=[pltpu.VMEM((D, D), BF), pltpu.VMEM((D, D), BF),
                        pltpu.VMEM((D, D), F32), pltpu.VMEM((D, D), F32),
                        pltpu.VMEM((D, D), BF), pltpu.VMEM((D, D), BF),
                        pltpu.VMEM((TM, D), F32),
                        pltpu.SemaphoreType.DMA((2 * NCHIP,)), pltpu.SemaphoreType.DMA((2 * NCHIP,))],
        input_output_aliases={11: 5},
        compiler_params=_params(),
    )(dh, z, z, pooled, c1, cc, pwcat, pscale, lg, lb, wfull, gbuf)


def _mix_b2(dp, dc1, z, wdw):
    nt = T // TM
    pg = D // NG
    hb = TM // HALO
    nhb = T // HALO
    n_ext = TM + HALO
    pad = SUBLANES

    def body(dp_ref, dpn_ref, dc_ref, dcn_ref, za_ref, zg_ref, wdw_ref,
             dza_ref, dw_ref, pa, pb, cbuf, sh, c0buf, dc0buf):
        i = pl.program_id(0)

        @pl.when(i == 0)
        def _():
            dw_ref[...] = jnp.zeros_like(dw_ref)
            pa[n_ext:n_ext + pad, :] = jnp.zeros((pad, D), F32)
            pb[n_ext:n_ext + pad, :] = jnp.zeros((pad, D), F32)

        more = (i < nt - 1).astype(F32)
        for g, w in enumerate(WINDOWS):
            ls = pl.ds(g * pg, pg)
            cur_dp = dp_ref[:, ls].astype(F32)
            pa[0:TM, ls] = cur_dp / _pool_counts(i, TM, w)
            pa[TM:n_ext, ls] = dpn_ref[:, ls].astype(F32) * (more / w)
            cur, nxt = pa, pb
            d = 1
            while d < w:
                nxt[pl.ds(0, n_ext), ls] = cur[pl.ds(0, n_ext), ls] + cur[pl.ds(d, n_ext), ls]
                cur, nxt = nxt, cur
                d *= 2
            dza_ref[:, ls] = (cur[pl.ds(0, TM), ls] - cur_dp).astype(BF)
        za = za_ref[...].astype(F32)
        sg = _sig(zg_ref[...].astype(F32))
        cbuf[0:TM, :] = dc_ref[...].astype(F32)
        cbuf[TM:n_ext, :] = dcn_ref[...].astype(F32) * more
        c0buf[...] = za * sg
        for l in range(D // LANES):
            lanes = pl.ds(l * LANES, LANES)
            _fill_shifted(cbuf, sh, lanes)

            def conv_rows(r, accs):
                base = r * (CHUNKS * SUBLANES)
                rows = [pl.ds(pl.multiple_of(base + j * SUBLANES, SUBLANES), SUBLANES) for j in range(CHUNKS)]
                c0v = [c0buf[rows[j], lanes] for j in range(CHUNKS)]
                acc = [jnp.zeros((SUBLANES, LANES), F32)] * CHUNKS
                new = list(accs)
                for k in range(KC):
                    wk = jnp.broadcast_to(wdw_ref[k:k + 1, lanes], (SUBLANES, LANES))
                    for j in range(CHUNKS):
                        src = _shifted_source(cbuf, sh, base + j * SUBLANES, KC - 1 - k, lanes)
                        acc[j] = acc[j] + wk * src
                        new[k] = new[k] + c0v[j] * src
                for j in range(CHUNKS):
                    dc0buf[rows[j], lanes] = acc[j]
                return tuple(new)

            init = tuple(jnp.zeros((SUBLANES, LANES), F32) for _ in range(KC))
            accs = lax.fori_loop(0, TM // (CHUNKS * SUBLANES), conv_rows, init)
            for k in range(KC):
                dw_ref[k:k + 1, lanes] += jnp.sum(accs[k], axis=0, keepdims=True)
        dc0 = dc0buf[...]
        dza_ref[:, D:2 * D] = (dc0 * sg).astype(BF)
        dza_ref[:, 2 * D:3 * D] = (dc0 * za * sg * (1.0 - sg)).astype(BF)

    tile = pl.BlockSpec((TM, D), lambda i: (i, 0))
    nxt_spec = pl.BlockSpec((HALO, D), lambda i: (jnp.minimum((i + 1) * hb, nhb - 1), 0))
    return pl.pallas_call(
        body, name="mix_b2", grid=(nt,),
        in_specs=[tile, nxt_spec, tile, nxt_spec, pl.BlockSpec((TM, D), lambda i: (i, 1)),
                  pl.BlockSpec((TM, D), lambda i: (i, 2)), pl.BlockSpec((HALO, D), lambda i: (0, 0))],
        out_specs=[pl.BlockSpec((TM, 3 * D), lambda i: (i, 0)), pl.BlockSpec((HALO, D), lambda i: (0, 0))],
        out_shape=[jax.ShapeDtypeStruct((T, 3 * D), BF), jax.ShapeDtypeStruct((HALO, D), F32)],
        scratch_shapes=[pltpu.VMEM((n_ext + pad, D), F32), pltpu.VMEM((n_ext + pad, D), F32),
                        pltpu.VMEM((n_ext, D), F32), pltpu.VMEM((SUBLANES - 1, n_ext - SUBLANES, LANES), F32),
                        pltpu.VMEM((TM, D), F32), pltpu.VMEM((TM, D), F32)],
        compiler_params=_params(),
    )(dp, dp, dc1, dc1, z, z, wdw)


def _mix_b3(h, dh, dza, dzb, g, wfull):
    nt = T // TM

    def body(h_ref, dh_ref, dza_ref, dzb_ref, g_ref, w_hbm, dho_ref, u_ref, dg_ref, win, sem):
        _load_weights_once(w_hbm, (("win", win),), sem)

        @pl.when(pl.program_id(0) == 0)
        def _():
            dg_ref[...] = jnp.zeros_like(dg_ref)

        x = h_ref[...]
        r = lax.rsqrt(jnp.mean(x * x, axis=-1, keepdims=True) + RMS_EPS)
        xh = x * r
        gv = g_ref[...]
        u_ref[...] = (xh * gv).astype(BF)
        du = _dot_nn(dza_ref[...], win[0:3 * D, :]) + _dot_nn(dzb_ref[...], win[3 * D:5 * D, :])
        dg_ref[...] += jnp.sum(du * xh, axis=0, keepdims=True)
        dxh = du * gv
        dho_ref[...] = dh_ref[...] + r * (dxh - xh * jnp.mean(dxh * xh, axis=-1, keepdims=True))

    tile = lambda w: pl.BlockSpec((TM, w), lambda i: (i, 0))
    vec = pl.BlockSpec((1, D), lambda i: (0, 0))
    return pl.pallas_call(
        body, name="mix_b3", grid=(nt,),
        in_specs=[tile(D), tile(D), tile(3 * D), tile(2 * D), vec, ANY],
        out_specs=[tile(D), tile(D), vec],
        out_shape=[jax.ShapeDtypeStruct((T, D), F32), jax.ShapeDtypeStruct((T, D), BF),
                   jax.ShapeDtypeStruct((1, D), F32)],
        scratch_shapes=[pltpu.VMEM((5 * D, D), BF), pltpu.SemaphoreType.DMA((NCHIP,))],
        compiler_params=_params(),
    )(h, dh, dza, dzb, g, wfull)


def _mesh_pos():
    x, y, c = lax.axis_index("x"), lax.axis_index("y"), lax.axis_index("c")
    chips = [(1 - x, y), (x, 1 - y), (1 - x, 1 - y)]
    return x, y, c, 2 * x + y, chips


def _handshake(peers):
    barrier = pltpu.get_barrier_semaphore()
    for peer in peers:
        pl.semaphore_signal(barrier, inc=1, device_id=peer, device_id_type=MESH)
    pl.semaphore_wait(barrier, len(peers))


def _run_comm(body, name, cid, ins, inouts, out_types, scratch):
    in_refs = [jax.new_ref(a, memory_space=HBM) for a in ins]
    inout_refs = [jax.new_ref(a, memory_space=HBM) for a in inouts]
    out_refs = [jax.empty_ref(t, memory_space=HBM) for t in out_types]

    @pl.kernel(mesh=plsc.ScalarSubcoreMesh(axis_name="seq", num_cores=1), name=name,
               scratch_types=scratch, compiler_params=pltpu.CompilerParams(collective_id=cid))
    def launch(*scr):
        body(*in_refs, *inout_refs, *out_refs, *scr)

    launch()
    return [r[...] for r in inout_refs + out_refs]


def _remote(src, dst, send_sem, recv_sem, to):
    return pltpu.make_async_remote_copy(src_ref=src, dst_ref=dst, send_sem=send_sem, recv_sem=recv_sem,
                                        device_id=to, device_id_type=MESH)


def _gather_layer(packed, cid):
    rtot = packed.shape[0]
    half = rtot // 2
    kme = 2 * lax.axis_index("x") + lax.axis_index("y")
    landing = lax.dynamic_update_slice(lax.empty((NCHIP, rtot, D), BF), packed[None], (kme, 0, 0))

    def body(p_ref, w_ref, send_sems, recv_sems):
        x, y, c, kme, chips = _mesh_pos()
        sib = (x, y, 1 - c)
        _handshake([(*ch, c) for ch in chips] + [sib])
        ks = [2 * cx + cy for cx, cy in chips]
        mine = pl.ds(pl.multiple_of(c * half, 16), half)
        other = pl.ds(pl.multiple_of((1 - c) * half, 16), half)
        first = [_remote(p_ref.at[mine], w_ref.at[kme, mine], send_sems.at[j], recv_sems.at[j], (*chips[j], c))
                 for j in range(3)]
        for cp in first:
            cp.start()
        passed = [_remote(w_ref.at[ks[j], mine], w_ref.at[ks[j], mine], send_sems.at[3 + j], recv_sems.at[3 + j], sib)
                  for j in range(3)]
        for j in range(3):
            _remote(p_ref.at[mine], w_ref.at[ks[j], mine], send_sems.at[j], recv_sems.at[j], sib).wait_recv()
            passed[j].start()
        for j in range(3):
            _remote(p_ref.at[mine], w_ref.at[ks[j], other], send_sems.at[3 + j], recv_sems.at[3 + j], sib).wait_recv()
        for cp in first + passed:
            cp.wait_send()

    return _run_comm(body, "gather_layer_%d" % cid, cid, [packed], [landing], [],
                     (pltpu.SemaphoreType.DMA((6,)), pltpu.SemaphoreType.DMA((6,))))[0]


def _sibling_swap(gbuf, cid):
    rtot = gbuf.shape[1]
    half = rtot // 2

    def body(g_ref, r_ref, send_sem, recv_sem):
        x, y, c, _, _ = _mesh_pos()
        sib = (x, y, 1 - c)
        _handshake([sib])
        other = pl.ds(pl.multiple_of((1 - c) * half, 16), half)
        cp = _remote(g_ref.at[:, other, :], r_ref, send_sem, recv_sem, sib)
        cp.start()
        cp.wait()

    return _run_comm(body, "sibling_swap_%d" % cid, cid, [gbuf], [],
                     [jax.ShapeDtypeStruct((NCHIP, half, D), BF)],
                     (pltpu.SemaphoreType.DMA, pltpu.SemaphoreType.DMA))[0]


def _sibling_swap_inline(gbuf):
    rtot = gbuf.shape[1]
    half = rtot // 2

    def body(g_ref, r_ref, send_sem, recv_sem):
        x, y, c, _, _ = _mesh_pos()
        other = pl.ds(pl.multiple_of((1 - c) * half, 16), half)
        cp = _remote(g_ref.at[:, other, :], r_ref, send_sem, recv_sem, (x, y, 1 - c))
        cp.start()
        cp.wait()

    return pl.pallas_call(
        body, name="sibling_swap_inline", in_specs=[ANY], out_specs=ANY,
        out_shape=jax.ShapeDtypeStruct((NCHIP, half, D), BF),
        scratch_shapes=[pltpu.SemaphoreType.DMA, pltpu.SemaphoreType.DMA],
    )(gbuf)


def _row_tile(rows):
    for cand in range(min(rows, 1280) // 16 * 16, 0, -16):
        if rows % cand == 0:
            return cand
    return rows


def _chip_sum(gbuf, rbuf, cidx):
    half = rbuf.shape[1]
    rt = _row_tile(half)
    nb = half // rt

    def body(c_ref, g_ref, r_ref, o_ref):
        o_ref[...] = (g_ref[...].astype(F32) + r_ref[...].astype(F32)).astype(BF)

    return pl.pallas_call(
        body, name="chip_sum",
        grid_spec=pltpu.PrefetchScalarGridSpec(
            num_scalar_prefetch=1, grid=(NCHIP, nb),
            in_specs=[pl.BlockSpec((None, rt, D), lambda k, r, c: (k, c[0] * nb + r, 0)),
                      pl.BlockSpec((None, rt, D), lambda k, r, c: (k, r, 0))],
            out_specs=pl.BlockSpec((None, rt, D), lambda k, r, c: (k, r, 0))),
        out_shape=jax.ShapeDtypeStruct((NCHIP, half, D), BF),
        compiler_params=_params(("arbitrary", "arbitrary")),
    )(cidx, gbuf, rbuf)


def _chip_exchange(sbuf, cid):
    half = sbuf.shape[1]

    def body(s_ref, x_ref, send_sems, recv_sems):
        x, y, c, _, chips = _mesh_pos()
        _handshake([(*ch, c) for ch in chips])
        cps = [_remote(s_ref.at[2 * cx + cy], x_ref.at[j], send_sems.at[j], recv_sems.at[j], (cx, cy, c))
               for j, (cx, cy) in enumerate(chips)]
        for cp in cps:
            cp.start()
        for cp in cps:
            cp.wait()

    return _run_comm(body, "chip_exchange_%d" % cid, cid, [sbuf], [],
                     [jax.ShapeDtypeStruct((3, half, D), BF)],
                     (pltpu.SemaphoreType.DMA((3,)), pltpu.SemaphoreType.DMA((3,))))[0]


def _shard_sum(gbuf, rbuf, xbuf, ck):
    half = rbuf.shape[1]
    rt = _row_tile(half)
    nb = half // rt

    def body(ck_ref, g_ref, r_ref, x_ref, o_ref):
        acc = g_ref[...].astype(F32) + r_ref[...].astype(F32)
        for j in range(3):
            acc = acc + x_ref[j].astype(F32)
        o_ref[...] = acc

    return pl.pallas_call(
        body, name="shard_sum",
        grid_spec=pltpu.PrefetchScalarGridSpec(
            num_scalar_prefetch=1, grid=(nb,),
            in_specs=[pl.BlockSpec((None, rt, D), lambda r, ck: (ck[1], ck[0] * nb + r, 0)),
                      pl.BlockSpec((None, rt, D), lambda r, ck: (ck[1], r, 0)),
                      pl.BlockSpec((3, rt, D), lambda r, ck: (0, r, 0))],
            out_specs=pl.BlockSpec((rt, D), lambda r, ck: (ck[0] * nb + r, 0))),
        out_shape=jax.ShapeDtypeStruct((2 * half, D), F32),
        compiler_params=_params(),
    )(ck, gbuf, rbuf, xbuf)


def _sibling_share(red, cid):
    half = red.shape[0] // 2

    def body(o_ref, send_sem, recv_sem):
        x, y, c, _, _ = _mesh_pos()
        sib = (x, y, 1 - c)
        _handshake([sib])
        mine = pl.ds(pl.multiple_of(c * half, 8), half)
        other = pl.ds(pl.multiple_of((1 - c) * half, 8), half)
        cp = _remote(o_ref.at[mine], o_ref.at[mine], send_sem, recv_sem, sib)
        cp.start()
        cp.wait_send()
        _remote(o_ref.at[mine], o_ref.at[other], send_sem, recv_sem, sib).wait_recv()

    return _run_comm(body, "sibling_share_%d" % cid, cid, [], [red], [],
                     (pltpu.SemaphoreType.DMA, pltpu.SemaphoreType.DMA))[0]


def _allreduce_small(v):
    rows = v.shape[0]
    ndev = 2 * NCHIP

    def body(v_ref, o_ref, gat, send_sems, recv_sems, lsem):
        x, y, c, _, chips = _mesh_pos()
        me, sib = (x, y, c), (x, y, 1 - c)

        def blk(px, py, pc):
            return gat.at[pl.ds((4 * px + 2 * py + pc) * rows, rows), :]

        def copy(k, block, to, src=None):
            return pltpu.make_async_remote_copy(
                src_ref=blk(*block) if src is None else src, dst_ref=blk(*block),
                send_sem=send_sems.at[k], recv_sem=recv_sems.at[k], device_id=to, device_id_type=MESH)

        mine = pltpu.make_async_copy(v_ref, blk(*me), lsem)
        mine.start()
        first = [copy(0, me, sib, src=v_ref)]
        first += [copy(1 + j, me, (*chip, c), src=v_ref) for j, chip in enumerate(chips)]
        for cp in first:
            cp.start()
        passed = [copy(4 + j, (*chip, c), sib) for j, chip in enumerate(chips)]
        for j, chip in enumerate(chips):
            copy(1 + j, (*chip, c), me).wait_recv()
            passed[j].start()
        copy(0, sib, me).wait_recv()
        for j, chip in enumerate(chips):
            copy(4 + j, (*chip, 1 - c), me).wait_recv()
        for cp in first + passed:
            cp.wait_send()
        mine.wait()
        acc = gat[0:rows, :]
        for d in range(1, ndev):
            acc = acc + gat[d * rows:(d + 1) * rows, :]
        o_ref[...] = acc

    vm = pl.BlockSpec(memory_space=pltpu.VMEM)
    return pl.pallas_call(
        body, name="allreduce_small", in_specs=[vm], out_specs=vm,
        out_shape=jax.ShapeDtypeStruct((rows, D), F32),
        scratch_shapes=[pltpu.VMEM((ndev * rows, D), F32), pltpu.SemaphoreType.DMA((7,)),
                        pltpu.SemaphoreType.DMA((7,)), pltpu.SemaphoreType.DMA],
    )(v)


def _adamw(w, g, m, v):
    shape = w.shape
    cols = shape[-1]
    rows = w.size // cols
    bm = rows
    for cand in (512, 256, 128, 64, 32, 16, 8):
        if rows % cand == 0:
            bm = cand
            break
    bc1 = 1.0 - ADAM_B1 ** ADAM_STEP
    bc2 = 1.0 - ADAM_B2 ** ADAM_STEP

    def body(w_ref, g_ref, m_ref, v_ref, d_ref, mo_ref, vo_ref):
        gv = g_ref[...]
        mn = ADAM_B1 * m_ref[...] + (1.0 - ADAM_B1) * gv
        vn = ADAM_B2 * v_ref[...] + (1.0 - ADAM_B2) * (gv * gv)
        mo_ref[...] = mn
        vo_ref[...] = vn
        d_ref[...] = -ADAM_LR * ((mn / bc1) / (jnp.sqrt(vn / bc2) + ADAM_EPS) + ADAM_WD * w_ref[...])

    spec = pl.BlockSpec((bm, cols), lambda i: (i, 0))
    out = jax.ShapeDtypeStruct((rows, cols), F32)
    d, mo, vo = pl.pallas_call(
        body, name="adamw", grid=(rows // bm,), in_specs=[spec] * 4, out_specs=[spec] * 3,
        out_shape=[out, out, out], compiler_params=_params(),
    )(*[t.reshape(rows, cols) for t in (w, g, m, v)])
    return d.reshape(shape), mo.reshape(shape), vo.reshape(shape)


def _adamw_layer(w, g, m, v, li, prev):
    cols = w.shape[-1]
    rows = w[0].size // cols
    bm = rows
    for cand in (512, 256, 128, 64, 32, 16, 8):
        if rows % cand == 0:
            bm = cand
            break
    bc1 = 1.0 - ADAM_B1 ** ADAM_STEP
    bc2 = 1.0 - ADAM_B2 ** ADAM_STEP

    def body(*refs):
        w_ref, g_ref, m_ref, v_ref = refs[:4]
        go_ref, d_ref, mo_ref, vo_ref = refs[-4:]
        gv = g_ref[...]
        mn = ADAM_B1 * m_ref[...] + (1.0 - ADAM_B1) * gv
        vn = ADAM_B2 * v_ref[...] + (1.0 - ADAM_B2) * (gv * gv)
        go_ref[...] = gv
        mo_ref[...] = mn
        vo_ref[...] = vn
        d_ref[...] = -ADAM_LR * ((mn / bc1) / (jnp.sqrt(vn / bc2) + ADAM_EPS) + ADAM_WD * w_ref[...])

    layer = pl.BlockSpec((None, bm, cols), lambda i: (li, i, 0))
    in_specs = [layer, pl.BlockSpec((bm, cols), lambda i: (i, 0)), layer, layer]
    args = [w.reshape(DEPTH, rows, cols), g.reshape(rows, cols), m.reshape(DEPTH, rows, cols),
            v.reshape(DEPTH, rows, cols)]
    aliases = {}
    if prev is not None:
        in_specs += [ANY] * 4
        args += list(prev)
        aliases = {4 + i: i for i in range(4)}
    out = jax.ShapeDtypeStruct((DEPTH, rows, cols), F32)
    return pl.pallas_call(
        body, name="adamw_layer", grid=(rows // bm,), in_specs=in_specs, out_specs=[layer] * 4,
        out_shape=[out] * 4, input_output_aliases=aliases, compiler_params=_params(),
    )(*args)


def _pack_shards(ws, li):
    pg = D // NG
    t = lambda a: jnp.swapaxes(a[li], 0, 1)
    parts = [t(ws["ffn1_w_gate"]), t(ws["ffn1_w_up"]), ws["ffn1_w_down"][li],
             t(ws["w_in"]), ws["conv_w_out"][li], ws["w_out"][li],
             t(ws["ffn2_w_gate"]), t(ws["ffn2_w_up"]), ws["ffn2_w_down"][li], ws["ple_w_gate"][li],
             t(ws["ple_w_proj"]).reshape(-1, D),
             jnp.swapaxes(ws["pool_w"][li], 0, 1).reshape(pg // NCHIP, D)]
    return jnp.concatenate([p.astype(BF) for p in parts], axis=0)


def _unpack_shards(parts):
    lay, _ = _layout()
    pg = D // NG

    def rows(n):
        off, rs = lay[n]
        for r0, arr in parts:
            if r0 <= off and off + rs <= r0 + arr.shape[0]:
                return arr[off - r0:off - r0 + rs, :]
        raise ValueError(n)

    t = lambda a: jnp.swapaxes(a, 0, 1)
    return {
        "ffn1_w_gate": t(rows("wg1")), "ffn1_w_up": t(rows("wu1")), "ffn1_w_down": rows("wd1"),
        "ffn2_w_gate": t(rows("wg2")), "ffn2_w_up": t(rows("wu2")), "ffn2_w_down": rows("wd2"),
        "w_in": t(rows("win")), "conv_w_out": rows("wco"), "w_out": rows("wo"), "ple_w_gate": rows("wpg"),
        "ple_w_proj": t(rows("wpp").reshape(D // NCHIP, PD)),
        "pool_w": jnp.swapaxes(rows("pw").reshape(pg // NCHIP, NG, pg), 0, 1),
    }


_BIG = ("ffn1_w_gate", "ffn1_w_up", "ffn1_w_down", "w_in", "pool_w", "conv_w_out", "w_out",
        "ffn2_w_gate", "ffn2_w_up", "ffn2_w_down", "ple_w_gate", "ple_w_proj")
_VECS = ("ffn1_norm", "mix_norm", "pool_scale", "conv_dw_b", "conv_ln_g", "conv_ln_b", "ffn2_norm", "ple_norm")
_WEIGHTS = ("ffn1_norm", "ffn1_w_gate", "ffn1_w_up", "ffn1_w_down", "mix_norm", "w_in", "pool_w", "pool_scale",
            "conv_dw_w", "conv_dw_b", "conv_ln_g", "conv_ln_b", "conv_w_out", "w_out", "ffn2_norm",
            "ffn2_w_gate", "ffn2_w_up", "ffn2_w_down", "ple_norm", "ple_w_gate", "ple_w_proj", "final_norm")


def _step(x, p, tgt, ws, ms, vs):
    lay, rtot = _layout()
    pg = D // NG
    cpos = lax.axis_index("c")
    kme = 2 * lax.axis_index("x") + lax.axis_index("y")
    cidx = jnp.stack([cpos]).astype(jnp.int32)
    ck = jnp.stack([cpos, kme]).astype(jnp.int32)
    h = x.reshape(T, D)
    tgt = tgt.reshape(T, D)

    nfirst = _first_rows()
    packed = [_pack_shards(ws, li) for li in range(DEPTH)]
    wfirst, wrest = [None] * DEPTH, [None] * DEPTH
    wfirst[0] = _gather_layer(packed[0][:nfirst], 0)
    piece, wfirst[0] = lax.optimization_barrier((packed[0][nfirst:], wfirst[0]))
    wrest[0] = _gather_layer(piece, 1)
    wppt, pwcat = [None] * DEPTH, [None] * DEPTH
    kk = ws["conv_dw_w"].shape[1]
    wdw_mine = jnp.zeros((DEPTH * HALO, D), F32)
    for li in range(DEPTH):
        blockw = jnp.zeros((kk, D), F32)
        mine = jnp.where(cpos == 0, ws["conv_dw_w"][li], 0.0)
        blockw = lax.dynamic_update_slice(blockw, mine, (0, kme * (D // NCHIP)))
        wdw_mine = wdw_mine.at[li * HALO:li * HALO + kk, :].set(blockw)
    wdw_all = _allreduce_small(wdw_mine)
    wdw = [wdw_all[li * HALO:(li + 1) * HALO, :] for li in range(DEPTH)]
    vec = lambda name, li: _row(ws[name][li])

    saved = []
    for li in range(DEPTH):
        if li > 0:
            wfirst[li], wrest[li], h = lax.optimization_barrier((wfirst[li], wrest[li], h))
        h0 = h
        h1, a1, b1 = _ffn_fwd(h0, vec("ffn1_norm", li), wfirst[li], 1)
        if li + 1 < DEPTH:
            piece, wrest[li], h1 = lax.optimization_barrier((packed[li + 1][:nfirst], wrest[li], h1))
            wfirst[li + 1] = _gather_layer(piece, 2 * li + 2)
        else:
            wrest[li], h1 = lax.optimization_barrier((wrest[li], h1))
        o, s = lay["wpp"]
        wppt[li] = wrest[li][:, o - nfirst:o - nfirst + s, :].reshape(D, PD)
        o, s = lay["pw"]
        pwcat[li] = wrest[li][:, o - nfirst:o - nfirst + s, :].reshape(pg, D)
        z = _mix_in_fwd(h1, vec("mix_norm", li), wrest[li])
        h2, pooled, c1, cc = _mix_mid_fwd(h1, z, pwcat[li], vec("pool_scale", li), wdw[li],
                                          vec("conv_dw_b", li), vec("conv_ln_g", li), vec("conv_ln_b", li),
                                          wrest[li])
        if li + 1 < DEPTH:
            piece, wfirst[li + 1], h2 = lax.optimization_barrier((packed[li + 1][nfirst:], wfirst[li + 1], h2))
            wrest[li + 1] = _gather_layer(piece, 2 * li + 3)
        h3, a2, b2 = _ffn_fwd(h2, vec("ffn2_norm", li), wrest[li], 2)
        h = _ple_fwd(h3, p[li, 0], vec("ple_norm", li), wppt[li], wrest[li])
        saved.append((h0, a1, b1, h1, z, pooled, c1, cc, h2, a2, b2, h3))

    dh, losscols, dgf = _loss_bwd(h, tgt, _row(ws["final_norm"]))
    loss = lax.psum(jnp.sum(losscols), ("x", "y", "c"))
    vecg = [dict() for _ in range(DEPTH)]
    dwdw = [None] * DEPTH
    def reduce_start(g, chain, inline=False):
        cid = 2 * DEPTH + 3 * chain
        return {"g": g, "r": _sibling_swap_inline(g) if inline else _sibling_swap(g, cid), "cid": cid}

    def reduce_mid(st, anchor):
        if anchor is not None:
            anchor, st["r"] = lax.optimization_barrier((anchor, st["r"]))
        sbuf = _chip_sum(st["g"], st["r"], cidx)
        if anchor is not None:
            anchor, sbuf = lax.optimization_barrier((anchor, sbuf))
        st["x"] = _chip_exchange(sbuf, st["cid"] + 1)
        return anchor

    def reduce_end(st, anchor):
        xb = st["x"]
        if anchor is not None:
            anchor, xb = lax.optimization_barrier((anchor, xb))
        rsum = _shard_sum(st["g"], st["r"], xb, ck)
        if anchor is not None:
            anchor, rsum = lax.optimization_barrier((anchor, rsum))
        return anchor, _sibling_share(rsum, st["cid"] + 2)

    parts = [[] for _ in range(DEPTH)]
    above = None
    nchain = 0
    for li in reversed(range(DEPTH)):
        h0, a1, b1, h1, z, pooled, c1, cc, h2, a2, b2, h3 = saved[li]
        w = wrest[li]
        dh, dgp, dwpp, gbuf = _ple_bwd(h3, dh, p[li, 0], vec("ple_norm", li), wppt[li], w)
        vecg[li]["ple_norm"] = dgp
        if above is not None:
            dh = reduce_mid(above[1], dh)
        dh_in, da, db, sact, n, dg = _ffn_bwd(h2, dh, a2, b2, vec("ffn2_norm", li), w, 2)
        vecg[li]["ffn2_norm"] = dg
        gbuf = _wgrad([da, db], n, gbuf, ("wg2", "wu2"), 0, F // 2, F // NCHIP)
        gbuf = _wgrad([sact], dh, gbuf, ("wd2",), 0, F // 2, F // NCHIP, yscale=0.5)
        dh = dh_in
        dp, dc1, dzb, small, dpw, gbuf = _mix_b1(dh, z, pooled, c1, cc, pwcat[li], vec("pool_scale", li),
                                                 vec("conv_ln_g", li), vec("conv_ln_b", li), w, gbuf)
        vecg[li]["pool_scale"] = small[0:1]
        vecg[li]["conv_dw_b"] = small[1:2]
        vecg[li]["conv_ln_g"] = small[2:3]
        vecg[li]["conv_ln_b"] = small[3:4]
        dza, dwdw[li] = _mix_b2(dp, dc1, z, wdw[li])
        if above is not None:
            dza, rsum = reduce_end(above[1], dza)
            parts[above[0]].append((0, rsum))
            above = None
        dh_in, u, dg = _mix_b3(h1, dh, dza, dzb, vec("mix_norm", li), w)
        vecg[li]["mix_norm"] = dg
        gbuf = _wgrad([dza], u, gbuf, ("win",), 0, D, D // NCHIP)
        gbuf = _wgrad([dzb], u, gbuf, ("win",), 3 * D, D, D // NCHIP)
        dh = dh_in
        o, s = lay["wpp"]
        small_rows = jnp.concatenate([dwpp.reshape(NCHIP, s, D), dpw.reshape(NCHIP, lay["pw"][1], D)], axis=1)
        gbuf = lax.dynamic_update_slice(gbuf, small_rows.astype(BF), (0, o, 0))
        if li == 0:
            rest = reduce_start(gbuf[:, nfirst:, :], nchain)
            nchain += 1
        dh_in, da, db, sact, n, dg = _ffn_bwd(h0, dh, a1, b1, vec("ffn1_norm", li), wfirst[li], 1)
        vecg[li]["ffn1_norm"] = dg
        if li == 0:
            da = reduce_mid(rest, da)
        gbuf = _wgrad([da, db], n, gbuf, ("wg1", "wu1"), 0, F // 2, F // NCHIP)
        gbuf = _wgrad([sact], dh, gbuf, ("wd1",), 0, F // 2, F // NCHIP, yscale=0.5)
        dh = dh_in
        if li == 0:
            first = reduce_start(gbuf[:, :nfirst, :], nchain, inline=True)
            nchain += 1
            reduce_mid(first, None)
        else:
            above = (li, reduce_start(gbuf, nchain))
            nchain += 1
    grad_x = dh.reshape(x.shape)
    rsum = _shard_sum(rest["g"], rest["r"], rest["x"], ck)
    first["x"], rsum = lax.optimization_barrier((first["x"], rsum))
    parts[0].append((nfirst, _sibling_share(rsum, rest["cid"] + 2)))
    updated = {n: None for n in _BIG}
    for li in reversed(range(1, DEPTH)):
        un = _unpack_shards(parts[li])
        for n in _BIG:
            updated[n] = _adamw_layer(ws[n], un[n], ms[n], vs[n], li, updated[n])
    first["x"], updated = lax.optimization_barrier((first["x"], updated))
    parts[0].append((0, reduce_end(first, None)[1]))
    un = _unpack_shards(parts[0])
    for n in _BIG:
        updated[n] = _adamw_layer(ws[n], un[n], ms[n], vs[n], 0, updated[n])

    rows = [vecg[li][n] for li in range(DEPTH) for n in _VECS] + [dgf]
    rows.append(jnp.zeros((8 - (len(rows) % 8), D), F32))
    vsum = _allreduce_small(jnp.concatenate(rows + dwdw, axis=0))
    nvec = len(_VECS)
    grads = {}
    for i, n in enumerate(_VECS):
        grads[n] = jnp.stack([vsum[li * nvec + i] for li in range(DEPTH)])
    grads["final_norm"] = vsum[DEPTH * nvec]
    base = DEPTH * nvec + 8 - ((DEPTH * nvec + 1) % 8) + 1
    dcols = D // NCHIP
    grads["conv_dw_w"] = jnp.stack([
        lax.dynamic_slice(vsum[base + li * HALO: base + li * HALO + kk, :], (0, kme * dcols), (kk, dcols))
        for li in range(DEPTH)])

    outs_g, outs_d, outs_m, outs_v = [], [], [], []
    for n in _WEIGHTS:
        if n in _BIG:
            gq, d, mo, vo = [t.reshape(ws[n].shape) for t in updated[n]]
        else:
            gq = grads[n]
            d, mo, vo = _adamw(ws[n], gq, ms[n], vs[n])
        outs_g.append(gq)
        outs_d.append(d)
        outs_m.append(mo)
        outs_v.append(vo)
    return (loss, grad_x, *outs_g, *outs_d, *outs_m, *outs_v)


def kernel(x, p, ffn1_norm, ffn1_w_gate, ffn1_w_up, ffn1_w_down, mix_norm, w_in, pool_w, pool_scale, conv_dw_w, conv_dw_b, conv_ln_g, conv_ln_b, conv_w_out, w_out, ffn2_norm, ffn2_w_gate, ffn2_w_up, ffn2_w_down, ple_norm, ple_w_gate, ple_w_proj, final_norm, loss_target, m_ffn1_norm, m_ffn1_w_gate, m_ffn1_w_up, m_ffn1_w_down, m_mix_norm, m_w_in, m_pool_w, m_pool_scale, m_conv_dw_w, m_conv_dw_b, m_conv_ln_g, m_conv_ln_b, m_conv_w_out, m_w_out, m_ffn2_norm, m_ffn2_w_gate, m_ffn2_w_up, m_ffn2_w_down, m_ple_norm, m_ple_w_gate, m_ple_w_proj, m_final_norm, v_ffn1_norm, v_ffn1_w_gate, v_ffn1_w_up, v_ffn1_w_down, v_mix_norm, v_w_in, v_pool_w, v_pool_scale, v_conv_dw_w, v_conv_dw_b, v_conv_ln_g, v_conv_ln_b, v_conv_w_out, v_w_out, v_ffn2_norm, v_ffn2_w_gate, v_ffn2_w_up, v_ffn2_w_down, v_ple_norm, v_ple_w_gate, v_ple_w_proj, v_final_norm):
    ws = dict(zip(_WEIGHTS, (ffn1_norm, ffn1_w_gate, ffn1_w_up, ffn1_w_down, mix_norm, w_in, pool_w, pool_scale, conv_dw_w, conv_dw_b, conv_ln_g, conv_ln_b, conv_w_out, w_out, ffn2_norm, ffn2_w_gate, ffn2_w_up, ffn2_w_down, ple_norm, ple_w_gate, ple_w_proj, final_norm)))
    ms = dict(zip(_WEIGHTS, (m_ffn1_norm, m_ffn1_w_gate, m_ffn1_w_up, m_ffn1_w_down, m_mix_norm, m_w_in, m_pool_w, m_pool_scale, m_conv_dw_w, m_conv_dw_b, m_conv_ln_g, m_conv_ln_b, m_conv_w_out, m_w_out, m_ffn2_norm, m_ffn2_w_gate, m_ffn2_w_up, m_ffn2_w_down, m_ple_norm, m_ple_w_gate, m_ple_w_proj, m_final_norm)))
    vs = dict(zip(_WEIGHTS, (v_ffn1_norm, v_ffn1_w_gate, v_ffn1_w_up, v_ffn1_w_down, v_mix_norm, v_w_in, v_pool_w, v_pool_scale, v_conv_dw_w, v_conv_dw_b, v_conv_ln_g, v_conv_ln_b, v_conv_w_out, v_w_out, v_ffn2_norm, v_ffn2_w_gate, v_ffn2_w_up, v_ffn2_w_down, v_ple_norm, v_ple_w_gate, v_ple_w_proj, v_final_norm)))
    return _step(x, p, loss_target, ws, ms, vs)
```

```python
import jax
import jax.numpy as jnp
from jax import lax
from jax.experimental import pallas as pl
from jax.experimental.pallas import tpu as pltpu
from jax.experimental.pallas import tpu_sc as plsc

T = 8192
D = 1024
F = 2816
PD = 256
NG = 4
WINDOWS = (2, 4, 8, 16)
KC = 31
HALO = 32
DEPTH = 2
NCHIP = 4
RMS_EPS = 1e-6
LN_EPS = 1e-5

ADAM_LR = 0.001
ADAM_B1 = 0.9
ADAM_B2 = 0.999
ADAM_EPS = 1e-08
ADAM_WD = 0.01
ADAM_STEP = 10

TM = 512
TMB = 256
TMW = 512
LANES = 128
SUBLANES = 8
CHUNKS = 4
VMEM_LIMIT = 56 * 1024 * 1024

BF = jnp.bfloat16
F32 = jnp.float32
MESH = pl.DeviceIdType.MESH
ANY = pl.BlockSpec(memory_space=pl.ANY)
HBM = pltpu.MemorySpace.HBM


def _layout():
    fs, ins, ds = F // NCHIP, 5 * D // NCHIP, D // NCHIP
    pps = ds * PD // D
    pws = NG * (D // NG // NCHIP) * (D // NG) // D
    names = [("wg1", fs), ("wu1", fs), ("wd1", fs), ("win", ins), ("wco", ds), ("wo", ds),
             ("wg2", fs), ("wu2", fs), ("wd2", fs), ("wpg", ds), ("wpp", pps), ("pw", pws)]
    off, r = {}, 0
    for n, s in names:
        off[n] = (r, s)
        r += s
    return off, r


def _sig(v):
    return pl.reciprocal(1.0 + jnp.exp(-v), approx=True)


def _dot_nn(a, b):
    return jnp.dot(a, b, preferred_element_type=F32)


def _dot_nt(a, b):
    return lax.dot_general(a, b, (((1,), (1,)), ((), ())), preferred_element_type=F32)


def _dot_tn(a, b):
    return lax.dot_general(a, b, (((0,), (0,)), ((), ())), preferred_element_type=F32)


def _params(sem=("arbitrary",)):
    return pltpu.CompilerParams(dimension_semantics=sem, vmem_limit_bytes=VMEM_LIMIT)


def _first_rows():
    return _layout()[0]["win"][0]


def _weight_copies(w_hbm, specs, sem):
    lay, _ = _layout()
    ra = _first_rows()
    cps = []
    for i, (name, dst) in enumerate(specs):
        off, rs = lay[name]
        off = off if off < ra else off - ra
        for k in range(NCHIP):
            cps.append(pltpu.make_async_copy(w_hbm.at[k, pl.ds(off, rs), :],
                                             dst.at[pl.ds(k * rs, rs), :], sem.at[i * NCHIP + k]))
    return cps


def _load_weights_once(w_hbm, specs, sem):
    @pl.when(pl.program_id(0) == 0)
    def _():
        cps = _weight_copies(w_hbm, specs, sem)
        for cp in cps:
            cp.start()
        for cp in cps:
            cp.wait()


def _grad_copies(stage, g_hbm, name, row0, rows, piece, sem, sem0):
    lay, _ = _layout()
    off, rs = lay[name]
    cps = []
    for i in range(rows // piece):
        rglob = row0 + i * piece
        k = rglob // rs
        loc = rglob - k * rs
        start = off + loc
        if not isinstance(start, int):
            start = pl.multiple_of(start, 16)
        dst = g_hbm.at[k, pl.ds(start, piece), :]
        cps.append(pltpu.make_async_copy(stage.at[pl.ds(i * piece, piece), :], dst, sem.at[sem0 + i]))
    return cps


def _row(v):
    return v.reshape(1, -1)


def _shifted_source(buf, sh, base, s, lanes):
    a, b = divmod(s, SUBLANES)
    rows = pl.ds(pl.multiple_of(base + SUBLANES * a, SUBLANES), SUBLANES)
    if b == 0:
        return buf[rows, lanes]
    return sh[b - 1, rows, :]


def _fill_shifted(buf, sh, lanes):
    rows = sh.shape[1]
    for b in range(1, SUBLANES):
        sh[b - 1, :, :] = buf[pl.ds(b, rows), lanes]


def _ffn_fwd(h, g, wfull, which):
    nt = T // TM
    fc = F // 2
    names = ("wg%d" % which, "wu%d" % which, "wd%d" % which)

    def body(h_ref, g_ref, w_hbm, ho_ref, a_ref, b_ref, wg, wu, wd, sem):
        _load_weights_once(w_hbm, ((names[0], wg), (names[1], wu), (names[2], wd)), sem)
        x = h_ref[...]
        r = lax.rsqrt(jnp.mean(x * x, axis=-1, keepdims=True) + RMS_EPS)
        n = (x * r * g_ref[...]).astype(BF)
        acc = jnp.zeros((TM, D), F32)
        for c in range(F // fc):
            sl = pl.ds(c * fc, fc)
            a = _dot_nt(n, wg[sl, :])
            b = _dot_nt(n, wu[sl, :])
            a_ref[:, sl] = a.astype(BF)
            b_ref[:, sl] = b.astype(BF)
            s = (a * _sig(a) * b).astype(BF)
            acc = acc + _dot_nn(s, wd[sl, :])
        ho_ref[...] = x + 0.5 * acc

    tile = lambda w: pl.BlockSpec((TM, w), lambda i: (i, 0))
    return pl.pallas_call(
        body, name="ffn_fwd", grid=(nt,),
        in_specs=[tile(D), pl.BlockSpec((1, D), lambda i: (0, 0)), ANY],
        out_specs=[tile(D), tile(F), tile(F)],
        out_shape=[jax.ShapeDtypeStruct((T, D), F32), jax.ShapeDtypeStruct((T, F), BF),
                   jax.ShapeDtypeStruct((T, F), BF)],
        scratch_shapes=[pltpu.VMEM((F, D), BF), pltpu.VMEM((F, D), BF), pltpu.VMEM((F, D), BF),
                        pltpu.SemaphoreType.DMA((3 * NCHIP,))],
        compiler_params=_params(),
    )(h, g, wfull)


def _mix_in_fwd(h, g, wfull):
    nt = T // TM
    nin = 5 * D

    def body(h_ref, g_ref, w_hbm, z_ref, win, sem):
        _load_weights_once(w_hbm, (("win", win),), sem)
        x = h_ref[...]
        r = lax.rsqrt(jnp.mean(x * x, axis=-1, keepdims=True) + RMS_EPS)
        u = (x * r * g_ref[...]).astype(BF)
        for c in range(5):
            sl = pl.ds(c * D, D)
            z_ref[:, sl] = _dot_nt(u, win[sl, :]).astype(BF)

    return pl.pallas_call(
        body, name="mix_in_fwd", grid=(nt,),
        in_specs=[pl.BlockSpec((TM, D), lambda i: (i, 0)), pl.BlockSpec((1, D), lambda i: (0, 0)), ANY],
        out_specs=pl.BlockSpec((TM, nin), lambda i: (i, 0)),
        out_shape=jax.ShapeDtypeStruct((T, nin), BF),
        scratch_shapes=[pltpu.VMEM((nin, D), BF), pltpu.SemaphoreType.DMA((NCHIP,))],
        compiler_params=_params(),
    )(h, g, wfull)


def _pool_counts(i, rows, w):
    t = i * TM + lax.broadcasted_iota(jnp.int32, (rows, 1), 0)
    return jnp.minimum(t + 1, w).astype(F32)


def _mix_mid_fwd(h, z, pwcat, pscale, wdw, bdw, lg, lb, wfull):
    nt = T // TM
    pg = D // NG
    hb = TM // HALO
    n_ext = HALO + TM
    pad = SUBLANES

    def body(h_ref, z_ref, zh_ref, pw_ref, ps_ref, wdw_ref, bdw_ref, lg_ref, lb_ref, w_hbm,
             h2_ref, p_ref, c1_ref, cc_ref, wco, wo, pa, pb, cbuf, sh, c1buf, ambuf, sem):
        i = pl.program_id(0)
        _load_weights_once(w_hbm, (("wco", wco), ("wo", wo)), sem)

        @pl.when(i == 0)
        def _():
            pa[0:pad, :] = jnp.zeros((pad, D), F32)
            pb[0:pad, :] = jnp.zeros((pad, D), F32)

        keep = (i > 0).astype(F32)
        zh = zh_ref[...].astype(F32) * keep
        za = z_ref[:, D:2 * D].astype(F32)
        zg = z_ref[:, 2 * D:3 * D].astype(F32)
        pa[pad:pad + HALO, :] = zh[:, 0:D]
        pa[pad + HALO:pad + n_ext, :] = z_ref[:, 0:D].astype(F32)
        cbuf[0:HALO, :] = zh[:, D:2 * D] * _sig(zh[:, 2 * D:3 * D])
        cbuf[HALO:n_ext, :] = za * _sig(zg)
        for g, w in enumerate(WINDOWS):
            ls = pl.ds(g * pg, pg)
            cur, nxt = pa, pb
            d = 1
            while d < w:
                nxt[pl.ds(pad, n_ext), ls] = cur[pl.ds(pad, n_ext), ls] + cur[pl.ds(pad - d, n_ext), ls]
                cur, nxt = nxt, cur
                d *= 2
            tok = z_ref[:, ls].astype(F32)
            pooled = (cur[pl.ds(pad + HALO, TM), ls] / _pool_counts(i, TM, w) - tok).astype(BF)
            p_ref[:, ls] = pooled
            ambuf[:, ls] = _dot_nn(pooled, pw_ref[:, ls])
        am = ambuf[...] * ps_ref[...]
        for l in range(D // LANES):
            lanes = pl.ds(l * LANES, LANES)
            _fill_shifted(cbuf, sh, lanes)
            bias = jnp.broadcast_to(bdw_ref[:, lanes], (SUBLANES, LANES))

            def conv_rows(r, carry):
                base = r * (CHUNKS * SUBLANES)
                accs = [bias] * CHUNKS
                for k in range(KC):
                    wk = jnp.broadcast_to(wdw_ref[k:k + 1, lanes], (SUBLANES, LANES))
                    for j in range(CHUNKS):
                        src = _shifted_source(cbuf, sh, base + j * SUBLANES, HALO - (KC - 1) + k, lanes)
                        accs[j] = accs[j] + wk * src
                for j in range(CHUNKS):
                    c1buf[pl.ds(pl.multiple_of(base + j * SUBLANES, SUBLANES), SUBLANES), lanes] = accs[j]
                return carry

            lax.fori_loop(0, TM // (CHUNKS * SUBLANES), conv_rows, 0)
        c1b = c1buf[...].astype(BF)
        c1_ref[...] = c1b
        c1 = c1b.astype(F32)
        mu = jnp.mean(c1, axis=-1, keepdims=True)
        xc = c1 - mu
        var = jnp.mean(xc * xc, axis=-1, keepdims=True)
        c2 = xc * lax.rsqrt(var + LN_EPS) * lg_ref[...] + lb_ref[...]
        c3 = (c2 * _sig(c2)).astype(BF)
        ccb = _dot_nn(c3, wco[...]).astype(BF)
        cc_ref[...] = ccb
        gp = z_ref[:, 3 * D:4 * D].astype(F32)
        gc = z_ref[:, 4 * D:5 * D].astype(F32)
        m = (_sig(gp) * am + _sig(gc) * ccb.astype(F32)).astype(BF)
        h2_ref[...] = h_ref[...] + _dot_nn(m, wo[...])

    tile = pl.BlockSpec((TM, D), lambda i: (i, 0))
    vec = pl.BlockSpec((1, D), lambda i: (0, 0))
    return pl.pallas_call(
        body, name="mix_mid_fwd", grid=(nt,),
        in_specs=[tile, pl.BlockSpec((TM, 5 * D), lambda i: (i, 0)),
                  pl.BlockSpec((HALO, 3 * D), lambda i: (jnp.maximum(i * hb - 1, 0), 0)),
                  pl.BlockSpec((pg, D), lambda i: (0, 0)), vec,
                  pl.BlockSpec((HALO, D), lambda i: (0, 0)), vec, vec, vec, ANY],
        out_specs=[tile, tile, tile, tile],
        out_shape=[jax.ShapeDtypeStruct((T, D), F32), jax.ShapeDtypeStruct((T, D), BF),
                   jax.ShapeDtypeStruct((T, D), BF), jax.ShapeDtypeStruct((T, D), BF)],
        scratch_shapes=[pltpu.VMEM((D, D), BF), pltpu.VMEM((D, D), BF),
                        pltpu.VMEM((pad + n_ext, D), F32), pltpu.VMEM((pad + n_ext, D), F32),
                        pltpu.VMEM((n_ext, D), F32), pltpu.VMEM((SUBLANES - 1, n_ext - SUBLANES, LANES), F32),
                        pltpu.VMEM((TM, D), F32), pltpu.VMEM((TM, D), F32),
                        pltpu.SemaphoreType.DMA((2 * NCHIP,))],
        compiler_params=_params(),
    )(h, z, z, pwcat, pscale, wdw, bdw, lg, lb, wfull)


def _ple_fwd(h, p, g, wppt, wfull):
    nt = T // TM

    def body(h_ref, p_ref, g_ref, wpp_ref, w_hbm, ho_ref, wpg, sem):
        _load_weights_once(w_hbm, (("wpg", wpg),), sem)
        x = h_ref[...]
        r = lax.rsqrt(jnp.mean(x * x, axis=-1, keepdims=True) + RMS_EPS)
        n = (x * r * g_ref[...]).astype(BF)
        gate = _sig(_dot_nn(n, wpg[...]))
        pe = _dot_nt(p_ref[...].astype(BF), wpp_ref[...])
        ho_ref[...] = x + gate * pe

    tile = pl.BlockSpec((TM, D), lambda i: (i, 0))
    return pl.pallas_call(
        body, name="ple_fwd", grid=(nt,),
        in_specs=[tile, pl.BlockSpec((TM, PD), lambda i: (i, 0)), pl.BlockSpec((1, D), lambda i: (0, 0)),
                  pl.BlockSpec((D, PD), lambda i: (0, 0)), ANY],
        out_specs=tile, out_shape=jax.ShapeDtypeStruct((T, D), F32),
        scratch_shapes=[pltpu.VMEM((D, D), BF), pltpu.SemaphoreType.DMA((NCHIP,))],
        compiler_params=_params(),
    )(h, p, g, wppt, wfull)


def _loss_bwd(h, tgt, g):
    nt = T // TM

    def body(h_ref, t_ref, g_ref, dh_ref, loss_ref, dg_ref):
        @pl.when(pl.program_id(0) == 0)
        def _():
            loss_ref[...] = jnp.zeros_like(loss_ref)
            dg_ref[...] = jnp.zeros_like(dg_ref)
        x = h_ref[...]
        r = lax.rsqrt(jnp.mean(x * x, axis=-1, keepdims=True) + RMS_EPS)
        xh = x * r
        gv = g_ref[...]
        e = xh * gv - t_ref[...]
        loss_ref[...] += jnp.sum(e * e, axis=0, keepdims=True) * (0.5 / D)
        dy = e * (1.0 / D)
        dg_ref[...] += jnp.sum(dy * xh, axis=0, keepdims=True)
        dxh = dy * gv
        dh_ref[...] = r * (dxh - xh * jnp.mean(dxh * xh, axis=-1, keepdims=True))

    tile = pl.BlockSpec((TM, D), lambda i: (i, 0))
    vec = pl.BlockSpec((1, D), lambda i: (0, 0))
    return pl.pallas_call(
        body, name="loss_bwd", grid=(nt,), in_specs=[tile, tile, vec], out_specs=[tile, vec, vec],
        out_shape=[jax.ShapeDtypeStruct((T, D), F32), jax.ShapeDtypeStruct((1, D), F32),
                   jax.ShapeDtypeStruct((1, D), F32)],
        compiler_params=_params(),
    )(h, tgt, g)


def _ple_bwd(h, dh, p, g, wppt, wfull):
    nt = T // TM
    _, rtot = _layout()

    def body(h_ref, dh_ref, p_ref, g_ref, wpp_ref, w_hbm,
             dho_ref, dg_ref, dwpp_ref, g_out, wpg, acc, stage, sem, osem):
        i = pl.program_id(0)
        _load_weights_once(w_hbm, (("wpg", wpg),), sem)

        @pl.when(i == 0)
        def _():
            dg_ref[...] = jnp.zeros_like(dg_ref)
            dwpp_ref[...] = jnp.zeros_like(dwpp_ref)
            acc[...] = jnp.zeros_like(acc)

        x = h_ref[...]
        r = lax.rsqrt(jnp.mean(x * x, axis=-1, keepdims=True) + RMS_EPS)
        xh = x * r
        gv = g_ref[...]
        n = (xh * gv).astype(BF)
        gate = _sig(_dot_nn(n, wpg[...]))
        pb = p_ref[...].astype(BF)
        pe = _dot_nt(pb, wpp_ref[...])
        d = dh_ref[...]
        dpe = (d * gate).astype(BF)
        dq = (d * pe * gate * (1.0 - gate)).astype(BF)
        dwpp_ref[...] += _dot_tn(dpe, pb)
        acc[...] += _dot_tn(n, dq)
        dn = _dot_nt(dq, wpg[...])
        dg_ref[...] += jnp.sum(dn * xh, axis=0, keepdims=True)
        dxh = dn * gv
        dho_ref[...] = d + r * (dxh - xh * jnp.mean(dxh * xh, axis=-1, keepdims=True))

        @pl.when(i == nt - 1)
        def _():
            stage[...] = acc[...].astype(BF)
            cps = _grad_copies(stage, g_out, "wpg", 0, D, D // NCHIP, osem, 0)
            for cp in cps:
                cp.start()
            for cp in cps:
                cp.wait()

    tile = pl.BlockSpec((TM, D), lambda i: (i, 0))
    vec = pl.BlockSpec((1, D), lambda i: (0, 0))
    return pl.pallas_call(
        body, name="ple_bwd", grid=(nt,),
        in_specs=[tile, tile, pl.BlockSpec((TM, PD), lambda i: (i, 0)), vec,
                  pl.BlockSpec((D, PD), lambda i: (0, 0)), ANY],
        out_specs=[tile, vec, pl.BlockSpec((D, PD), lambda i: (0, 0)), ANY],
        out_shape=[jax.ShapeDtypeStruct((T, D), F32), jax.ShapeDtypeStruct((1, D), F32),
                   jax.ShapeDtypeStruct((D, PD), F32),
                   jax.ShapeDtypeStruct((NCHIP, rtot, D), BF)],
        scratch_shapes=[pltpu.VMEM((D, D), BF), pltpu.VMEM((D, D), F32), pltpu.VMEM((D, D), BF),
                        pltpu.SemaphoreType.DMA((NCHIP,)), pltpu.SemaphoreType.DMA((NCHIP,))],
        compiler_params=_params(),
    )(h, dh, p, g, wppt, wfull)


def _ffn_bwd(h, dh, a, b, g, wfull, which):
    tm = TMB
    nt = T // tm
    fc = F // 2
    names = ("wg%d" % which, "wu%d" % which, "wd%d" % which)

    def body(h_ref, dh_ref, a_ref, b_ref, g_ref, w_hbm,
             dho_ref, da_ref, db_ref, s_ref, n_ref, dg_ref, wg, wu, wd, sem):
        _load_weights_once(w_hbm, ((names[0], wg), (names[1], wu), (names[2], wd)), sem)

        @pl.when(pl.program_id(0) == 0)
        def _():
            dg_ref[...] = jnp.zeros_like(dg_ref)

        x = h_ref[...]
        r = lax.rsqrt(jnp.mean(x * x, axis=-1, keepdims=True) + RMS_EPS)
        xh = x * r
        gv = g_ref[...]
        n_ref[...] = (xh * gv).astype(BF)
        d = dh_ref[...]
        df = (0.5 * d).astype(BF)
        dn = jnp.zeros((tm, D), F32)
        for c in range(F // fc):
            sl = pl.ds(c * fc, fc)
            av = a_ref[:, sl].astype(F32)
            bv = b_ref[:, sl].astype(F32)
            ds = _dot_nt(df, wd[sl, :])
            sg = _sig(av)
            sil = av * sg
            s_ref[:, sl] = (sil * bv).astype(BF)
            da = (ds * bv * (sg * (1.0 + av * (1.0 - sg)))).astype(BF)
            db = (ds * sil).astype(BF)
            da_ref[:, sl] = da
            db_ref[:, sl] = db
            dn = dn + _dot_nn(da, wg[sl, :]) + _dot_nn(db, wu[sl, :])
        dg_ref[...] += jnp.sum(dn * xh, axis=0, keepdims=True)
        dxh = dn * gv
        dho_ref[...] = d + r * (dxh - xh * jnp.mean(dxh * xh, axis=-1, keepdims=True))

    tile = lambda w: pl.BlockSpec((tm, w), lambda i: (i, 0))
    vec = pl.BlockSpec((1, D), lambda i: (0, 0))
    return pl.pallas_call(
        body, name="ffn_bwd", grid=(nt,),
        in_specs=[tile(D), tile(D), tile(F), tile(F), vec, ANY],
        out_specs=[tile(D), tile(F), tile(F), tile(F), tile(D), vec],
        out_shape=[jax.ShapeDtypeStruct((T, D), F32), jax.ShapeDtypeStruct((T, F), BF),
                   jax.ShapeDtypeStruct((T, F), BF), jax.ShapeDtypeStruct((T, F), BF),
                   jax.ShapeDtypeStruct((T, D), BF), jax.ShapeDtypeStruct((1, D), F32)],
        scratch_shapes=[pltpu.VMEM((F, D), BF), pltpu.VMEM((F, D), BF), pltpu.VMEM((F, D), BF),
                        pltpu.SemaphoreType.DMA((3 * NCHIP,))],
        compiler_params=_params(),
    )(h, dh, a, b, g, wfull)


def _wgrad(xs, y, gbuf, names, row0, rb, piece, yscale=None):
    nx = len(xs)
    rx = xs[0].shape[1]
    nj = rx // rb
    nt = T // TMW
    npiece = rb // piece

    def body(*refs):
        x_refs = refs[:nx]
        y_ref = refs[nx]
        g_out = refs[nx + 2]
        accs = refs[nx + 3:2 * nx + 3]
        stages = refs[2 * nx + 3:3 * nx + 3]
        osem = refs[3 * nx + 3]
        j = pl.program_id(0)
        t = pl.program_id(1)

        @pl.when(t == 0)
        def _():
            for acc in accs:
                acc[...] = jnp.zeros_like(acc)

        yv = y_ref[...]
        if yscale is not None:
            yv = (yscale * yv).astype(BF)
        for x_ref, acc in zip(x_refs, accs):
            acc[...] += _dot_tn(x_ref[...], yv)

        @pl.when(t == nt - 1)
        def _():
            cps = []
            for xi in range(nx):
                stages[xi][...] = accs[xi][...].astype(BF)
                cps += _grad_copies(stages[xi], g_out, names[xi], row0 + j * rb, rb, piece,
                                    osem, xi * npiece)
            for cp in cps:
                cp.start()
            for cp in cps:
                cp.wait()

    in_specs = [pl.BlockSpec((TMW, rb), lambda j, t: (t, j)) for _ in xs]
    in_specs += [pl.BlockSpec((TMW, D), lambda j, t: (t, 0)), ANY]
    return pl.pallas_call(
        body, name="wgrad", grid=(nj, nt), in_specs=in_specs, out_specs=ANY,
        out_shape=jax.ShapeDtypeStruct(gbuf.shape, gbuf.dtype),
        scratch_shapes=([pltpu.VMEM((rb, D), F32) for _ in xs] + [pltpu.VMEM((rb, D), BF) for _ in xs]
                        + [pltpu.SemaphoreType.DMA((nx * npiece,))]),
        input_output_aliases={nx + 1: 0},
        compiler_params=_params(("arbitrary", "arbitrary")),
    )(*xs, y, gbuf)


def _mix_b1(dh, z, pooled, c1, cc, pwcat, pscale, lg, lb, wfull, gbuf):
    nt = T // TM
    pg = D // NG

    def body(dh_ref, gp_ref, gc_ref, p_ref, c1_ref, cc_ref, pw_ref, ps_ref, lg_ref, lb_ref, w_hbm, _g_in,
             dp_ref, dc1_ref, dzb_ref, small_ref, dpw_ref, g_out,
             wco, wo, acc_o, acc_co, stage_o, stage_co, qbuf, sem, osem):
        i = pl.program_id(0)
        _load_weights_once(w_hbm, (("wco", wco), ("wo", wo)), sem)

        @pl.when(i == 0)
        def _():
            small_ref[...] = jnp.zeros_like(small_ref)
            dpw_ref[...] = jnp.zeros_like(dpw_ref)
            acc_o[...] = jnp.zeros_like(acc_o)
            acc_co[...] = jnp.zeros_like(acc_co)

        dhb = dh_ref[...].astype(BF)
        dm = _dot_nt(dhb, wo[...])
        sp = _sig(gp_ref[...].astype(F32))
        sc = _sig(gc_ref[...].astype(F32))
        for g in range(NG):
            ls = pl.ds(g * pg, pg)
            qbuf[:, ls] = _dot_nn(p_ref[:, ls], pw_ref[:, ls])
        q = qbuf[...]
        psv = ps_ref[...]
        am = q * psv
        ccv = cc_ref[...].astype(F32)
        m = (sp * am + sc * ccv).astype(BF)
        acc_o[...] += _dot_tn(m, dhb)
        dam = dm * sp
        dzb_ref[:, 0:D] = (dm * am * sp * (1.0 - sp)).astype(BF)
        dccb = (dm * sc).astype(BF)
        dzb_ref[:, D:2 * D] = (dm * ccv * sc * (1.0 - sc)).astype(BF)
        small_ref[0:1, :] += jnp.sum(dam * q, axis=0, keepdims=True)
        dq = (dam * psv).astype(BF)
        for g in range(NG):
            ls = pl.ds(g * pg, pg)
            dqg = dq[:, g * pg:(g + 1) * pg]
            dp_ref[:, ls] = _dot_nt(dqg, pw_ref[:, ls]).astype(BF)
            dpw_ref[:, ls] += _dot_tn(p_ref[:, ls], dqg)
        c1v = c1_ref[...].astype(F32)
        mu = jnp.mean(c1v, axis=-1, keepdims=True)
        xc = c1v - mu
        var = jnp.mean(xc * xc, axis=-1, keepdims=True)
        rs = lax.rsqrt(var + LN_EPS)
        c2n = xc * rs
        lgv = lg_ref[...]
        c2 = c2n * lgv + lb_ref[...]
        sg2 = _sig(c2)
        c3 = (c2 * sg2).astype(BF)
        acc_co[...] += _dot_tn(c3, dccb)
        dc3 = _dot_nt(dccb, wco[...])
        dc2 = dc3 * (sg2 * (1.0 + c2 * (1.0 - sg2)))
        small_ref[2:3, :] += jnp.sum(dc2 * c2n, axis=0, keepdims=True)
        small_ref[3:4, :] += jnp.sum(dc2, axis=0, keepdims=True)
        dc2n = dc2 * lgv
        dc1 = rs * (dc2n - jnp.mean(dc2n, axis=-1, keepdims=True)
                    - c2n * jnp.mean(dc2n * c2n, axis=-1, keepdims=True))
        small_ref[1:2, :] += jnp.sum(dc1, axis=0, keepdims=True)
        dc1_ref[...] = dc1.astype(BF)

        @pl.when(i == nt - 1)
        def _():
            stage_o[...] = acc_o[...].astype(BF)
            stage_co[...] = acc_co[...].astype(BF)
            cps = _grad_copies(stage_o, g_out, "wo", 0, D, D // NCHIP, osem, 0)
            cps += _grad_copies(stage_co, g_out, "wco", 0, D, D // NCHIP, osem, NCHIP)
            for cp in cps:
                cp.start()
            for cp in cps:
                cp.wait()

    tile = pl.BlockSpec((TM, D), lambda i: (i, 0))
    vec = pl.BlockSpec((1, D), lambda i: (0, 0))
    full = lambda r: pl.BlockSpec((r, D), lambda i: (0, 0))
    return pl.pallas_call(
        body, name="mix_b1", grid=(nt,),
        in_specs=[tile, pl.BlockSpec((TM, D), lambda i: (i, 3)), pl.BlockSpec((TM, D), lambda i: (i, 4)),
                  tile, tile, tile, full(pg), vec, vec, vec, ANY, ANY],
        out_specs=[tile, tile, pl.BlockSpec((TM, 2 * D), lambda i: (i, 0)), full(8), full(pg), ANY],
        out_shape=[jax.ShapeDtypeStruct((T, D), BF), jax.ShapeDtypeStruct((T, D), BF),
                   jax.ShapeDtypeStruct((T, 2 * D), BF), jax.ShapeDtypeStruct((8, D), F32),
                   jax.ShapeDtypeStruct((pg, D), F32), jax.ShapeDtypeStruct(gbuf.shape, gbuf.dtype)],
        scratch_shapes=[pltpu.VMEM((D, D), BF), pltpu.VMEM((D, D), BF),
                        pltpu.VMEM((D, D), F32), pltpu.VMEM((D, D), F32),
                        pltpu.VMEM((D, D), BF), pltpu.VMEM((D, D), BF),
                        pltpu.VMEM((TM, D), F32),
                        pltpu.SemaphoreType.DMA((2 * NCHIP,)), pltpu.SemaphoreType.DMA((2 * NCHIP,))],
        input_output_aliases={11: 5},
        compiler_params=_params(),
    )(dh, z, z, pooled, c1, cc, pwcat, pscale, lg, lb, wfull, gbuf)


def _mix_b2(dp, dc1, z, wdw):
    nt = T // TM
    pg = D // NG
    hb = TM // HALO
    nhb = T // HALO
    n_ext = TM + HALO
    pad = SUBLANES

    def body(dp_ref, dpn_ref, dc_ref, dcn_ref, za_ref, zg_ref, wdw_ref,
             dza_ref, dw_ref, pa, pb, cbuf, sh, c0buf, dc0buf):
        i = pl.program_id(0)

        @pl.when(i == 0)
        def _():
            dw_ref[...] = jnp.zeros_like(dw_ref)
            pa[n_ext:n_ext + pad, :] = jnp.zeros((pad, D), F32)
            pb[n_ext:n_ext + pad, :] = jnp.zeros((pad, D), F32)

        more = (i < nt - 1).astype(F32)
        for g, w in enumerate(WINDOWS):
            ls = pl.ds(g * pg, pg)
            cur_dp = dp_ref[:, ls].astype(F32)
            pa[0:TM, ls] = cur_dp / _pool_counts(i, TM, w)
            pa[TM:n_ext, ls] = dpn_ref[:, ls].astype(F32) * (more / w)
            cur, nxt = pa, pb
            d = 1
            while d < w:
                nxt[pl.ds(0, n_ext), ls] = cur[pl.ds(0, n_ext), ls] + cur[pl.ds(d, n_ext), ls]
                cur, nxt = nxt, cur
                d *= 2
            dza_ref[:, ls] = (cur[pl.ds(0, TM), ls] - cur_dp).astype(BF)
        za = za_ref[...].astype(F32)
        sg = _sig(zg_ref[...].astype(F32))
        cbuf[0:TM, :] = dc_ref[...].astype(F32)
        cbuf[TM:n_ext, :] = dcn_ref[...].astype(F32) * more
        c0buf[...] = za * sg
        for l in range(D // LANES):
            lanes = pl.ds(l * LANES, LANES)
            _fill_shifted(cbuf, sh, lanes)

            def conv_rows(r, accs):
                base = r * (CHUNKS * SUBLANES)
                rows = [pl.ds(pl.multiple_of(base + j * SUBLANES, SUBLANES), SUBLANES) for j in range(CHUNKS)]
                c0v = [c0buf[rows[j], lanes] for j in range(CHUNKS)]
                acc = [jnp.zeros((SUBLANES, LANES), F32)] * CHUNKS
                new = list(accs)
                for k in range(KC):
                    wk = jnp.broadcast_to(wdw_ref[k:k + 1, lanes], (SUBLANES, LANES))
                    for j in range(CHUNKS):
                        src = _shifted_source(cbuf, sh, base + j * SUBLANES, KC - 1 - k, lanes)
                        acc[j] = acc[j] + wk * src
                        new[k] = new[k] + c0v[j] * src
                for j in range(CHUNKS):
                    dc0buf[rows[j], lanes] = acc[j]
                return tuple(new)

            init = tuple(jnp.zeros((SUBLANES, LANES), F32) for _ in range(KC))
            accs = lax.fori_loop(0, TM // (CHUNKS * SUBLANES), conv_rows, init)
            for k in range(KC):
                dw_ref[k:k + 1, lanes] += jnp.sum(accs[k], axis=0, keepdims=True)
        dc0 = dc0buf[...]
        dza_ref[:, D:2 * D] = (dc0 * sg).astype(BF)
        dza_ref[:, 2 * D:3 * D] = (dc0 * za * sg * (1.0 - sg)).astype(BF)

    tile = pl.BlockSpec((TM, D), lambda i: (i, 0))
    nxt_spec = pl.BlockSpec((HALO, D), lambda i: (jnp.minimum((i + 1) * hb, nhb - 1), 0))
    return pl.pallas_call(
        body, name="mix_b2", grid=(nt,),
        in_specs=[tile, nxt_spec, tile, nxt_spec, pl.BlockSpec((TM, D), lambda i: (i, 1)),
                  pl.BlockSpec((TM, D), lambda i: (i, 2)), pl.BlockSpec((HALO, D), lambda i: (0, 0))],
        out_specs=[pl.BlockSpec((TM, 3 * D), lambda i: (i, 0)), pl.BlockSpec((HALO, D), lambda i: (0, 0))],
        out_shape=[jax.ShapeDtypeStruct((T, 3 * D), BF), jax.ShapeDtypeStruct((HALO, D), F32)],
        scratch_shapes=[pltpu.VMEM((n_ext + pad, D), F32), pltpu.VMEM((n_ext + pad, D), F32),
                        pltpu.VMEM((n_ext, D), F32), pltpu.VMEM((SUBLANES - 1, n_ext - SUBLANES, LANES), F32),
                        pltpu.VMEM((TM, D), F32), pltpu.VMEM((TM, D), F32)],
        compiler_params=_params(),
    )(dp, dp, dc1, dc1, z, z, wdw)


def _mix_b3(h, dh, dza, dzb, g, wfull):
    nt = T // TM

    def body(h_ref, dh_ref, dza_ref, dzb_ref, g_ref, w_hbm, dho_ref, u_ref, dg_ref, win, sem):
        _load_weights_once(w_hbm, (("win", win),), sem)

        @pl.when(pl.program_id(0) == 0)
        def _():
            dg_ref[...] = jnp.zeros_like(dg_ref)

        x = h_ref[...]
        r = lax.rsqrt(jnp.mean(x * x, axis=-1, keepdims=True) + RMS_EPS)
        xh = x * r
        gv = g_ref[...]
        u_ref[...] = (xh * gv).astype(BF)
        du = _dot_nn(dza_ref[...], win[0:3 * D, :]) + _dot_nn(dzb_ref[...], win[3 * D:5 * D, :])
        dg_ref[...] += jnp.sum(du * xh, axis=0, keepdims=True)
        dxh = du * gv
        dho_ref[...] = dh_ref[...] + r * (dxh - xh * jnp.mean(dxh * xh, axis=-1, keepdims=True))

    tile = lambda w: pl.BlockSpec((TM, w), lambda i: (i, 0))
    vec = pl.BlockSpec((1, D), lambda i: (0, 0))
    return pl.pallas_call(
        body, name="mix_b3", grid=(nt,),
        in_specs=[tile(D), tile(D), tile(3 * D), tile(2 * D), vec, ANY],
        out_specs=[tile(D), tile(D), vec],
        out_shape=[jax.ShapeDtypeStruct((T, D), F32), jax.ShapeDtypeStruct((T, D), BF),
                   jax.ShapeDtypeStruct((1, D), F32)],
        scratch_shapes=[pltpu.VMEM((5 * D, D), BF), pltpu.SemaphoreType.DMA((NCHIP,))],
        compiler_params=_params(),
    )(h, dh, dza, dzb, g, wfull)


def _mesh_pos():
    x, y, c = lax.axis_index("x"), lax.axis_index("y"), lax.axis_index("c")
    chips = [(1 - x, y), (x, 1 - y), (1 - x, 1 - y)]
    return x, y, c, 2 * x + y, chips


def _handshake(peers):
    barrier = pltpu.get_barrier_semaphore()
    for peer in peers:
        pl.semaphore_signal(barrier, inc=1, device_id=peer, device_id_type=MESH)
    pl.semaphore_wait(barrier, len(peers))


def _run_comm(body, name, cid, ins, inouts, out_types, scratch):
    in_refs = [jax.new_ref(a, memory_space=HBM) for a in ins]
    inout_refs = [jax.new_ref(a, memory_space=HBM) for a in inouts]
    out_refs = [jax.empty_ref(t, memory_space=HBM) for t in out_types]

    @pl.kernel(mesh=plsc.ScalarSubcoreMesh(axis_name="seq", num_cores=1), name=name,
               scratch_types=scratch, compiler_params=pltpu.CompilerParams(collective_id=cid))
    def launch(*scr):
        body(*in_refs, *inout_refs, *out_refs, *scr)

    launch()
    return [r[...] for r in inout_refs + out_refs]


def _remote(src, dst, send_sem, recv_sem, to):
    return pltpu.make_async_remote_copy(src_ref=src, dst_ref=dst, send_sem=send_sem, recv_sem=recv_sem,
                                        device_id=to, device_id_type=MESH)


def _gather_layer(packed, cid):
    rtot = packed.shape[0]
    half = rtot // 2
    kme = 2 * lax.axis_index("x") + lax.axis_index("y")
    landing = lax.dynamic_update_slice(lax.empty((NCHIP, rtot, D), BF), packed[None], (kme, 0, 0))

    def body(p_ref, w_ref, send_sems, recv_sems):
        x, y, c, kme, chips = _mesh_pos()
        sib = (x, y, 1 - c)
        _handshake([(*ch, c) for ch in chips] + [sib])
        ks = [2 * cx + cy for cx, cy in chips]
        mine = pl.ds(pl.multiple_of(c * half, 16), half)
        other = pl.ds(pl.multiple_of((1 - c) * half, 16), half)
        first = [_remote(p_ref.at[mine], w_ref.at[kme, mine], send_sems.at[j], recv_sems.at[j], (*chips[j], c))
                 for j in range(3)]
        for cp in first:
            cp.start()
        passed = [_remote(w_ref.at[ks[j], mine], w_ref.at[ks[j], mine], send_sems.at[3 + j], recv_sems.at[3 + j], sib)
                  for j in range(3)]
        for j in range(3):
            _remote(p_ref.at[mine], w_ref.at[ks[j], mine], send_sems.at[j], recv_sems.at[j], sib).wait_recv()
            passed[j].start()
        for j in range(3):
            _remote(p_ref.at[mine], w_ref.at[ks[j], other], send_sems.at[3 + j], recv_sems.at[3 + j], sib).wait_recv()
        for cp in first + passed:
            cp.wait_send()

    return _run_comm(body, "gather_layer_%d" % cid, cid, [packed], [landing], [],
                     (pltpu.SemaphoreType.DMA((6,)), pltpu.SemaphoreType.DMA((6,))))[0]


def _sibling_swap(gbuf, cid):
    rtot = gbuf.shape[1]
    half = rtot // 2

    def body(g_ref, r_ref, send_sem, recv_sem):
        x, y, c, _, _ = _mesh_pos()
        sib = (x, y, 1 - c)
        _handshake([sib])
        other = pl.ds(pl.multiple_of((1 - c) * half, 16), half)
        cp = _remote(g_ref.at[:, other, :], r_ref, send_sem, recv_sem, sib)
        cp.start()
        cp.wait()

    return _run_comm(body, "sibling_swap_%d" % cid, cid, [gbuf], [],
                     [jax.ShapeDtypeStruct((NCHIP, half, D), BF)],
                     (pltpu.SemaphoreType.DMA, pltpu.SemaphoreType.DMA))[0]


def _sibling_swap_inline(gbuf):
    rtot = gbuf.shape[1]
    half = rtot // 2

    def body(g_ref, r_ref, send_sem, recv_sem):
        x, y, c, _, _ = _mesh_pos()
        other = pl.ds(pl.multiple_of((1 - c) * half, 16), half)
        cp = _remote(g_ref.at[:, other, :], r_ref, send_sem, recv_sem, (x, y, 1 - c))
        cp.start()
        cp.wait()

    return pl.pallas_call(
        body, name="sibling_swap_inline", in_specs=[ANY], out_specs=ANY,
        out_shape=jax.ShapeDtypeStruct((NCHIP, half, D), BF),
        scratch_shapes=[pltpu.SemaphoreType.DMA, pltpu.SemaphoreType.DMA],
    )(gbuf)


def _row_tile(rows):
    for cand in range(min(rows, 1280) // 16 * 16, 0, -16):
        if rows % cand == 0:
            return cand
    return rows


def _chip_sum(gbuf, rbuf, cidx):
    half = rbuf.shape[1]
    rt = _row_tile(half)
    nb = half // rt

    def body(c_ref, g_ref, r_ref, o_ref):
        o_ref[...] = (g_ref[...].astype(F32) + r_ref[...].astype(F32)).astype(BF)

    return pl.pallas_call(
        body, name="chip_sum",
        grid_spec=pltpu.PrefetchScalarGridSpec(
            num_scalar_prefetch=1, grid=(NCHIP, nb),
            in_specs=[pl.BlockSpec((None, rt, D), lambda k, r, c: (k, c[0] * nb + r, 0)),
                      pl.BlockSpec((None, rt, D), lambda k, r, c: (k, r, 0))],
            out_specs=pl.BlockSpec((None, rt, D), lambda k, r, c: (k, r, 0))),
        out_shape=jax.ShapeDtypeStruct((NCHIP, half, D), BF),
        compiler_params=_params(("arbitrary", "arbitrary")),
    )(cidx, gbuf, rbuf)


def _chip_exchange(sbuf, cid):
    half = sbuf.shape[1]

    def body(s_ref, x_ref, send_sems, recv_sems):
        x, y, c, _, chips = _mesh_pos()
        _handshake([(*ch, c) for ch in chips])
        cps = [_remote(s_ref.at[2 * cx + cy], x_ref.at[j], send_sems.at[j], recv_sems.at[j], (cx, cy, c))
               for j, (cx, cy) in enumerate(chips)]
        for cp in cps:
            cp.start()
        for cp in cps:
            cp.wait()

    return _run_comm(body, "chip_exchange_%d" % cid, cid, [sbuf], [],
                     [jax.ShapeDtypeStruct((3, half, D), BF)],
                     (pltpu.SemaphoreType.DMA((3,)), pltpu.SemaphoreType.DMA((3,))))[0]


def _shard_sum(gbuf, rbuf, xbuf, ck):
    half = rbuf.shape[1]
    rt = _row_tile(half)
    nb = half // rt

    def body(ck_ref, g_ref, r_ref, x_ref, o_ref):
        acc = g_ref[...].astype(F32) + r_ref[...].astype(F32)
        for j in range(3):
            acc = acc + x_ref[j].astype(F32)
        o_ref[...] = acc

    return pl.pallas_call(
        body, name="shard_sum",
        grid_spec=pltpu.PrefetchScalarGridSpec(
            num_scalar_prefetch=1, grid=(nb,),
            in_specs=[pl.BlockSpec((None, rt, D), lambda r, ck: (ck[1], ck[0] * nb + r, 0)),
                      pl.BlockSpec((None, rt, D), lambda r, ck: (ck[1], r, 0)),
                      pl.BlockSpec((3, rt, D), lambda r, ck: (0, r, 0))],
            out_specs=pl.BlockSpec((rt, D), lambda r, ck: (ck[0] * nb + r, 0))),
        out_shape=jax.ShapeDtypeStruct((2 * half, D), F32),
        compiler_params=_params(),
    )(ck, gbuf, rbuf, xbuf)


def _sibling_share(red, cid):
    half = red.shape[0] // 2

    def body(o_ref, send_sem, recv_sem):
        x, y, c, _, _ = _mesh_pos()
        sib = (x, y, 1 - c)
        _handshake([sib])
        mine = pl.ds(pl.multiple_of(c * half, 8), half)
        other = pl.ds(pl.multiple_of((1 - c) * half, 8), half)
        cp = _remote(o_ref.at[mine], o_ref.at[mine], send_sem, recv_sem, sib)
        cp.start()
        cp.wait_send()
        _remote(o_ref.at[mine], o_ref.at[other], send_sem, recv_sem, sib).wait_recv()

    return _run_comm(body, "sibling_share_%d" % cid, cid, [], [red], [],
                     (pltpu.SemaphoreType.DMA, pltpu.SemaphoreType.DMA))[0]


def _allreduce_small(v):
    rows = v.shape[0]
    ndev = 2 * NCHIP

    def body(v_ref, o_ref, gat, send_sems, recv_sems, lsem):
        x, y, c, _, chips = _mesh_pos()
        me, sib = (x, y, c), (x, y, 1 - c)

        def blk(px, py, pc):
            return gat.at[pl.ds((4 * px + 2 * py + pc) * rows, rows), :]

        def copy(k, block, to, src=None):
            return pltpu.make_async_remote_copy(
                src_ref=blk(*block) if src is None else src, dst_ref=blk(*block),
                send_sem=send_sems.at[k], recv_sem=recv_sems.at[k], device_id=to, device_id_type=MESH)

        mine = pltpu.make_async_copy(v_ref, blk(*me), lsem)
        mine.start()
        first = [copy(0, me, sib, src=v_ref)]
        first += [copy(1 + j, me, (*chip, c), src=v_ref) for j, chip in enumerate(chips)]
        for cp in first:
            cp.start()
        passed = [copy(4 + j, (*chip, c), sib) for j, chip in enumerate(chips)]
        for j, chip in enumerate(chips):
            copy(1 + j, (*chip, c), me).wait_recv()
            passed[j].start()
        copy(0, sib, me).wait_recv()
        for j, chip in enumerate(chips):
            copy(4 + j, (*chip, 1 - c), me).wait_recv()
        for cp in first + passed:
            cp.wait_send()
        mine.wait()
        acc = gat[0:rows, :]
        for d in range(1, ndev):
            acc = acc + gat[d * rows:(d + 1) * rows, :]
        o_ref[...] = acc

    vm = pl.BlockSpec(memory_space=pltpu.VMEM)
    return pl.pallas_call(
        body, name="allreduce_small", in_specs=[vm], out_specs=vm,
        out_shape=jax.ShapeDtypeStruct((rows, D), F32),
        scratch_shapes=[pltpu.VMEM((ndev * rows, D), F32), pltpu.SemaphoreType.DMA((7,)),
                        pltpu.SemaphoreType.DMA((7,)), pltpu.SemaphoreType.DMA],
    )(v)


def _adamw(w, g, m, v):
    shape = w.shape
    cols = shape[-1]
    rows = w.size // cols
    bm = rows
    for cand in range(512, 0, -SUBLANES):
        if rows % cand == 0:
            bm = cand
            break
    bc1 = 1.0 - ADAM_B1 ** ADAM_STEP
    bc2 = 1.0 - ADAM_B2 ** ADAM_STEP

    def body(w_ref, g_ref, m_ref, v_ref, d_ref, mo_ref, vo_ref):
        gv = g_ref[...]
        mn = ADAM_B1 * m_ref[...] + (1.0 - ADAM_B1) * gv
        vn = ADAM_B2 * v_ref[...] + (1.0 - ADAM_B2) * (gv * gv)
        mo_ref[...] = mn
        vo_ref[...] = vn
        d_ref[...] = -ADAM_LR * ((mn / bc1) / (jnp.sqrt(vn / bc2) + ADAM_EPS) + ADAM_WD * w_ref[...])

    spec = pl.BlockSpec((bm, cols), lambda i: (i, 0))
    out = jax.ShapeDtypeStruct((rows, cols), F32)
    d, mo, vo = pl.pallas_call(
        body, name="adamw", grid=(rows // bm,), in_specs=[spec] * 4, out_specs=[spec] * 3,
        out_shape=[out, out, out], compiler_params=_params(),
    )(*[t.reshape(rows, cols) for t in (w, g, m, v)])
    return d.reshape(shape), mo.reshape(shape), vo.reshape(shape)


def _adamw_layer(w, g, m, v, li, prev):
    cols = w.shape[-1]
    rows = w[0].size // cols
    bm = rows
    for cand in range(512, 0, -SUBLANES):
        if rows % cand == 0:
            bm = cand
            break
    bc1 = 1.0 - ADAM_B1 ** ADAM_STEP
    bc2 = 1.0 - ADAM_B2 ** ADAM_STEP

    def body(*refs):
        w_ref, g_ref, m_ref, v_ref = refs[:4]
        go_ref, d_ref, mo_ref, vo_ref = refs[-4:]
        gv = g_ref[...]
        mn = ADAM_B1 * m_ref[...] + (1.0 - ADAM_B1) * gv
        vn = ADAM_B2 * v_ref[...] + (1.0 - ADAM_B2) * (gv * gv)
        go_ref[...] = gv
        mo_ref[...] = mn
        vo_ref[...] = vn
        d_ref[...] = -ADAM_LR * ((mn / bc1) / (jnp.sqrt(vn / bc2) + ADAM_EPS) + ADAM_WD * w_ref[...])

    layer = pl.BlockSpec((None, bm, cols), lambda i: (li, i, 0))
    in_specs = [layer, pl.BlockSpec((bm, cols), lambda i: (i, 0)), layer, layer]
    args = [w.reshape(DEPTH, rows, cols), g.reshape(rows, cols), m.reshape(DEPTH, rows, cols),
            v.reshape(DEPTH, rows, cols)]
    aliases = {}
    if prev is not None:
        in_specs += [ANY] * 4
        args += list(prev)
        aliases = {4 + i: i for i in range(4)}
    out = jax.ShapeDtypeStruct((DEPTH, rows, cols), F32)
    return pl.pallas_call(
        body, name="adamw_layer", grid=(rows // bm,), in_specs=in_specs, out_specs=[layer] * 4,
        out_shape=[out] * 4, input_output_aliases=aliases, compiler_params=_params(),
    )(*args)


def _pack_shards(ws, li):
    pg = D // NG
    t = lambda a: jnp.swapaxes(a[li], 0, 1)
    parts = [t(ws["ffn1_w_gate"]), t(ws["ffn1_w_up"]), ws["ffn1_w_down"][li],
             t(ws["w_in"]), ws["conv_w_out"][li], ws["w_out"][li],
             t(ws["ffn2_w_gate"]), t(ws["ffn2_w_up"]), ws["ffn2_w_down"][li], ws["ple_w_gate"][li],
             t(ws["ple_w_proj"]).reshape(-1, D),
             jnp.swapaxes(ws["pool_w"][li], 0, 1).reshape(pg // NCHIP, D)]
    return jnp.concatenate([p.astype(BF) for p in parts], axis=0)


def _unpack_shards(parts):
    lay, _ = _layout()
    pg = D // NG

    def rows(n):
        off, rs = lay[n]
        for r0, arr in parts:
            if r0 <= off and off + rs <= r0 + arr.shape[0]:
                return arr[off - r0:off - r0 + rs, :]
        raise ValueError(n)

    t = lambda a: jnp.swapaxes(a, 0, 1)
    return {
        "ffn1_w_gate": rows("wg1"), "ffn1_w_up": rows("wu1"), "ffn1_w_down": rows("wd1"),
        "ffn2_w_gate": rows("wg2"), "ffn2_w_up": rows("wu2"), "ffn2_w_down": rows("wd2"),
        "w_in": t(rows("win")), "conv_w_out": rows("wco"), "w_out": rows("wo"), "ple_w_gate": rows("wpg"),
        "ple_w_proj": t(rows("wpp").reshape(D // NCHIP, PD)),
        "pool_w": jnp.swapaxes(rows("pw").reshape(pg // NCHIP, NG, pg), 0, 1),
    }


_BIG = ("ffn1_w_gate", "ffn1_w_up", "ffn1_w_down", "w_in", "pool_w", "conv_w_out", "w_out",
        "ffn2_w_gate", "ffn2_w_up", "ffn2_w_down", "ple_w_gate", "ple_w_proj")
_TRANSPOSED = ("ffn1_w_gate", "ffn1_w_up", "ffn2_w_gate", "ffn2_w_up")
_VECS = ("ffn1_norm", "mix_norm", "pool_scale", "conv_dw_b", "conv_ln_g", "conv_ln_b", "ffn2_norm", "ple_norm")
_WEIGHTS = ("ffn1_norm", "ffn1_w_gate", "ffn1_w_up", "ffn1_w_down", "mix_norm", "w_in", "pool_w", "pool_scale",
            "conv_dw_w", "conv_dw_b", "conv_ln_g", "conv_ln_b", "conv_w_out", "w_out", "ffn2_norm",
            "ffn2_w_gate", "ffn2_w_up", "ffn2_w_down", "ple_norm", "ple_w_gate", "ple_w_proj", "final_norm")


def _step(x, p, tgt, ws, ms, vs):
    lay, rtot = _layout()
    pg = D // NG
    cpos = lax.axis_index("c")
    kme = 2 * lax.axis_index("x") + lax.axis_index("y")
    cidx = jnp.stack([cpos]).astype(jnp.int32)
    ck = jnp.stack([cpos, kme]).astype(jnp.int32)
    h = x.reshape(T, D)
    tgt = tgt.reshape(T, D)

    nfirst = _first_rows()
    packed = [_pack_shards(ws, li) for li in range(DEPTH)]
    wfirst, wrest = [None] * DEPTH, [None] * DEPTH
    wfirst[0] = _gather_layer(packed[0][:nfirst], 0)
    piece, wfirst[0] = lax.optimization_barrier((packed[0][nfirst:], wfirst[0]))
    wrest[0] = _gather_layer(piece, 1)
    wppt, pwcat = [None] * DEPTH, [None] * DEPTH
    kk = ws["conv_dw_w"].shape[1]
    wdw_mine = jnp.zeros((DEPTH * HALO, D), F32)
    for li in range(DEPTH):
        blockw = jnp.zeros((kk, D), F32)
        mine = jnp.where(cpos == 0, ws["conv_dw_w"][li], 0.0)
        blockw = lax.dynamic_update_slice(blockw, mine, (0, kme * (D // NCHIP)))
        wdw_mine = wdw_mine.at[li * HALO:li * HALO + kk, :].set(blockw)
    wdw_all = _allreduce_small(wdw_mine)
    wdw = [wdw_all[li * HALO:(li + 1) * HALO, :] for li in range(DEPTH)]
    vec = lambda name, li: _row(ws[name][li])

    saved = []
    for li in range(DEPTH):
        if li > 0:
            wfirst[li], wrest[li], h = lax.optimization_barrier((wfirst[li], wrest[li], h))
        h0 = h
        h1, a1, b1 = _ffn_fwd(h0, vec("ffn1_norm", li), wfirst[li], 1)
        if li + 1 < DEPTH:
            piece, wrest[li], h1 = lax.optimization_barrier((packed[li + 1][:nfirst], wrest[li], h1))
            wfirst[li + 1] = _gather_layer(piece, 2 * li + 2)
        else:
            wrest[li], h1 = lax.optimization_barrier((wrest[li], h1))
        o, s = lay["wpp"]
        wppt[li] = wrest[li][:, o - nfirst:o - nfirst + s, :].reshape(D, PD)
        o, s = lay["pw"]
        pwcat[li] = wrest[li][:, o - nfirst:o - nfirst + s, :].reshape(pg, D)
        z = _mix_in_fwd(h1, vec("mix_norm", li), wrest[li])
        h2, pooled, c1, cc = _mix_mid_fwd(h1, z, pwcat[li], vec("pool_scale", li), wdw[li],
                                          vec("conv_dw_b", li), vec("conv_ln_g", li), vec("conv_ln_b", li),
                                          wrest[li])
        if li + 1 < DEPTH:
            piece, wfirst[li + 1], h2 = lax.optimization_barrier((packed[li + 1][nfirst:], wfirst[li + 1], h2))
            wrest[li + 1] = _gather_layer(piece, 2 * li + 3)
        h3, a2, b2 = _ffn_fwd(h2, vec("ffn2_norm", li), wrest[li], 2)
        h = _ple_fwd(h3, p[li, 0], vec("ple_norm", li), wppt[li], wrest[li])
        saved.append((h0, a1, b1, h1, z, pooled, c1, cc, h2, a2, b2, h3))

    dh, losscols, dgf = _loss_bwd(h, tgt, _row(ws["final_norm"]))
    loss = lax.psum(jnp.sum(losscols), ("x", "y", "c"))
    vecg = [dict() for _ in range(DEPTH)]
    dwdw = [None] * DEPTH
    def reduce_start(g, chain, inline=False):
        cid = 2 * DEPTH + 3 * chain
        return {"g": g, "r": _sibling_swap_inline(g) if inline else _sibling_swap(g, cid), "cid": cid}

    def reduce_mid(st, anchor):
        if anchor is not None:
            anchor, st["r"] = lax.optimization_barrier((anchor, st["r"]))
        sbuf = _chip_sum(st["g"], st["r"], cidx)
        if anchor is not None:
            anchor, sbuf = lax.optimization_barrier((anchor, sbuf))
        st["x"] = _chip_exchange(sbuf, st["cid"] + 1)
        return anchor

    def reduce_end(st, anchor):
        xb = st["x"]
        if anchor is not None:
            anchor, xb = lax.optimization_barrier((anchor, xb))
        rsum = _shard_sum(st["g"], st["r"], xb, ck)
        if anchor is not None:
            anchor, rsum = lax.optimization_barrier((anchor, rsum))
        return anchor, _sibling_share(rsum, st["cid"] + 2)

    parts = [[] for _ in range(DEPTH)]
    above = None
    nchain = 0
    for li in reversed(range(DEPTH)):
        h0, a1, b1, h1, z, pooled, c1, cc, h2, a2, b2, h3 = saved[li]
        w = wrest[li]
        dh, dgp, dwpp, gbuf = _ple_bwd(h3, dh, p[li, 0], vec("ple_norm", li), wppt[li], w)
        vecg[li]["ple_norm"] = dgp
        if above is not None:
            dh = reduce_mid(above[1], dh)
        dh_in, da, db, sact, n, dg = _ffn_bwd(h2, dh, a2, b2, vec("ffn2_norm", li), w, 2)
        vecg[li]["ffn2_norm"] = dg
        gbuf = _wgrad([da, db], n, gbuf, ("wg2", "wu2"), 0, F // 2, F // NCHIP)
        gbuf = _wgrad([sact], dh, gbuf, ("wd2",), 0, F // 2, F // NCHIP, yscale=0.5)
        dh = dh_in
        dp, dc1, dzb, small, dpw, gbuf = _mix_b1(dh, z, pooled, c1, cc, pwcat[li], vec("pool_scale", li),
                                                 vec("conv_ln_g", li), vec("conv_ln_b", li), w, gbuf)
        vecg[li]["pool_scale"] = small[0:1]
        vecg[li]["conv_dw_b"] = small[1:2]
        vecg[li]["conv_ln_g"] = small[2:3]
        vecg[li]["conv_ln_b"] = small[3:4]
        dza, dwdw[li] = _mix_b2(dp, dc1, z, wdw[li])
        if above is not None:
            dza, rsum = reduce_end(above[1], dza)
            parts[above[0]].append((0, rsum))
            above = None
        dh_in, u, dg = _mix_b3(h1, dh, dza, dzb, vec("mix_norm", li), w)
        vecg[li]["mix_norm"] = dg
        gbuf = _wgrad([dza], u, gbuf, ("win",), 0, D, D // NCHIP)
        gbuf = _wgrad([dzb], u, gbuf, ("win",), 3 * D, D, D // NCHIP)
        dh = dh_in
        o, s = lay["wpp"]
        small_rows = jnp.concatenate([dwpp.reshape(NCHIP, s, D), dpw.reshape(NCHIP, lay["pw"][1], D)], axis=1)
        gbuf = lax.dynamic_update_slice(gbuf, small_rows.astype(BF), (0, o, 0))
        if li == 0:
            rest = reduce_start(gbuf[:, nfirst:, :], nchain)
            nchain += 1
        dh_in, da, db, sact, n, dg = _ffn_bwd(h0, dh, a1, b1, vec("ffn1_norm", li), wfirst[li], 1)
        vecg[li]["ffn1_norm"] = dg
        if li == 0:
            da = reduce_mid(rest, da)
        gbuf = _wgrad([da, db], n, gbuf, ("wg1", "wu1"), 0, F // 2, F // NCHIP)
        gbuf = _wgrad([sact], dh, gbuf, ("wd1",), 0, F // 2, F // NCHIP, yscale=0.5)
        dh = dh_in
        if li == 0:
            first = reduce_start(gbuf[:, :nfirst, :], nchain, inline=True)
            nchain += 1
            reduce_mid(first, None)
        else:
            above = (li, reduce_start(gbuf, nchain))
            nchain += 1
    grad_x = dh.reshape(x.shape)
    rsum = _shard_sum(rest["g"], rest["r"], rest["x"], ck)
    first["x"], rsum = lax.optimization_barrier((first["x"], rsum))
    parts[0].append((nfirst, _sibling_share(rsum, rest["cid"] + 2)))
    updated = {n: None for n in _BIG}
    view = lambda n, a: jnp.swapaxes(a, 1, 2) if n in _TRANSPOSED else a
    for li in reversed(range(1, DEPTH)):
        un = _unpack_shards(parts[li])
        for n in _BIG:
            updated[n] = _adamw_layer(view(n, ws[n]), un[n], view(n, ms[n]), view(n, vs[n]), li, updated[n])
    first["x"], updated = lax.optimization_barrier((first["x"], updated))
    parts[0].append((0, reduce_end(first, None)[1]))
    un = _unpack_shards(parts[0])
    for n in _BIG:
        updated[n] = _adamw_layer(view(n, ws[n]), un[n], view(n, ms[n]), view(n, vs[n]), 0, updated[n])

    rows = [vecg[li][n] for li in range(DEPTH) for n in _VECS] + [dgf]
    rows.append(jnp.zeros((8 - (len(rows) % 8), D), F32))
    vsum = _allreduce_small(jnp.concatenate(rows + dwdw, axis=0))
    nvec = len(_VECS)
    grads = {}
    for i, n in enumerate(_VECS):
        grads[n] = jnp.stack([vsum[li * nvec + i] for li in range(DEPTH)])
    grads["final_norm"] = vsum[DEPTH * nvec]
    base = DEPTH * nvec + 8 - ((DEPTH * nvec + 1) % 8) + 1
    dcols = D // NCHIP
    grads["conv_dw_w"] = jnp.stack([
        lax.dynamic_slice(vsum[base + li * HALO: base + li * HALO + kk, :], (0, kme * dcols), (kk, dcols))
        for li in range(DEPTH)])

    outs_g, outs_d, outs_m, outs_v = [], [], [], []
    for n in _WEIGHTS:
        if n in _BIG:
            gq, d, mo, vo = [view(n, t.reshape(view(n, ws[n]).shape)) for t in updated[n]]
        else:
            gq = grads[n]
            d, mo, vo = _adamw(ws[n], gq, ms[n], vs[n])
        outs_g.append(gq)
        outs_d.append(d)
        outs_m.append(mo)
        outs_v.append(vo)
    return (loss, grad_x, *outs_g, *outs_d, *outs_m, *outs_v)


def kernel(x, p, ffn1_norm, ffn1_w_gate, ffn1_w_up, ffn1_w_down, mix_norm, w_in, pool_w, pool_scale, conv_dw_w, conv_dw_b, conv_ln_g, conv_ln_b, conv_w_out, w_out, ffn2_norm, ffn2_w_gate, ffn2_w_up, ffn2_w_down, ple_norm, ple_w_gate, ple_w_proj, final_norm, loss_target, m_ffn1_norm, m_ffn1_w_gate, m_ffn1_w_up, m_ffn1_w_down, m_mix_norm, m_w_in, m_pool_w, m_pool_scale, m_conv_dw_w, m_conv_dw_b, m_conv_ln_g, m_conv_ln_b, m_conv_w_out, m_w_out, m_ffn2_norm, m_ffn2_w_gate, m_ffn2_w_up, m_ffn2_w_down, m_ple_norm, m_ple_w_gate, m_ple_w_proj, m_final_norm, v_ffn1_norm, v_ffn1_w_gate, v_ffn1_w_up, v_ffn1_w_down, v_mix_norm, v_w_in, v_pool_w, v_pool_scale, v_conv_dw_w, v_conv_dw_b, v_conv_ln_g, v_conv_ln_b, v_conv_w_out, v_w_out, v_ffn2_norm, v_ffn2_w_gate, v_ffn2_w_up, v_ffn2_w_down, v_ple_norm, v_ple_w_gate, v_ple_w_proj, v_final_norm):
    ws = dict(zip(_WEIGHTS, (ffn1_norm, ffn1_w_gate, ffn1_w_up, ffn1_w_down, mix_norm, w_in, pool_w, pool_scale, conv_dw_w, conv_dw_b, conv_ln_g, conv_ln_b, conv_w_out, w_out, ffn2_norm, ffn2_w_gate, ffn2_w_up, ffn2_w_down, ple_norm, ple_w_gate, ple_w_proj, final_norm)))
    ms = dict(zip(_WEIGHTS, (m_ffn1_norm, m_ffn1_w_gate, m_ffn1_w_up, m_ffn1_w_down, m_mix_norm, m_w_in, m_pool_w, m_pool_scale, m_conv_dw_w, m_conv_dw_b, m_conv_ln_g, m_conv_ln_b, m_conv_w_out, m_w_out, m_ffn2_norm, m_ffn2_w_gate, m_ffn2_w_up, m_ffn2_w_down, m_ple_norm, m_ple_w_gate, m_ple_w_proj, m_final_norm)))
    vs = dict(zip(_WEIGHTS, (v_ffn1_norm, v_ffn1_w_gate, v_ffn1_w_up, v_ffn1_w_down, v_mix_norm, v_w_in, v_pool_w, v_pool_scale, v_conv_dw_w, v_conv_dw_b, v_conv_ln_g, v_conv_ln_b, v_conv_w_out, v_w_out, v_ffn2_norm, v_ffn2_w_gate, v_ffn2_w_up, v_ffn2_w_down, v_ple_norm, v_ple_w_gate, v_ple_w_proj, v_final_norm)))
    return _step(x, p, loss_target, ws, ms, vs)
```

```python
import jax
import jax.numpy as jnp
from jax import lax
from jax.experimental import pallas as pl
from jax.experimental.pallas import tpu as pltpu
from jax.experimental.pallas import tpu_sc as plsc

T = 8192
D = 1024
F = 2816
PD = 256
NG = 4
WINDOWS = (2, 4, 8, 16)
KC = 31
HALO = 32
DEPTH = 2
NCHIP = 4
RMS_EPS = 1e-6
LN_EPS = 1e-5

ADAM_LR = 0.001
ADAM_B1 = 0.9
ADAM_B2 = 0.999
ADAM_EPS = 1e-08
ADAM_WD = 0.01
ADAM_STEP = 10

TM = 512
TMB = 256
TMW = 1024
MXU_TILE = 256
LANES = 128
SUBLANES = 8
CHUNKS = 4
VMEM_LIMIT = 56 * 1024 * 1024

BF = jnp.bfloat16
F32 = jnp.float32
MESH = pl.DeviceIdType.MESH
ANY = pl.BlockSpec(memory_space=pl.ANY)
HBM = pltpu.MemorySpace.HBM


def _layout():
    fs, ins, ds = F // NCHIP, 5 * D // NCHIP, D // NCHIP
    pps = ds * PD // D
    pws = NG * (D // NG // NCHIP) * (D // NG) // D
    names = [("wg1", fs), ("wu1", fs), ("wd1", fs), ("win", ins), ("wco", ds), ("wo", ds),
             ("wg2", fs), ("wu2", fs), ("wd2", fs), ("wpg", ds), ("wpp", pps), ("pw", pws)]
    off, r = {}, 0
    for n, s in names:
        off[n] = (r, s)
        r += s
    return off, r


def _sig(v):
    return 0.5 * jnp.tanh(0.5 * v) + 0.5


def _dot_nn(a, b):
    return jnp.dot(a, b, preferred_element_type=F32)


def _dot_nt(a, b):
    return lax.dot_general(a, b, (((1,), (1,)), ((), ())), preferred_element_type=F32)


def _dot_tn(a, b):
    return lax.dot_general(a, b, (((0,), (0,)), ((), ())), preferred_element_type=F32)


def _params(sem=("arbitrary",)):
    return pltpu.CompilerParams(dimension_semantics=sem, vmem_limit_bytes=VMEM_LIMIT)


def _first_rows():
    return _layout()[0]["win"][0]


def _weight_copies(w_hbm, specs, sem):
    lay, _ = _layout()
    ra = _first_rows()
    cps = []
    for i, (name, dst) in enumerate(specs):
        off, rs = lay[name]
        off = off if off < ra else off - ra
        for k in range(NCHIP):
            cps.append(pltpu.make_async_copy(w_hbm.at[k, pl.ds(off, rs), :],
                                             dst.at[pl.ds(k * rs, rs), :], sem.at[i * NCHIP + k]))
    return cps


def _load_weights_once(w_hbm, specs, sem):
    @pl.when(pl.program_id(0) == 0)
    def _():
        cps = _weight_copies(w_hbm, specs, sem)
        for cp in cps:
            cp.start()
        for cp in cps:
            cp.wait()


def _grad_copies(stage, g_hbm, name, row0, rows, piece, sem, sem0):
    lay, _ = _layout()
    off, rs = lay[name]
    cps = []
    for i in range(rows // piece):
        rglob = row0 + i * piece
        k = rglob // rs
        loc = rglob - k * rs
        start = off + loc
        if not isinstance(start, int):
            start = pl.multiple_of(start, 16)
        dst = g_hbm.at[k, pl.ds(start, piece), :]
        cps.append(pltpu.make_async_copy(stage.at[pl.ds(i * piece, piece), :], dst, sem.at[sem0 + i]))
    return cps


def _row(v):
    return v.reshape(1, -1)


def _shifted_source(buf, sh, base, s, lanes):
    a, b = divmod(s, SUBLANES)
    rows = pl.ds(pl.multiple_of(base + SUBLANES * a, SUBLANES), SUBLANES)
    if b == 0:
        return buf[rows, lanes]
    return sh[b - 1, rows, :]


def _fill_shifted(buf, sh, lanes):
    rows = sh.shape[1]
    for b in range(1, SUBLANES):
        sh[b - 1, :, :] = buf[pl.ds(b, rows), lanes]


def _hidden_chunks():
    tiles = F // MXU_TILE
    first = (tiles + 1) // 2 * MXU_TILE if F % MXU_TILE == 0 else F // 2
    return [pl.ds(0, first), pl.ds(first, F - first)]


def _ffn_fwd(h, g, wfull, which):
    nt = T // TM
    names = ("wg%d" % which, "wu%d" % which, "wd%d" % which)

    def body(h_ref, g_ref, w_hbm, ho_ref, a_ref, b_ref, wg, wu, wd, sem):
        _load_weights_once(w_hbm, ((names[0], wg), (names[1], wu), (names[2], wd)), sem)
        x = h_ref[...]
        r = lax.rsqrt(jnp.mean(x * x, axis=-1, keepdims=True) + RMS_EPS)
        n = (x * r * g_ref[...]).astype(BF)
        acc = jnp.zeros((TM, D), F32)
        for sl in _hidden_chunks():
            a = _dot_nt(n, wg[sl, :])
            b = _dot_nt(n, wu[sl, :])
            a_ref[:, sl] = a.astype(BF)
            b_ref[:, sl] = b.astype(BF)
            s = (a * _sig(a) * b).astype(BF)
            acc = acc + _dot_nn(s, wd[sl, :])
        ho_ref[...] = x + 0.5 * acc

    tile = lambda w: pl.BlockSpec((TM, w), lambda i: (i, 0))
    return pl.pallas_call(
        body, name="ffn_fwd", grid=(nt,),
        in_specs=[tile(D), pl.BlockSpec((1, D), lambda i: (0, 0)), ANY],
        out_specs=[tile(D), tile(F), tile(F)],
        out_shape=[jax.ShapeDtypeStruct((T, D), F32), jax.ShapeDtypeStruct((T, F), BF),
                   jax.ShapeDtypeStruct((T, F), BF)],
        scratch_shapes=[pltpu.VMEM((F, D), BF), pltpu.VMEM((F, D), BF), pltpu.VMEM((F, D), BF),
                        pltpu.SemaphoreType.DMA((3 * NCHIP,))],
        compiler_params=_params(),
    )(h, g, wfull)


def _mix_in_fwd(h, g, wfull):
    tm = TM
    nt = T // tm
    nin = 5 * D

    def body(h_ref, g_ref, w_hbm, z_ref, win, sem):
        _load_weights_once(w_hbm, (("win", win),), sem)
        x = h_ref[...]
        r = lax.rsqrt(jnp.mean(x * x, axis=-1, keepdims=True) + RMS_EPS)
        u = (x * r * g_ref[...]).astype(BF)
        for c in range(5):
            sl = pl.ds(c * D, D)
            z_ref[:, sl] = _dot_nt(u, win[sl, :]).astype(BF)

    return pl.pallas_call(
        body, name="mix_in_fwd", grid=(nt,),
        in_specs=[pl.BlockSpec((tm, D), lambda i: (i, 0)), pl.BlockSpec((1, D), lambda i: (0, 0)), ANY],
        out_specs=pl.BlockSpec((tm, nin), lambda i: (i, 0)),
        out_shape=jax.ShapeDtypeStruct((T, nin), BF),
        scratch_shapes=[pltpu.VMEM((nin, D), BF), pltpu.SemaphoreType.DMA((NCHIP,))],
        compiler_params=_params(),
    )(h, g, wfull)


def _pool_counts(i, rows, w):
    t = i * TM + lax.broadcasted_iota(jnp.int32, (rows, 1), 0)
    return jnp.minimum(t + 1, w).astype(F32)


def _mix_mid_fwd(h, z, pwcat, pscale, wdw, bdw, lg, lb, wfull):
    nt = T // TM
    pg = D // NG
    hb = TM // HALO
    n_ext = HALO + TM
    pad = SUBLANES

    def body(h_ref, z_ref, zh_ref, pw_ref, ps_ref, wdw_ref, bdw_ref, lg_ref, lb_ref, w_hbm,
             h2_ref, p_ref, c1_ref, cc_ref, wco, wo, pa, pb, cbuf, sh, c1buf, ambuf, sem):
        i = pl.program_id(0)
        _load_weights_once(w_hbm, (("wco", wco), ("wo", wo)), sem)

        @pl.when(i == 0)
        def _():
            pa[0:pad, :] = jnp.zeros((pad, D), F32)
            pb[0:pad, :] = jnp.zeros((pad, D), F32)

        keep = (i > 0).astype(F32)
        zh = zh_ref[...].astype(F32) * keep
        za = z_ref[:, D:2 * D].astype(F32)
        zg = z_ref[:, 2 * D:3 * D].astype(F32)
        pa[pad:pad + HALO, :] = zh[:, 0:D]
        pa[pad + HALO:pad + n_ext, :] = z_ref[:, 0:D].astype(F32)
        cbuf[0:HALO, :] = zh[:, D:2 * D] * _sig(zh[:, 2 * D:3 * D])
        cbuf[HALO:n_ext, :] = za * _sig(zg)
        for g, w in enumerate(WINDOWS):
            ls = pl.ds(g * pg, pg)
            cur, nxt = pa, pb
            d = 1
            while d < w:
                nxt[pl.ds(pad, n_ext), ls] = cur[pl.ds(pad, n_ext), ls] + cur[pl.ds(pad - d, n_ext), ls]
                cur, nxt = nxt, cur
                d *= 2
            tok = z_ref[:, ls].astype(F32)
            pooled = (cur[pl.ds(pad + HALO, TM), ls] / _pool_counts(i, TM, w) - tok).astype(BF)
            p_ref[:, ls] = pooled
            ambuf[:, ls] = _dot_nn(pooled, pw_ref[:, ls])
        am = ambuf[...] * ps_ref[...]
        for l in range(D // LANES):
            lanes = pl.ds(l * LANES, LANES)
            _fill_shifted(cbuf, sh, lanes)
            bias = jnp.broadcast_to(bdw_ref[:, lanes], (SUBLANES, LANES))

            def conv_rows(r, carry):
                base = r * (CHUNKS * SUBLANES)
                accs = [bias] * CHUNKS
                for k in range(KC):
                    wk = jnp.broadcast_to(wdw_ref[k:k + 1, lanes], (SUBLANES, LANES))
                    for j in range(CHUNKS):
                        src = _shifted_source(cbuf, sh, base + j * SUBLANES, HALO - (KC - 1) + k, lanes)
                        accs[j] = accs[j] + wk * src
                for j in range(CHUNKS):
                    c1buf[pl.ds(pl.multiple_of(base + j * SUBLANES, SUBLANES), SUBLANES), lanes] = accs[j]
                return carry

            lax.fori_loop(0, TM // (CHUNKS * SUBLANES), conv_rows, 0)
        c1b = c1buf[...].astype(BF)
        c1_ref[...] = c1b
        c1 = c1b.astype(F32)
        mu = jnp.mean(c1, axis=-1, keepdims=True)
        xc = c1 - mu
        var = jnp.mean(xc * xc, axis=-1, keepdims=True)
        c2 = xc * lax.rsqrt(var + LN_EPS) * lg_ref[...] + lb_ref[...]
        c3 = (c2 * _sig(c2)).astype(BF)
        ccb = _dot_nn(c3, wco[...]).astype(BF)
        cc_ref[...] = ccb
        gp = z_ref[:, 3 * D:4 * D].astype(F32)
        gc = z_ref[:, 4 * D:5 * D].astype(F32)
        m = (_sig(gp) * am + _sig(gc) * ccb.astype(F32)).astype(BF)
        h2_ref[...] = h_ref[...] + _dot_nn(m, wo[...])

    tile = pl.BlockSpec((TM, D), lambda i: (i, 0))
    vec = pl.BlockSpec((1, D), lambda i: (0, 0))
    return pl.pallas_call(
        body, name="mix_mid_fwd", grid=(nt,),
        in_specs=[tile, pl.BlockSpec((TM, 5 * D), lambda i: (i, 0)),
                  pl.BlockSpec((HALO, 3 * D), lambda i: (jnp.maximum(i * hb - 1, 0), 0)),
                  pl.BlockSpec((pg, D), lambda i: (0, 0)), vec,
                  pl.BlockSpec((HALO, D), lambda i: (0, 0)), vec, vec, vec, ANY],
        out_specs=[tile, tile, tile, tile],
        out_shape=[jax.ShapeDtypeStruct((T, D), F32), jax.ShapeDtypeStruct((T, D), BF),
                   jax.ShapeDtypeStruct((T, D), BF), jax.ShapeDtypeStruct((T, D), BF)],
        scratch_shapes=[pltpu.VMEM((D, D), BF), pltpu.VMEM((D, D), BF),
                        pltpu.VMEM((pad + n_ext, D), F32), pltpu.VMEM((pad + n_ext, D), F32),
                        pltpu.VMEM((n_ext, D), F32), pltpu.VMEM((SUBLANES - 1, n_ext - SUBLANES, LANES), F32),
                        pltpu.VMEM((TM, D), F32), pltpu.VMEM((TM, D), F32),
                        pltpu.SemaphoreType.DMA((2 * NCHIP,))],
        compiler_params=_params(),
    )(h, z, z, pwcat, pscale, wdw, bdw, lg, lb, wfull)


def _ple_fwd(h, p, g, wppt, wfull):
    nt = T // TM

    def body(h_ref, p_ref, g_ref, wpp_ref, w_hbm, ho_ref, wpg, sem):
        _load_weights_once(w_hbm, (("wpg", wpg),), sem)
        x = h_ref[...]
        r = lax.rsqrt(jnp.mean(x * x, axis=-1, keepdims=True) + RMS_EPS)
        n = (x * r * g_ref[...]).astype(BF)
        gate = _sig(_dot_nn(n, wpg[...]))
        pe = _dot_nt(p_ref[...].astype(BF), wpp_ref[...])
        ho_ref[...] = x + gate * pe

    tile = pl.BlockSpec((TM, D), lambda i: (i, 0))
    return pl.pallas_call(
        body, name="ple_fwd", grid=(nt,),
        in_specs=[tile, pl.BlockSpec((TM, PD), lambda i: (i, 0)), pl.BlockSpec((1, D), lambda i: (0, 0)),
                  pl.BlockSpec((D, PD), lambda i: (0, 0)), ANY],
        out_specs=tile, out_shape=jax.ShapeDtypeStruct((T, D), F32),
        scratch_shapes=[pltpu.VMEM((D, D), BF), pltpu.SemaphoreType.DMA((NCHIP,))],
        compiler_params=_params(),
    )(h, p, g, wppt, wfull)


def _loss_bwd(h, tgt, g):
    nt = T // TM

    def body(h_ref, t_ref, g_ref, dh_ref, loss_ref, dg_ref):
        @pl.when(pl.program_id(0) == 0)
        def _():
            loss_ref[...] = jnp.zeros_like(loss_ref)
            dg_ref[...] = jnp.zeros_like(dg_ref)
        x = h_ref[...]
        r = lax.rsqrt(jnp.mean(x * x, axis=-1, keepdims=True) + RMS_EPS)
        xh = x * r
        gv = g_ref[...]
        e = xh * gv - t_ref[...]
        loss_ref[...] += jnp.sum(e * e, axis=0, keepdims=True) * (0.5 / D)
        dy = e * (1.0 / D)
        dg_ref[...] += jnp.sum(dy * xh, axis=0, keepdims=True)
        dxh = dy * gv
        dh_ref[...] = r * (dxh - xh * jnp.mean(dxh * xh, axis=-1, keepdims=True))

    tile = pl.BlockSpec((TM, D), lambda i: (i, 0))
    vec = pl.BlockSpec((1, D), lambda i: (0, 0))
    return pl.pallas_call(
        body, name="loss_bwd", grid=(nt,), in_specs=[tile, tile, vec], out_specs=[tile, vec, vec],
        out_shape=[jax.ShapeDtypeStruct((T, D), F32), jax.ShapeDtypeStruct((1, D), F32),
                   jax.ShapeDtypeStruct((1, D), F32)],
        compiler_params=_params(),
    )(h, tgt, g)


def _ple_bwd(h, dh, p, g, wppt, wfull):
    nt = T // TM
    _, rtot = _layout()

    def body(h_ref, dh_ref, p_ref, g_ref, wpp_ref, w_hbm,
             dho_ref, dg_ref, dwpp_ref, g_out, wpg, acc, stage, sem, osem):
        i = pl.program_id(0)
        _load_weights_once(w_hbm, (("wpg", wpg),), sem)

        @pl.when(i == 0)
        def _():
            dg_ref[...] = jnp.zeros_like(dg_ref)
            dwpp_ref[...] = jnp.zeros_like(dwpp_ref)
            acc[...] = jnp.zeros_like(acc)

        x = h_ref[...]
        r = lax.rsqrt(jnp.mean(x * x, axis=-1, keepdims=True) + RMS_EPS)
        xh = x * r
        gv = g_ref[...]
        n = (xh * gv).astype(BF)
        gate = _sig(_dot_nn(n, wpg[...]))
        pb = p_ref[...].astype(BF)
        pe = _dot_nt(pb, wpp_ref[...])
        d = dh_ref[...]
        dpe = (d * gate).astype(BF)
        dq = (d * pe * gate * (1.0 - gate)).astype(BF)
        dwpp_ref[...] += _dot_tn(dpe, pb)
        acc[...] += _dot_tn(n, dq)
        dn = _dot_nt(dq, wpg[...])
        dg_ref[...] += jnp.sum(dn * xh, axis=0, keepdims=True)
        dxh = dn * gv
        dho_ref[...] = d + r * (dxh - xh * jnp.mean(dxh * xh, axis=-1, keepdims=True))

        @pl.when(i == nt - 1)
        def _():
            stage[...] = acc[...].astype(BF)
            cps = _grad_copies(stage, g_out, "wpg", 0, D, D // NCHIP, osem, 0)
            for cp in cps:
                cp.start()
            for cp in cps:
                cp.wait()

    tile = pl.BlockSpec((TM, D), lambda i: (i, 0))
    vec = pl.BlockSpec((1, D), lambda i: (0, 0))
    return pl.pallas_call(
        body, name="ple_bwd", grid=(nt,),
        in_specs=[tile, tile, pl.BlockSpec((TM, PD), lambda i: (i, 0)), vec,
                  pl.BlockSpec((D, PD), lambda i: (0, 0)), ANY],
        out_specs=[tile, vec, pl.BlockSpec((D, PD), lambda i: (0, 0)), ANY],
        out_shape=[jax.ShapeDtypeStruct((T, D), F32), jax.ShapeDtypeStruct((1, D), F32),
                   jax.ShapeDtypeStruct((D, PD), F32),
                   jax.ShapeDtypeStruct((NCHIP, rtot, D), BF)],
        scratch_shapes=[pltpu.VMEM((D, D), BF), pltpu.VMEM((D, D), F32), pltpu.VMEM((D, D), BF),
                        pltpu.SemaphoreType.DMA((NCHIP,)), pltpu.SemaphoreType.DMA((NCHIP,))],
        compiler_params=_params(),
    )(h, dh, p, g, wppt, wfull)


def _ffn_bwd(h, dh, a, b, g, wfull, which):
    tm = TMB
    nt = T // tm
    names = ("wg%d" % which, "wu%d" % which, "wd%d" % which)

    def body(h_ref, dh_ref, a_ref, b_ref, g_ref, w_hbm,
             dho_ref, da_ref, db_ref, s_ref, n_ref, dg_ref, wg, wu, wd, sem):
        _load_weights_once(w_hbm, ((names[0], wg), (names[1], wu), (names[2], wd)), sem)

        @pl.when(pl.program_id(0) == 0)
        def _():
            dg_ref[...] = jnp.zeros_like(dg_ref)

        x = h_ref[...]
        r = lax.rsqrt(jnp.mean(x * x, axis=-1, keepdims=True) + RMS_EPS)
        xh = x * r
        gv = g_ref[...]
        n_ref[...] = (xh * gv).astype(BF)
        d = dh_ref[...]
        df = (0.5 * d).astype(BF)
        dn = jnp.zeros((tm, D), F32)
        for sl in _hidden_chunks():
            av = a_ref[:, sl].astype(F32)
            bv = b_ref[:, sl].astype(F32)
            ds = _dot_nt(df, wd[sl, :])
            sg = _sig(av)
            sil = av * sg
            s_ref[:, sl] = (sil * bv).astype(BF)
            da = (ds * bv * (sg * (1.0 + av * (1.0 - sg)))).astype(BF)
            db = (ds * sil).astype(BF)
            da_ref[:, sl] = da
            db_ref[:, sl] = db
            dn = dn + _dot_nn(da, wg[sl, :]) + _dot_nn(db, wu[sl, :])
        dg_ref[...] += jnp.sum(dn * xh, axis=0, keepdims=True)
        dxh = dn * gv
        dho_ref[...] = d + r * (dxh - xh * jnp.mean(dxh * xh, axis=-1, keepdims=True))

    tile = lambda w: pl.BlockSpec((tm, w), lambda i: (i, 0))
    vec = pl.BlockSpec((1, D), lambda i: (0, 0))
    return pl.pallas_call(
        body, name="ffn_bwd", grid=(nt,),
        in_specs=[tile(D), tile(D), tile(F), tile(F), vec, ANY],
        out_specs=[tile(D), tile(F), tile(F), tile(F), tile(D), vec],
        out_shape=[jax.ShapeDtypeStruct((T, D), F32), jax.ShapeDtypeStruct((T, F), BF),
                   jax.ShapeDtypeStruct((T, F), BF), jax.ShapeDtypeStruct((T, F), BF),
                   jax.ShapeDtypeStruct((T, D), BF), jax.ShapeDtypeStruct((1, D), F32)],
        scratch_shapes=[pltpu.VMEM((F, D), BF), pltpu.VMEM((F, D), BF), pltpu.VMEM((F, D), BF),
                        pltpu.SemaphoreType.DMA((3 * NCHIP,))],
        compiler_params=_params(),
    )(h, dh, a, b, g, wfull)


def _wgrad(xs, y, gbuf, names, row0, rb, piece, yscale=None):
    nx = len(xs)
    rx = xs[0].shape[1]
    nj = rx // rb
    nt = T // TMW
    npiece = rb // piece

    def body(*refs):
        x_refs = refs[:nx]
        y_ref = refs[nx]
        g_out = refs[nx + 2]
        accs = refs[nx + 3:2 * nx + 3]
        stages = refs[2 * nx + 3:3 * nx + 3]
        osem = refs[3 * nx + 3]
        j = pl.program_id(0)
        t = pl.program_id(1)

        @pl.when(t == 0)
        def _():
            for acc in accs:
                acc[...] = jnp.zeros_like(acc)

        yv = y_ref[...]
        if yscale is not None:
            yv = (yscale * yv).astype(BF)
        for x_ref, acc in zip(x_refs, accs):
            acc[...] += _dot_tn(x_ref[...], yv)

        @pl.when(t == nt - 1)
        def _():
            cps = []
            for xi in range(nx):
                stages[xi][...] = accs[xi][...].astype(BF)
                cps += _grad_copies(stages[xi], g_out, names[xi], row0 + j * rb, rb, piece,
                                    osem, xi * npiece)
            for cp in cps:
                cp.start()
            for cp in cps:
                cp.wait()

    in_specs = [pl.BlockSpec((TMW, rb), lambda j, t: (t, j)) for _ in xs]
    in_specs += [pl.BlockSpec((TMW, D), lambda j, t: (t, 0)), ANY]
    return pl.pallas_call(
        body, name="wgrad", grid=(nj, nt), in_specs=in_specs, out_specs=ANY,
        out_shape=jax.ShapeDtypeStruct(gbuf.shape, gbuf.dtype),
        scratch_shapes=([pltpu.VMEM((rb, D), F32) for _ in xs] + [pltpu.VMEM((rb, D), BF) for _ in xs]
                        + [pltpu.SemaphoreType.DMA((nx * npiece,))]),
        input_output_aliases={nx + 1: 0},
        compiler_params=_params(("arbitrary", "arbitrary")),
    )(*xs, y, gbuf)


def _mix_b1(dh, z, pooled, c1, cc, pwcat, pscale, lg, lb, wfull, gbuf):
    nt = T // TM
    pg = D // NG

    def body(dh_ref, gp_ref, gc_ref, p_ref, c1_ref, cc_ref, pw_ref, ps_ref, lg_ref, lb_ref, w_hbm, _g_in,
             dp_ref, dc1_ref, dzb_ref, small_ref, dpw_ref, g_out,
             wco, wo, acc_o, acc_co, stage_o, stage_co, qbuf, sem, osem):
        i = pl.program_id(0)
        _load_weights_once(w_hbm, (("wco", wco), ("wo", wo)), sem)

        @pl.when(i == 0)
        def _():
            small_ref[...] = jnp.zeros_like(small_ref)
            dpw_ref[...] = jnp.zeros_like(dpw_ref)
            acc_o[...] = jnp.zeros_like(acc_o)
            acc_co[...] = jnp.zeros_like(acc_co)

        dhb = dh_ref[...].astype(BF)
        dm = _dot_nt(dhb, wo[...])
        sp = _sig(gp_ref[...].astype(F32))
        sc = _sig(gc_ref[...].astype(F32))
        for g in range(NG):
            ls = pl.ds(g * pg, pg)
            qbuf[:, ls] = _dot_nn(p_ref[:, ls], pw_ref[:, ls])
        q = qbuf[...]
        psv = ps_ref[...]
        am = q * psv
        ccv = cc_ref[...].astype(F32)
        m = (sp * am + sc * ccv).astype(BF)
        acc_o[...] += _dot_tn(m, dhb)
        dam = dm * sp
        dzb_ref[:, 0:D] = (dm * am * sp * (1.0 - sp)).astype(BF)
        dccb = (dm * sc).astype(BF)
        dzb_ref[:, D:2 * D] = (dm * ccv * sc * (1.0 - sc)).astype(BF)
        small_ref[0:1, :] += jnp.sum(dam * q, axis=0, keepdims=True)
        dq = (dam * psv).astype(BF)
        for g in range(NG):
            ls = pl.ds(g * pg, pg)
            dqg = dq[:, g * pg:(g + 1) * pg]
            dp_ref[:, ls] = _dot_nt(dqg, pw_ref[:, ls]).astype(BF)
            dpw_ref[:, ls] += _dot_tn(p_ref[:, ls], dqg)
        c1v = c1_ref[...].astype(F32)
        mu = jnp.mean(c1v, axis=-1, keepdims=True)
        xc = c1v - mu
        var = jnp.mean(xc * xc, axis=-1, keepdims=True)
        rs = lax.rsqrt(var + LN_EPS)
        c2n = xc * rs
        lgv = lg_ref[...]
        c2 = c2n * lgv + lb_ref[...]
        sg2 = _sig(c2)
        c3 = (c2 * sg2).astype(BF)
        acc_co[...] += _dot_tn(c3, dccb)
        dc3 = _dot_nt(dccb, wco[...])
        dc2 = dc3 * (sg2 * (1.0 + c2 * (1.0 - sg2)))
        small_ref[2:3, :] += jnp.sum(dc2 * c2n, axis=0, keepdims=True)
        small_ref[3:4, :] += jnp.sum(dc2, axis=0, keepdims=True)
        dc2n = dc2 * lgv
        dc1 = rs * (dc2n - jnp.mean(dc2n, axis=-1, keepdims=True)
                    - c2n * jnp.mean(dc2n * c2n, axis=-1, keepdims=True))
        small_ref[1:2, :] += jnp.sum(dc1, axis=0, keepdims=True)
        dc1_ref[...] = dc1.astype(BF)

        @pl.when(i == nt - 1)
        def _():
            stage_o[...] = acc_o[...].astype(BF)
            stage_co[...] = acc_co[...].astype(BF)
            cps = _grad_copies(stage_o, g_out, "wo", 0, D, D // NCHIP, osem, 0)
            cps += _grad_copies(stage_co, g_out, "wco", 0, D, D // NCHIP, osem, NCHIP)
            for cp in cps:
                cp.start()
            for cp in cps:
                cp.wait()

    tile = pl.BlockSpec((TM, D), lambda i: (i, 0))
    vec = pl.BlockSpec((1, D), lambda i: (0, 0))
    full = lambda r: pl.BlockSpec((r, D), lambda i: (0, 0))
    return pl.pallas_call(
        body, name="mix_b1", grid=(nt,),
        in_specs=[tile, pl.BlockSpec((TM, D), lambda i: (i, 3)), pl.BlockSpec((TM, D), lambda i: (i, 4)),
                  tile, tile, tile, full(pg), vec, vec, vec, ANY, ANY],
        out_specs=[tile, tile, pl.BlockSpec((TM, 2 * D), lambda i: (i, 0)), full(8), full(pg), ANY],
        out_shape=[jax.ShapeDtypeStruct((T, D), BF), jax.ShapeDtypeStruct((T, D), BF),
                   jax.ShapeDtypeStruct((T, 2 * D), BF), jax.ShapeDtypeStruct((8, D), F32),
                   jax.ShapeDtypeStruct((pg, D), F32), jax.ShapeDtypeStruct(gbuf.shape, gbuf.dtype)],
        scratch_shapes=[pltpu.VMEM((D, D), BF), pltpu.VMEM((D, D), BF),
                        pltpu.VMEM((D, D), F32), pltpu.VMEM((D, D), F32),
                        pltpu.VMEM((D, D), BF), pltpu.VMEM((D, D), BF),
                        pltpu.VMEM((TM, D), F32),
                        pltpu.SemaphoreType.DMA((2 * NCHIP,)), pltpu.SemaphoreType.DMA((2 * NCHIP,))],
        input_output_aliases={11: 5},
        compiler_params=_params(),
    )(dh, z, z, pooled, c1, cc, pwcat, pscale, lg, lb, wfull, gbuf)


def _mix_b2(dp, dc1, z, wdw):
    nt = T // TM
    pg = D // NG
    hb = TM // HALO
    nhb = T // HALO
    n_ext = TM + HALO
    pad = SUBLANES

    def body(dp_ref, dpn_ref, dc_ref, dcn_ref, za_ref, zg_ref, wdw_ref,
             dza_ref, dw_ref, pa, pb, cbuf, sh, c0buf, dc0buf):
        i = pl.program_id(0)

        @pl.when(i == 0)
        def _():
            dw_ref[...] = jnp.zeros_like(dw_ref)
            pa[n_ext:n_ext + pad, :] = jnp.zeros((pad, D), F32)
            pb[n_ext:n_ext + pad, :] = jnp.zeros((pad, D), F32)

        more = (i < nt - 1).astype(F32)
        for g, w in enumerate(WINDOWS):
            ls = pl.ds(g * pg, pg)
            cur_dp = dp_ref[:, ls].astype(F32)
            pa[0:TM, ls] = cur_dp / _pool_counts(i, TM, w)
            pa[TM:n_ext, ls] = dpn_ref[:, ls].astype(F32) * (more / w)
            cur, nxt = pa, pb
            d = 1
            while d < w:
                nxt[pl.ds(0, n_ext), ls] = cur[pl.ds(0, n_ext), ls] + cur[pl.ds(d, n_ext), ls]
                cur, nxt = nxt, cur
                d *= 2
            dza_ref[:, ls] = (cur[pl.ds(0, TM), ls] - cur_dp).astype(BF)
        za = za_ref[...].astype(F32)
        sg = _sig(zg_ref[...].astype(F32))
        cbuf[0:TM, :] = dc_ref[...].astype(F32)
        cbuf[TM:n_ext, :] = dcn_ref[...].astype(F32) * more
        c0buf[...] = za * sg
        for l in range(D // LANES):
            lanes = pl.ds(l * LANES, LANES)
            _fill_shifted(cbuf, sh, lanes)

            def conv_rows(r, accs):
                base = r * (CHUNKS * SUBLANES)
                rows = [pl.ds(pl.multiple_of(base + j * SUBLANES, SUBLANES), SUBLANES) for j in range(CHUNKS)]
                c0v = [c0buf[rows[j], lanes] for j in range(CHUNKS)]
                acc = [jnp.zeros((SUBLANES, LANES), F32)] * CHUNKS
                new = list(accs)
                for k in range(KC):
                    wk = jnp.broadcast_to(wdw_ref[k:k + 1, lanes], (SUBLANES, LANES))
                    for j in range(CHUNKS):
                        src = _shifted_source(cbuf, sh, base + j * SUBLANES, KC - 1 - k, lanes)
                        acc[j] = acc[j] + wk * src
                        new[k] = new[k] + c0v[j] * src
                for j in range(CHUNKS):
                    dc0buf[rows[j], lanes] = acc[j]
                return tuple(new)

            init = tuple(jnp.zeros((SUBLANES, LANES), F32) for _ in range(KC))
            accs = lax.fori_loop(0, TM // (CHUNKS * SUBLANES), conv_rows, init)
            for k in range(KC):
                dw_ref[k:k + 1, lanes] += jnp.sum(accs[k], axis=0, keepdims=True)
        dc0 = dc0buf[...]
        dza_ref[:, D:2 * D] = (dc0 * sg).astype(BF)
        dza_ref[:, 2 * D:3 * D] = (dc0 * za * sg * (1.0 - sg)).astype(BF)

    tile = pl.BlockSpec((TM, D), lambda i: (i, 0))
    nxt_spec = pl.BlockSpec((HALO, D), lambda i: (jnp.minimum((i + 1) * hb, nhb - 1), 0))
    return pl.pallas_call(
        body, name="mix_b2", grid=(nt,),
        in_specs=[tile, nxt_spec, tile, nxt_spec, pl.BlockSpec((TM, D), lambda i: (i, 1)),
                  pl.BlockSpec((TM, D), lambda i: (i, 2)), pl.BlockSpec((HALO, D), lambda i: (0, 0))],
        out_specs=[pl.BlockSpec((TM, 3 * D), lambda i: (i, 0)), pl.BlockSpec((HALO, D), lambda i: (0, 0))],
        out_shape=[jax.ShapeDtypeStruct((T, 3 * D), BF), jax.ShapeDtypeStruct((HALO, D), F32)],
        scratch_shapes=[pltpu.VMEM((n_ext + pad, D), F32), pltpu.VMEM((n_ext + pad, D), F32),
                        pltpu.VMEM((n_ext, D), F32), pltpu.VMEM((SUBLANES - 1, n_ext - SUBLANES, LANES), F32),
                        pltpu.VMEM((TM, D), F32), pltpu.VMEM((TM, D), F32)],
        compiler_params=_params(),
    )(dp, dp, dc1, dc1, z, z, wdw)


def _mix_b3(h, dh, dza, dzb, g, wfull):
    nt = T // TM

    def body(h_ref, dh_ref, dza_ref, dzb_ref, g_ref, w_hbm, dho_ref, u_ref, dg_ref, win, sem):
        _load_weights_once(w_hbm, (("win", win),), sem)

        @pl.when(pl.program_id(0) == 0)
        def _():
            dg_ref[...] = jnp.zeros_like(dg_ref)

        x = h_ref[...]
        r = lax.rsqrt(jnp.mean(x * x, axis=-1, keepdims=True) + RMS_EPS)
        xh = x * r
        gv = g_ref[...]
        u_ref[...] = (xh * gv).astype(BF)
        du = _dot_nn(dza_ref[...], win[0:3 * D, :]) + _dot_nn(dzb_ref[...], win[3 * D:5 * D, :])
        dg_ref[...] += jnp.sum(du * xh, axis=0, keepdims=True)
        dxh = du * gv
        dho_ref[...] = dh_ref[...] + r * (dxh - xh * jnp.mean(dxh * xh, axis=-1, keepdims=True))

    tile = lambda w: pl.BlockSpec((TM, w), lambda i: (i, 0))
    vec = pl.BlockSpec((1, D), lambda i: (0, 0))
    return pl.pallas_call(
        body, name="mix_b3", grid=(nt,),
        in_specs=[tile(D), tile(D), tile(3 * D), tile(2 * D), vec, ANY],
        out_specs=[tile(D), tile(D), vec],
        out_shape=[jax.ShapeDtypeStruct((T, D), F32), jax.ShapeDtypeStruct((T, D), BF),
                   jax.ShapeDtypeStruct((1, D), F32)],
        scratch_shapes=[pltpu.VMEM((5 * D, D), BF), pltpu.SemaphoreType.DMA((NCHIP,))],
        compiler_params=_params(),
    )(h, dh, dza, dzb, g, wfull)


def _mesh_pos():
    x, y, c = lax.axis_index("x"), lax.axis_index("y"), lax.axis_index("c")
    chips = [(1 - x, y), (x, 1 - y), (1 - x, 1 - y)]
    return x, y, c, 2 * x + y, chips


def _handshake(peers):
    barrier = pltpu.get_barrier_semaphore()
    for peer in peers:
        pl.semaphore_signal(barrier, inc=1, device_id=peer, device_id_type=MESH)
    pl.semaphore_wait(barrier, len(peers))


def _run_comm(body, name, cid, ins, inouts, out_types, scratch):
    in_refs = [jax.new_ref(a, memory_space=HBM) for a in ins]
    inout_refs = [jax.new_ref(a, memory_space=HBM) for a in inouts]
    out_refs = [jax.empty_ref(t, memory_space=HBM) for t in out_types]

    @pl.kernel(mesh=plsc.ScalarSubcoreMesh(axis_name="seq", num_cores=1), name=name,
               scratch_types=scratch, compiler_params=pltpu.CompilerParams(collective_id=cid))
    def launch(*scr):
        body(*in_refs, *inout_refs, *out_refs, *scr)

    launch()
    return [r[...] for r in inout_refs + out_refs]


def _remote(src, dst, send_sem, recv_sem, to):
    return pltpu.make_async_remote_copy(src_ref=src, dst_ref=dst, send_sem=send_sem, recv_sem=recv_sem,
                                        device_id=to, device_id_type=MESH)


def _gather_layer(packed, cid):
    rtot = packed.shape[0]
    half = rtot // 2
    kme = 2 * lax.axis_index("x") + lax.axis_index("y")
    landing = lax.dynamic_update_slice(lax.empty((NCHIP, rtot, D), BF), packed[None], (kme, 0, 0))

    def body(p_ref, w_ref, send_sems, recv_sems):
        x, y, c, kme, chips = _mesh_pos()
        sib = (x, y, 1 - c)
        _handshake([(*ch, c) for ch in chips] + [sib])
        ks = [2 * cx + cy for cx, cy in chips]
        mine = pl.ds(pl.multiple_of(c * half, 16), half)
        other = pl.ds(pl.multiple_of((1 - c) * half, 16), half)
        first = [_remote(p_ref.at[mine], w_ref.at[kme, mine], send_sems.at[j], recv_sems.at[j], (*chips[j], c))
                 for j in range(3)]
        for cp in first:
            cp.start()
        passed = [_remote(w_ref.at[ks[j], mine], w_ref.at[ks[j], mine], send_sems.at[3 + j], recv_sems.at[3 + j], sib)
                  for j in range(3)]
        for j in range(3):
            _remote(p_ref.at[mine], w_ref.at[ks[j], mine], send_sems.at[j], recv_sems.at[j], sib).wait_recv()
            passed[j].start()
        for j in range(3):
            _remote(p_ref.at[mine], w_ref.at[ks[j], other], send_sems.at[3 + j], recv_sems.at[3 + j], sib).wait_recv()
        for cp in first + passed:
            cp.wait_send()

    return _run_comm(body, "gather_layer_%d" % cid, cid, [packed], [landing], [],
                     (pltpu.SemaphoreType.DMA((6,)), pltpu.SemaphoreType.DMA((6,))))[0]


def _sibling_swap(gbuf, cid):
    rtot = gbuf.shape[1]
    half = rtot // 2

    def body(g_ref, r_ref, send_sem, recv_sem):
        x, y, c, _, _ = _mesh_pos()
        sib = (x, y, 1 - c)
        _handshake([sib])
        other = pl.ds(pl.multiple_of((1 - c) * half, 16), half)
        cp = _remote(g_ref.at[:, other, :], r_ref, send_sem, recv_sem, sib)
        cp.start()
        cp.wait()

    return _run_comm(body, "sibling_swap_%d" % cid, cid, [gbuf], [],
                     [jax.ShapeDtypeStruct((NCHIP, half, D), BF)],
                     (pltpu.SemaphoreType.DMA, pltpu.SemaphoreType.DMA))[0]


def _sibling_swap_inline(gbuf):
    rtot = gbuf.shape[1]
    half = rtot // 2

    def body(g_ref, r_ref, send_sem, recv_sem):
        x, y, c, _, _ = _mesh_pos()
        other = pl.ds(pl.multiple_of((1 - c) * half, 16), half)
        cp = _remote(g_ref.at[:, other, :], r_ref, send_sem, recv_sem, (x, y, 1 - c))
        cp.start()
        cp.wait()

    return pl.pallas_call(
        body, name="sibling_swap_inline", in_specs=[ANY], out_specs=ANY,
        out_shape=jax.ShapeDtypeStruct((NCHIP, half, D), BF),
        scratch_shapes=[pltpu.SemaphoreType.DMA, pltpu.SemaphoreType.DMA],
    )(gbuf)


def _row_tile(rows):
    for cand in range(min(rows, 1280) // 16 * 16, 0, -16):
        if rows % cand == 0:
            return cand
    return rows


def _chip_sum(gbuf, rbuf, cidx):
    half = rbuf.shape[1]
    rt = _row_tile(half)
    nb = half // rt

    def body(c_ref, g_ref, r_ref, o_ref):
        o_ref[...] = (g_ref[...].astype(F32) + r_ref[...].astype(F32)).astype(BF)

    return pl.pallas_call(
        body, name="chip_sum",
        grid_spec=pltpu.PrefetchScalarGridSpec(
            num_scalar_prefetch=1, grid=(NCHIP, nb),
            in_specs=[pl.BlockSpec((None, rt, D), lambda k, r, c: (k, c[0] * nb + r, 0)),
                      pl.BlockSpec((None, rt, D), lambda k, r, c: (k, r, 0))],
            out_specs=pl.BlockSpec((None, rt, D), lambda k, r, c: (k, r, 0))),
        out_shape=jax.ShapeDtypeStruct((NCHIP, half, D), BF),
        compiler_params=_params(("arbitrary", "arbitrary")),
    )(cidx, gbuf, rbuf)


def _chip_exchange(sbuf, cid):
    half = sbuf.shape[1]

    def body(s_ref, x_ref, send_sems, recv_sems):
        x, y, c, _, chips = _mesh_pos()
        _handshake([(*ch, c) for ch in chips])
        cps = [_remote(s_ref.at[2 * cx + cy], x_ref.at[j], send_sems.at[j], recv_sems.at[j], (cx, cy, c))
               for j, (cx, cy) in enumerate(chips)]
        for cp in cps:
            cp.start()
        for cp in cps:
            cp.wait()

    return _run_comm(body, "chip_exchange_%d" % cid, cid, [sbuf], [],
                     [jax.ShapeDtypeStruct((3, half, D), BF)],
                     (pltpu.SemaphoreType.DMA((3,)), pltpu.SemaphoreType.DMA((3,))))[0]


def _shard_sum(gbuf, rbuf, xbuf, ck):
    half = rbuf.shape[1]
    rt = _row_tile(half)
    nb = half // rt

    def body(ck_ref, g_ref, r_ref, x_ref, o_ref):
        acc = g_ref[...].astype(F32) + r_ref[...].astype(F32)
        for j in range(3):
            acc = acc + x_ref[j].astype(F32)
        o_ref[...] = acc

    return pl.pallas_call(
        body, name="shard_sum",
        grid_spec=pltpu.PrefetchScalarGridSpec(
            num_scalar_prefetch=1, grid=(nb,),
            in_specs=[pl.BlockSpec((None, rt, D), lambda r, ck: (ck[1], ck[0] * nb + r, 0)),
                      pl.BlockSpec((None, rt, D), lambda r, ck: (ck[1], r, 0)),
                      pl.BlockSpec((3, rt, D), lambda r, ck: (0, r, 0))],
            out_specs=pl.BlockSpec((rt, D), lambda r, ck: (ck[0] * nb + r, 0))),
        out_shape=jax.ShapeDtypeStruct((2 * half, D), F32),
        compiler_params=_params(),
    )(ck, gbuf, rbuf, xbuf)


def _sibling_share(red, cid):
    half = red.shape[0] // 2

    def body(o_ref, send_sem, recv_sem):
        x, y, c, _, _ = _mesh_pos()
        sib = (x, y, 1 - c)
        _handshake([sib])
        mine = pl.ds(pl.multiple_of(c * half, 8), half)
        other = pl.ds(pl.multiple_of((1 - c) * half, 8), half)
        cp = _remote(o_ref.at[mine], o_ref.at[mine], send_sem, recv_sem, sib)
        cp.start()
        cp.wait_send()
        _remote(o_ref.at[mine], o_ref.at[other], send_sem, recv_sem, sib).wait_recv()

    return _run_comm(body, "sibling_share_%d" % cid, cid, [], [red], [],
                     (pltpu.SemaphoreType.DMA, pltpu.SemaphoreType.DMA))[0]


def _allreduce_small(v):
    rows = v.shape[0]
    ndev = 2 * NCHIP

    def body(v_ref, o_ref, gat, send_sems, recv_sems, lsem):
        x, y, c, _, chips = _mesh_pos()
        me, sib = (x, y, c), (x, y, 1 - c)

        def blk(px, py, pc):
            return gat.at[pl.ds((4 * px + 2 * py + pc) * rows, rows), :]

        def copy(k, block, to, src=None):
            return pltpu.make_async_remote_copy(
                src_ref=blk(*block) if src is None else src, dst_ref=blk(*block),
                send_sem=send_sems.at[k], recv_sem=recv_sems.at[k], device_id=to, device_id_type=MESH)

        mine = pltpu.make_async_copy(v_ref, blk(*me), lsem)
        mine.start()
        first = [copy(0, me, sib, src=v_ref)]
        first += [copy(1 + j, me, (*chip, c), src=v_ref) for j, chip in enumerate(chips)]
        for cp in first:
            cp.start()
        passed = [copy(4 + j, (*chip, c), sib) for j, chip in enumerate(chips)]
        for j, chip in enumerate(chips):
            copy(1 + j, (*chip, c), me).wait_recv()
            passed[j].start()
        copy(0, sib, me).wait_recv()
        for j, chip in enumerate(chips):
            copy(4 + j, (*chip, 1 - c), me).wait_recv()
        for cp in first + passed:
            cp.wait_send()
        mine.wait()
        acc = gat[0:rows, :]
        for d in range(1, ndev):
            acc = acc + gat[d * rows:(d + 1) * rows, :]
        o_ref[...] = acc

    vm = pl.BlockSpec(memory_space=pltpu.VMEM)
    return pl.pallas_call(
        body, name="allreduce_small", in_specs=[vm], out_specs=vm,
        out_shape=jax.ShapeDtypeStruct((rows, D), F32),
        scratch_shapes=[pltpu.VMEM((ndev * rows, D), F32), pltpu.SemaphoreType.DMA((7,)),
                        pltpu.SemaphoreType.DMA((7,)), pltpu.SemaphoreType.DMA],
    )(v)


def _adamw(w, g, m, v):
    shape = w.shape
    cols = shape[-1]
    rows = w.size // cols
    bm = rows
    for cand in range(512, 0, -SUBLANES):
        if rows % cand == 0:
            bm = cand
            break
    bc1 = 1.0 - ADAM_B1 ** ADAM_STEP
    bc2 = 1.0 - ADAM_B2 ** ADAM_STEP

    def body(w_ref, g_ref, m_ref, v_ref, d_ref, mo_ref, vo_ref):
        gv = g_ref[...]
        mn = ADAM_B1 * m_ref[...] + (1.0 - ADAM_B1) * gv
        vn = ADAM_B2 * v_ref[...] + (1.0 - ADAM_B2) * (gv * gv)
        mo_ref[...] = mn
        vo_ref[...] = vn
        d_ref[...] = -ADAM_LR * ((mn / bc1) / (jnp.sqrt(vn / bc2) + ADAM_EPS) + ADAM_WD * w_ref[...])

    spec = pl.BlockSpec((bm, cols), lambda i: (i, 0))
    out = jax.ShapeDtypeStruct((rows, cols), F32)
    d, mo, vo = pl.pallas_call(
        body, name="adamw", grid=(rows // bm,), in_specs=[spec] * 4, out_specs=[spec] * 3,
        out_shape=[out, out, out], compiler_params=_params(),
    )(*[t.reshape(rows, cols) for t in (w, g, m, v)])
    return d.reshape(shape), mo.reshape(shape), vo.reshape(shape)


def _adamw_layer(w, g, m, v, li, prev):
    cols = w.shape[-1]
    rows = w[0].size // cols
    bm = rows
    for cand in range(512, 0, -SUBLANES):
        if rows % cand == 0:
            bm = cand
            break
    bc1 = 1.0 - ADAM_B1 ** ADAM_STEP
    bc2 = 1.0 - ADAM_B2 ** ADAM_STEP

    def body(*refs):
        w_ref, g_ref, m_ref, v_ref = refs[:4]
        go_ref, d_ref, mo_ref, vo_ref = refs[-4:]
        gv = g_ref[...]
        mn = ADAM_B1 * m_ref[...] + (1.0 - ADAM_B1) * gv
        vn = ADAM_B2 * v_ref[...] + (1.0 - ADAM_B2) * (gv * gv)
        go_ref[...] = gv
        mo_ref[...] = mn
        vo_ref[...] = vn
        d_ref[...] = -ADAM_LR * ((mn / bc1) / (jnp.sqrt(vn / bc2) + ADAM_EPS) + ADAM_WD * w_ref[...])

    layer = pl.BlockSpec((None, bm, cols), lambda i: (li, i, 0))
    in_specs = [layer, pl.BlockSpec((bm, cols), lambda i: (i, 0)), layer, layer]
    args = [w.reshape(DEPTH, rows, cols), g.reshape(rows, cols), m.reshape(DEPTH, rows, cols),
            v.reshape(DEPTH, rows, cols)]
    aliases = {}
    if prev is not None:
        in_specs += [ANY] * 4
        args += list(prev)
        aliases = {4 + i: i for i in range(4)}
    out = jax.ShapeDtypeStruct((DEPTH, rows, cols), F32)
    return pl.pallas_call(
        body, name="adamw_layer", grid=(rows // bm,), in_specs=in_specs, out_specs=[layer] * 4,
        out_shape=[out] * 4, input_output_aliases=aliases, compiler_params=_params(),
    )(*args)


def _pack_shards(ws, li):
    pg = D // NG
    t = lambda a: jnp.swapaxes(a[li], 0, 1)
    parts = [t(ws["ffn1_w_gate"]), t(ws["ffn1_w_up"]), ws["ffn1_w_down"][li],
             t(ws["w_in"]), ws["conv_w_out"][li], ws["w_out"][li],
             t(ws["ffn2_w_gate"]), t(ws["ffn2_w_up"]), ws["ffn2_w_down"][li], ws["ple_w_gate"][li],
             t(ws["ple_w_proj"]).reshape(-1, D),
             jnp.swapaxes(ws["pool_w"][li], 0, 1).reshape(pg // NCHIP, D)]
    return jnp.concatenate([p.astype(BF) for p in parts], axis=0)


def _unpack_shards(parts):
    lay, _ = _layout()
    pg = D // NG

    def rows(n):
        off, rs = lay[n]
        for r0, arr in parts:
            if r0 <= off and off + rs <= r0 + arr.shape[0]:
                return arr[off - r0:off - r0 + rs, :]
        raise ValueError(n)

    t = lambda a: jnp.swapaxes(a, 0, 1)
    return {
        "ffn1_w_gate": rows("wg1"), "ffn1_w_up": rows("wu1"), "ffn1_w_down": rows("wd1"),
        "ffn2_w_gate": rows("wg2"), "ffn2_w_up": rows("wu2"), "ffn2_w_down": rows("wd2"),
        "w_in": t(rows("win")), "conv_w_out": rows("wco"), "w_out": rows("wo"), "ple_w_gate": rows("wpg"),
        "ple_w_proj": t(rows("wpp").reshape(D // NCHIP, PD)),
        "pool_w": jnp.swapaxes(rows("pw").reshape(pg // NCHIP, NG, pg), 0, 1),
    }


_BIG = ("ffn1_w_gate", "ffn1_w_up", "ffn1_w_down", "w_in", "pool_w", "conv_w_out", "w_out",
        "ffn2_w_gate", "ffn2_w_up", "ffn2_w_down", "ple_w_gate", "ple_w_proj")
_TRANSPOSED = ("ffn1_w_gate", "ffn1_w_up", "ffn2_w_gate", "ffn2_w_up")
_VECS = ("ffn1_norm", "mix_norm", "pool_scale", "conv_dw_b", "conv_ln_g", "conv_ln_b", "ffn2_norm", "ple_norm")
_WEIGHTS = ("ffn1_norm", "ffn1_w_gate", "ffn1_w_up", "ffn1_w_down", "mix_norm", "w_in", "pool_w", "pool_scale",
            "conv_dw_w", "conv_dw_b", "conv_ln_g", "conv_ln_b", "conv_w_out", "w_out", "ffn2_norm",
            "ffn2_w_gate", "ffn2_w_up", "ffn2_w_down", "ple_norm", "ple_w_gate", "ple_w_proj", "final_norm")


def _step(x, p, tgt, ws, ms, vs):
    lay, rtot = _layout()
    pg = D // NG
    cpos = lax.axis_index("c")
    kme = 2 * lax.axis_index("x") + lax.axis_index("y")
    cidx = jnp.stack([cpos]).astype(jnp.int32)
    ck = jnp.stack([cpos, kme]).astype(jnp.int32)
    h = x.reshape(T, D)
    tgt = tgt.reshape(T, D)

    nfirst = _first_rows()
    packed = [_pack_shards(ws, li) for li in range(DEPTH)]
    wfirst, wrest = [None] * DEPTH, [None] * DEPTH
    wfirst[0] = _gather_layer(packed[0][:nfirst], 0)
    piece, wfirst[0] = lax.optimization_barrier((packed[0][nfirst:], wfirst[0]))
    wrest[0] = _gather_layer(piece, 1)
    wppt, pwcat = [None] * DEPTH, [None] * DEPTH
    kk = ws["conv_dw_w"].shape[1]
    wdw_mine = jnp.zeros((DEPTH * HALO, D), F32)
    for li in range(DEPTH):
        blockw = jnp.zeros((kk, D), F32)
        mine = jnp.where(cpos == 0, ws["conv_dw_w"][li], 0.0)
        blockw = lax.dynamic_update_slice(blockw, mine, (0, kme * (D // NCHIP)))
        wdw_mine = wdw_mine.at[li * HALO:li * HALO + kk, :].set(blockw)
    wdw_all = _allreduce_small(wdw_mine)
    wdw = [wdw_all[li * HALO:(li + 1) * HALO, :] for li in range(DEPTH)]
    vec = lambda name, li: _row(ws[name][li])

    saved = []
    for li in range(DEPTH):
        if li > 0:
            wfirst[li], wrest[li], h = lax.optimization_barrier((wfirst[li], wrest[li], h))
        h0 = h
        h1, a1, b1 = _ffn_fwd(h0, vec("ffn1_norm", li), wfirst[li], 1)
        if li + 1 < DEPTH:
            piece, wrest[li], h1 = lax.optimization_barrier((packed[li + 1][:nfirst], wrest[li], h1))
            wfirst[li + 1] = _gather_layer(piece, 2 * li + 2)
        else:
            wrest[li], h1 = lax.optimization_barrier((wrest[li], h1))
        o, s = lay["wpp"]
        wppt[li] = wrest[li][:, o - nfirst:o - nfirst + s, :].reshape(D, PD)
        o, s = lay["pw"]
        pwcat[li] = wrest[li][:, o - nfirst:o - nfirst + s, :].reshape(pg, D)
        z = _mix_in_fwd(h1, vec("mix_norm", li), wrest[li])
        h2, pooled, c1, cc = _mix_mid_fwd(h1, z, pwcat[li], vec("pool_scale", li), wdw[li],
                                          vec("conv_dw_b", li), vec("conv_ln_g", li), vec("conv_ln_b", li),
                                          wrest[li])
        if li + 1 < DEPTH:
            piece, wfirst[li + 1], h2 = lax.optimization_barrier((packed[li + 1][nfirst:], wfirst[li + 1], h2))
            wrest[li + 1] = _gather_layer(piece, 2 * li + 3)
        h3, a2, b2 = _ffn_fwd(h2, vec("ffn2_norm", li), wrest[li], 2)
        h = _ple_fwd(h3, p[li, 0], vec("ple_norm", li), wppt[li], wrest[li])
        saved.append((h0, a1, b1, h1, z, pooled, c1, cc, h2, a2, b2, h3))

    dh, losscols, dgf = _loss_bwd(h, tgt, _row(ws["final_norm"]))
    loss = lax.psum(jnp.sum(losscols), ("x", "y", "c"))
    vecg = [dict() for _ in range(DEPTH)]
    dwdw = [None] * DEPTH
    def reduce_start(g, chain, inline=False):
        cid = 2 * DEPTH + 3 * chain
        return {"g": g, "r": _sibling_swap_inline(g) if inline else _sibling_swap(g, cid), "cid": cid}

    def reduce_mid(st, anchor):
        if anchor is not None:
            anchor, st["r"] = lax.optimization_barrier((anchor, st["r"]))
        sbuf = _chip_sum(st["g"], st["r"], cidx)
        if anchor is not None:
            anchor, sbuf = lax.optimization_barrier((anchor, sbuf))
        st["x"] = _chip_exchange(sbuf, st["cid"] + 1)
        return anchor

    def reduce_end(st, anchor):
        xb = st["x"]
        if anchor is not None:
            anchor, xb = lax.optimization_barrier((anchor, xb))
        rsum = _shard_sum(st["g"], st["r"], xb, ck)
        if anchor is not None:
            anchor, rsum = lax.optimization_barrier((anchor, rsum))
        return anchor, _sibling_share(rsum, st["cid"] + 2)

    parts = [[] for _ in range(DEPTH)]
    above = None
    nchain = 0
    for li in reversed(range(DEPTH)):
        h0, a1, b1, h1, z, pooled, c1, cc, h2, a2, b2, h3 = saved[li]
        w = wrest[li]
        dh, dgp, dwpp, gbuf = _ple_bwd(h3, dh, p[li, 0], vec("ple_norm", li), wppt[li], w)
        vecg[li]["ple_norm"] = dgp
        if above is not None:
            dh = reduce_mid(above[1], dh)
        dh_in, da, db, sact, n, dg = _ffn_bwd(h2, dh, a2, b2, vec("ffn2_norm", li), w, 2)
        vecg[li]["ffn2_norm"] = dg
        gbuf = _wgrad([da, db], n, gbuf, ("wg2", "wu2"), 0, F // 2, F // NCHIP)
        gbuf = _wgrad([sact], dh, gbuf, ("wd2",), 0, F // 2, F // NCHIP, yscale=0.5)
        dh = dh_in
        dp, dc1, dzb, small, dpw, gbuf = _mix_b1(dh, z, pooled, c1, cc, pwcat[li], vec("pool_scale", li),
                                                 vec("conv_ln_g", li), vec("conv_ln_b", li), w, gbuf)
        vecg[li]["pool_scale"] = small[0:1]
        vecg[li]["conv_dw_b"] = small[1:2]
        vecg[li]["conv_ln_g"] = small[2:3]
        vecg[li]["conv_ln_b"] = small[3:4]
        dza, dwdw[li] = _mix_b2(dp, dc1, z, wdw[li])
        if above is not None:
            dza, rsum = reduce_end(above[1], dza)
            parts[above[0]].append((0, rsum))
            above = None
        dh_in, u, dg = _mix_b3(h1, dh, dza, dzb, vec("mix_norm", li), w)
        vecg[li]["mix_norm"] = dg
        gbuf = _wgrad([dza], u, gbuf, ("win",), 0, D, D // NCHIP)
        gbuf = _wgrad([dzb], u, gbuf, ("win",), 3 * D, D, D // NCHIP)
        dh = dh_in
        o, s = lay["wpp"]
        small_rows = jnp.concatenate([dwpp.reshape(NCHIP, s, D), dpw.reshape(NCHIP, lay["pw"][1], D)], axis=1)
        gbuf = lax.dynamic_update_slice(gbuf, small_rows.astype(BF), (0, o, 0))
        if li == 0:
            rest = reduce_start(gbuf[:, nfirst:, :], nchain)
            nchain += 1
        dh_in, da, db, sact, n, dg = _ffn_bwd(h0, dh, a1, b1, vec("ffn1_norm", li), wfirst[li], 1)
        vecg[li]["ffn1_norm"] = dg
        if li == 0:
            da = reduce_mid(rest, da)
        gbuf = _wgrad([da, db], n, gbuf, ("wg1", "wu1"), 0, F // 2, F // NCHIP)
        gbuf = _wgrad([sact], dh, gbuf, ("wd1",), 0, F // 2, F // NCHIP, yscale=0.5)
        dh = dh_in
        if li == 0:
            first = reduce_start(gbuf[:, :nfirst, :], nchain, inline=True)
            nchain += 1
            reduce_mid(first, None)
        else:
            above = (li, reduce_start(gbuf, nchain))
            nchain += 1
    grad_x = dh.reshape(x.shape)
    rsum = _shard_sum(rest["g"], rest["r"], rest["x"], ck)
    first["x"], rsum = lax.optimization_barrier((first["x"], rsum))
    parts[0].append((nfirst, _sibling_share(rsum, rest["cid"] + 2)))
    updated = {n: None for n in _BIG}
    view = lambda n, a: jnp.swapaxes(a, 1, 2) if n in _TRANSPOSED else a
    for li in reversed(range(1, DEPTH)):
        un = _unpack_shards(parts[li])
        for n in _BIG:
            updated[n] = _adamw_layer(view(n, ws[n]), un[n], view(n, ms[n]), view(n, vs[n]), li, updated[n])
    first["x"], updated = lax.optimization_barrier((first["x"], updated))
    parts[0].append((0, reduce_end(first, None)[1]))
    un = _unpack_shards(parts[0])
    for n in _BIG:
        updated[n] = _adamw_layer(view(n, ws[n]), un[n], view(n, ms[n]), view(n, vs[n]), 0, updated[n])

    rows = [vecg[li][n] for li in range(DEPTH) for n in _VECS] + [dgf]
    rows.append(jnp.zeros((8 - (len(rows) % 8), D), F32))
    vsum = _allreduce_small(jnp.concatenate(rows + dwdw, axis=0))
    nvec = len(_VECS)
    grads = {}
    for i, n in enumerate(_VECS):
        grads[n] = jnp.stack([vsum[li * nvec + i] for li in range(DEPTH)])
    grads["final_norm"] = vsum[DEPTH * nvec]
    base = DEPTH * nvec + 8 - ((DEPTH * nvec + 1) % 8) + 1
    dcols = D // NCHIP
    grads["conv_dw_w"] = jnp.stack([
        lax.dynamic_slice(vsum[base + li * HALO: base + li * HALO + kk, :], (0, kme * dcols), (kk, dcols))
        for li in range(DEPTH)])

    outs_g, outs_d, outs_m, outs_v = [], [], [], []
    for n in _WEIGHTS:
        if n in _BIG:
            gq, d, mo, vo = [view(n, t.reshape(view(n, ws[n]).shape)) for t in updated[n]]
        else:
            gq = grads[n]
            d, mo, vo = _adamw(ws[n], gq, ms[n], vs[n])
        outs_g.append(gq)
        outs_d.append(d)
        outs_m.append(mo)
        outs_v.append(vo)
    return (loss, grad_x, *outs_g, *outs_d, *outs_m, *outs_v)


def kernel(x, p, ffn1_norm, ffn1_w_gate, ffn1_w_up, ffn1_w_down, mix_norm, w_in, pool_w, pool_scale, conv_dw_w, conv_dw_b, conv_ln_g, conv_ln_b, conv_w_out, w_out, ffn2_norm, ffn2_w_gate, ffn2_w_up, ffn2_w_down, ple_norm, ple_w_gate, ple_w_proj, final_norm, loss_target, m_ffn1_norm, m_ffn1_w_gate, m_ffn1_w_up, m_ffn1_w_down, m_mix_norm, m_w_in, m_pool_w, m_pool_scale, m_conv_dw_w, m_conv_dw_b, m_conv_ln_g, m_conv_ln_b, m_conv_w_out, m_w_out, m_ffn2_norm, m_ffn2_w_gate, m_ffn2_w_up, m_ffn2_w_down, m_ple_norm, m_ple_w_gate, m_ple_w_proj, m_final_norm, v_ffn1_norm, v_ffn1_w_gate, v_ffn1_w_up, v_ffn1_w_down, v_mix_norm, v_w_in, v_pool_w, v_pool_scale, v_conv_dw_w, v_conv_dw_b, v_conv_ln_g, v_conv_ln_b, v_conv_w_out, v_w_out, v_ffn2_norm, v_ffn2_w_gate, v_ffn2_w_up, v_ffn2_w_down, v_ple_norm, v_ple_w_gate, v_ple_w_proj, v_final_norm):
    ws = dict(zip(_WEIGHTS, (ffn1_norm, ffn1_w_gate, ffn1_w_up, ffn1_w_down, mix_norm, w_in, pool_w, pool_scale, conv_dw_w, conv_dw_b, conv_ln_g, conv_ln_b, conv_w_out, w_out, ffn2_norm, ffn2_w_gate, ffn2_w_up, ffn2_w_down, ple_norm, ple_w_gate, ple_w_proj, final_norm)))
    ms = dict(zip(_WEIGHTS, (m_ffn1_norm, m_ffn1_w_gate, m_ffn1_w_up, m_ffn1_w_down, m_mix_norm, m_w_in, m_pool_w, m_pool_scale, m_conv_dw_w, m_conv_dw_b, m_conv_ln_g, m_conv_ln_b, m_conv_w_out, m_w_out, m_ffn2_norm, m_ffn2_w_gate, m_ffn2_w_up, m_ffn2_w_down, m_ple_norm, m_ple_w_gate, m_ple_w_proj, m_final_norm)))
    vs = dict(zip(_WEIGHTS, (v_ffn1_norm, v_ffn1_w_gate, v_ffn1_w_up, v_ffn1_w_down, v_mix_norm, v_w_in, v_pool_w, v_pool_scale, v_conv_dw_w, v_conv_dw_b, v_conv_ln_g, v_conv_ln_b, v_conv_w_out, v_w_out, v_ffn2_norm, v_ffn2_w_gate, v_ffn2_w_up, v_ffn2_w_down, v_ple_norm, v_ple_w_gate, v_ple_w_proj, v_final_norm)))
    return _step(x, p, loss_target, ws, ms, vs)
```

```python
import jax
import jax.numpy as jnp
from jax import lax
from jax.experimental import pallas as pl
from jax.experimental.pallas import tpu as pltpu
from jax.experimental.pallas import tpu_sc as plsc

T = 8192
D = 1024
F = 2816
PD = 256
NG = 4
WINDOWS = (2, 4, 8, 16)
KC = 31
HALO = 32
DEPTH = 2
NCHIP = 4
RMS_EPS = 1e-6
LN_EPS = 1e-5

ADAM_LR = 0.001
ADAM_B1 = 0.9
ADAM_B2 = 0.999
ADAM_EPS = 1e-08
ADAM_WD = 0.01
ADAM_STEP = 10

TM = 512
TMB = 256
TMW = 1024
MXU_TILE = 256
LANES = 128
SUBLANES = 8
CHUNKS = 4
VMEM_LIMIT = 56 * 1024 * 1024

BF = jnp.bfloat16
F32 = jnp.float32
MESH = pl.DeviceIdType.MESH
ANY = pl.BlockSpec(memory_space=pl.ANY)
HBM = pltpu.MemorySpace.HBM


def _layout():
    fs, ins, ds = F // NCHIP, 5 * D // NCHIP, D // NCHIP
    pps = ds * PD // D
    pws = NG * (D // NG // NCHIP) * (D // NG) // D
    names = [("wg1", fs), ("wu1", fs), ("wd1", fs), ("win", ins), ("wco", ds), ("wo", ds),
             ("wg2", fs), ("wu2", fs), ("wd2", fs), ("wpg", ds), ("wpp", pps), ("pw", pws)]
    off, r = {}, 0
    for n, s in names:
        off[n] = (r, s)
        r += s
    return off, r


def _sig(v):
    return 0.5 * jnp.tanh(0.5 * v) + 0.5


def _dot_nn(a, b):
    return jnp.dot(a, b, preferred_element_type=F32)


def _dot_nt(a, b):
    return lax.dot_general(a, b, (((1,), (1,)), ((), ())), preferred_element_type=F32)


def _dot_tn(a, b):
    return lax.dot_general(a, b, (((0,), (0,)), ((), ())), preferred_element_type=F32)


def _params(sem=("arbitrary",)):
    return pltpu.CompilerParams(dimension_semantics=sem, vmem_limit_bytes=VMEM_LIMIT)


def _first_rows():
    return _layout()[0]["win"][0]


def _weight_copies(w_hbm, specs, sem):
    lay, _ = _layout()
    ra = _first_rows()
    cps = []
    for i, (name, dst) in enumerate(specs):
        off, rs = lay[name]
        off = off if off < ra else off - ra
        for k in range(NCHIP):
            cps.append(pltpu.make_async_copy(w_hbm.at[k, pl.ds(off, rs), :],
                                             dst.at[pl.ds(k * rs, rs), :], sem.at[i * NCHIP + k]))
    return cps


def _load_weights_once(w_hbm, specs, sem):
    @pl.when(pl.program_id(0) == 0)
    def _():
        cps = _weight_copies(w_hbm, specs, sem)
        for cp in cps:
            cp.start()
        for cp in cps:
            cp.wait()


def _grad_copies(stage, g_hbm, name, row0, rows, piece, sem, sem0):
    lay, _ = _layout()
    off, rs = lay[name]
    cps = []
    for i in range(rows // piece):
        rglob = row0 + i * piece
        k = rglob // rs
        loc = rglob - k * rs
        start = off + loc
        if not isinstance(start, int):
            start = pl.multiple_of(start, 16)
        dst = g_hbm.at[k, pl.ds(start, piece), :]
        cps.append(pltpu.make_async_copy(stage.at[pl.ds(i * piece, piece), :], dst, sem.at[sem0 + i]))
    return cps


def _row(v):
    return v.reshape(1, -1)


def _shifted_source(buf, sh, base, s, lanes):
    a, b = divmod(s, SUBLANES)
    rows = pl.ds(pl.multiple_of(base + SUBLANES * a, SUBLANES), SUBLANES)
    if b == 0:
        return buf[rows, lanes]
    return sh[b - 1, rows, :]


def _fill_shifted(buf, sh, lanes):
    rows = sh.shape[1]
    for b in range(1, SUBLANES):
        sh[b - 1, :, :] = buf[pl.ds(b, rows), lanes]


def _hidden_chunks():
    tiles = F // MXU_TILE
    first = (tiles + 1) // 2 * MXU_TILE if F % MXU_TILE == 0 else F // 2
    return [pl.ds(0, first), pl.ds(first, F - first)]


def _ffn_fwd(h, g, wfull, which):
    nt = T // TM
    names = ("wg%d" % which, "wu%d" % which, "wd%d" % which)

    def body(h_ref, g_ref, w_hbm, ho_ref, a_ref, b_ref, wg, wu, wd, sem):
        _load_weights_once(w_hbm, ((names[0], wg), (names[1], wu), (names[2], wd)), sem)
        x = h_ref[...]
        r = lax.rsqrt(jnp.mean(x * x, axis=-1, keepdims=True) + RMS_EPS)
        n = (x * r * g_ref[...]).astype(BF)
        acc = jnp.zeros((TM, D), F32)
        for sl in _hidden_chunks():
            a = _dot_nt(n, wg[sl, :])
            b = _dot_nt(n, wu[sl, :])
            a_ref[:, sl] = a.astype(BF)
            b_ref[:, sl] = b.astype(BF)
            s = (a * _sig(a) * b).astype(BF)
            acc = acc + _dot_nn(s, wd[sl, :])
        ho_ref[...] = x + 0.5 * acc

    tile = lambda w: pl.BlockSpec((TM, w), lambda i: (i, 0))
    return pl.pallas_call(
        body, name="ffn_fwd", grid=(nt,),
        in_specs=[tile(D), pl.BlockSpec((1, D), lambda i: (0, 0)), ANY],
        out_specs=[tile(D), tile(F), tile(F)],
        out_shape=[jax.ShapeDtypeStruct((T, D), F32), jax.ShapeDtypeStruct((T, F), BF),
                   jax.ShapeDtypeStruct((T, F), BF)],
        scratch_shapes=[pltpu.VMEM((F, D), BF), pltpu.VMEM((F, D), BF), pltpu.VMEM((F, D), BF),
                        pltpu.SemaphoreType.DMA((3 * NCHIP,))],
        compiler_params=_params(),
    )(h, g, wfull)


def _mix_in_fwd(h, g, wfull):
    tm = TM
    nt = T // tm
    nin = 5 * D

    def body(h_ref, g_ref, w_hbm, z_ref, win, sem):
        _load_weights_once(w_hbm, (("win", win),), sem)
        x = h_ref[...]
        r = lax.rsqrt(jnp.mean(x * x, axis=-1, keepdims=True) + RMS_EPS)
        u = (x * r * g_ref[...]).astype(BF)
        for c in range(5):
            sl = pl.ds(c * D, D)
            z_ref[:, sl] = _dot_nt(u, win[sl, :]).astype(BF)

    return pl.pallas_call(
        body, name="mix_in_fwd", grid=(nt,),
        in_specs=[pl.BlockSpec((tm, D), lambda i: (i, 0)), pl.BlockSpec((1, D), lambda i: (0, 0)), ANY],
        out_specs=pl.BlockSpec((tm, nin), lambda i: (i, 0)),
        out_shape=jax.ShapeDtypeStruct((T, nin), BF),
        scratch_shapes=[pltpu.VMEM((nin, D), BF), pltpu.SemaphoreType.DMA((NCHIP,))],
        compiler_params=_params(),
    )(h, g, wfull)


def _pool_counts(i, rows, w):
    t = i * TM + lax.broadcasted_iota(jnp.int32, (rows, 1), 0)
    return jnp.minimum(t + 1, w).astype(F32)


def _mix_mid_fwd(h, z, pwcat, pscale, wdw, bdw, lg, lb, wfull):
    nt = T // TM
    pg = D // NG
    hb = TM // HALO
    n_ext = HALO + TM
    pad = SUBLANES

    def body(h_ref, z_ref, zh_ref, pw_ref, ps_ref, wdw_ref, bdw_ref, lg_ref, lb_ref, w_hbm,
             h2_ref, p_ref, c1_ref, cc_ref, wco, wo, pa, pb, cbuf, sh, c1buf, ambuf, sem):
        i = pl.program_id(0)
        _load_weights_once(w_hbm, (("wco", wco), ("wo", wo)), sem)

        @pl.when(i == 0)
        def _():
            pa[0:pad, :] = jnp.zeros((pad, D), F32)
            pb[0:pad, :] = jnp.zeros((pad, D), F32)

        keep = (i > 0).astype(F32)
        zh = zh_ref[...].astype(F32) * keep
        za = z_ref[:, D:2 * D].astype(F32)
        zg = z_ref[:, 2 * D:3 * D].astype(F32)
        pa[pad:pad + HALO, :] = zh[:, 0:D]
        pa[pad + HALO:pad + n_ext, :] = z_ref[:, 0:D].astype(F32)
        cbuf[0:HALO, :] = zh[:, D:2 * D] * _sig(zh[:, 2 * D:3 * D])
        cbuf[HALO:n_ext, :] = za * _sig(zg)
        for g, w in enumerate(WINDOWS):
            ls = pl.ds(g * pg, pg)
            cur, nxt = pa, pb
            d = 1
            while d < w:
                nxt[pl.ds(pad, n_ext), ls] = cur[pl.ds(pad, n_ext), ls] + cur[pl.ds(pad - d, n_ext), ls]
                cur, nxt = nxt, cur
                d *= 2
            tok = z_ref[:, ls].astype(F32)
            pooled = (cur[pl.ds(pad + HALO, TM), ls] / _pool_counts(i, TM, w) - tok).astype(BF)
            p_ref[:, ls] = pooled
            ambuf[:, ls] = _dot_nn(pooled, pw_ref[:, ls])
        am = ambuf[...] * ps_ref[...]
        for l in range(D // LANES):
            lanes = pl.ds(l * LANES, LANES)
            _fill_shifted(cbuf, sh, lanes)
            bias = jnp.broadcast_to(bdw_ref[:, lanes], (SUBLANES, LANES))

            def conv_rows(r, carry):
                base = r * (CHUNKS * SUBLANES)
                accs = [bias] * CHUNKS
                for k in range(KC):
                    wk = jnp.broadcast_to(wdw_ref[k:k + 1, lanes], (SUBLANES, LANES))
                    for j in range(CHUNKS):
                        src = _shifted_source(cbuf, sh, base + j * SUBLANES, HALO - (KC - 1) + k, lanes)
                        accs[j] = accs[j] + wk * src
                for j in range(CHUNKS):
                    c1buf[pl.ds(pl.multiple_of(base + j * SUBLANES, SUBLANES), SUBLANES), lanes] = accs[j]
                return carry

            lax.fori_loop(0, TM // (CHUNKS * SUBLANES), conv_rows, 0)
        c1b = c1buf[...].astype(BF)
        c1_ref[...] = c1b
        c1 = c1b.astype(F32)
        mu = jnp.mean(c1, axis=-1, keepdims=True)
        xc = c1 - mu
        var = jnp.mean(xc * xc, axis=-1, keepdims=True)
        c2 = xc * lax.rsqrt(var + LN_EPS) * lg_ref[...] + lb_ref[...]
        c3 = (c2 * _sig(c2)).astype(BF)
        ccb = _dot_nn(c3, wco[...]).astype(BF)
        cc_ref[...] = ccb
        gp = z_ref[:, 3 * D:4 * D].astype(F32)
        gc = z_ref[:, 4 * D:5 * D].astype(F32)
        m = (_sig(gp) * am + _sig(gc) * ccb.astype(F32)).astype(BF)
        h2_ref[...] = h_ref[...] + _dot_nn(m, wo[...])

    tile = pl.BlockSpec((TM, D), lambda i: (i, 0))
    vec = pl.BlockSpec((1, D), lambda i: (0, 0))
    return pl.pallas_call(
        body, name="mix_mid_fwd", grid=(nt,),
        in_specs=[tile, pl.BlockSpec((TM, 5 * D), lambda i: (i, 0)),
                  pl.BlockSpec((HALO, 3 * D), lambda i: (jnp.maximum(i * hb - 1, 0), 0)),
                  pl.BlockSpec((pg, D), lambda i: (0, 0)), vec,
                  pl.BlockSpec((HALO, D), lambda i: (0, 0)), vec, vec, vec, ANY],
        out_specs=[tile, tile, tile, tile],
        out_shape=[jax.ShapeDtypeStruct((T, D), F32), jax.ShapeDtypeStruct((T, D), BF),
                   jax.ShapeDtypeStruct((T, D), BF), jax.ShapeDtypeStruct((T, D), BF)],
        scratch_shapes=[pltpu.VMEM((D, D), BF), pltpu.VMEM((D, D), BF),
                        pltpu.VMEM((pad + n_ext, D), F32), pltpu.VMEM((pad + n_ext, D), F32),
                        pltpu.VMEM((n_ext, D), F32), pltpu.VMEM((SUBLANES - 1, n_ext - SUBLANES, LANES), F32),
                        pltpu.VMEM((TM, D), F32), pltpu.VMEM((TM, D), F32),
                        pltpu.SemaphoreType.DMA((2 * NCHIP,))],
        compiler_params=_params(),
    )(h, z, z, pwcat, pscale, wdw, bdw, lg, lb, wfull)


def _ple_fwd(h, p, g, wppt, wfull):
    nt = T // TM

    def body(h_ref, p_ref, g_ref, wpp_ref, w_hbm, ho_ref, wpg, sem):
        _load_weights_once(w_hbm, (("wpg", wpg),), sem)
        x = h_ref[...]
        r = lax.rsqrt(jnp.mean(x * x, axis=-1, keepdims=True) + RMS_EPS)
        n = (x * r * g_ref[...]).astype(BF)
        gate = _sig(_dot_nn(n, wpg[...]))
        pe = _dot_nt(p_ref[...].astype(BF), wpp_ref[...])
        ho_ref[...] = x + gate * pe

    tile = pl.BlockSpec((TM, D), lambda i: (i, 0))
    return pl.pallas_call(
        body, name="ple_fwd", grid=(nt,),
        in_specs=[tile, pl.BlockSpec((TM, PD), lambda i: (i, 0)), pl.BlockSpec((1, D), lambda i: (0, 0)),
                  pl.BlockSpec((D, PD), lambda i: (0, 0)), ANY],
        out_specs=tile, out_shape=jax.ShapeDtypeStruct((T, D), F32),
        scratch_shapes=[pltpu.VMEM((D, D), BF), pltpu.SemaphoreType.DMA((NCHIP,))],
        compiler_params=_params(),
    )(h, p, g, wppt, wfull)


def _loss_bwd(h, tgt, g):
    nt = T // TM

    def body(h_ref, t_ref, g_ref, dh_ref, loss_ref, dg_ref):
        @pl.when(pl.program_id(0) == 0)
        def _():
            loss_ref[...] = jnp.zeros_like(loss_ref)
            dg_ref[...] = jnp.zeros_like(dg_ref)
        x = h_ref[...]
        r = lax.rsqrt(jnp.mean(x * x, axis=-1, keepdims=True) + RMS_EPS)
        xh = x * r
        gv = g_ref[...]
        e = xh * gv - t_ref[...]
        loss_ref[...] += jnp.sum(e * e, axis=0, keepdims=True) * (0.5 / D)
        dy = e * (1.0 / D)
        dg_ref[...] += jnp.sum(dy * xh, axis=0, keepdims=True)
        dxh = dy * gv
        dh_ref[...] = r * (dxh - xh * jnp.mean(dxh * xh, axis=-1, keepdims=True))

    tile = pl.BlockSpec((TM, D), lambda i: (i, 0))
    vec = pl.BlockSpec((1, D), lambda i: (0, 0))
    return pl.pallas_call(
        body, name="loss_bwd", grid=(nt,), in_specs=[tile, tile, vec], out_specs=[tile, vec, vec],
        out_shape=[jax.ShapeDtypeStruct((T, D), F32), jax.ShapeDtypeStruct((1, D), F32),
                   jax.ShapeDtypeStruct((1, D), F32)],
        compiler_params=_params(),
    )(h, tgt, g)


def _ple_bwd(h, dh, p, g, wppt, wfull):
    nt = T // TM
    _, rtot = _layout()

    def body(h_ref, dh_ref, p_ref, g_ref, wpp_ref, w_hbm,
             dho_ref, dg_ref, dwpp_ref, g_out, wpg, acc, stage, sem, osem):
        i = pl.program_id(0)
        _load_weights_once(w_hbm, (("wpg", wpg),), sem)

        @pl.when(i == 0)
        def _():
            dg_ref[...] = jnp.zeros_like(dg_ref)
            dwpp_ref[...] = jnp.zeros_like(dwpp_ref)
            acc[...] = jnp.zeros_like(acc)

        x = h_ref[...]
        r = lax.rsqrt(jnp.mean(x * x, axis=-1, keepdims=True) + RMS_EPS)
        xh = x * r
        gv = g_ref[...]
        n = (xh * gv).astype(BF)
        gate = _sig(_dot_nn(n, wpg[...]))
        pb = p_ref[...].astype(BF)
        pe = _dot_nt(pb, wpp_ref[...])
        d = dh_ref[...]
        dpe = (d * gate).astype(BF)
        dq = (d * pe * gate * (1.0 - gate)).astype(BF)
        dwpp_ref[...] += _dot_tn(dpe, pb)
        acc[...] += _dot_tn(n, dq)
        dn = _dot_nt(dq, wpg[...])
        dg_ref[...] += jnp.sum(dn * xh, axis=0, keepdims=True)
        dxh = dn * gv
        dho_ref[...] = d + r * (dxh - xh * jnp.mean(dxh * xh, axis=-1, keepdims=True))

        @pl.when(i == nt - 1)
        def _():
            stage[...] = acc[...].astype(BF)
            cps = _grad_copies(stage, g_out, "wpg", 0, D, D // NCHIP, osem, 0)
            for cp in cps:
                cp.start()
            for cp in cps:
                cp.wait()

    tile = pl.BlockSpec((TM, D), lambda i: (i, 0))
    vec = pl.BlockSpec((1, D), lambda i: (0, 0))
    return pl.pallas_call(
        body, name="ple_bwd", grid=(nt,),
        in_specs=[tile, tile, pl.BlockSpec((TM, PD), lambda i: (i, 0)), vec,
                  pl.BlockSpec((D, PD), lambda i: (0, 0)), ANY],
        out_specs=[tile, vec, pl.BlockSpec((D, PD), lambda i: (0, 0)), ANY],
        out_shape=[jax.ShapeDtypeStruct((T, D), F32), jax.ShapeDtypeStruct((1, D), F32),
                   jax.ShapeDtypeStruct((D, PD), F32),
                   jax.ShapeDtypeStruct((NCHIP, rtot, D), BF)],
        scratch_shapes=[pltpu.VMEM((D, D), BF), pltpu.VMEM((D, D), F32), pltpu.VMEM((D, D), BF),
                        pltpu.SemaphoreType.DMA((NCHIP,)), pltpu.SemaphoreType.DMA((NCHIP,))],
        compiler_params=_params(),
    )(h, dh, p, g, wppt, wfull)


def _ffn_bwd(h, dh, a, b, g, wfull, which):
    tm = TMB
    nt = T // tm
    names = ("wg%d" % which, "wu%d" % which, "wd%d" % which)

    def body(h_ref, dh_ref, a_ref, b_ref, g_ref, w_hbm,
             dho_ref, da_ref, db_ref, s_ref, n_ref, dg_ref, wg, wu, wd, sem):
        _load_weights_once(w_hbm, ((names[0], wg), (names[1], wu), (names[2], wd)), sem)

        @pl.when(pl.program_id(0) == 0)
        def _():
            dg_ref[...] = jnp.zeros_like(dg_ref)

        x = h_ref[...]
        r = lax.rsqrt(jnp.mean(x * x, axis=-1, keepdims=True) + RMS_EPS)
        xh = x * r
        gv = g_ref[...]
        n_ref[...] = (xh * gv).astype(BF)
        d = dh_ref[...]
        df = (0.5 * d).astype(BF)
        dn = jnp.zeros((tm, D), F32)
        for sl in _hidden_chunks():
            av = a_ref[:, sl].astype(F32)
            bv = b_ref[:, sl].astype(F32)
            ds = _dot_nt(df, wd[sl, :])
            sg = _sig(av)
            sil = av * sg
            s_ref[:, sl] = (sil * bv).astype(BF)
            da = (ds * bv * (sg * (1.0 + av * (1.0 - sg)))).astype(BF)
            db = (ds * sil).astype(BF)
            da_ref[:, sl] = da
            db_ref[:, sl] = db
            dn = dn + _dot_nn(da, wg[sl, :]) + _dot_nn(db, wu[sl, :])
        dg_ref[...] += jnp.sum(dn * xh, axis=0, keepdims=True)
        dxh = dn * gv
        dho_ref[...] = d + r * (dxh - xh * jnp.mean(dxh * xh, axis=-1, keepdims=True))

    tile = lambda w: pl.BlockSpec((tm, w), lambda i: (i, 0))
    vec = pl.BlockSpec((1, D), lambda i: (0, 0))
    return pl.pallas_call(
        body, name="ffn_bwd", grid=(nt,),
        in_specs=[tile(D), tile(D), tile(F), tile(F), vec, ANY],
        out_specs=[tile(D), tile(F), tile(F), tile(F), tile(D), vec],
        out_shape=[jax.ShapeDtypeStruct((T, D), F32), jax.ShapeDtypeStruct((T, F), BF),
                   jax.ShapeDtypeStruct((T, F), BF), jax.ShapeDtypeStruct((T, F), BF),
                   jax.ShapeDtypeStruct((T, D), BF), jax.ShapeDtypeStruct((1, D), F32)],
        scratch_shapes=[pltpu.VMEM((F, D), BF), pltpu.VMEM((F, D), BF), pltpu.VMEM((F, D), BF),
                        pltpu.SemaphoreType.DMA((3 * NCHIP,))],
        compiler_params=_params(),
    )(h, dh, a, b, g, wfull)


def _wgrad(xs, y, gbuf, names, row0, rb, piece, yscale=None):
    nx = len(xs)
    rx = xs[0].shape[1]
    nj = rx // rb
    nt = T // TMW
    npiece = rb // piece

    def body(*refs):
        x_refs = refs[:nx]
        y_ref = refs[nx]
        g_out = refs[nx + 2]
        accs = refs[nx + 3:2 * nx + 3]
        stages = refs[2 * nx + 3:3 * nx + 3]
        osem = refs[3 * nx + 3]
        j = pl.program_id(0)
        t = pl.program_id(1)

        @pl.when(t == 0)
        def _():
            for acc in accs:
                acc[...] = jnp.zeros_like(acc)

        yv = y_ref[...]
        if yscale is not None:
            yv = (yscale * yv).astype(BF)
        for x_ref, acc in zip(x_refs, accs):
            acc[...] += _dot_tn(x_ref[...], yv)

        @pl.when(t == nt - 1)
        def _():
            cps = []
            for xi in range(nx):
                stages[xi][...] = accs[xi][...].astype(BF)
                cps += _grad_copies(stages[xi], g_out, names[xi], row0 + j * rb, rb, piece,
                                    osem, xi * npiece)
            for cp in cps:
                cp.start()
            for cp in cps:
                cp.wait()

    in_specs = [pl.BlockSpec((TMW, rb), lambda j, t: (t, j)) for _ in xs]
    in_specs += [pl.BlockSpec((TMW, D), lambda j, t: (t, 0)), ANY]
    return pl.pallas_call(
        body, name="wgrad", grid=(nj, nt), in_specs=in_specs, out_specs=ANY,
        out_shape=jax.ShapeDtypeStruct(gbuf.shape, gbuf.dtype),
        scratch_shapes=([pltpu.VMEM((rb, D), F32) for _ in xs] + [pltpu.VMEM((rb, D), BF) for _ in xs]
                        + [pltpu.SemaphoreType.DMA((nx * npiece,))]),
        input_output_aliases={nx + 1: 0},
        compiler_params=_params(("arbitrary", "arbitrary")),
    )(*xs, y, gbuf)


def _mix_b1(dh, z, pooled, c1, cc, pwcat, pscale, lg, lb, wfull, gbuf):
    nt = T // TM
    pg = D // NG

    def body(dh_ref, gp_ref, gc_ref, p_ref, c1_ref, cc_ref, pw_ref, ps_ref, lg_ref, lb_ref, w_hbm, _g_in,
             dp_ref, dc1_ref, dzb_ref, small_ref, dpw_ref, g_out,
             wco, wo, acc_o, acc_co, stage_o, stage_co, qbuf, sem, osem):
        i = pl.program_id(0)
        _load_weights_once(w_hbm, (("wco", wco), ("wo", wo)), sem)

        @pl.when(i == 0)
        def _():
            small_ref[...] = jnp.zeros_like(small_ref)
            dpw_ref[...] = jnp.zeros_like(dpw_ref)
            acc_o[...] = jnp.zeros_like(acc_o)
            acc_co[...] = jnp.zeros_like(acc_co)

        dhb = dh_ref[...].astype(BF)
        dm = _dot_nt(dhb, wo[...])
        sp = _sig(gp_ref[...].astype(F32))
        sc = _sig(gc_ref[...].astype(F32))
        for g in range(NG):
            ls = pl.ds(g * pg, pg)
            qbuf[:, ls] = _dot_nn(p_ref[:, ls], pw_ref[:, ls])
        q = qbuf[...]
        psv = ps_ref[...]
        am = q * psv
        ccv = cc_ref[...].astype(F32)
        m = (sp * am + sc * ccv).astype(BF)
        acc_o[...] += _dot_tn(m, dhb)
        dam = dm * sp
        dzb_ref[:, 0:D] = (dm * am * sp * (1.0 - sp)).astype(BF)
        dccb = (dm * sc).astype(BF)
        dzb_ref[:, D:2 * D] = (dm * ccv * sc * (1.0 - sc)).astype(BF)
        small_ref[0:1, :] += jnp.sum(dam * q, axis=0, keepdims=True)
        dq = (dam * psv).astype(BF)
        for g in range(NG):
            ls = pl.ds(g * pg, pg)
            dqg = dq[:, g * pg:(g + 1) * pg]
            dp_ref[:, ls] = _dot_nt(dqg, pw_ref[:, ls]).astype(BF)
            dpw_ref[:, ls] += _dot_tn(p_ref[:, ls], dqg)
        c1v = c1_ref[...].astype(F32)
        mu = jnp.mean(c1v, axis=-1, keepdims=True)
        xc = c1v - mu
        var = jnp.mean(xc * xc, axis=-1, keepdims=True)
        rs = lax.rsqrt(var + LN_EPS)
        c2n = xc * rs
        lgv = lg_ref[...]
        c2 = c2n * lgv + lb_ref[...]
        sg2 = _sig(c2)
        c3 = (c2 * sg2).astype(BF)
        acc_co[...] += _dot_tn(c3, dccb)
        dc3 = _dot_nt(dccb, wco[...])
        dc2 = dc3 * (sg2 * (1.0 + c2 * (1.0 - sg2)))
        small_ref[2:3, :] += jnp.sum(dc2 * c2n, axis=0, keepdims=True)
        small_ref[3:4, :] += jnp.sum(dc2, axis=0, keepdims=True)
        dc2n = dc2 * lgv
        dc1 = rs * (dc2n - jnp.mean(dc2n, axis=-1, keepdims=True)
                    - c2n * jnp.mean(dc2n * c2n, axis=-1, keepdims=True))
        small_ref[1:2, :] += jnp.sum(dc1, axis=0, keepdims=True)
        dc1_ref[...] = dc1.astype(BF)

        @pl.when(i == nt - 1)
        def _():
            stage_o[...] = acc_o[...].astype(BF)
            stage_co[...] = acc_co[...].astype(BF)
            cps = _grad_copies(stage_o, g_out, "wo", 0, D, D // NCHIP, osem, 0)
            cps += _grad_copies(stage_co, g_out, "wco", 0, D, D // NCHIP, osem, NCHIP)
            for cp in cps:
                cp.start()
            for cp in cps:
                cp.wait()

    tile = pl.BlockSpec((TM, D), lambda i: (i, 0))
    vec = pl.BlockSpec((1, D), lambda i: (0, 0))
    full = lambda r: pl.BlockSpec((r, D), lambda i: (0, 0))
    return pl.pallas_call(
        body, name="mix_b1", grid=(nt,),
        in_specs=[tile, pl.BlockSpec((TM, D), lambda i: (i, 3)), pl.BlockSpec((TM, D), lambda i: (i, 4)),
                  tile, tile, tile, full(pg), vec, vec, vec, ANY, ANY],
        out_specs=[tile, tile, pl.BlockSpec((TM, 2 * D), lambda i: (i, 0)), full(8), full(pg), ANY],
        out_shape=[jax.ShapeDtypeStruct((T, D), BF), jax.ShapeDtypeStruct((T, D), BF),
                   jax.ShapeDtypeStruct((T, 2 * D), BF), jax.ShapeDtypeStruct((8, D), F32),
                   jax.ShapeDtypeStruct((pg, D), F32), jax.ShapeDtypeStruct(gbuf.shape, gbuf.dtype)],
        scratch_shapes=[pltpu.VMEM((D, D), BF), pltpu.VMEM((D, D), BF),
                        pltpu.VMEM((D, D), F32), pltpu.VMEM((D, D), F32),
                        pltpu.VMEM((D, D), BF), pltpu.VMEM((D, D), BF),
                        pltpu.VMEM((TM, D), F32),
                        pltpu.SemaphoreType.DMA((2 * NCHIP,)), pltpu.SemaphoreType.DMA((2 * NCHIP,))],
        input_output_aliases={11: 5},
        compiler_params=_params(),
    )(dh, z, z, pooled, c1, cc, pwcat, pscale, lg, lb, wfull, gbuf)


def _mix_b2(dp, dc1, z, wdw):
    nt = T // TM
    pg = D // NG
    hb = TM // HALO
    nhb = T // HALO
    n_ext = TM + HALO
    pad = SUBLANES

    def body(dp_ref, dpn_ref, dc_ref, dcn_ref, za_ref, zg_ref, wdw_ref,
             dza_ref, dw_ref, pa, pb, cbuf, sh, c0buf, dc0buf):
        i = pl.program_id(0)

        @pl.when(i == 0)
        def _():
            dw_ref[...] = jnp.zeros_like(dw_ref)
            pa[n_ext:n_ext + pad, :] = jnp.zeros((pad, D), F32)
            pb[n_ext:n_ext + pad, :] = jnp.zeros((pad, D), F32)

        more = (i < nt - 1).astype(F32)
        for g, w in enumerate(WINDOWS):
            ls = pl.ds(g * pg, pg)
            cur_dp = dp_ref[:, ls].astype(F32)
            pa[0:TM, ls] = cur_dp / _pool_counts(i, TM, w)
            pa[TM:n_ext, ls] = dpn_ref[:, ls].astype(F32) * (more / w)
            cur, nxt = pa, pb
            d = 1
            while d < w:
                nxt[pl.ds(0, n_ext), ls] = cur[pl.ds(0, n_ext), ls] + cur[pl.ds(d, n_ext), ls]
                cur, nxt = nxt, cur
                d *= 2
            dza_ref[:, ls] = (cur[pl.ds(0, TM), ls] - cur_dp).astype(BF)
        za = za_ref[...].astype(F32)
        sg = _sig(zg_ref[...].astype(F32))
        cbuf[0:TM, :] = dc_ref[...].astype(F32)
        cbuf[TM:n_ext, :] = dcn_ref[...].astype(F32) * more
        c0buf[...] = za * sg
        for l in range(D // LANES):
            lanes = pl.ds(l * LANES, LANES)
            _fill_shifted(cbuf, sh, lanes)

            def conv_rows(r, accs):
                base = r * (CHUNKS * SUBLANES)
                rows = [pl.ds(pl.multiple_of(base + j * SUBLANES, SUBLANES), SUBLANES) for j in range(CHUNKS)]
                c0v = [c0buf[rows[j], lanes] for j in range(CHUNKS)]
                acc = [jnp.zeros((SUBLANES, LANES), F32)] * CHUNKS
                new = list(accs)
                for k in range(KC):
                    wk = jnp.broadcast_to(wdw_ref[k:k + 1, lanes], (SUBLANES, LANES))
                    for j in range(CHUNKS):
                        src = _shifted_source(cbuf, sh, base + j * SUBLANES, KC - 1 - k, lanes)
                        acc[j] = acc[j] + wk * src
                        new[k] = new[k] + c0v[j] * src
                for j in range(CHUNKS):
                    dc0buf[rows[j], lanes] = acc[j]
                return tuple(new)

            init = tuple(jnp.zeros((SUBLANES, LANES), F32) for _ in range(KC))
            accs = lax.fori_loop(0, TM // (CHUNKS * SUBLANES), conv_rows, init)
            for k in range(KC):
                dw_ref[k:k + 1, lanes] += jnp.sum(accs[k], axis=0, keepdims=True)
        dc0 = dc0buf[...]
        dza_ref[:, D:2 * D] = (dc0 * sg).astype(BF)
        dza_ref[:, 2 * D:3 * D] = (dc0 * za * sg * (1.0 - sg)).astype(BF)

    tile = pl.BlockSpec((TM, D), lambda i: (i, 0))
    nxt_spec = pl.BlockSpec((HALO, D), lambda i: (jnp.minimum((i + 1) * hb, nhb - 1), 0))
    return pl.pallas_call(
        body, name="mix_b2", grid=(nt,),
        in_specs=[tile, nxt_spec, tile, nxt_spec, pl.BlockSpec((TM, D), lambda i: (i, 1)),
                  pl.BlockSpec((TM, D), lambda i: (i, 2)), pl.BlockSpec((HALO, D), lambda i: (0, 0))],
        out_specs=[pl.BlockSpec((TM, 3 * D), lambda i: (i, 0)), pl.BlockSpec((HALO, D), lambda i: (0, 0))],
        out_shape=[jax.ShapeDtypeStruct((T, 3 * D), BF), jax.ShapeDtypeStruct((HALO, D), F32)],
        scratch_shapes=[pltpu.VMEM((n_ext + pad, D), F32), pltpu.VMEM((n_ext + pad, D), F32),
                        pltpu.VMEM((n_ext, D), F32), pltpu.VMEM((SUBLANES - 1, n_ext - SUBLANES, LANES), F32),
                        pltpu.VMEM((TM, D), F32), pltpu.VMEM((TM, D), F32)],
        compiler_params=_params(),
    )(dp, dp, dc1, dc1, z, z, wdw)


def _mix_b3(h, dh, dza, dzb, g, wfull):
    nt = T // TM

    def body(h_ref, dh_ref, dza_ref, dzb_ref, g_ref, w_hbm, dho_ref, u_ref, dg_ref, win, sem):
        _load_weights_once(w_hbm, (("win", win),), sem)

        @pl.when(pl.program_id(0) == 0)
        def _():
            dg_ref[...] = jnp.zeros_like(dg_ref)

        x = h_ref[...]
        r = lax.rsqrt(jnp.mean(x * x, axis=-1, keepdims=True) + RMS_EPS)
        xh = x * r
        gv = g_ref[...]
        u_ref[...] = (xh * gv).astype(BF)
        du = _dot_nn(dza_ref[...], win[0:3 * D, :]) + _dot_nn(dzb_ref[...], win[3 * D:5 * D, :])
        dg_ref[...] += jnp.sum(du * xh, axis=0, keepdims=True)
        dxh = du * gv
        dho_ref[...] = dh_ref[...] + r * (dxh - xh * jnp.mean(dxh * xh, axis=-1, keepdims=True))

    tile = lambda w: pl.BlockSpec((TM, w), lambda i: (i, 0))
    vec = pl.BlockSpec((1, D), lambda i: (0, 0))
    return pl.pallas_call(
        body, name="mix_b3", grid=(nt,),
        in_specs=[tile(D), tile(D), tile(3 * D), tile(2 * D), vec, ANY],
        out_specs=[tile(D), tile(D), vec],
        out_shape=[jax.ShapeDtypeStruct((T, D), F32), jax.ShapeDtypeStruct((T, D), BF),
                   jax.ShapeDtypeStruct((1, D), F32)],
        scratch_shapes=[pltpu.VMEM((5 * D, D), BF), pltpu.SemaphoreType.DMA((NCHIP,))],
        compiler_params=_params(),
    )(h, dh, dza, dzb, g, wfull)


def _mesh_pos():
    x, y, c = lax.axis_index("x"), lax.axis_index("y"), lax.axis_index("c")
    chips = [(1 - x, y), (x, 1 - y), (1 - x, 1 - y)]
    return x, y, c, 2 * x + y, chips


def _handshake(peers):
    barrier = pltpu.get_barrier_semaphore()
    for peer in peers:
        pl.semaphore_signal(barrier, inc=1, device_id=peer, device_id_type=MESH)
    pl.semaphore_wait(barrier, len(peers))


def _run_comm(body, name, cid, ins, inouts, out_types, scratch):
    in_refs = [jax.new_ref(a, memory_space=HBM) for a in ins]
    inout_refs = [jax.new_ref(a, memory_space=HBM) for a in inouts]
    out_refs = [jax.empty_ref(t, memory_space=HBM) for t in out_types]

    @pl.kernel(mesh=plsc.ScalarSubcoreMesh(axis_name="seq", num_cores=1), name=name,
               scratch_types=scratch, compiler_params=pltpu.CompilerParams(collective_id=cid))
    def launch(*scr):
        body(*in_refs, *inout_refs, *out_refs, *scr)

    launch()
    return [r[...] for r in inout_refs + out_refs]


def _remote(src, dst, send_sem, recv_sem, to):
    return pltpu.make_async_remote_copy(src_ref=src, dst_ref=dst, send_sem=send_sem, recv_sem=recv_sem,
                                        device_id=to, device_id_type=MESH)


def _gather_layer(packed, cid):
    rtot = packed.shape[0]
    half = rtot // 2
    kme = 2 * lax.axis_index("x") + lax.axis_index("y")
    landing = lax.dynamic_update_slice(lax.empty((NCHIP, rtot, D), BF), packed[None], (kme, 0, 0))

    def body(p_ref, w_ref, send_sems, recv_sems):
        x, y, c, kme, chips = _mesh_pos()
        sib = (x, y, 1 - c)
        _handshake([(*ch, c) for ch in chips] + [sib])
        ks = [2 * cx + cy for cx, cy in chips]
        mine = pl.ds(pl.multiple_of(c * half, 16), half)
        other = pl.ds(pl.multiple_of((1 - c) * half, 16), half)
        first = [_remote(p_ref.at[mine], w_ref.at[kme, mine], send_sems.at[j], recv_sems.at[j], (*chips[j], c))
                 for j in range(3)]
        for cp in first:
            cp.start()
        passed = [_remote(w_ref.at[ks[j], mine], w_ref.at[ks[j], mine], send_sems.at[3 + j], recv_sems.at[3 + j], sib)
                  for j in range(3)]
        for j in range(3):
            _remote(p_ref.at[mine], w_ref.at[ks[j], mine], send_sems.at[j], recv_sems.at[j], sib).wait_recv()
            passed[j].start()
        for j in range(3):
            _remote(p_ref.at[mine], w_ref.at[ks[j], other], send_sems.at[3 + j], recv_sems.at[3 + j], sib).wait_recv()
        for cp in first + passed:
            cp.wait_send()

    return _run_comm(body, "gather_layer_%d" % cid, cid, [packed], [landing], [],
                     (pltpu.SemaphoreType.DMA((6,)), pltpu.SemaphoreType.DMA((6,))))[0]


def _sibling_swap(gbuf, cid):
    rtot = gbuf.shape[1]
    half = rtot // 2

    def body(g_ref, r_ref, send_sem, recv_sem):
        x, y, c, _, _ = _mesh_pos()
        sib = (x, y, 1 - c)
        _handshake([sib])
        other = pl.ds(pl.multiple_of((1 - c) * half, 16), half)
        cp = _remote(g_ref.at[:, other, :], r_ref, send_sem, recv_sem, sib)
        cp.start()
        cp.wait()

    return _run_comm(body, "sibling_swap_%d" % cid, cid, [gbuf], [],
                     [jax.ShapeDtypeStruct((NCHIP, half, D), BF)],
                     (pltpu.SemaphoreType.DMA, pltpu.SemaphoreType.DMA))[0]


def _sibling_swap_inline(gbuf):
    rtot = gbuf.shape[1]
    half = rtot // 2

    def body(g_ref, r_ref, send_sem, recv_sem):
        x, y, c, _, _ = _mesh_pos()
        other = pl.ds(pl.multiple_of((1 - c) * half, 16), half)
        cp = _remote(g_ref.at[:, other, :], r_ref, send_sem, recv_sem, (x, y, 1 - c))
        cp.start()
        cp.wait()

    return pl.pallas_call(
        body, name="sibling_swap_inline", in_specs=[ANY], out_specs=ANY,
        out_shape=jax.ShapeDtypeStruct((NCHIP, half, D), BF),
        scratch_shapes=[pltpu.SemaphoreType.DMA, pltpu.SemaphoreType.DMA],
    )(gbuf)


def _row_tile(rows):
    for cand in range(min(rows, 1280) // 16 * 16, 0, -16):
        if rows % cand == 0:
            return cand
    return rows


def _chip_sum(gbuf, rbuf, cidx):
    half = rbuf.shape[1]
    rt = _row_tile(half)
    nb = half // rt

    def body(c_ref, g_ref, r_ref, o_ref):
        o_ref[...] = (g_ref[...].astype(F32) + r_ref[...].astype(F32)).astype(BF)

    return pl.pallas_call(
        body, name="chip_sum",
        grid_spec=pltpu.PrefetchScalarGridSpec(
            num_scalar_prefetch=1, grid=(NCHIP, nb),
            in_specs=[pl.BlockSpec((None, rt, D), lambda k, r, c: (k, c[0] * nb + r, 0)),
                      pl.BlockSpec((None, rt, D), lambda k, r, c: (k, r, 0))],
            out_specs=pl.BlockSpec((None, rt, D), lambda k, r, c: (k, r, 0))),
        out_shape=jax.ShapeDtypeStruct((NCHIP, half, D), BF),
        compiler_params=_params(("arbitrary", "arbitrary")),
    )(cidx, gbuf, rbuf)


def _chip_exchange(sbuf, cid):
    half = sbuf.shape[1]

    def body(s_ref, x_ref, send_sems, recv_sems):
        x, y, c, _, chips = _mesh_pos()
        _handshake([(*ch, c) for ch in chips])
        cps = [_remote(s_ref.at[2 * cx + cy], x_ref.at[j], send_sems.at[j], recv_sems.at[j], (cx, cy, c))
               for j, (cx, cy) in enumerate(chips)]
        for cp in cps:
            cp.start()
        for cp in cps:
            cp.wait()

    return _run_comm(body, "chip_exchange_%d" % cid, cid, [sbuf], [],
                     [jax.ShapeDtypeStruct((3, half, D), BF)],
                     (pltpu.SemaphoreType.DMA((3,)), pltpu.SemaphoreType.DMA((3,))))[0]


def _shard_sum(gbuf, rbuf, xbuf, ck):
    half = rbuf.shape[1]
    rt = _row_tile(half)
    nb = half // rt

    def body(ck_ref, g_ref, r_ref, x_ref, o_ref):
        acc = g_ref[...].astype(F32) + r_ref[...].astype(F32)
        for j in range(3):
            acc = acc + x_ref[j].astype(F32)
        o_ref[...] = acc

    return pl.pallas_call(
        body, name="shard_sum",
        grid_spec=pltpu.PrefetchScalarGridSpec(
            num_scalar_prefetch=1, grid=(nb,),
            in_specs=[pl.BlockSpec((None, rt, D), lambda r, ck: (ck[1], ck[0] * nb + r, 0)),
                      pl.BlockSpec((None, rt, D), lambda r, ck: (ck[1], r, 0)),
                      pl.BlockSpec((3, rt, D), lambda r, ck: (0, r, 0))],
            out_specs=pl.BlockSpec((rt, D), lambda r, ck: (ck[0] * nb + r, 0))),
        out_shape=jax.ShapeDtypeStruct((2 * half, D), F32),
        compiler_params=_params(),
    )(ck, gbuf, rbuf, xbuf)


def _sibling_share(red, cid):
    half = red.shape[0] // 2

    def body(o_ref, send_sem, recv_sem):
        x, y, c, _, _ = _mesh_pos()
        sib = (x, y, 1 - c)
        _handshake([sib])
        mine = pl.ds(pl.multiple_of(c * half, 8), half)
        other = pl.ds(pl.multiple_of((1 - c) * half, 8), half)
        cp = _remote(o_ref.at[mine], o_ref.at[mine], send_sem, recv_sem, sib)
        cp.start()
        cp.wait_send()
        _remote(o_ref.at[mine], o_ref.at[other], send_sem, recv_sem, sib).wait_recv()

    return _run_comm(body, "sibling_share_%d" % cid, cid, [], [red], [],
                     (pltpu.SemaphoreType.DMA, pltpu.SemaphoreType.DMA))[0]


def _allreduce_small(v):
    rows = v.shape[0]
    ndev = 2 * NCHIP

    def body(v_ref, o_ref, gat, send_sems, recv_sems, lsem):
        x, y, c, _, chips = _mesh_pos()
        me, sib = (x, y, c), (x, y, 1 - c)

        def blk(px, py, pc):
            return gat.at[pl.ds((4 * px + 2 * py + pc) * rows, rows), :]

        def copy(k, block, to, src=None):
            return pltpu.make_async_remote_copy(
                src_ref=blk(*block) if src is None else src, dst_ref=blk(*block),
                send_sem=send_sems.at[k], recv_sem=recv_sems.at[k], device_id=to, device_id_type=MESH)

        mine = pltpu.make_async_copy(v_ref, blk(*me), lsem)
        mine.start()
        first = [copy(0, me, sib, src=v_ref)]
        first += [copy(1 + j, me, (*chip, c), src=v_ref) for j, chip in enumerate(chips)]
        for cp in first:
            cp.start()
        passed = [copy(4 + j, (*chip, c), sib) for j, chip in enumerate(chips)]
        for j, chip in enumerate(chips):
            copy(1 + j, (*chip, c), me).wait_recv()
            passed[j].start()
        copy(0, sib, me).wait_recv()
        for j, chip in enumerate(chips):
            copy(4 + j, (*chip, 1 - c), me).wait_recv()
        for cp in first + passed:
            cp.wait_send()
        mine.wait()
        acc = gat[0:rows, :]
        for d in range(1, ndev):
            acc = acc + gat[d * rows:(d + 1) * rows, :]
        o_ref[...] = acc

    vm = pl.BlockSpec(memory_space=pltpu.VMEM)
    return pl.pallas_call(
        body, name="allreduce_small", in_specs=[vm], out_specs=vm,
        out_shape=jax.ShapeDtypeStruct((rows, D), F32),
        scratch_shapes=[pltpu.VMEM((ndev * rows, D), F32), pltpu.SemaphoreType.DMA((7,)),
                        pltpu.SemaphoreType.DMA((7,)), pltpu.SemaphoreType.DMA],
    )(v)


def _adamw(w, g, m, v):
    shape = w.shape
    cols = shape[-1]
    rows = w.size // cols
    bm = rows
    for cand in range(512, 0, -SUBLANES):
        if rows % cand == 0:
            bm = cand
            break
    bc1 = 1.0 - ADAM_B1 ** ADAM_STEP
    bc2 = 1.0 - ADAM_B2 ** ADAM_STEP

    def body(w_ref, g_ref, m_ref, v_ref, d_ref, mo_ref, vo_ref):
        gv = g_ref[...]
        mn = ADAM_B1 * m_ref[...] + (1.0 - ADAM_B1) * gv
        vn = ADAM_B2 * v_ref[...] + (1.0 - ADAM_B2) * (gv * gv)
        mo_ref[...] = mn
        vo_ref[...] = vn
        d_ref[...] = -ADAM_LR * ((mn / bc1) / (jnp.sqrt(vn / bc2) + ADAM_EPS) + ADAM_WD * w_ref[...])

    spec = pl.BlockSpec((bm, cols), lambda i: (i, 0))
    out = jax.ShapeDtypeStruct((rows, cols), F32)
    d, mo, vo = pl.pallas_call(
        body, name="adamw", grid=(rows // bm,), in_specs=[spec] * 4, out_specs=[spec] * 3,
        out_shape=[out, out, out], compiler_params=_params(),
    )(*[t.reshape(rows, cols) for t in (w, g, m, v)])
    return d.reshape(shape), mo.reshape(shape), vo.reshape(shape)


def _adamw_layer(w, g, m, v, li, prev):
    cols = w.shape[-1]
    rows = w[0].size // cols
    bm = rows
    for cand in range(512, 0, -SUBLANES):
        if rows % cand == 0:
            bm = cand
            break
    bc1 = 1.0 - ADAM_B1 ** ADAM_STEP
    bc2 = 1.0 - ADAM_B2 ** ADAM_STEP

    def body(*refs):
        w_ref, g_ref, m_ref, v_ref = refs[:4]
        go_ref, d_ref, mo_ref, vo_ref = refs[-4:]
        gv = g_ref[...]
        mn = ADAM_B1 * m_ref[...] + (1.0 - ADAM_B1) * gv
        vn = ADAM_B2 * v_ref[...] + (1.0 - ADAM_B2) * (gv * gv)
        go_ref[...] = gv
        mo_ref[...] = mn
        vo_ref[...] = vn
        d_ref[...] = -ADAM_LR * ((mn / bc1) / (jnp.sqrt(vn / bc2) + ADAM_EPS) + ADAM_WD * w_ref[...])

    layer = pl.BlockSpec((None, bm, cols), lambda i: (li, i, 0))
    in_specs = [layer, pl.BlockSpec((bm, cols), lambda i: (i, 0)), layer, layer]
    args = [w.reshape(DEPTH, rows, cols), g.reshape(rows, cols), m.reshape(DEPTH, rows, cols),
            v.reshape(DEPTH, rows, cols)]
    aliases = {}
    if prev is not None:
        in_specs += [ANY] * 4
        args += list(prev)
        aliases = {4 + i: i for i in range(4)}
    out = jax.ShapeDtypeStruct((DEPTH, rows, cols), F32)
    return pl.pallas_call(
        body, name="adamw_layer", grid=(rows // bm,), in_specs=in_specs, out_specs=[layer] * 4,
        out_shape=[out] * 4, input_output_aliases=aliases, compiler_params=_params(),
    )(*args)


def _pack_shards(ws, li):
    pg = D // NG
    t = lambda a: jnp.swapaxes(a[li], 0, 1)
    parts = [t(ws["ffn1_w_gate"]), t(ws["ffn1_w_up"]), ws["ffn1_w_down"][li],
             t(ws["w_in"]), ws["conv_w_out"][li], ws["w_out"][li],
             t(ws["ffn2_w_gate"]), t(ws["ffn2_w_up"]), ws["ffn2_w_down"][li], ws["ple_w_gate"][li],
             t(ws["ple_w_proj"]).reshape(-1, D),
             jnp.swapaxes(ws["pool_w"][li], 0, 1).reshape(pg // NCHIP, D)]
    return jnp.concatenate([p.astype(BF) for p in parts], axis=0)


def _unpack_shards(parts):
    lay, _ = _layout()
    pg = D // NG

    def rows(n):
        off, rs = lay[n]
        for r0, arr in parts:
            if r0 <= off and off + rs <= r0 + arr.shape[0]:
                return arr[off - r0:off - r0 + rs, :]
        raise ValueError(n)

    t = lambda a: jnp.swapaxes(a, 0, 1)
    return {
        "ffn1_w_gate": rows("wg1"), "ffn1_w_up": rows("wu1"), "ffn1_w_down": rows("wd1"),
        "ffn2_w_gate": rows("wg2"), "ffn2_w_up": rows("wu2"), "ffn2_w_down": rows("wd2"),
        "w_in": t(rows("win")), "conv_w_out": rows("wco"), "w_out": rows("wo"), "ple_w_gate": rows("wpg"),
        "ple_w_proj": t(rows("wpp").reshape(D // NCHIP, PD)),
        "pool_w": jnp.swapaxes(rows("pw").reshape(pg // NCHIP, NG, pg), 0, 1),
    }


_BIG = ("ffn1_w_gate", "ffn1_w_up", "ffn1_w_down", "w_in", "pool_w", "conv_w_out", "w_out",
        "ffn2_w_gate", "ffn2_w_up", "ffn2_w_down", "ple_w_gate", "ple_w_proj")
_TRANSPOSED = ("ffn1_w_gate", "ffn1_w_up", "ffn2_w_gate", "ffn2_w_up")
_VECS = ("ffn1_norm", "mix_norm", "pool_scale", "conv_dw_b", "conv_ln_g", "conv_ln_b", "ffn2_norm", "ple_norm")
_WEIGHTS = ("ffn1_norm", "ffn1_w_gate", "ffn1_w_up", "ffn1_w_down", "mix_norm", "w_in", "pool_w", "pool_scale",
            "conv_dw_w", "conv_dw_b", "conv_ln_g", "conv_ln_b", "conv_w_out", "w_out", "ffn2_norm",
            "ffn2_w_gate", "ffn2_w_up", "ffn2_w_down", "ple_norm", "ple_w_gate", "ple_w_proj", "final_norm")


def _step(x, p, tgt, ws, ms, vs):
    lay, rtot = _layout()
    pg = D // NG
    cpos = lax.axis_index("c")
    kme = 2 * lax.axis_index("x") + lax.axis_index("y")
    cidx = jnp.stack([cpos]).astype(jnp.int32)
    ck = jnp.stack([cpos, kme]).astype(jnp.int32)
    h = x.reshape(T, D)
    tgt = tgt.reshape(T, D)

    nfirst = _first_rows()
    packed = [_pack_shards(ws, li) for li in range(DEPTH)]
    wfirst, wrest = [None] * DEPTH, [None] * DEPTH
    wfirst[0] = _gather_layer(packed[0][:nfirst], 0)
    kk = ws["conv_dw_w"].shape[1]
    wdw_mine = jnp.zeros((DEPTH * HALO, D), F32)
    for li in range(DEPTH):
        blockw = jnp.zeros((kk, D), F32)
        mine = jnp.where(cpos == 0, ws["conv_dw_w"][li], 0.0)
        blockw = lax.dynamic_update_slice(blockw, mine, (0, kme * (D // NCHIP)))
        wdw_mine = wdw_mine.at[li * HALO:li * HALO + kk, :].set(blockw)
    wdw_all = _allreduce_small(wdw_mine)
    piece, wfirst[0], wdw_all = lax.optimization_barrier((packed[0][nfirst:], wfirst[0], wdw_all))
    wrest[0] = _gather_layer(piece, 1)
    wppt, pwcat = [None] * DEPTH, [None] * DEPTH
    wdw = [wdw_all[li * HALO:(li + 1) * HALO, :] for li in range(DEPTH)]
    vec = lambda name, li: _row(ws[name][li])

    saved = []
    for li in range(DEPTH):
        if li > 0:
            wfirst[li], wrest[li], h = lax.optimization_barrier((wfirst[li], wrest[li], h))
        h0 = h
        h1, a1, b1 = _ffn_fwd(h0, vec("ffn1_norm", li), wfirst[li], 1)
        if li + 1 < DEPTH:
            piece, wrest[li], h1 = lax.optimization_barrier((packed[li + 1][:nfirst], wrest[li], h1))
            wfirst[li + 1] = _gather_layer(piece, 2 * li + 2)
        else:
            wrest[li], h1 = lax.optimization_barrier((wrest[li], h1))
        o, s = lay["wpp"]
        wppt[li] = wrest[li][:, o - nfirst:o - nfirst + s, :].reshape(D, PD)
        o, s = lay["pw"]
        pwcat[li] = wrest[li][:, o - nfirst:o - nfirst + s, :].reshape(pg, D)
        z = _mix_in_fwd(h1, vec("mix_norm", li), wrest[li])
        h2, pooled, c1, cc = _mix_mid_fwd(h1, z, pwcat[li], vec("pool_scale", li), wdw[li],
                                          vec("conv_dw_b", li), vec("conv_ln_g", li), vec("conv_ln_b", li),
                                          wrest[li])
        if li + 1 < DEPTH:
            piece, wfirst[li + 1], h2 = lax.optimization_barrier((packed[li + 1][nfirst:], wfirst[li + 1], h2))
            wrest[li + 1] = _gather_layer(piece, 2 * li + 3)
        h3, a2, b2 = _ffn_fwd(h2, vec("ffn2_norm", li), wrest[li], 2)
        h = _ple_fwd(h3, p[li, 0], vec("ple_norm", li), wppt[li], wrest[li])
        saved.append((h0, a1, b1, h1, z, pooled, c1, cc, h2, a2, b2, h3))

    dh, losscols, dgf = _loss_bwd(h, tgt, _row(ws["final_norm"]))
    vecg = [dict() for _ in range(DEPTH)]
    dwdw = [None] * DEPTH
    def reduce_start(g, chain, inline=False):
        cid = 2 * DEPTH + 3 * chain
        return {"g": g, "r": _sibling_swap_inline(g) if inline else _sibling_swap(g, cid), "cid": cid}

    def reduce_mid(st, anchor):
        if anchor is not None:
            anchor, st["r"] = lax.optimization_barrier((anchor, st["r"]))
        sbuf = _chip_sum(st["g"], st["r"], cidx)
        if anchor is not None:
            anchor, sbuf = lax.optimization_barrier((anchor, sbuf))
        st["x"] = _chip_exchange(sbuf, st["cid"] + 1)
        return anchor

    def reduce_end(st, anchor):
        xb = st["x"]
        if anchor is not None:
            anchor, xb = lax.optimization_barrier((anchor, xb))
        rsum = _shard_sum(st["g"], st["r"], xb, ck)
        if anchor is not None:
            anchor, rsum = lax.optimization_barrier((anchor, rsum))
        return anchor, _sibling_share(rsum, st["cid"] + 2)

    parts = [[] for _ in range(DEPTH)]
    above = None
    nchain = 0
    for li in reversed(range(DEPTH)):
        h0, a1, b1, h1, z, pooled, c1, cc, h2, a2, b2, h3 = saved[li]
        w = wrest[li]
        dh, dgp, dwpp, gbuf = _ple_bwd(h3, dh, p[li, 0], vec("ple_norm", li), wppt[li], w)
        vecg[li]["ple_norm"] = dgp
        if above is not None:
            dh = reduce_mid(above[1], dh)
        dh_in, da, db, sact, n, dg = _ffn_bwd(h2, dh, a2, b2, vec("ffn2_norm", li), w, 2)
        vecg[li]["ffn2_norm"] = dg
        gbuf = _wgrad([da, db], n, gbuf, ("wg2", "wu2"), 0, F // 2, F // NCHIP)
        gbuf = _wgrad([sact], dh, gbuf, ("wd2",), 0, F // 2, F // NCHIP, yscale=0.5)
        dh = dh_in
        dp, dc1, dzb, small, dpw, gbuf = _mix_b1(dh, z, pooled, c1, cc, pwcat[li], vec("pool_scale", li),
                                                 vec("conv_ln_g", li), vec("conv_ln_b", li), w, gbuf)
        vecg[li]["pool_scale"] = small[0:1]
        vecg[li]["conv_dw_b"] = small[1:2]
        vecg[li]["conv_ln_g"] = small[2:3]
        vecg[li]["conv_ln_b"] = small[3:4]
        dza, dwdw[li] = _mix_b2(dp, dc1, z, wdw[li])
        if above is not None:
            dza, rsum = reduce_end(above[1], dza)
            parts[above[0]].append((0, rsum))
            above = None
        dh_in, u, dg = _mix_b3(h1, dh, dza, dzb, vec("mix_norm", li), w)
        vecg[li]["mix_norm"] = dg
        gbuf = _wgrad([dza], u, gbuf, ("win",), 0, D, D // NCHIP)
        gbuf = _wgrad([dzb], u, gbuf, ("win",), 3 * D, D, D // NCHIP)
        dh = dh_in
        o, s = lay["wpp"]
        small_rows = jnp.concatenate([dwpp.reshape(NCHIP, s, D), dpw.reshape(NCHIP, lay["pw"][1], D)], axis=1)
        gbuf = lax.dynamic_update_slice(gbuf, small_rows.astype(BF), (0, o, 0))
        if li == 0:
            rest = reduce_start(gbuf[:, nfirst:, :], nchain)
            nchain += 1
        dh_in, da, db, sact, n, dg = _ffn_bwd(h0, dh, a1, b1, vec("ffn1_norm", li), wfirst[li], 1)
        vecg[li]["ffn1_norm"] = dg
        if li == 0:
            da = reduce_mid(rest, da)
        gbuf = _wgrad([da, db], n, gbuf, ("wg1", "wu1"), 0, F // 2, F // NCHIP)
        if li == 0:
            ngu = 2 * (F // NCHIP)
            gateup = reduce_start(gbuf[:, :ngu, :], nchain, inline=True)
            nchain += 1
        gbuf = _wgrad([sact], dh, gbuf, ("wd1",), 0, F // 2, F // NCHIP, yscale=0.5)
        dh = dh_in
        if li == 0:
            reduce_mid(gateup, None)
            first = reduce_start(gbuf[:, ngu:nfirst, :], nchain, inline=True)
            nchain += 1
            reduce_mid(first, None)
        else:
            above = (li, reduce_start(gbuf, nchain))
            nchain += 1
    grad_x = dh.reshape(x.shape)
    rsum = _shard_sum(rest["g"], rest["r"], rest["x"], ck)
    gateup["r"], rsum = lax.optimization_barrier((gateup["r"], rsum))
    parts[0].append((nfirst, _sibling_share(rsum, rest["cid"] + 2)))
    updated = {n: None for n in _BIG}
    view = lambda n, a: jnp.swapaxes(a, 1, 2) if n in _TRANSPOSED else a
    for li in reversed(range(1, DEPTH)):
        un = _unpack_shards(parts[li])
        for n in _BIG:
            updated[n] = _adamw_layer(view(n, ws[n]), un[n], view(n, ms[n]), view(n, vs[n]), li, updated[n])
    gateup["r"], updated = lax.optimization_barrier((gateup["r"], updated))
    rsum = _shard_sum(gateup["g"], gateup["r"], gateup["x"], ck)
    first["r"], rsum = lax.optimization_barrier((first["r"], rsum))
    parts[0].append((0, _sibling_share(rsum, gateup["cid"] + 2)))
    parts[0].append((ngu, reduce_end(first, None)[1]))
    un = _unpack_shards(parts[0])
    for n in _BIG:
        updated[n] = _adamw_layer(view(n, ws[n]), un[n], view(n, ms[n]), view(n, vs[n]), 0, updated[n])

    rows = [vecg[li][n] for li in range(DEPTH) for n in _VECS] + [dgf, losscols]
    rows.append(jnp.zeros((-len(rows) % 8, D), F32))
    base = sum(r.shape[0] for r in rows)
    vsum = _allreduce_small(jnp.concatenate(rows + dwdw, axis=0))
    nvec = len(_VECS)
    grads = {}
    for i, n in enumerate(_VECS):
        grads[n] = jnp.stack([vsum[li * nvec + i] for li in range(DEPTH)])
    grads["final_norm"] = vsum[DEPTH * nvec]
    loss = jnp.sum(vsum[DEPTH * nvec + 1])
    dcols = D // NCHIP
    grads["conv_dw_w"] = jnp.stack([
        lax.dynamic_slice(vsum[base + li * HALO: base + li * HALO + kk, :], (0, kme * dcols), (kk, dcols))
        for li in range(DEPTH)])

    outs_g, outs_d, outs_m, outs_v = [], [], [], []
    for n in _WEIGHTS:
        if n in _BIG:
            gq, d, mo, vo = [view(n, t.reshape(view(n, ws[n]).shape)) for t in updated[n]]
        else:
            gq = grads[n]
            d, mo, vo = _adamw(ws[n], gq, ms[n], vs[n])
        outs_g.append(gq)
        outs_d.append(d)
        outs_m.append(mo)
        outs_v.append(vo)
    return (loss, grad_x, *outs_g, *outs_d, *outs_m, *outs_v)


def kernel(x, p, ffn1_norm, ffn1_w_gate, ffn1_w_up, ffn1_w_down, mix_norm, w_in, pool_w, pool_scale, conv_dw_w, conv_dw_b, conv_ln_g, conv_ln_b, conv_w_out, w_out, ffn2_norm, ffn2_w_gate, ffn2_w_up, ffn2_w_down, ple_norm, ple_w_gate, ple_w_proj, final_norm, loss_target, m_ffn1_norm, m_ffn1_w_gate, m_ffn1_w_up, m_ffn1_w_down, m_mix_norm, m_w_in, m_pool_w, m_pool_scale, m_conv_dw_w, m_conv_dw_b, m_conv_ln_g, m_conv_ln_b, m_conv_w_out, m_w_out, m_ffn2_norm, m_ffn2_w_gate, m_ffn2_w_up, m_ffn2_w_down, m_ple_norm, m_ple_w_gate, m_ple_w_proj, m_final_norm, v_ffn1_norm, v_ffn1_w_gate, v_ffn1_w_up, v_ffn1_w_down, v_mix_norm, v_w_in, v_pool_w, v_pool_scale, v_conv_dw_w, v_conv_dw_b, v_conv_ln_g, v_conv_ln_b, v_conv_w_out, v_w_out, v_ffn2_norm, v_ffn2_w_gate, v_ffn2_w_up, v_ffn2_w_down, v_ple_norm, v_ple_w_gate, v_ple_w_proj, v_final_norm):
    ws = dict(zip(_WEIGHTS, (ffn1_norm, ffn1_w_gate, ffn1_w_up, ffn1_w_down, mix_norm, w_in, pool_w, pool_scale, conv_dw_w, conv_dw_b, conv_ln_g, conv_ln_b, conv_w_out, w_out, ffn2_norm, ffn2_w_gate, ffn2_w_up, ffn2_w_down, ple_norm, ple_w_gate, ple_w_proj, final_norm)))
    ms = dict(zip(_WEIGHTS, (m_ffn1_norm, m_ffn1_w_gate, m_ffn1_w_up, m_ffn1_w_down, m_mix_norm, m_w_in, m_pool_w, m_pool_scale, m_conv_dw_w, m_conv_dw_b, m_conv_ln_g, m_conv_ln_b, m_conv_w_out, m_w_out, m_ffn2_norm, m_ffn2_w_gate, m_ffn2_w_up, m_ffn2_w_down, m_ple_norm, m_ple_w_gate, m_ple_w_proj, m_final_norm)))
    vs = dict(zip(_WEIGHTS, (v_ffn1_norm, v_ffn1_w_gate, v_ffn1_w_up, v_ffn1_w_down, v_mix_norm, v_w_in, v_pool_w, v_pool_scale, v_conv_dw_w, v_conv_dw_b, v_conv_ln_g, v_conv_ln_b, v_conv_w_out, v_w_out, v_ffn2_norm, v_ffn2_w_gate, v_ffn2_w_up, v_ffn2_w_down, v_ple_norm, v_ple_w_gate, v_ple_w_proj, v_final_norm)))
    return _step(x, p, loss_target, ws, ms, vs)
```

```python
import jax
import jax.numpy as jnp
from jax import lax
from jax.experimental import pallas as pl
from jax.experimental.pallas import tpu as pltpu
from jax.experimental.pallas import tpu_sc as plsc

T = 8192
D = 1024
F = 2816
PD = 256
NG = 4
WINDOWS = (2, 4, 8, 16)
KC = 31
HALO = 32
DEPTH = 2
NCHIP = 4
RMS_EPS = 1e-6
LN_EPS = 1e-5

ADAM_LR = 0.001
ADAM_B1 = 0.9
ADAM_B2 = 0.999
ADAM_EPS = 1e-08
ADAM_WD = 0.01
ADAM_STEP = 10

TM = 512
TMB = 256
TMW = 1024
MXU_TILE = 256
LANES = 128
SUBLANES = 8
CHUNKS = 4
VMEM_LIMIT = 56 * 1024 * 1024

BF = jnp.bfloat16
F32 = jnp.float32
MESH = pl.DeviceIdType.MESH
ANY = pl.BlockSpec(memory_space=pl.ANY)
HBM = pltpu.MemorySpace.HBM


def _layout():
    fs, ins, ds = F // NCHIP, 5 * D // NCHIP, D // NCHIP
    pps = ds * PD // D
    pws = NG * (D // NG // NCHIP) * (D // NG) // D
    names = [("wg1", fs), ("wu1", fs), ("wd1", fs), ("win", ins), ("wco", ds), ("wo", ds),
             ("wg2", fs), ("wu2", fs), ("wd2", fs), ("wpg", ds), ("wpp", pps), ("pw", pws)]
    off, r = {}, 0
    for n, s in names:
        off[n] = (r, s)
        r += s
    return off, r


def _sig(v):
    return 0.5 * jnp.tanh(0.5 * v) + 0.5


def _dot_nn(a, b):
    return jnp.dot(a, b, preferred_element_type=F32)


def _dot_nt(a, b):
    return lax.dot_general(a, b, (((1,), (1,)), ((), ())), preferred_element_type=F32)


def _dot_tn(a, b):
    return lax.dot_general(a, b, (((0,), (0,)), ((), ())), preferred_element_type=F32)


def _params(sem=("arbitrary",)):
    return pltpu.CompilerParams(dimension_semantics=sem, vmem_limit_bytes=VMEM_LIMIT)


def _first_rows():
    return _layout()[0]["win"][0]


def _weight_copies(w_hbm, specs, sem):
    lay, _ = _layout()
    ra = _first_rows()
    cps = []
    for i, (name, dst) in enumerate(specs):
        off, rs = lay[name]
        off = off if off < ra else off - ra
        for k in range(NCHIP):
            cps.append(pltpu.make_async_copy(w_hbm.at[k, pl.ds(off, rs), :],
                                             dst.at[pl.ds(k * rs, rs), :], sem.at[i * NCHIP + k]))
    return cps


def _load_weights_once(w_hbm, specs, sem):
    @pl.when(pl.program_id(0) == 0)
    def _():
        cps = _weight_copies(w_hbm, specs, sem)
        for cp in cps:
            cp.start()
        for cp in cps:
            cp.wait()


def _grad_copies(stage, g_hbm, name, row0, rows, piece, sem, sem0):
    lay, rtot = _layout()
    off, rs = lay[name]
    nfirst, ngu = _first_rows(), lay["wd1"][0]
    ranges = [(0, rtot), (nfirst, rtot - nfirst), (0, ngu), (ngu, nfirst - ngu)]
    base = [b for b, size in ranges if size == g_hbm.shape[1] and b <= off < b + size][0]
    off -= base
    cps = []
    for i in range(rows // piece):
        rglob = row0 + i * piece
        k = rglob // rs
        loc = rglob - k * rs
        start = off + loc
        if not isinstance(start, int):
            start = pl.multiple_of(start, 16)
        dst = g_hbm.at[k, pl.ds(start, piece), :]
        cps.append(pltpu.make_async_copy(stage.at[pl.ds(i * piece, piece), :], dst, sem.at[sem0 + i]))
    return cps


def _row(v):
    return v.reshape(1, -1)


def _shifted_source(buf, sh, base, s, lanes):
    a, b = divmod(s, SUBLANES)
    rows = pl.ds(pl.multiple_of(base + SUBLANES * a, SUBLANES), SUBLANES)
    if b == 0:
        return buf[rows, lanes]
    return sh[b - 1, rows, :]


def _fill_shifted(buf, sh, lanes):
    rows = sh.shape[1]
    for b in range(1, SUBLANES):
        sh[b - 1, :, :] = buf[pl.ds(b, rows), lanes]


def _hidden_chunks():
    tiles = F // MXU_TILE
    first = (tiles + 1) // 2 * MXU_TILE if F % MXU_TILE == 0 else F // 2
    return [pl.ds(0, first), pl.ds(first, F - first)]


def _ffn_fwd(h, g, wfull, which):
    nt = T // TM
    names = ("wg%d" % which, "wu%d" % which, "wd%d" % which)

    def body(h_ref, g_ref, w_hbm, ho_ref, a_ref, b_ref, wg, wu, wd, sem):
        _load_weights_once(w_hbm, ((names[0], wg), (names[1], wu), (names[2], wd)), sem)
        x = h_ref[...]
        r = lax.rsqrt(jnp.mean(x * x, axis=-1, keepdims=True) + RMS_EPS)
        n = (x * r * g_ref[...]).astype(BF)
        acc = jnp.zeros((TM, D), F32)
        for sl in _hidden_chunks():
            a = _dot_nt(n, wg[sl, :])
            b = _dot_nt(n, wu[sl, :])
            a_ref[:, sl] = a.astype(BF)
            b_ref[:, sl] = b.astype(BF)
            s = (a * _sig(a) * b).astype(BF)
            acc = acc + _dot_nn(s, wd[sl, :])
        ho_ref[...] = x + 0.5 * acc

    tile = lambda w: pl.BlockSpec((TM, w), lambda i: (i, 0))
    return pl.pallas_call(
        body, name="ffn_fwd", grid=(nt,),
        in_specs=[tile(D), pl.BlockSpec((1, D), lambda i: (0, 0)), ANY],
        out_specs=[tile(D), tile(F), tile(F)],
        out_shape=[jax.ShapeDtypeStruct((T, D), F32), jax.ShapeDtypeStruct((T, F), BF),
                   jax.ShapeDtypeStruct((T, F), BF)],
        scratch_shapes=[pltpu.VMEM((F, D), BF), pltpu.VMEM((F, D), BF), pltpu.VMEM((F, D), BF),
                        pltpu.SemaphoreType.DMA((3 * NCHIP,))],
        compiler_params=_params(),
    )(h, g, wfull)


def _mix_in_fwd(h, g, wfull):
    tm = TM
    nt = T // tm
    nin = 5 * D

    def body(h_ref, g_ref, w_hbm, z_ref, win, sem):
        _load_weights_once(w_hbm, (("win", win),), sem)
        x = h_ref[...]
        r = lax.rsqrt(jnp.mean(x * x, axis=-1, keepdims=True) + RMS_EPS)
        u = (x * r * g_ref[...]).astype(BF)
        for c in range(5):
            sl = pl.ds(c * D, D)
            z_ref[:, sl] = _dot_nt(u, win[sl, :]).astype(BF)

    return pl.pallas_call(
        body, name="mix_in_fwd", grid=(nt,),
        in_specs=[pl.BlockSpec((tm, D), lambda i: (i, 0)), pl.BlockSpec((1, D), lambda i: (0, 0)), ANY],
        out_specs=pl.BlockSpec((tm, nin), lambda i: (i, 0)),
        out_shape=jax.ShapeDtypeStruct((T, nin), BF),
        scratch_shapes=[pltpu.VMEM((nin, D), BF), pltpu.SemaphoreType.DMA((NCHIP,))],
        compiler_params=_params(),
    )(h, g, wfull)


def _pool_counts(i, rows, w):
    t = i * TM + lax.broadcasted_iota(jnp.int32, (rows, 1), 0)
    return jnp.minimum(t + 1, w).astype(F32)


def _mix_mid_fwd(h, z, pwcat, pscale, wdw, bdw, lg, lb, wfull):
    nt = T // TM
    pg = D // NG
    hb = TM // HALO
    n_ext = HALO + TM
    pad = SUBLANES

    def body(h_ref, z_ref, zh_ref, pw_ref, ps_ref, wdw_ref, bdw_ref, lg_ref, lb_ref, w_hbm,
             h2_ref, p_ref, c1_ref, cc_ref, wco, wo, pa, pb, cbuf, sh, c1buf, ambuf, sem):
        i = pl.program_id(0)
        _load_weights_once(w_hbm, (("wco", wco), ("wo", wo)), sem)

        @pl.when(i == 0)
        def _():
            pa[0:pad, :] = jnp.zeros((pad, D), F32)
            pb[0:pad, :] = jnp.zeros((pad, D), F32)

        keep = (i > 0).astype(F32)
        zh = zh_ref[...].astype(F32) * keep
        za = z_ref[:, D:2 * D].astype(F32)
        zg = z_ref[:, 2 * D:3 * D].astype(F32)
        pa[pad:pad + HALO, :] = zh[:, 0:D]
        pa[pad + HALO:pad + n_ext, :] = z_ref[:, 0:D].astype(F32)
        cbuf[0:HALO, :] = zh[:, D:2 * D] * _sig(zh[:, 2 * D:3 * D])
        cbuf[HALO:n_ext, :] = za * _sig(zg)
        for g, w in enumerate(WINDOWS):
            ls = pl.ds(g * pg, pg)
            cur, nxt = pa, pb
            d = 1
            while d < w:
                nxt[pl.ds(pad, n_ext), ls] = cur[pl.ds(pad, n_ext), ls] + cur[pl.ds(pad - d, n_ext), ls]
                cur, nxt = nxt, cur
                d *= 2
            tok = z_ref[:, ls].astype(F32)
            pooled = (cur[pl.ds(pad + HALO, TM), ls] / _pool_counts(i, TM, w) - tok).astype(BF)
            p_ref[:, ls] = pooled
            ambuf[:, ls] = _dot_nn(pooled, pw_ref[:, ls])
        am = ambuf[...] * ps_ref[...]
        for l in range(D // LANES):
            lanes = pl.ds(l * LANES, LANES)
            _fill_shifted(cbuf, sh, lanes)
            bias = jnp.broadcast_to(bdw_ref[:, lanes], (SUBLANES, LANES))

            def conv_rows(r, carry):
                base = r * (CHUNKS * SUBLANES)
                accs = [bias] * CHUNKS
                for k in range(KC):
                    wk = jnp.broadcast_to(wdw_ref[k:k + 1, lanes], (SUBLANES, LANES))
                    for j in range(CHUNKS):
                        src = _shifted_source(cbuf, sh, base + j * SUBLANES, HALO - (KC - 1) + k, lanes)
                        accs[j] = accs[j] + wk * src
                for j in range(CHUNKS):
                    c1buf[pl.ds(pl.multiple_of(base + j * SUBLANES, SUBLANES), SUBLANES), lanes] = accs[j]
                return carry

            lax.fori_loop(0, TM // (CHUNKS * SUBLANES), conv_rows, 0)
        c1b = c1buf[...].astype(BF)
        c1_ref[...] = c1b
        c1 = c1b.astype(F32)
        mu = jnp.mean(c1, axis=-1, keepdims=True)
        xc = c1 - mu
        var = jnp.mean(xc * xc, axis=-1, keepdims=True)
        c2 = xc * lax.rsqrt(var + LN_EPS) * lg_ref[...] + lb_ref[...]
        c3 = (c2 * _sig(c2)).astype(BF)
        ccb = _dot_nn(c3, wco[...]).astype(BF)
        cc_ref[...] = ccb
        gp = z_ref[:, 3 * D:4 * D].astype(F32)
        gc = z_ref[:, 4 * D:5 * D].astype(F32)
        m = (_sig(gp) * am + _sig(gc) * ccb.astype(F32)).astype(BF)
        h2_ref[...] = h_ref[...] + _dot_nn(m, wo[...])

    tile = pl.BlockSpec((TM, D), lambda i: (i, 0))
    vec = pl.BlockSpec((1, D), lambda i: (0, 0))
    return pl.pallas_call(
        body, name="mix_mid_fwd", grid=(nt,),
        in_specs=[tile, pl.BlockSpec((TM, 5 * D), lambda i: (i, 0)),
                  pl.BlockSpec((HALO, 3 * D), lambda i: (jnp.maximum(i * hb - 1, 0), 0)),
                  pl.BlockSpec((pg, D), lambda i: (0, 0)), vec,
                  pl.BlockSpec((HALO, D), lambda i: (0, 0)), vec, vec, vec, ANY],
        out_specs=[tile, tile, tile, tile],
        out_shape=[jax.ShapeDtypeStruct((T, D), F32), jax.ShapeDtypeStruct((T, D), BF),
                   jax.ShapeDtypeStruct((T, D), BF), jax.ShapeDtypeStruct((T, D), BF)],
        scratch_shapes=[pltpu.VMEM((D, D), BF), pltpu.VMEM((D, D), BF),
                        pltpu.VMEM((pad + n_ext, D), F32), pltpu.VMEM((pad + n_ext, D), F32),
                        pltpu.VMEM((n_ext, D), F32), pltpu.VMEM((SUBLANES - 1, n_ext - SUBLANES, LANES), F32),
                        pltpu.VMEM((TM, D), F32), pltpu.VMEM((TM, D), F32),
                        pltpu.SemaphoreType.DMA((2 * NCHIP,))],
        compiler_params=_params(),
    )(h, z, z, pwcat, pscale, wdw, bdw, lg, lb, wfull)


def _ple_fwd(h, p, g, wppt, wfull):
    nt = T // TM

    def body(h_ref, p_ref, g_ref, wpp_ref, w_hbm, ho_ref, wpg, sem):
        _load_weights_once(w_hbm, (("wpg", wpg),), sem)
        x = h_ref[...]
        r = lax.rsqrt(jnp.mean(x * x, axis=-1, keepdims=True) + RMS_EPS)
        n = (x * r * g_ref[...]).astype(BF)
        gate = _sig(_dot_nn(n, wpg[...]))
        pe = _dot_nt(p_ref[...].astype(BF), wpp_ref[...])
        ho_ref[...] = x + gate * pe

    tile = pl.BlockSpec((TM, D), lambda i: (i, 0))
    return pl.pallas_call(
        body, name="ple_fwd", grid=(nt,),
        in_specs=[tile, pl.BlockSpec((TM, PD), lambda i: (i, 0)), pl.BlockSpec((1, D), lambda i: (0, 0)),
                  pl.BlockSpec((D, PD), lambda i: (0, 0)), ANY],
        out_specs=tile, out_shape=jax.ShapeDtypeStruct((T, D), F32),
        scratch_shapes=[pltpu.VMEM((D, D), BF), pltpu.SemaphoreType.DMA((NCHIP,))],
        compiler_params=_params(),
    )(h, p, g, wppt, wfull)


def _loss_bwd(h, tgt, g):
    nt = T // TM

    def body(h_ref, t_ref, g_ref, dh_ref, loss_ref, dg_ref):
        @pl.when(pl.program_id(0) == 0)
        def _():
            loss_ref[...] = jnp.zeros_like(loss_ref)
            dg_ref[...] = jnp.zeros_like(dg_ref)
        x = h_ref[...]
        r = lax.rsqrt(jnp.mean(x * x, axis=-1, keepdims=True) + RMS_EPS)
        xh = x * r
        gv = g_ref[...]
        e = xh * gv - t_ref[...]
        loss_ref[...] += jnp.sum(e * e, axis=0, keepdims=True) * (0.5 / D)
        dy = e * (1.0 / D)
        dg_ref[...] += jnp.sum(dy * xh, axis=0, keepdims=True)
        dxh = dy * gv
        dh_ref[...] = r * (dxh - xh * jnp.mean(dxh * xh, axis=-1, keepdims=True))

    tile = pl.BlockSpec((TM, D), lambda i: (i, 0))
    vec = pl.BlockSpec((1, D), lambda i: (0, 0))
    return pl.pallas_call(
        body, name="loss_bwd", grid=(nt,), in_specs=[tile, tile, vec], out_specs=[tile, vec, vec],
        out_shape=[jax.ShapeDtypeStruct((T, D), F32), jax.ShapeDtypeStruct((1, D), F32),
                   jax.ShapeDtypeStruct((1, D), F32)],
        compiler_params=_params(),
    )(h, tgt, g)


def _ple_bwd(h, dh, p, g, wppt, wfull, rtot):
    nt = T // TM

    def body(h_ref, dh_ref, p_ref, g_ref, wpp_ref, w_hbm,
             dho_ref, dg_ref, dwpp_ref, g_out, wpg, acc, stage, sem, osem):
        i = pl.program_id(0)
        _load_weights_once(w_hbm, (("wpg", wpg),), sem)

        @pl.when(i == 0)
        def _():
            dg_ref[...] = jnp.zeros_like(dg_ref)
            dwpp_ref[...] = jnp.zeros_like(dwpp_ref)
            acc[...] = jnp.zeros_like(acc)

        x = h_ref[...]
        r = lax.rsqrt(jnp.mean(x * x, axis=-1, keepdims=True) + RMS_EPS)
        xh = x * r
        gv = g_ref[...]
        n = (xh * gv).astype(BF)
        gate = _sig(_dot_nn(n, wpg[...]))
        pb = p_ref[...].astype(BF)
        pe = _dot_nt(pb, wpp_ref[...])
        d = dh_ref[...]
        dpe = (d * gate).astype(BF)
        dq = (d * pe * gate * (1.0 - gate)).astype(BF)
        dwpp_ref[...] += _dot_tn(dpe, pb)
        acc[...] += _dot_tn(n, dq)
        dn = _dot_nt(dq, wpg[...])
        dg_ref[...] += jnp.sum(dn * xh, axis=0, keepdims=True)
        dxh = dn * gv
        dho_ref[...] = d + r * (dxh - xh * jnp.mean(dxh * xh, axis=-1, keepdims=True))

        @pl.when(i == nt - 1)
        def _():
            stage[...] = acc[...].astype(BF)
            cps = _grad_copies(stage, g_out, "wpg", 0, D, D // NCHIP, osem, 0)
            for cp in cps:
                cp.start()
            for cp in cps:
                cp.wait()

    tile = pl.BlockSpec((TM, D), lambda i: (i, 0))
    vec = pl.BlockSpec((1, D), lambda i: (0, 0))
    return pl.pallas_call(
        body, name="ple_bwd", grid=(nt,),
        in_specs=[tile, tile, pl.BlockSpec((TM, PD), lambda i: (i, 0)), vec,
                  pl.BlockSpec((D, PD), lambda i: (0, 0)), ANY],
        out_specs=[tile, vec, pl.BlockSpec((D, PD), lambda i: (0, 0)), ANY],
        out_shape=[jax.ShapeDtypeStruct((T, D), F32), jax.ShapeDtypeStruct((1, D), F32),
                   jax.ShapeDtypeStruct((D, PD), F32),
                   jax.ShapeDtypeStruct((NCHIP, rtot, D), BF)],
        scratch_shapes=[pltpu.VMEM((D, D), BF), pltpu.VMEM((D, D), F32), pltpu.VMEM((D, D), BF),
                        pltpu.SemaphoreType.DMA((NCHIP,)), pltpu.SemaphoreType.DMA((NCHIP,))],
        compiler_params=_params(),
    )(h, dh, p, g, wppt, wfull)


def _ffn_bwd(h, dh, a, b, g, wfull, which):
    tm = TMB
    nt = T // tm
    names = ("wg%d" % which, "wu%d" % which, "wd%d" % which)

    def body(h_ref, dh_ref, a_ref, b_ref, g_ref, w_hbm,
             dho_ref, da_ref, db_ref, s_ref, n_ref, dg_ref, wg, wu, wd, sem):
        _load_weights_once(w_hbm, ((names[0], wg), (names[1], wu), (names[2], wd)), sem)

        @pl.when(pl.program_id(0) == 0)
        def _():
            dg_ref[...] = jnp.zeros_like(dg_ref)

        x = h_ref[...]
        r = lax.rsqrt(jnp.mean(x * x, axis=-1, keepdims=True) + RMS_EPS)
        xh = x * r
        gv = g_ref[...]
        n_ref[...] = (xh * gv).astype(BF)
        d = dh_ref[...]
        df = (0.5 * d).astype(BF)
        dn = jnp.zeros((tm, D), F32)
        for sl in _hidden_chunks():
            av = a_ref[:, sl].astype(F32)
            bv = b_ref[:, sl].astype(F32)
            ds = _dot_nt(df, wd[sl, :])
            sg = _sig(av)
            sil = av * sg
            s_ref[:, sl] = (sil * bv).astype(BF)
            da = (ds * bv * (sg * (1.0 + av * (1.0 - sg)))).astype(BF)
            db = (ds * sil).astype(BF)
            da_ref[:, sl] = da
            db_ref[:, sl] = db
            dn = dn + _dot_nn(da, wg[sl, :]) + _dot_nn(db, wu[sl, :])
        dg_ref[...] += jnp.sum(dn * xh, axis=0, keepdims=True)
        dxh = dn * gv
        dho_ref[...] = d + r * (dxh - xh * jnp.mean(dxh * xh, axis=-1, keepdims=True))

    tile = lambda w: pl.BlockSpec((tm, w), lambda i: (i, 0))
    vec = pl.BlockSpec((1, D), lambda i: (0, 0))
    return pl.pallas_call(
        body, name="ffn_bwd", grid=(nt,),
        in_specs=[tile(D), tile(D), tile(F), tile(F), vec, ANY],
        out_specs=[tile(D), tile(F), tile(F), tile(F), tile(D), vec],
        out_shape=[jax.ShapeDtypeStruct((T, D), F32), jax.ShapeDtypeStruct((T, F), BF),
                   jax.ShapeDtypeStruct((T, F), BF), jax.ShapeDtypeStruct((T, F), BF),
                   jax.ShapeDtypeStruct((T, D), BF), jax.ShapeDtypeStruct((1, D), F32)],
        scratch_shapes=[pltpu.VMEM((F, D), BF), pltpu.VMEM((F, D), BF), pltpu.VMEM((F, D), BF),
                        pltpu.SemaphoreType.DMA((3 * NCHIP,))],
        compiler_params=_params(),
    )(h, dh, a, b, g, wfull)


def _wgrad(xs, y, gbuf, names, row0, rb, piece, yscale=None, new_rows=None):
    nx = len(xs)
    rx = xs[0].shape[1]
    nj = rx // rb
    nt = T // TMW
    npiece = rb // piece
    na = 0 if gbuf is None else 1
    gshape = (NCHIP, new_rows, D) if gbuf is None else gbuf.shape

    def body(*refs):
        x_refs = refs[:nx]
        y_ref = refs[nx]
        g_out = refs[nx + 1 + na]
        accs = refs[nx + 2 + na:2 * nx + 2 + na]
        stages = refs[2 * nx + 2 + na:3 * nx + 2 + na]
        osem = refs[3 * nx + 2 + na]
        j = pl.program_id(0)
        t = pl.program_id(1)

        @pl.when(t == 0)
        def _():
            for acc in accs:
                acc[...] = jnp.zeros_like(acc)

        yv = y_ref[...]
        if yscale is not None:
            yv = (yscale * yv).astype(BF)
        for x_ref, acc in zip(x_refs, accs):
            acc[...] += _dot_tn(x_ref[...], yv)

        @pl.when(t == nt - 1)
        def _():
            cps = []
            for xi in range(nx):
                stages[xi][...] = accs[xi][...].astype(BF)
                cps += _grad_copies(stages[xi], g_out, names[xi], row0 + j * rb, rb, piece,
                                    osem, xi * npiece)
            for cp in cps:
                cp.start()
            for cp in cps:
                cp.wait()

    in_specs = [pl.BlockSpec((TMW, rb), lambda j, t: (t, j)) for _ in xs]
    in_specs += [pl.BlockSpec((TMW, D), lambda j, t: (t, 0))] + [ANY] * na
    return pl.pallas_call(
        body, name="wgrad", grid=(nj, nt), in_specs=in_specs, out_specs=ANY,
        out_shape=jax.ShapeDtypeStruct(gshape, BF),
        scratch_shapes=([pltpu.VMEM((rb, D), F32) for _ in xs] + [pltpu.VMEM((rb, D), BF) for _ in xs]
                        + [pltpu.SemaphoreType.DMA((nx * npiece,))]),
        input_output_aliases={nx + 1: 0} if na else {},
        compiler_params=_params(("arbitrary", "arbitrary")),
    )(*xs, y, *([gbuf] if na else []))


def _mix_b1(dh, z, pooled, c1, cc, pwcat, pscale, lg, lb, wfull, gbuf):
    nt = T // TM
    pg = D // NG

    def body(dh_ref, gp_ref, gc_ref, p_ref, c1_ref, cc_ref, pw_ref, ps_ref, lg_ref, lb_ref, w_hbm, _g_in,
             dp_ref, dc1_ref, dzb_ref, small_ref, dpw_ref, g_out,
             wco, wo, acc_o, acc_co, stage_o, stage_co, qbuf, sem, osem):
        i = pl.program_id(0)
        _load_weights_once(w_hbm, (("wco", wco), ("wo", wo)), sem)

        @pl.when(i == 0)
        def _():
            small_ref[...] = jnp.zeros_like(small_ref)
            dpw_ref[...] = jnp.zeros_like(dpw_ref)
            acc_o[...] = jnp.zeros_like(acc_o)
            acc_co[...] = jnp.zeros_like(acc_co)

        dhb = dh_ref[...].astype(BF)
        dm = _dot_nt(dhb, wo[...])
        sp = _sig(gp_ref[...].astype(F32))
        sc = _sig(gc_ref[...].astype(F32))
        for g in range(NG):
            ls = pl.ds(g * pg, pg)
            qbuf[:, ls] = _dot_nn(p_ref[:, ls], pw_ref[:, ls])
        q = qbuf[...]
        psv = ps_ref[...]
        am = q * psv
        ccv = cc_ref[...].astype(F32)
        m = (sp * am + sc * ccv).astype(BF)
        acc_o[...] += _dot_tn(m, dhb)
        dam = dm * sp
        dzb_ref[:, 0:D] = (dm * am * sp * (1.0 - sp)).astype(BF)
        dccb = (dm * sc).astype(BF)
        dzb_ref[:, D:2 * D] = (dm * ccv * sc * (1.0 - sc)).astype(BF)
        small_ref[0:1, :] += jnp.sum(dam * q, axis=0, keepdims=True)
        dq = (dam * psv).astype(BF)
        for g in range(NG):
            ls = pl.ds(g * pg, pg)
            dqg = dq[:, g * pg:(g + 1) * pg]
            dp_ref[:, ls] = _dot_nt(dqg, pw_ref[:, ls]).astype(BF)
            dpw_ref[:, ls] += _dot_tn(p_ref[:, ls], dqg)
        c1v = c1_ref[...].astype(F32)
        mu = jnp.mean(c1v, axis=-1, keepdims=True)
        xc = c1v - mu
        var = jnp.mean(xc * xc, axis=-1, keepdims=True)
        rs = lax.rsqrt(var + LN_EPS)
        c2n = xc * rs
        lgv = lg_ref[...]
        c2 = c2n * lgv + lb_ref[...]
        sg2 = _sig(c2)
        c3 = (c2 * sg2).astype(BF)
        acc_co[...] += _dot_tn(c3, dccb)
        dc3 = _dot_nt(dccb, wco[...])
        dc2 = dc3 * (sg2 * (1.0 + c2 * (1.0 - sg2)))
        small_ref[2:3, :] += jnp.sum(dc2 * c2n, axis=0, keepdims=True)
        small_ref[3:4, :] += jnp.sum(dc2, axis=0, keepdims=True)
        dc2n = dc2 * lgv
        dc1 = rs * (dc2n - jnp.mean(dc2n, axis=-1, keepdims=True)
                    - c2n * jnp.mean(dc2n * c2n, axis=-1, keepdims=True))
        small_ref[1:2, :] += jnp.sum(dc1, axis=0, keepdims=True)
        dc1_ref[...] = dc1.astype(BF)

        @pl.when(i == nt - 1)
        def _():
            stage_o[...] = acc_o[...].astype(BF)
            stage_co[...] = acc_co[...].astype(BF)
            cps = _grad_copies(stage_o, g_out, "wo", 0, D, D // NCHIP, osem, 0)
            cps += _grad_copies(stage_co, g_out, "wco", 0, D, D // NCHIP, osem, NCHIP)
            for cp in cps:
                cp.start()
            for cp in cps:
                cp.wait()

    tile = pl.BlockSpec((TM, D), lambda i: (i, 0))
    vec = pl.BlockSpec((1, D), lambda i: (0, 0))
    full = lambda r: pl.BlockSpec((r, D), lambda i: (0, 0))
    return pl.pallas_call(
        body, name="mix_b1", grid=(nt,),
        in_specs=[tile, pl.BlockSpec((TM, D), lambda i: (i, 3)), pl.BlockSpec((TM, D), lambda i: (i, 4)),
                  tile, tile, tile, full(pg), vec, vec, vec, ANY, ANY],
        out_specs=[tile, tile, pl.BlockSpec((TM, 2 * D), lambda i: (i, 0)), full(8), full(pg), ANY],
        out_shape=[jax.ShapeDtypeStruct((T, D), BF), jax.ShapeDtypeStruct((T, D), BF),
                   jax.ShapeDtypeStruct((T, 2 * D), BF), jax.ShapeDtypeStruct((8, D), F32),
                   jax.ShapeDtypeStruct((pg, D), F32), jax.ShapeDtypeStruct(gbuf.shape, gbuf.dtype)],
        scratch_shapes=[pltpu.VMEM((D, D), BF), pltpu.VMEM((D, D), BF),
                        pltpu.VMEM((D, D), F32), pltpu.VMEM((D, D), F32),
                        pltpu.VMEM((D, D), BF), pltpu.VMEM((D, D), BF),
                        pltpu.VMEM((TM, D), F32),
                        pltpu.SemaphoreType.DMA((2 * NCHIP,)), pltpu.SemaphoreType.DMA((2 * NCHIP,))],
        input_output_aliases={11: 5},
        compiler_params=_params(),
    )(dh, z, z, pooled, c1, cc, pwcat, pscale, lg, lb, wfull, gbuf)


def _mix_b2(dp, dc1, z, wdw):
    nt = T // TM
    pg = D // NG
    hb = TM // HALO
    nhb = T // HALO
    n_ext = TM + HALO
    pad = SUBLANES

    def body(dp_ref, dpn_ref, dc_ref, dcn_ref, za_ref, zg_ref, wdw_ref,
             dza_ref, dw_ref, pa, pb, cbuf, sh, c0buf, dc0buf):
        i = pl.program_id(0)

        @pl.when(i == 0)
        def _():
            dw_ref[...] = jnp.zeros_like(dw_ref)
            pa[n_ext:n_ext + pad, :] = jnp.zeros((pad, D), F32)
            pb[n_ext:n_ext + pad, :] = jnp.zeros((pad, D), F32)

        more = (i < nt - 1).astype(F32)
        for g, w in enumerate(WINDOWS):
            ls = pl.ds(g * pg, pg)
            cur_dp = dp_ref[:, ls].astype(F32)
            pa[0:TM, ls] = cur_dp / _pool_counts(i, TM, w)
            pa[TM:n_ext, ls] = dpn_ref[:, ls].astype(F32) * (more / w)
            cur, nxt = pa, pb
            d = 1
            while d < w:
                nxt[pl.ds(0, n_ext), ls] = cur[pl.ds(0, n_ext), ls] + cur[pl.ds(d, n_ext), ls]
                cur, nxt = nxt, cur
                d *= 2
            dza_ref[:, ls] = (cur[pl.ds(0, TM), ls] - cur_dp).astype(BF)
        za = za_ref[...].astype(F32)
        sg = _sig(zg_ref[...].astype(F32))
        cbuf[0:TM, :] = dc_ref[...].astype(F32)
        cbuf[TM:n_ext, :] = dcn_ref[...].astype(F32) * more
        c0buf[...] = za * sg
        for l in range(D // LANES):
            lanes = pl.ds(l * LANES, LANES)
            _fill_shifted(cbuf, sh, lanes)

            def conv_rows(r, accs):
                base = r * (CHUNKS * SUBLANES)
                rows = [pl.ds(pl.multiple_of(base + j * SUBLANES, SUBLANES), SUBLANES) for j in range(CHUNKS)]
                c0v = [c0buf[rows[j], lanes] for j in range(CHUNKS)]
                acc = [jnp.zeros((SUBLANES, LANES), F32)] * CHUNKS
                new = list(accs)
                for k in range(KC):
                    wk = jnp.broadcast_to(wdw_ref[k:k + 1, lanes], (SUBLANES, LANES))
                    for j in range(CHUNKS):
                        src = _shifted_source(cbuf, sh, base + j * SUBLANES, KC - 1 - k, lanes)
                        acc[j] = acc[j] + wk * src
                        new[k] = new[k] + c0v[j] * src
                for j in range(CHUNKS):
                    dc0buf[rows[j], lanes] = acc[j]
                return tuple(new)

            init = tuple(jnp.zeros((SUBLANES, LANES), F32) for _ in range(KC))
            accs = lax.fori_loop(0, TM // (CHUNKS * SUBLANES), conv_rows, init)
            for k in range(KC):
                dw_ref[k:k + 1, lanes] += jnp.sum(accs[k], axis=0, keepdims=True)
        dc0 = dc0buf[...]
        dza_ref[:, D:2 * D] = (dc0 * sg).astype(BF)
        dza_ref[:, 2 * D:3 * D] = (dc0 * za * sg * (1.0 - sg)).astype(BF)

    tile = pl.BlockSpec((TM, D), lambda i: (i, 0))
    nxt_spec = pl.BlockSpec((HALO, D), lambda i: (jnp.minimum((i + 1) * hb, nhb - 1), 0))
    return pl.pallas_call(
        body, name="mix_b2", grid=(nt,),
        in_specs=[tile, nxt_spec, tile, nxt_spec, pl.BlockSpec((TM, D), lambda i: (i, 1)),
                  pl.BlockSpec((TM, D), lambda i: (i, 2)), pl.BlockSpec((HALO, D), lambda i: (0, 0))],
        out_specs=[pl.BlockSpec((TM, 3 * D), lambda i: (i, 0)), pl.BlockSpec((HALO, D), lambda i: (0, 0))],
        out_shape=[jax.ShapeDtypeStruct((T, 3 * D), BF), jax.ShapeDtypeStruct((HALO, D), F32)],
        scratch_shapes=[pltpu.VMEM((n_ext + pad, D), F32), pltpu.VMEM((n_ext + pad, D), F32),
                        pltpu.VMEM((n_ext, D), F32), pltpu.VMEM((SUBLANES - 1, n_ext - SUBLANES, LANES), F32),
                        pltpu.VMEM((TM, D), F32), pltpu.VMEM((TM, D), F32)],
        compiler_params=_params(),
    )(dp, dp, dc1, dc1, z, z, wdw)


def _mix_b3(h, dh, dza, dzb, g, wfull):
    nt = T // TM

    def body(h_ref, dh_ref, dza_ref, dzb_ref, g_ref, w_hbm, dho_ref, u_ref, dg_ref, win, sem):
        _load_weights_once(w_hbm, (("win", win),), sem)

        @pl.when(pl.program_id(0) == 0)
        def _():
            dg_ref[...] = jnp.zeros_like(dg_ref)

        x = h_ref[...]
        r = lax.rsqrt(jnp.mean(x * x, axis=-1, keepdims=True) + RMS_EPS)
        xh = x * r
        gv = g_ref[...]
        u_ref[...] = (xh * gv).astype(BF)
        du = _dot_nn(dza_ref[...], win[0:3 * D, :]) + _dot_nn(dzb_ref[...], win[3 * D:5 * D, :])
        dg_ref[...] += jnp.sum(du * xh, axis=0, keepdims=True)
        dxh = du * gv
        dho_ref[...] = dh_ref[...] + r * (dxh - xh * jnp.mean(dxh * xh, axis=-1, keepdims=True))

    tile = lambda w: pl.BlockSpec((TM, w), lambda i: (i, 0))
    vec = pl.BlockSpec((1, D), lambda i: (0, 0))
    return pl.pallas_call(
        body, name="mix_b3", grid=(nt,),
        in_specs=[tile(D), tile(D), tile(3 * D), tile(2 * D), vec, ANY],
        out_specs=[tile(D), tile(D), vec],
        out_shape=[jax.ShapeDtypeStruct((T, D), F32), jax.ShapeDtypeStruct((T, D), BF),
                   jax.ShapeDtypeStruct((1, D), F32)],
        scratch_shapes=[pltpu.VMEM((5 * D, D), BF), pltpu.SemaphoreType.DMA((NCHIP,))],
        compiler_params=_params(),
    )(h, dh, dza, dzb, g, wfull)


def _mesh_pos():
    x, y, c = lax.axis_index("x"), lax.axis_index("y"), lax.axis_index("c")
    chips = [(1 - x, y), (x, 1 - y), (1 - x, 1 - y)]
    return x, y, c, 2 * x + y, chips


def _handshake(peers):
    barrier = pltpu.get_barrier_semaphore()
    for peer in peers:
        pl.semaphore_signal(barrier, inc=1, device_id=peer, device_id_type=MESH)
    pl.semaphore_wait(barrier, len(peers))


def _run_comm(body, name, cid, ins, inouts, out_types, scratch):
    in_refs = [jax.new_ref(a, memory_space=HBM) for a in ins]
    inout_refs = [jax.new_ref(a, memory_space=HBM) for a in inouts]
    out_refs = [jax.empty_ref(t, memory_space=HBM) for t in out_types]

    @pl.kernel(mesh=plsc.ScalarSubcoreMesh(axis_name="seq", num_cores=1), name=name,
               scratch_types=scratch, compiler_params=pltpu.CompilerParams(collective_id=cid))
    def launch(*scr):
        body(*in_refs, *inout_refs, *out_refs, *scr)

    launch()
    return [r[...] for r in inout_refs + out_refs]


def _remote(src, dst, send_sem, recv_sem, to):
    return pltpu.make_async_remote_copy(src_ref=src, dst_ref=dst, send_sem=send_sem, recv_sem=recv_sem,
                                        device_id=to, device_id_type=MESH)


def _gather_layer(packed, cid):
    rtot = packed.shape[0]
    half = rtot // 2
    kme = 2 * lax.axis_index("x") + lax.axis_index("y")
    landing = lax.dynamic_update_slice(lax.empty((NCHIP, rtot, D), BF), packed[None], (kme, 0, 0))

    def body(p_ref, w_ref, send_sems, recv_sems):
        x, y, c, kme, chips = _mesh_pos()
        sib = (x, y, 1 - c)
        _handshake([(*ch, c) for ch in chips] + [sib])
        ks = [2 * cx + cy for cx, cy in chips]
        mine = pl.ds(pl.multiple_of(c * half, 16), half)
        other = pl.ds(pl.multiple_of((1 - c) * half, 16), half)
        first = [_remote(p_ref.at[mine], w_ref.at[kme, mine], send_sems.at[j], recv_sems.at[j], (*chips[j], c))
                 for j in range(3)]
        for cp in first:
            cp.start()
        passed = [_remote(w_ref.at[ks[j], mine], w_ref.at[ks[j], mine], send_sems.at[3 + j], recv_sems.at[3 + j], sib)
                  for j in range(3)]
        for j in range(3):
            _remote(p_ref.at[mine], w_ref.at[ks[j], mine], send_sems.at[j], recv_sems.at[j], sib).wait_recv()
            passed[j].start()
        for j in range(3):
            _remote(p_ref.at[mine], w_ref.at[ks[j], other], send_sems.at[3 + j], recv_sems.at[3 + j], sib).wait_recv()
        for cp in first + passed:
            cp.wait_send()

    return _run_comm(body, "gather_layer_%d" % cid, cid, [packed], [landing], [],
                     (pltpu.SemaphoreType.DMA((6,)), pltpu.SemaphoreType.DMA((6,))))[0]


def _sibling_swap(gbuf, cid):
    rtot = gbuf.shape[1]
    half = rtot // 2

    def body(g_ref, r_ref, send_sem, recv_sem):
        x, y, c, _, _ = _mesh_pos()
        sib = (x, y, 1 - c)
        _handshake([sib])
        other = pl.ds(pl.multiple_of((1 - c) * half, 16), half)
        cp = _remote(g_ref.at[:, other, :], r_ref, send_sem, recv_sem, sib)
        cp.start()
        cp.wait()

    return _run_comm(body, "sibling_swap_%d" % cid, cid, [gbuf], [],
                     [jax.ShapeDtypeStruct((NCHIP, half, D), BF)],
                     (pltpu.SemaphoreType.DMA, pltpu.SemaphoreType.DMA))[0]


def _sibling_swap_inline(gbuf):
    rtot = gbuf.shape[1]
    half = rtot // 2

    def body(g_ref, r_ref, send_sem, recv_sem):
        x, y, c, _, _ = _mesh_pos()
        other = pl.ds(pl.multiple_of((1 - c) * half, 16), half)
        cp = _remote(g_ref.at[:, other, :], r_ref, send_sem, recv_sem, (x, y, 1 - c))
        cp.start()
        cp.wait()

    return pl.pallas_call(
        body, name="sibling_swap_inline", in_specs=[ANY], out_specs=ANY,
        out_shape=jax.ShapeDtypeStruct((NCHIP, half, D), BF),
        scratch_shapes=[pltpu.SemaphoreType.DMA, pltpu.SemaphoreType.DMA],
    )(gbuf)


def _row_tile(rows):
    for cand in range(min(rows, 1280) // 16 * 16, 0, -16):
        if rows % cand == 0:
            return cand
    return rows


def _chip_sum(gbuf, rbuf, cidx):
    half = rbuf.shape[1]
    rt = _row_tile(half)
    nb = half // rt

    def body(c_ref, g_ref, r_ref, o_ref):
        o_ref[...] = (g_ref[...].astype(F32) + r_ref[...].astype(F32)).astype(BF)

    return pl.pallas_call(
        body, name="chip_sum",
        grid_spec=pltpu.PrefetchScalarGridSpec(
            num_scalar_prefetch=1, grid=(NCHIP, nb),
            in_specs=[pl.BlockSpec((None, rt, D), lambda k, r, c: (k, c[0] * nb + r, 0)),
                      pl.BlockSpec((None, rt, D), lambda k, r, c: (k, r, 0))],
            out_specs=pl.BlockSpec((None, rt, D), lambda k, r, c: (k, r, 0))),
        out_shape=jax.ShapeDtypeStruct((NCHIP, half, D), BF),
        compiler_params=_params(("arbitrary", "arbitrary")),
    )(cidx, gbuf, rbuf)


def _chip_exchange(sbuf, cid):
    half = sbuf.shape[1]

    def body(s_ref, x_ref, send_sems, recv_sems):
        x, y, c, _, chips = _mesh_pos()
        _handshake([(*ch, c) for ch in chips])
        cps = [_remote(s_ref.at[2 * cx + cy], x_ref.at[j], send_sems.at[j], recv_sems.at[j], (cx, cy, c))
               for j, (cx, cy) in enumerate(chips)]
        for cp in cps:
            cp.start()
        for cp in cps:
            cp.wait()

    return _run_comm(body, "chip_exchange_%d" % cid, cid, [sbuf], [],
                     [jax.ShapeDtypeStruct((3, half, D), BF)],
                     (pltpu.SemaphoreType.DMA((3,)), pltpu.SemaphoreType.DMA((3,))))[0]


def _shard_sum(gbuf, rbuf, xbuf, ck):
    half = rbuf.shape[1]
    rt = _row_tile(half)
    nb = half // rt

    def body(ck_ref, g_ref, r_ref, x_ref, o_ref):
        acc = g_ref[...].astype(F32) + r_ref[...].astype(F32)
        for j in range(3):
            acc = acc + x_ref[j].astype(F32)
        o_ref[...] = acc

    return pl.pallas_call(
        body, name="shard_sum",
        grid_spec=pltpu.PrefetchScalarGridSpec(
            num_scalar_prefetch=1, grid=(nb,),
            in_specs=[pl.BlockSpec((None, rt, D), lambda r, ck: (ck[1], ck[0] * nb + r, 0)),
                      pl.BlockSpec((None, rt, D), lambda r, ck: (ck[1], r, 0)),
                      pl.BlockSpec((3, rt, D), lambda r, ck: (0, r, 0))],
            out_specs=pl.BlockSpec((rt, D), lambda r, ck: (ck[0] * nb + r, 0))),
        out_shape=jax.ShapeDtypeStruct((2 * half, D), F32),
        compiler_params=_params(),
    )(ck, gbuf, rbuf, xbuf)


def _sibling_share(red, cid):
    half = red.shape[0] // 2

    def body(o_ref, send_sem, recv_sem):
        x, y, c, _, _ = _mesh_pos()
        sib = (x, y, 1 - c)
        _handshake([sib])
        mine = pl.ds(pl.multiple_of(c * half, 8), half)
        other = pl.ds(pl.multiple_of((1 - c) * half, 8), half)
        cp = _remote(o_ref.at[mine], o_ref.at[mine], send_sem, recv_sem, sib)
        cp.start()
        cp.wait_send()
        _remote(o_ref.at[mine], o_ref.at[other], send_sem, recv_sem, sib).wait_recv()

    return _run_comm(body, "sibling_share_%d" % cid, cid, [], [red], [],
                     (pltpu.SemaphoreType.DMA, pltpu.SemaphoreType.DMA))[0]


def _allreduce_small(v):
    rows = v.shape[0]
    ndev = 2 * NCHIP

    def body(v_ref, o_ref, gat, send_sems, recv_sems, lsem):
        x, y, c, _, chips = _mesh_pos()
        me, sib = (x, y, c), (x, y, 1 - c)

        def blk(px, py, pc):
            return gat.at[pl.ds((4 * px + 2 * py + pc) * rows, rows), :]

        def copy(k, block, to, src=None):
            return pltpu.make_async_remote_copy(
                src_ref=blk(*block) if src is None else src, dst_ref=blk(*block),
                send_sem=send_sems.at[k], recv_sem=recv_sems.at[k], device_id=to, device_id_type=MESH)

        mine = pltpu.make_async_copy(v_ref, blk(*me), lsem)
        mine.start()
        first = [copy(0, me, sib, src=v_ref)]
        first += [copy(1 + j, me, (*chip, c), src=v_ref) for j, chip in enumerate(chips)]
        for cp in first:
            cp.start()
        passed = [copy(4 + j, (*chip, c), sib) for j, chip in enumerate(chips)]
        for j, chip in enumerate(chips):
            copy(1 + j, (*chip, c), me).wait_recv()
            passed[j].start()
        copy(0, sib, me).wait_recv()
        for j, chip in enumerate(chips):
            copy(4 + j, (*chip, 1 - c), me).wait_recv()
        for cp in first + passed:
            cp.wait_send()
        mine.wait()
        acc = gat[0:rows, :]
        for d in range(1, ndev):
            acc = acc + gat[d * rows:(d + 1) * rows, :]
        o_ref[...] = acc

    vm = pl.BlockSpec(memory_space=pltpu.VMEM)
    return pl.pallas_call(
        body, name="allreduce_small", in_specs=[vm], out_specs=vm,
        out_shape=jax.ShapeDtypeStruct((rows, D), F32),
        scratch_shapes=[pltpu.VMEM((ndev * rows, D), F32), pltpu.SemaphoreType.DMA((7,)),
                        pltpu.SemaphoreType.DMA((7,)), pltpu.SemaphoreType.DMA],
    )(v)


def _adamw(w, g, m, v):
    shape = w.shape
    cols = shape[-1]
    rows = w.size // cols
    bm = rows
    for cand in range(512, 0, -SUBLANES):
        if rows % cand == 0:
            bm = cand
            break
    bc1 = 1.0 - ADAM_B1 ** ADAM_STEP
    bc2 = 1.0 - ADAM_B2 ** ADAM_STEP

    def body(w_ref, g_ref, m_ref, v_ref, d_ref, mo_ref, vo_ref):
        gv = g_ref[...]
        mn = ADAM_B1 * m_ref[...] + (1.0 - ADAM_B1) * gv
        vn = ADAM_B2 * v_ref[...] + (1.0 - ADAM_B2) * (gv * gv)
        mo_ref[...] = mn
        vo_ref[...] = vn
        d_ref[...] = -ADAM_LR * ((mn / bc1) / (jnp.sqrt(vn / bc2) + ADAM_EPS) + ADAM_WD * w_ref[...])

    spec = pl.BlockSpec((bm, cols), lambda i: (i, 0))
    out = jax.ShapeDtypeStruct((rows, cols), F32)
    d, mo, vo = pl.pallas_call(
        body, name="adamw", grid=(rows // bm,), in_specs=[spec] * 4, out_specs=[spec] * 3,
        out_shape=[out, out, out], compiler_params=_params(),
    )(*[t.reshape(rows, cols) for t in (w, g, m, v)])
    return d.reshape(shape), mo.reshape(shape), vo.reshape(shape)


def _adamw_layer(w, g, m, v, li, prev):
    cols = w.shape[-1]
    rows = w[0].size // cols
    bm = rows
    for cand in range(512, 0, -SUBLANES):
        if rows % cand == 0:
            bm = cand
            break
    bc1 = 1.0 - ADAM_B1 ** ADAM_STEP
    bc2 = 1.0 - ADAM_B2 ** ADAM_STEP

    def body(*refs):
        w_ref, g_ref, m_ref, v_ref = refs[:4]
        go_ref, d_ref, mo_ref, vo_ref = refs[-4:]
        gv = g_ref[...]
        mn = ADAM_B1 * m_ref[...] + (1.0 - ADAM_B1) * gv
        vn = ADAM_B2 * v_ref[...] + (1.0 - ADAM_B2) * (gv * gv)
        go_ref[...] = gv
        mo_ref[...] = mn
        vo_ref[...] = vn
        d_ref[...] = -ADAM_LR * ((mn / bc1) / (jnp.sqrt(vn / bc2) + ADAM_EPS) + ADAM_WD * w_ref[...])

    layer = pl.BlockSpec((None, bm, cols), lambda i: (li, i, 0))
    in_specs = [layer, pl.BlockSpec((bm, cols), lambda i: (i, 0)), layer, layer]
    args = [w.reshape(DEPTH, rows, cols), g.reshape(rows, cols), m.reshape(DEPTH, rows, cols),
            v.reshape(DEPTH, rows, cols)]
    aliases = {}
    if prev is not None:
        in_specs += [ANY] * 4
        args += list(prev)
        aliases = {4 + i: i for i in range(4)}
    out = jax.ShapeDtypeStruct((DEPTH, rows, cols), F32)
    return pl.pallas_call(
        body, name="adamw_layer", grid=(rows // bm,), in_specs=in_specs, out_specs=[layer] * 4,
        out_shape=[out] * 4, input_output_aliases=aliases, compiler_params=_params(),
    )(*args)


def _pack_shards(ws, li):
    pg = D // NG
    t = lambda a: jnp.swapaxes(a[li], 0, 1)
    parts = [t(ws["ffn1_w_gate"]), t(ws["ffn1_w_up"]), ws["ffn1_w_down"][li],
             t(ws["w_in"]), ws["conv_w_out"][li], ws["w_out"][li],
             t(ws["ffn2_w_gate"]), t(ws["ffn2_w_up"]), ws["ffn2_w_down"][li], ws["ple_w_gate"][li],
             t(ws["ple_w_proj"]).reshape(-1, D),
             jnp.swapaxes(ws["pool_w"][li], 0, 1).reshape(pg // NCHIP, D)]
    return jnp.concatenate([p.astype(BF) for p in parts], axis=0)


def _unpack_shards(parts):
    lay, _ = _layout()
    pg = D // NG

    def rows(n):
        off, rs = lay[n]
        for r0, arr in parts:
            if r0 <= off and off + rs <= r0 + arr.shape[0]:
                return arr[off - r0:off - r0 + rs, :]
        raise ValueError(n)

    t = lambda a: jnp.swapaxes(a, 0, 1)
    return {
        "ffn1_w_gate": rows("wg1"), "ffn1_w_up": rows("wu1"), "ffn1_w_down": rows("wd1"),
        "ffn2_w_gate": rows("wg2"), "ffn2_w_up": rows("wu2"), "ffn2_w_down": rows("wd2"),
        "w_in": t(rows("win")), "conv_w_out": rows("wco"), "w_out": rows("wo"), "ple_w_gate": rows("wpg"),
        "ple_w_proj": t(rows("wpp").reshape(D // NCHIP, PD)),
        "pool_w": jnp.swapaxes(rows("pw").reshape(pg // NCHIP, NG, pg), 0, 1),
    }


_BIG = ("ffn1_w_gate", "ffn1_w_up", "ffn1_w_down", "w_in", "pool_w", "conv_w_out", "w_out",
        "ffn2_w_gate", "ffn2_w_up", "ffn2_w_down", "ple_w_gate", "ple_w_proj")
_TRANSPOSED = ("ffn1_w_gate", "ffn1_w_up", "ffn2_w_gate", "ffn2_w_up")
_VECS = ("ffn1_norm", "mix_norm", "pool_scale", "conv_dw_b", "conv_ln_g", "conv_ln_b", "ffn2_norm", "ple_norm")
_WEIGHTS = ("ffn1_norm", "ffn1_w_gate", "ffn1_w_up", "ffn1_w_down", "mix_norm", "w_in", "pool_w", "pool_scale",
            "conv_dw_w", "conv_dw_b", "conv_ln_g", "conv_ln_b", "conv_w_out", "w_out", "ffn2_norm",
            "ffn2_w_gate", "ffn2_w_up", "ffn2_w_down", "ple_norm", "ple_w_gate", "ple_w_proj", "final_norm")


def _step(x, p, tgt, ws, ms, vs):
    lay, rtot = _layout()
    pg = D // NG
    cpos = lax.axis_index("c")
    kme = 2 * lax.axis_index("x") + lax.axis_index("y")
    cidx = jnp.stack([cpos]).astype(jnp.int32)
    ck = jnp.stack([cpos, kme]).astype(jnp.int32)
    h = x.reshape(T, D)
    tgt = tgt.reshape(T, D)

    nfirst = _first_rows()
    packed = [_pack_shards(ws, li) for li in range(DEPTH)]
    wfirst, wrest = [None] * DEPTH, [None] * DEPTH
    wfirst[0] = _gather_layer(packed[0][:nfirst], 0)
    kk = ws["conv_dw_w"].shape[1]
    wdw_mine = jnp.zeros((DEPTH * HALO, D), F32)
    for li in range(DEPTH):
        blockw = jnp.zeros((kk, D), F32)
        mine = jnp.where(cpos == 0, ws["conv_dw_w"][li], 0.0)
        blockw = lax.dynamic_update_slice(blockw, mine, (0, kme * (D // NCHIP)))
        wdw_mine = wdw_mine.at[li * HALO:li * HALO + kk, :].set(blockw)
    wdw_all = _allreduce_small(wdw_mine)
    piece, wfirst[0], wdw_all = lax.optimization_barrier((packed[0][nfirst:], wfirst[0], wdw_all))
    wrest[0] = _gather_layer(piece, 1)
    wppt, pwcat = [None] * DEPTH, [None] * DEPTH
    wdw = [wdw_all[li * HALO:(li + 1) * HALO, :] for li in range(DEPTH)]
    vec = lambda name, li: _row(ws[name][li])

    saved = []
    for li in range(DEPTH):
        if li > 0:
            wfirst[li], wrest[li], h = lax.optimization_barrier((wfirst[li], wrest[li], h))
        h0 = h
        h1, a1, b1 = _ffn_fwd(h0, vec("ffn1_norm", li), wfirst[li], 1)
        if li + 1 < DEPTH:
            piece, wrest[li], h1 = lax.optimization_barrier((packed[li + 1][:nfirst], wrest[li], h1))
            wfirst[li + 1] = _gather_layer(piece, 2 * li + 2)
        else:
            wrest[li], h1 = lax.optimization_barrier((wrest[li], h1))
        o, s = lay["wpp"]
        wppt[li] = wrest[li][:, o - nfirst:o - nfirst + s, :].reshape(D, PD)
        o, s = lay["pw"]
        pwcat[li] = wrest[li][:, o - nfirst:o - nfirst + s, :].reshape(pg, D)
        z = _mix_in_fwd(h1, vec("mix_norm", li), wrest[li])
        h2, pooled, c1, cc = _mix_mid_fwd(h1, z, pwcat[li], vec("pool_scale", li), wdw[li],
                                          vec("conv_dw_b", li), vec("conv_ln_g", li), vec("conv_ln_b", li),
                                          wrest[li])
        if li + 1 < DEPTH:
            piece, wfirst[li + 1], h2 = lax.optimization_barrier((packed[li + 1][nfirst:], wfirst[li + 1], h2))
            wrest[li + 1] = _gather_layer(piece, 2 * li + 3)
        h3, a2, b2 = _ffn_fwd(h2, vec("ffn2_norm", li), wrest[li], 2)
        h = _ple_fwd(h3, p[li, 0], vec("ple_norm", li), wppt[li], wrest[li])
        saved.append((h0, a1, b1, h1, z, pooled, c1, cc, h2, a2, b2, h3))

    dh, losscols, dgf = _loss_bwd(h, tgt, _row(ws["final_norm"]))
    vecg = [dict() for _ in range(DEPTH)]
    dwdw = [None] * DEPTH
    def reduce_start(g, chain, inline=False):
        cid = 2 * DEPTH + 3 * chain
        return {"g": g, "r": _sibling_swap_inline(g) if inline else _sibling_swap(g, cid), "cid": cid}

    def reduce_mid(st, anchor):
        if anchor is not None:
            anchor, st["r"] = lax.optimization_barrier((anchor, st["r"]))
        sbuf = _chip_sum(st["g"], st["r"], cidx)
        if anchor is not None:
            anchor, sbuf = lax.optimization_barrier((anchor, sbuf))
        st["x"] = _chip_exchange(sbuf, st["cid"] + 1)
        return anchor

    def reduce_end(st, anchor):
        xb = st["x"]
        if anchor is not None:
            anchor, xb = lax.optimization_barrier((anchor, xb))
        rsum = _shard_sum(st["g"], st["r"], xb, ck)
        if anchor is not None:
            anchor, rsum = lax.optimization_barrier((anchor, rsum))
        return anchor, _sibling_share(rsum, st["cid"] + 2)

    parts = [[] for _ in range(DEPTH)]
    above = None
    nchain = 0
    for li in reversed(range(DEPTH)):
        h0, a1, b1, h1, z, pooled, c1, cc, h2, a2, b2, h3 = saved[li]
        w = wrest[li]
        gbase = nfirst if li == 0 else 0
        dh, dgp, dwpp, gbuf = _ple_bwd(h3, dh, p[li, 0], vec("ple_norm", li), wppt[li], w, rtot - gbase)
        vecg[li]["ple_norm"] = dgp
        if above is not None:
            dh = reduce_mid(above[1], dh)
        dh_in, da, db, sact, n, dg = _ffn_bwd(h2, dh, a2, b2, vec("ffn2_norm", li), w, 2)
        vecg[li]["ffn2_norm"] = dg
        gbuf = _wgrad([da, db], n, gbuf, ("wg2", "wu2"), 0, F // 2, F // NCHIP)
        gbuf = _wgrad([sact], dh, gbuf, ("wd2",), 0, F // 2, F // NCHIP, yscale=0.5)
        dh = dh_in
        dp, dc1, dzb, small, dpw, gbuf = _mix_b1(dh, z, pooled, c1, cc, pwcat[li], vec("pool_scale", li),
                                                 vec("conv_ln_g", li), vec("conv_ln_b", li), w, gbuf)
        vecg[li]["pool_scale"] = small[0:1]
        vecg[li]["conv_dw_b"] = small[1:2]
        vecg[li]["conv_ln_g"] = small[2:3]
        vecg[li]["conv_ln_b"] = small[3:4]
        dza, dwdw[li] = _mix_b2(dp, dc1, z, wdw[li])
        if above is not None:
            dza, rsum = reduce_end(above[1], dza)
            parts[above[0]].append((0, rsum))
            above = None
        dh_in, u, dg = _mix_b3(h1, dh, dza, dzb, vec("mix_norm", li), w)
        vecg[li]["mix_norm"] = dg
        gbuf = _wgrad([dza], u, gbuf, ("win",), 0, D, D // NCHIP)
        gbuf = _wgrad([dzb], u, gbuf, ("win",), 3 * D, D, D // NCHIP)
        dh = dh_in
        o, s = lay["wpp"]
        small_rows = jnp.concatenate([dwpp.reshape(NCHIP, s, D), dpw.reshape(NCHIP, lay["pw"][1], D)], axis=1)
        gbuf = lax.dynamic_update_slice(gbuf, small_rows.astype(BF), (0, o - gbase, 0))
        if li == 0:
            rest = reduce_start(gbuf, nchain)
            nchain += 1
        dh_in, da, db, sact, n, dg = _ffn_bwd(h0, dh, a1, b1, vec("ffn1_norm", li), wfirst[li], 1)
        vecg[li]["ffn1_norm"] = dg
        if li == 0:
            da = reduce_mid(rest, da)
            ngu = 2 * (F // NCHIP)
            ggu = _wgrad([da, db], n, None, ("wg1", "wu1"), 0, F // 2, F // NCHIP, new_rows=ngu)
            gateup = reduce_start(ggu, nchain, inline=True)
            nchain += 1
            gdown = _wgrad([sact], dh, None, ("wd1",), 0, F // 2, F // NCHIP, yscale=0.5, new_rows=nfirst - ngu)
            reduce_mid(gateup, None)
            first = reduce_start(gdown, nchain, inline=True)
            nchain += 1
            reduce_mid(first, None)
        else:
            gbuf = _wgrad([da, db], n, gbuf, ("wg1", "wu1"), 0, F // 2, F // NCHIP)
            gbuf = _wgrad([sact], dh, gbuf, ("wd1",), 0, F // 2, F // NCHIP, yscale=0.5)
        dh = dh_in
        if li > 0:
            above = (li, reduce_start(gbuf, nchain))
            nchain += 1
    grad_x = dh.reshape(x.shape)
    rsum = _shard_sum(rest["g"], rest["r"], rest["x"], ck)
    gateup["r"], rsum = lax.optimization_barrier((gateup["r"], rsum))
    parts[0].append((nfirst, _sibling_share(rsum, rest["cid"] + 2)))
    updated = {n: None for n in _BIG}
    view = lambda n, a: jnp.swapaxes(a, 1, 2) if n in _TRANSPOSED else a
    for li in reversed(range(1, DEPTH)):
        un = _unpack_shards(parts[li])
        for n in _BIG:
            updated[n] = _adamw_layer(view(n, ws[n]), un[n], view(n, ms[n]), view(n, vs[n]), li, updated[n])
    gateup["r"], updated = lax.optimization_barrier((gateup["r"], updated))
    rsum = _shard_sum(gateup["g"], gateup["r"], gateup["x"], ck)
    first["r"], rsum = lax.optimization_barrier((first["r"], rsum))
    parts[0].append((0, _sibling_share(rsum, gateup["cid"] + 2)))
    parts[0].append((ngu, reduce_end(first, None)[1]))
    un = _unpack_shards(parts[0])
    for n in _BIG:
        updated[n] = _adamw_layer(view(n, ws[n]), un[n], view(n, ms[n]), view(n, vs[n]), 0, updated[n])

    rows = [vecg[li][n] for li in range(DEPTH) for n in _VECS] + [dgf, losscols]
    rows.append(jnp.zeros((-len(rows) % 8, D), F32))
    base = sum(r.shape[0] for r in rows)
    vsum = _allreduce_small(jnp.concatenate(rows + dwdw, axis=0))
    nvec = len(_VECS)
    grads = {}
    for i, n in enumerate(_VECS):
        grads[n] = jnp.stack([vsum[li * nvec + i] for li in range(DEPTH)])
    grads["final_norm"] = vsum[DEPTH * nvec]
    loss = jnp.sum(vsum[DEPTH * nvec + 1])
    dcols = D // NCHIP
    grads["conv_dw_w"] = jnp.stack([
        lax.dynamic_slice(vsum[base + li * HALO: base + li * HALO + kk, :], (0, kme * dcols), (kk, dcols))
        for li in range(DEPTH)])

    outs_g, outs_d, outs_m, outs_v = [], [], [], []
    for n in _WEIGHTS:
        if n in _BIG:
            gq, d, mo, vo = [view(n, t.reshape(view(n, ws[n]).shape)) for t in updated[n]]
        else:
            gq = grads[n]
            d, mo, vo = _adamw(ws[n], gq, ms[n], vs[n])
        outs_g.append(gq)
        outs_d.append(d)
        outs_m.append(mo)
        outs_v.append(vo)
    return (loss, grad_x, *outs_g, *outs_d, *outs_m, *outs_v)


def kernel(x, p, ffn1_norm, ffn1_w_gate, ffn1_w_up, ffn1_w_down, mix_norm, w_in, pool_w, pool_scale, conv_dw_w, conv_dw_b, conv_ln_g, conv_ln_b, conv_w_out, w_out, ffn2_norm, ffn2_w_gate, ffn2_w_up, ffn2_w_down, ple_norm, ple_w_gate, ple_w_proj, final_norm, loss_target, m_ffn1_norm, m_ffn1_w_gate, m_ffn1_w_up, m_ffn1_w_down, m_mix_norm, m_w_in, m_pool_w, m_pool_scale, m_conv_dw_w, m_conv_dw_b, m_conv_ln_g, m_conv_ln_b, m_conv_w_out, m_w_out, m_ffn2_norm, m_ffn2_w_gate, m_ffn2_w_up, m_ffn2_w_down, m_ple_norm, m_ple_w_gate, m_ple_w_proj, m_final_norm, v_ffn1_norm, v_ffn1_w_gate, v_ffn1_w_up, v_ffn1_w_down, v_mix_norm, v_w_in, v_pool_w, v_pool_scale, v_conv_dw_w, v_conv_dw_b, v_conv_ln_g, v_conv_ln_b, v_conv_w_out, v_w_out, v_ffn2_norm, v_ffn2_w_gate, v_ffn2_w_up, v_ffn2_w_down, v_ple_norm, v_ple_w_gate, v_ple_w_proj, v_final_norm):
    ws = dict(zip(_WEIGHTS, (ffn1_norm, ffn1_w_gate, ffn1_w_up, ffn1_w_down, mix_norm, w_in, pool_w, pool_scale, conv_dw_w, conv_dw_b, conv_ln_g, conv_ln_b, conv_w_out, w_out, ffn2_norm, ffn2_w_gate, ffn2_w_up, ffn2_w_down, ple_norm, ple_w_gate, ple_w_proj, final_norm)))
    ms = dict(zip(_WEIGHTS, (m_ffn1_norm, m_ffn1_w_gate, m_ffn1_w_up, m_ffn1_w_down, m_mix_norm, m_w_in, m_pool_w, m_pool_scale, m_conv_dw_w, m_conv_dw_b, m_conv_ln_g, m_conv_ln_b, m_conv_w_out, m_w_out, m_ffn2_norm, m_ffn2_w_gate, m_ffn2_w_up, m_ffn2_w_down, m_ple_norm, m_ple_w_gate, m_ple_w_proj, m_final_norm)))
    vs = dict(zip(_WEIGHTS, (v_ffn1_norm, v_ffn1_w_gate, v_ffn1_w_up, v_ffn1_w_down, v_mix_norm, v_w_in, v_pool_w, v_pool_scale, v_conv_dw_w, v_conv_dw_b, v_conv_ln_g, v_conv_ln_b, v_conv_w_out, v_w_out, v_ffn2_norm, v_ffn2_w_gate, v_ffn2_w_up, v_ffn2_w_down, v_ple_norm, v_ple_w_gate, v_ple_w_proj, v_final_norm)))
    return _step(x, p, loss_target, ws, ms, vs)
```
